```python
import functools
import jax
import jax.numpy as jnp
from jax import lax
import numpy as np

D_MODEL = 2048
BATCH = 2
SEQ = 4096
DEPTH = 1
DEC_BATCH = 32
DEC_SEQ = 8
PAST_LEN = 16384
PAGE_SIZE = 128

PLE_DIM = 256
HEAD_DIM = 64
N_Q_HEADS = D_MODEL // 128
N_KV_HEADS = N_Q_HEADS // 4
GQA_GROUP = N_Q_HEADS // N_KV_HEADS
ATTN_WIDTH = N_Q_HEADS * HEAD_DIM
KV_WIDTH = N_KV_HEADS * HEAD_DIM
WINDOW = 128
ALIBI_MAX = 8.0
R_HEAD = 64
R_WIDTH = D_MODEL // 2
R_HEADS = R_WIDTH // R_HEAD
DECAY_LORA = 64
AAA_LORA = 64
GATE_LORA = 160
RWKV_COLS = 3 * R_WIDTH + DECAY_LORA + AAA_LORA + GATE_LORA
IN_COLS = ATTN_WIDTH + 2 * KV_WIDTH + RWKV_COLS + 2 * D_MODEL
IN_SPLITS = (ATTN_WIDTH, ATTN_WIDTH + KV_WIDTH, ATTN_WIDTH + 2 * KV_WIDTH,
             ATTN_WIDTH + 2 * KV_WIDTH + RWKV_COLS, ATTN_WIDTH + 2 * KV_WIDTH + RWKV_COLS + D_MODEL)
R_SPLITS = (R_WIDTH, 2 * R_WIDTH, 3 * R_WIDTH, 3 * R_WIDTH + DECAY_LORA, 3 * R_WIDTH + DECAY_LORA + AAA_LORA)
D_FF = 4 * D_MODEL
NORM_EPS = 1e-6
GN_EPS = 64e-5

kernel_name = 'hybrid_swa_sink_rwkv7_step'


def rms_norm(x, g):
    x32 = x.astype(jnp.float32)
    y = x32 * lax.rsqrt(jnp.mean(x32 * x32, axis=-1, keepdims=True) + NORM_EPS)
    return (y * g.astype(jnp.float32)).astype(x.dtype)


def alibi_slopes():
    h = jnp.arange(1, N_Q_HEADS + 1, dtype=jnp.float32)
    return jnp.exp2(-ALIBI_MAX * h / N_Q_HEADS).reshape(N_KV_HEADS, GQA_GROUP)


def sink_attention(q, k, v, dist, valid, sinks):
    s = jnp.einsum('bnqhgd,bnshd->bnhgqs', q, k).astype(jnp.float32) * (HEAD_DIM ** -0.5)
    s = s - alibi_slopes()[None, None, :, :, None, None] * dist[None, :, None, None]
    s = jnp.where(valid[None, :, None, None], s, -jnp.inf)
    sink = sinks.astype(jnp.float32).reshape(N_KV_HEADS, GQA_GROUP)[None, None, :, :, None, None]
    m = jnp.maximum(jnp.max(s, axis=-1, keepdims=True), sink)
    p = jnp.exp(s - m)
    probs = p / (jnp.sum(p, axis=-1, keepdims=True) + jnp.exp(sink - m))
    return jnp.einsum('bnhgqs,bnshd->bnqhgd', probs.astype(v.dtype), v)


def window_attention_prompt(q, k, v, sinks):
    b, t = q.shape[0], q.shape[1]
    nb = t // WINDOW
    qb = q.reshape(b, nb, WINDOW, N_KV_HEADS, GQA_GROUP, HEAD_DIM)

    def band(z):
        pad = jnp.zeros((b, WINDOW, N_KV_HEADS, HEAD_DIM), z.dtype)
        zb = jnp.concatenate([pad, z], axis=1).reshape(b, nb + 1, WINDOW, N_KV_HEADS, HEAD_DIM)
        return jnp.concatenate([zb[:, :-1], zb[:, 1:]], axis=2)

    qi = jnp.arange(WINDOW)[:, None]
    kj = jnp.arange(2 * WINDOW)[None, :]
    dist = qi - kj + WINDOW
    key_pos = jnp.arange(nb)[:, None, None] * WINDOW - WINDOW + kj[None]
    valid = (dist >= 0) & (dist <= WINDOW) & (key_pos >= 0)
    o = sink_attention(qb, band(k), band(v), dist[None].astype(jnp.float32), valid, sinks)
    return o.reshape(b, t, ATTN_WIDTH), k[:, -WINDOW:], v[:, -WINDOW:]


def window_attention_sample(k_buf, v_buf, q, k, v, sinks):
    b, t = q.shape[0], q.shape[1]
    kf = jnp.concatenate([k_buf.astype(k.dtype), k], axis=1)
    vf = jnp.concatenate([v_buf.astype(v.dtype), v], axis=1)
    qi = jnp.arange(t)[:, None]
    kj = jnp.arange(WINDOW + t)[None, :]
    dist = qi - kj + WINDOW
    valid = (dist >= 0) & (dist <= WINDOW)
    o = sink_attention(q[:, None], kf[:, None], vf[:, None], dist[None].astype(jnp.float32), valid[None], sinks)
    return o.reshape(b, t, ATTN_WIDTH), kf[:, -WINDOW:], vf[:, -WINDOW:]


def wkv_scan(r, w, k, v, kk, a, s0):
    seq = tuple(jnp.moveaxis(z, 1, 0) for z in (r, w, k, v, kk, a))

    def step(S, inp):
        r_t, w_t, k_t, v_t, kk_t, a_t = inp
        sa = jnp.einsum('bhvk,bhk->bhv', S, -kk_t)
        S = (S * w_t[:, :, None, :] + sa[..., None] * (kk_t * a_t)[:, :, None, :]
             + v_t[..., None] * k_t[:, :, None, :])
        return S, jnp.einsum('bhvk,bhk->bhv', S, r_t)

    S, y = lax.scan(step, s0.astype(jnp.float32), seq)
    return jnp.moveaxis(y, 0, 1), S


def rwkv_time_mix(cols, prev, s0, mu, w0, w2, a0, a2, g2, k_k, k_a, r_k, ln_w, ln_b):
    b, t = cols.shape[0], cols.shape[1]
    f32 = jnp.float32
    shifted = jnp.concatenate([prev[:, None].astype(cols.dtype), cols[:, :-1]], axis=1)
    xx = cols + (shifted - cols) * mu
    r, k, v, wd, ad, gd = jnp.split(xx, R_SPLITS, axis=-1)
    wlog = -jax.nn.softplus(-(w0 + jnp.tanh(wd) @ w2).astype(f32)) - 0.5
    decay = jnp.exp(-jnp.exp(wlog))
    a = jax.nn.sigmoid((a0 + ad @ a2).astype(f32))
    g = jax.nn.sigmoid(gd) @ g2

    def heads(z):
        return z.reshape(b, t, R_HEADS, R_HEAD)

    kk = heads((k * k_k).astype(f32))
    kk = kk * lax.rsqrt(jnp.maximum(jnp.sum(kk * kk, axis=-1, keepdims=True), 1e-24))
    k32 = heads(k.astype(f32) * (1.0 + (a - 1.0) * k_a))
    r32 = heads(r.astype(f32))
    v32 = heads(v.astype(f32))
    y, S = wkv_scan(r32, heads(decay), k32, v32, kk, heads(a), s0)
    mean = jnp.mean(y, axis=-1, keepdims=True)
    var = jnp.mean(jnp.square(y - mean), axis=-1, keepdims=True)
    y = (y - mean) * lax.rsqrt(var + GN_EPS) * ln_w.reshape(R_HEADS, R_HEAD) + ln_b.reshape(R_HEADS, R_HEAD)
    y = y + jnp.sum(r32 * k32 * r_k, axis=-1, keepdims=True) * v32
    y = y.reshape(b, t, R_WIDTH).astype(cols.dtype) * g
    return y, S, cols[:, -1]


def run_layer(x, pe, attend, s0, prev, lw):
    (norm_mix, w_in, attn_sinks, rwkv_mu, rwkv_w0, rwkv_w2, rwkv_a0, rwkv_a2, rwkv_g2,
     rwkv_k_k, rwkv_k_a, rwkv_r_k, rwkv_ln_w, rwkv_ln_b, w_branch_attn, w_branch_rwkv,
     w_out, norm_ffn, w_ff_up, w_ff_down, w_ple_proj, w_ple_gate) = lw
    b, t = x.shape[0], x.shape[1]
    h = rms_norm(x, norm_mix)
    proj = h @ w_in
    q, k, v, cols, gate_a, gate_r = jnp.split(proj, IN_SPLITS, axis=-1)
    q = q.reshape(b, t, N_KV_HEADS, GQA_GROUP, HEAD_DIM)
    k = k.reshape(b, t, N_KV_HEADS, HEAD_DIM)
    v = v.reshape(b, t, N_KV_HEADS, HEAD_DIM)
    y_a, k_buf, v_buf = attend(q, k, v, attn_sinks)
    y_r, s_new, prev_new = rwkv_time_mix(cols, prev, s0, rwkv_mu, rwkv_w0, rwkv_w2, rwkv_a0, rwkv_a2,
                                         rwkv_g2, rwkv_k_k, rwkv_k_a, rwkv_r_k, rwkv_ln_w, rwkv_ln_b)
    mixed = (jax.nn.sigmoid(gate_a) * (y_a @ w_branch_attn)
             + jax.nn.sigmoid(gate_r) * (y_r @ w_branch_rwkv))
    x = x + mixed @ w_out
    u = rms_norm(x, norm_ffn) @ w_ff_up
    x = x + jnp.square(jax.nn.relu(u)) @ w_ff_down
    x = x + jax.nn.sigmoid(x @ w_ple_gate) * (pe @ w_ple_proj)
    return x, k_buf, v_buf, s_new.astype(x.dtype), prev_new


def setup_inputs(seed: int = 0) -> dict:
    key = jax.random.key(seed)
    ks = list(jax.random.split(key, 48))
    f32 = jnp.float32
    L = DEPTH

    def nk():
        return ks.pop()

    def normal(shape, scale=1.0):
        return jax.random.normal(nk(), shape, f32) * scale

    def dense(shape, fan_in, scale=1.0):
        return normal(shape, scale * fan_in ** -0.5)

    def gain(shape):
        return 1.0 + normal(shape, 0.01)

    return {
        'x_prompt': normal((BATCH, SEQ, D_MODEL)),
        'x_sample': normal((DEC_BATCH, DEC_SEQ, D_MODEL)),
        'cache_k_win': normal((L, DEC_BATCH, WINDOW, N_KV_HEADS, HEAD_DIM)),
        'cache_v_win': normal((L, DEC_BATCH, WINDOW, N_KV_HEADS, HEAD_DIM)),
        'state_wkv': normal((L, DEC_BATCH, R_HEADS, R_HEAD, R_HEAD), 0.5),
        'state_shift': normal((L, DEC_BATCH, RWKV_COLS)),
        'p_prompt': normal((L, BATCH, SEQ, PLE_DIM)),
        'p_sample': normal((L, DEC_BATCH, DEC_SEQ, PLE_DIM)),
        'norm_mix': gain((L, D_MODEL)),
        'w_in': dense((L, D_MODEL, IN_COLS), D_MODEL),
        'attn_sinks': normal((L, N_Q_HEADS)),
        'rwkv_mu': jax.random.uniform(nk(), (L, RWKV_COLS), f32),
        'rwkv_w0': jax.random.uniform(nk(), (L, R_WIDTH), f32, -6.0, 1.0),
        'rwkv_w2': dense((L, DECAY_LORA, R_WIDTH), DECAY_LORA, 0.1),
        'rwkv_a0': normal((L, R_WIDTH), 0.1),
        'rwkv_a2': dense((L, AAA_LORA, R_WIDTH), AAA_LORA, 0.1),
        'rwkv_g2': dense((L, GATE_LORA, R_WIDTH), GATE_LORA),
        'rwkv_k_k': 0.85 + normal((L, R_WIDTH), 0.02),
        'rwkv_k_a': 1.0 + normal((L, R_WIDTH), 0.02),
        'rwkv_r_k': normal((L, R_HEADS, R_HEAD), 0.1),
        'rwkv_ln_w': gain((L, R_WIDTH)),
        'rwkv_ln_b': normal((L, R_WIDTH), 0.01),
        'w_branch_attn': dense((L, ATTN_WIDTH, D_MODEL), ATTN_WIDTH),
        'w_branch_rwkv': dense((L, R_WIDTH, D_MODEL), R_WIDTH),
        'w_out': dense((L, D_MODEL, D_MODEL), D_MODEL),
        'norm_ffn': gain((L, D_MODEL)),
        'w_ff_up': dense((L, D_MODEL, D_FF), D_MODEL),
        'w_ff_down': dense((L, D_FF, D_MODEL), D_FF),
        'w_ple_proj': dense((L, PLE_DIM, D_MODEL), PLE_DIM),
        'w_ple_gate': dense((L, D_MODEL, D_MODEL), D_MODEL),
        'norm_final': gain((D_MODEL,)),
    }


def reference(x_prompt, x_sample, cache_k_win, cache_v_win, state_wkv, state_shift, p_prompt, p_sample,
              norm_mix, w_in, attn_sinks, rwkv_mu, rwkv_w0, rwkv_w2, rwkv_a0, rwkv_a2, rwkv_g2,
              rwkv_k_k, rwkv_k_a, rwkv_r_k, rwkv_ln_w, rwkv_ln_b, w_branch_attn, w_branch_rwkv,
              w_out, norm_ffn, w_ff_up, w_ff_down, w_ple_proj, w_ple_gate, norm_final):
    xp, xs = x_prompt, x_sample
    bp = x_prompt.shape[0]
    kp_l, vp_l, sp_l, hp_l = [], [], [], []
    ks_l, vs_l, ss_l, hs_l = [], [], [], []
    for i in range(DEPTH):
        lw = (norm_mix[i], w_in[i], attn_sinks[i], rwkv_mu[i], rwkv_w0[i], rwkv_w2[i], rwkv_a0[i],
              rwkv_a2[i], rwkv_g2[i], rwkv_k_k[i], rwkv_k_a[i], rwkv_r_k[i], rwkv_ln_w[i], rwkv_ln_b[i],
              w_branch_attn[i], w_branch_rwkv[i], w_out[i], norm_ffn[i], w_ff_up[i], w_ff_down[i],
              w_ple_proj[i], w_ple_gate[i])
        s0_p = jnp.zeros((bp, R_HEADS, R_HEAD, R_HEAD), jnp.float32)
        prev_p = jnp.zeros((bp, RWKV_COLS), xp.dtype)
        xp, kb, vb, sn, hn = run_layer(xp, p_prompt[i], window_attention_prompt, s0_p, prev_p, lw)
        kp_l.append(kb); vp_l.append(vb); sp_l.append(sn); hp_l.append(hn)
        attend_s = functools.partial(window_attention_sample, cache_k_win[i], cache_v_win[i])
        xs, kb, vb, sn, hn = run_layer(xs, p_sample[i], attend_s, state_wkv[i], state_shift[i], lw)
        ks_l.append(kb); vs_l.append(vb); ss_l.append(sn); hs_l.append(hn)
    y_prompt = rms_norm(xp, norm_final)
    y_sample = rms_norm(xs, norm_final)
    new_k_win_prompt = jnp.stack(kp_l)
    new_v_win_prompt = jnp.stack(vp_l)
    new_wkv_prompt = jnp.stack(sp_l)
    new_shift_prompt = jnp.stack(hp_l)
    new_k_win_sample = jnp.stack(ks_l)
    new_v_win_sample = jnp.stack(vs_l)
    new_wkv_sample = jnp.stack(ss_l)
    new_shift_sample = jnp.stack(hs_l)
    return (y_prompt, y_sample, new_k_win_prompt, new_v_win_prompt, new_wkv_prompt, new_shift_prompt,
            new_k_win_sample, new_v_win_sample, new_wkv_sample, new_shift_sample)
```

```python
import functools

import numpy as np
import jax
import jax.numpy as jnp
from jax import lax
from jax.experimental import pallas as pl
from jax.experimental.pallas import tpu as pltpu

F32 = jnp.float32
BF16 = jnp.bfloat16

D_MODEL = 2048
PLE_DIM = 256
HEAD_DIM = 64
N_Q_HEADS = 16
N_KV_HEADS = 4
GQA_GROUP = 4
ATTN_WIDTH = 1024
KV_WIDTH = 256
WINDOW = 128
ALIBI_MAX = 8.0
R_HEAD = 64
R_WIDTH = 1024
R_HEADS = 16
DECAY_LORA = 64
AAA_LORA = 64
GATE_LORA = 160
LORA_COLS = DECAY_LORA + AAA_LORA + GATE_LORA
LORA_PAD = 512
RWKV_COLS = 3 * R_WIDTH + LORA_COLS
D_FF = 4 * D_MODEL
NORM_EPS = 1e-6
GN_EPS = 64e-5

P_Q = 0
P_R = 1024
P_GA = 4096
P_GR = 6144
P_K = 8192
P_V = 8448
P_L = 8704
P_COLS = 9216
SHIFT_COLS = 3 * R_WIDTH + LORA_PAD

PAIR = 128
N_PAIRS = R_WIDTH // PAIR
SEG = 256

VMEM_LIMIT = 56 * 1024 * 1024


def _mm(a, b):
    return jnp.dot(a, b, preferred_element_type=F32)


def _nt(a, b):
    return lax.dot_general(a, b, (((1,), (1,)), ((), ())), preferred_element_type=F32)


def _tn(a, b):
    return lax.dot_general(a, b, (((0,), (0,)), ((), ())), preferred_element_type=F32)


def _rms(x, g):
    ms = jnp.mean(x * x, axis=-1, keepdims=True)
    return x * lax.rsqrt(ms + NORM_EPS) * g


def _alibi_slope(hq):
    return float(2.0 ** (-ALIBI_MAX * (hq + 1) / N_Q_HEADS))


def _proj_kernel(x_ref, g_ref, w_ref, o_ref, h_ref):
    @pl.when(pl.program_id(1) == 0)
    def _():
        h_ref[...] = _rms(x_ref[...], g_ref[...]).astype(BF16)

    o_ref[...] = _mm(h_ref[...], w_ref[...])


def _proj(x, g, w, tm, tn):
    m = x.shape[0]
    return pl.pallas_call(
        _proj_kernel,
        out_shape=jax.ShapeDtypeStruct((m, P_COLS), F32),
        grid=(m // tm, P_COLS // tn),
        in_specs=[
            pl.BlockSpec((tm, D_MODEL), lambda i, j: (i, 0)),
            pl.BlockSpec((1, D_MODEL), lambda i, j: (0, 0)),
            pl.BlockSpec((D_MODEL, tn), lambda i, j: (0, j)),
        ],
        out_specs=pl.BlockSpec((tm, tn), lambda i, j: (i, j)),
        scratch_shapes=[pltpu.VMEM((tm, D_MODEL), BF16)],
        compiler_params=pltpu.CompilerParams(
            dimension_semantics=("arbitrary", "arbitrary"), vmem_limit_bytes=VMEM_LIMIT),
        name="proj",
    )(x, g, w)


def _softmax_pv(parts, sink):
    m = sink
    for s, _ in parts:
        m = jnp.maximum(m, jnp.max(s, axis=-1, keepdims=True))
    ps = [jnp.exp(s - m) for s, _ in parts]
    den = jnp.exp(sink - m)
    for p in ps:
        den = den + jnp.sum(p, axis=-1, keepdims=True)
    o = None
    for p, (_, v) in zip(ps, parts):
        t = _mm((p / den).astype(BF16), v)
        o = t if o is None else o + t
    return o


def _attn_prompt_kernel(sink_ref, q_ref, kp_ref, kc_ref, vp_ref, vc_ref, o_ref):
    qb = pl.program_id(1)
    w = WINDOW
    ti = lax.broadcasted_iota(jnp.int32, (w, 2 * w), 0)
    kj = lax.broadcasted_iota(jnp.int32, (w, 2 * w), 1)
    dist = ti - kj + w
    valid = (dist >= 0) & (dist <= w) & ((kj >= w) | (qb > 0))
    distf = dist.astype(F32)
    q = q_ref[...]
    k2 = jnp.concatenate([kp_ref[...], kc_ref[...]], axis=0).astype(BF16)
    v2 = jnp.concatenate([vp_ref[...], vc_ref[...]], axis=0).astype(BF16)
    for h in range(N_KV_HEADS):
        kh = k2[:, h * HEAD_DIM:(h + 1) * HEAD_DIM]
        vh = v2[:, h * HEAD_DIM:(h + 1) * HEAD_DIM]
        for g in range(GQA_GROUP):
            hq = h * GQA_GROUP + g
            qh = q[:, hq * HEAD_DIM:(hq + 1) * HEAD_DIM].astype(BF16)
            s = _nt(qh, kh) * (HEAD_DIM ** -0.5) - _alibi_slope(hq) * distf
            s = jnp.where(valid, s, -jnp.inf)
            o_ref[:, hq * HEAD_DIM:(hq + 1) * HEAD_DIM] = _softmax_pv([(s, vh)], sink_ref[hq])


def _attn_prompt(proj, sinks, batch, seq):
    nb = seq // WINDOW
    kcol, vcol = P_K // KV_WIDTH, P_V // KV_WIDTH

    def cur(c):
        return lambda b, i: (b * nb + i, c)

    def prev(c):
        return lambda b, i: (b * nb + jnp.maximum(i - 1, 0), c)

    return pl.pallas_call(
        _attn_prompt_kernel,
        out_shape=jax.ShapeDtypeStruct((batch * seq, ATTN_WIDTH), F32),
        grid=(batch, nb),
        in_specs=[
            pl.BlockSpec(memory_space=pltpu.SMEM),
            pl.BlockSpec((WINDOW, ATTN_WIDTH), cur(P_Q // ATTN_WIDTH)),
            pl.BlockSpec((WINDOW, KV_WIDTH), prev(kcol)),
            pl.BlockSpec((WINDOW, KV_WIDTH), cur(kcol)),
            pl.BlockSpec((WINDOW, KV_WIDTH), prev(vcol)),
            pl.BlockSpec((WINDOW, KV_WIDTH), cur(vcol)),
        ],
        out_specs=pl.BlockSpec((WINDOW, ATTN_WIDTH), lambda b, i: (b * nb + i, 0)),
        compiler_params=pltpu.CompilerParams(dimension_semantics=("arbitrary", "arbitrary")),
        name="attn_prompt",
    )(sinks, proj, proj, proj, proj, proj)


def _attn_sample_kernel(sink_ref, q_ref, kn_ref, vn_ref, ck_ref, cv_ref, o_ref, nk_ref, nv_ref):
    t, w = q_ref.shape[0], WINDOW
    q = q_ref[...]
    kn, vn = kn_ref[...], vn_ref[...]
    ck, cv = ck_ref[0], cv_ref[0]
    nk_ref[0, 0:w - t, :] = ck[t:, :]
    nk_ref[0, w - t:w, :] = kn
    nv_ref[0, 0:w - t, :] = cv[t:, :]
    nv_ref[0, w - t:w, :] = vn
    ti = lax.broadcasted_iota(jnp.int32, (t, w), 0)
    cj = lax.broadcasted_iota(jnp.int32, (t, w), 1)
    dist_c = ti - cj + w
    valid_c = dist_c <= w
    ti2 = lax.broadcasted_iota(jnp.int32, (t, t), 0)
    tj2 = lax.broadcasted_iota(jnp.int32, (t, t), 1)
    dist_n = ti2 - tj2
    valid_n = dist_n >= 0
    dist_cf, dist_nf = dist_c.astype(F32), dist_n.astype(F32)
    ckb, cvb, knb, vnb = ck.astype(BF16), cv.astype(BF16), kn.astype(BF16), vn.astype(BF16)
    for h in range(N_KV_HEADS):
        hs = slice(h * HEAD_DIM, (h + 1) * HEAD_DIM)
        for g in range(GQA_GROUP):
            hq = h * GQA_GROUP + g
            qh = q[:, hq * HEAD_DIM:(hq + 1) * HEAD_DIM].astype(BF16)
            slope = _alibi_slope(hq)
            sc = _nt(qh, ckb[:, hs]) * (HEAD_DIM ** -0.5) - slope * dist_cf
            sc = jnp.where(valid_c, sc, -jnp.inf)
            sn = _nt(qh, knb[:, hs]) * (HEAD_DIM ** -0.5) - slope * dist_nf
            sn = jnp.where(valid_n, sn, -jnp.inf)
            o_ref[:, hq * HEAD_DIM:(hq + 1) * HEAD_DIM] = _softmax_pv(
                [(sc, cvb[:, hs]), (sn, vnb[:, hs])], sink_ref[hq])


def _attn_sample(proj, sinks, cache_k, cache_v, batch, seq):
    kcol, vcol = P_K // KV_WIDTH, P_V // KV_WIDTH
    win = jax.ShapeDtypeStruct((batch, WINDOW, KV_WIDTH), F32)
    return pl.pallas_call(
        _attn_sample_kernel,
        out_shape=(jax.ShapeDtypeStruct((batch * seq, ATTN_WIDTH), F32), win, win),
        grid=(batch,),
        in_specs=[
            pl.BlockSpec(memory_space=pltpu.SMEM),
            pl.BlockSpec((seq, ATTN_WIDTH), lambda b: (b, P_Q // ATTN_WIDTH)),
            pl.BlockSpec((seq, KV_WIDTH), lambda b: (b, kcol)),
            pl.BlockSpec((seq, KV_WIDTH), lambda b: (b, vcol)),
            pl.BlockSpec((1, WINDOW, KV_WIDTH), lambda b: (b, 0, 0)),
            pl.BlockSpec((1, WINDOW, KV_WIDTH), lambda b: (b, 0, 0)),
        ],
        out_specs=(
            pl.BlockSpec((seq, ATTN_WIDTH), lambda b: (b, 0)),
            pl.BlockSpec((1, WINDOW, KV_WIDTH), lambda b: (b, 0, 0)),
            pl.BlockSpec((1, WINDOW, KV_WIDTH), lambda b: (b, 0, 0)),
        ),
        compiler_params=pltpu.CompilerParams(dimension_semantics=("arbitrary",)),
        name="attn_sample",
    )(sinks, proj, proj, proj, cache_k, cache_v)


def _seg_sum(x, bones):
    hi = x.astype(BF16)
    lo = (x - hi.astype(F32)).astype(BF16)
    outs = []
    for j in range(R_WIDTH // SEG):
        sl = slice(j * SEG, (j + 1) * SEG)
        outs.append(_mm(hi[:, sl], bones) + _mm(lo[:, sl], bones))
    return jnp.concatenate(outs, axis=1)


def _wkv_kernel(pr_ref, pk_ref, pv_ref, pl_ref, prev_ref, s0_ref,
                mu_ref, mul_ref, w0_ref, a0_ref, kk_ref, ka_ref, rk_ref, lnw_ref, lnb_ref,
                wl_ref, bones_ref,
                y_ref, so_ref, s_ref, carry_ref, *, chunk, n_chunks):
    c = pl.program_id(1)
    cs = chunk
    gc = 2 * cs
    hd = R_HEAD

    @pl.when(c == 0)
    def _init():
        carry_ref[...] = prev_ref[0]
        s_ref[...] = jnp.zeros(s_ref.shape, F32)
        for p in range(N_PAIRS):
            s_ref[p, 0:hd, 0:hd] = s0_ref[0, 2 * p]
            s_ref[p, hd:2 * hd, hd:2 * hd] = s0_ref[0, 2 * p + 1]

    row = lax.broadcasted_iota(jnp.int32, (cs, 1), 0)

    def token_shift(x, prev_row, mu):
        shifted = jnp.where(row == 0, prev_row, pltpu.roll(x, 1, axis=0))
        return x + (shifted - x) * mu

    r_raw, k_raw, v_raw, l_raw = pr_ref[...], pk_ref[...], pv_ref[...], pl_ref[...]
    w = R_WIDTH
    xr = token_shift(r_raw, carry_ref[:, 0:w], mu_ref[:, 0:w])
    xk = token_shift(k_raw, carry_ref[:, w:2 * w], mu_ref[:, w:2 * w])
    xv = token_shift(v_raw, carry_ref[:, 2 * w:3 * w], mu_ref[:, 2 * w:3 * w])
    xl = token_shift(l_raw, carry_ref[:, 3 * w:], mul_ref[...])
    carry_ref[:, 0:w] = r_raw[cs - 1:cs, :]
    carry_ref[:, w:2 * w] = k_raw[cs - 1:cs, :]
    carry_ref[:, 2 * w:3 * w] = v_raw[cs - 1:cs, :]
    carry_ref[:, 3 * w:] = l_raw[cs - 1:cs, :]

    lane_l = lax.broadcasted_iota(jnp.int32, (1, LORA_PAD), 1)
    act = jnp.where(lane_l < DECAY_LORA, jnp.tanh(xl),
                    jnp.where(lane_l < DECAY_LORA + AAA_LORA, xl, jax.nn.sigmoid(xl)))
    up = _mm(act.astype(BF16), wl_ref[...])
    z = -(w0_ref[...] + up[:, 0:w])
    softplus = jnp.maximum(z, 0.0) + jnp.log1p(jnp.exp(-jnp.abs(z)))
    lwd = -jnp.exp(-softplus - 0.5)
    a = jax.nn.sigmoid(a0_ref[...] + up[:, w:2 * w])
    gate = up[:, 2 * w:3 * w]

    bones = bones_ref[...]
    kkn = xk * kk_ref[...]
    kk = kkn * lax.rsqrt(jnp.maximum(_seg_sum(kkn * kkn, bones), 1e-24))
    k2 = xk * (1.0 + (a - 1.0) * ka_ref[...])
    b = kk * a

    tri = (lax.broadcasted_iota(jnp.int32, (cs, cs), 0) >= lax.broadcasted_iota(jnp.int32, (cs, cs), 1))
    cum = jnp.dot(tri.astype(F32), lwd, precision=lax.Precision.HIGHEST, preferred_element_type=F32)
    cum_last = cum[cs - 1:cs, :]
    e_inv = jnp.exp(-cum)
    e_last = jnp.exp(cum_last - cum)
    kq = kk * jnp.exp(cum - lwd)
    rq = xr * jnp.exp(cum)
    kd = k2 * e_inv
    bd = b * e_inv
    kdp = k2 * e_last
    bdp = b * e_last
    p_last = jnp.exp(cum_last)

    ri = lax.broadcasted_iota(jnp.int32, (gc, gc), 0)
    ci = lax.broadcasted_iota(jnp.int32, (gc, gc), 1)
    same_head = (ri >= cs) == (ci >= cs)
    strict = same_head & (ci < ri)
    incl = same_head & (ci <= ri)
    eye = (ri == ci).astype(F32)
    head0 = lax.broadcasted_iota(jnp.int32, (1, PAIR), 1) < hd

    def stack(x):
        return jnp.concatenate([jnp.where(head0, x, 0.0), jnp.where(head0, 0.0, x)], axis=0).astype(BF16)

    ys = []
    for p in range(N_PAIRS):
        sl = slice(p * PAIR, (p + 1) * PAIR)
        kq_s, rq_s, kd_s, bd_s = stack(kq[:, sl]), stack(rq[:, sl]), stack(kd[:, sl]), stack(bd[:, sl])
        kdp_s, bdp_s, v_s = stack(kdp[:, sl]), stack(bdp[:, sl]), stack(xv[:, sl])
        s_old = s_ref[p]
        s_b = s_old.astype(BF16)
        a_kb = jnp.where(strict, _nt(kq_s, bd_s), 0.0)
        a_kk = jnp.where(strict, _nt(kq_s, kd_s), 0.0)
        a_rk = jnp.where(incl, _nt(rq_s, kd_s), 0.0)
        a_rb = jnp.where(incl, _nt(rq_s, bd_s), 0.0)
        inv = eye - a_kb
        apow = a_kb
        n = 1
        while 2 * n < cs:
            apow_b = apow.astype(BF16)
            apow = _mm(apow_b, apow_b)
            inv = inv + _mm(inv.astype(BF16), apow.astype(BF16))
            n *= 2
        rhs = _nt(kq_s, s_b) + _mm(a_kk.astype(BF16), v_s)
        u_b = _mm(inv.astype(BF16), rhs.astype(BF16)).astype(BF16)
        y_s = _nt(rq_s, s_b) + _mm(a_rk.astype(BF16), v_s) - _mm(a_rb.astype(BF16), u_b)
        ys.append(y_s[0:cs] + y_s[cs:gc])
        s_ref[p] = s_old * p_last[:, sl] + _tn(v_s, kdp_s) - _tn(u_b, bdp_s)

    y = jnp.concatenate(ys, axis=1)
    mean = _seg_sum(y, bones) * (1.0 / hd)
    d = y - mean
    var = _seg_sum(d * d, bones) * (1.0 / hd)
    yn = d * lax.rsqrt(var + GN_EPS) * lnw_ref[...] + lnb_ref[...]
    bonus = _seg_sum(xr * k2 * rk_ref[...], bones) * xv
    y_ref[...] = (yn + bonus) * gate

    @pl.when(c == n_chunks - 1)
    def _fin():
        for p in range(N_PAIRS):
            so_ref[0, 2 * p] = s_ref[p, 0:hd, 0:hd]
            so_ref[0, 2 * p + 1] = s_ref[p, hd:2 * hd, hd:2 * hd]


def _wkv(proj, prev0, s0, params, wl, bones, batch, seq, chunk):
    nc = seq // chunk
    rw = R_WIDTH

    def col(cb):
        return lambda b, c: (b * nc + c, cb)

    const2 = lambda b, c: (0, 0)
    vec = lambda n: pl.BlockSpec((1, n), const2)
    mu, mul, w0, a0, k_k, k_a, r_k, ln_w, ln_b = params
    return pl.pallas_call(
        functools.partial(_wkv_kernel, chunk=chunk, n_chunks=nc),
        out_shape=(jax.ShapeDtypeStruct((batch * seq, rw), F32),
                   jax.ShapeDtypeStruct((batch, R_HEADS, R_HEAD, R_HEAD), F32)),
        grid=(batch, nc),
        in_specs=[
            pl.BlockSpec((chunk, rw), col(P_R // rw)),
            pl.BlockSpec((chunk, rw), col(P_R // rw + 1)),
            pl.BlockSpec((chunk, rw), col(P_R // rw + 2)),
            pl.BlockSpec((chunk, LORA_PAD), col(P_L // LORA_PAD)),
            pl.BlockSpec((1, 1, SHIFT_COLS), lambda b, c: (b, 0, 0)),
            pl.BlockSpec((1, R_HEADS, R_HEAD, R_HEAD), lambda b, c: (b, 0, 0, 0)),
            vec(3 * rw), vec(LORA_PAD), vec(rw), vec(rw), vec(rw), vec(rw), vec(rw), vec(rw), vec(rw),
            pl.BlockSpec((LORA_PAD, 3 * rw), const2),
            pl.BlockSpec((SEG, SEG), const2),
        ],
        out_specs=(
            pl.BlockSpec((chunk, rw), lambda b, c: (b * nc + c, 0)),
            pl.BlockSpec((1, R_HEADS, R_HEAD, R_HEAD), lambda b, c: (b, 0, 0, 0)),
        ),
        scratch_shapes=[pltpu.VMEM((N_PAIRS, PAIR, PAIR), F32), pltpu.VMEM((1, SHIFT_COLS), F32)],
        compiler_params=pltpu.CompilerParams(
            dimension_semantics=("arbitrary", "arbitrary"), vmem_limit_bytes=VMEM_LIMIT),
        name="wkv",
    )(proj, proj, proj, proj, prev0, s0, mu, mul, w0, a0, k_k, k_a, r_k, ln_w, ln_b, wl, bones)


def _mix_kernel(x_ref, ya_ref, yr_ref, ga_ref, gr_ref, wba_ref, wbr_ref, wo_ref, o_ref):
    ta = _mm(ya_ref[...].astype(BF16), wba_ref[...])
    tr = _mm(yr_ref[...].astype(BF16), wbr_ref[...])
    mixed = jax.nn.sigmoid(ga_ref[...]) * ta + jax.nn.sigmoid(gr_ref[...]) * tr
    o_ref[...] = x_ref[...] + _mm(mixed.astype(BF16), wo_ref[...])


def _resident(shape):
    return pl.BlockSpec(shape, lambda *_: (0,) * len(shape), pipeline_mode=pl.Buffered(1))


def _mix(x, ya, yr, proj, wba, wbr, wo, tm):
    m = x.shape[0]
    row = lambda i: (i, 0)
    return pl.pallas_call(
        _mix_kernel,
        out_shape=jax.ShapeDtypeStruct((m, D_MODEL), F32),
        grid=(m // tm,),
        in_specs=[
            pl.BlockSpec((tm, D_MODEL), row),
            pl.BlockSpec((tm, ATTN_WIDTH), row),
            pl.BlockSpec((tm, R_WIDTH), row),
            pl.BlockSpec((tm, D_MODEL), lambda i: (i, P_GA // D_MODEL)),
            pl.BlockSpec((tm, D_MODEL), lambda i: (i, P_GR // D_MODEL)),
            _resident((ATTN_WIDTH, D_MODEL)),
            _resident((R_WIDTH, D_MODEL)),
            _resident((D_MODEL, D_MODEL)),
        ],
        out_specs=pl.BlockSpec((tm, D_MODEL), row),
        compiler_params=pltpu.CompilerParams(
            dimension_semantics=("arbitrary",), vmem_limit_bytes=VMEM_LIMIT),
        name="mix",
    )(x, ya, yr, proj, proj, wba, wbr, wo)


def _ffn_kernel(x_ref, g_ref, wu_ref, wd_ref, o_ref, h_ref):
    @pl.when(pl.program_id(1) == 0)
    def _():
        x = x_ref[...]
        h_ref[...] = _rms(x, g_ref[...]).astype(BF16)
        o_ref[...] = x

    u = _mm(h_ref[...], wu_ref[...])
    o_ref[...] += _mm(jnp.square(jnp.maximum(u, 0.0)).astype(BF16), wd_ref[...])


def _ffn(x, g, wu, wd, tm, tk):
    m = x.shape[0]
    return pl.pallas_call(
        _ffn_kernel,
        out_shape=jax.ShapeDtypeStruct((m, D_MODEL), F32),
        grid=(m // tm, D_FF // tk),
        in_specs=[
            pl.BlockSpec((tm, D_MODEL), lambda i, k: (i, 0)),
            pl.BlockSpec((1, D_MODEL), lambda i, k: (0, 0)),
            pl.BlockSpec((D_MODEL, tk), lambda i, k: (0, k)),
            pl.BlockSpec((tk, D_MODEL), lambda i, k: (k, 0)),
        ],
        out_specs=pl.BlockSpec((tm, D_MODEL), lambda i, k: (i, 0)),
        scratch_shapes=[pltpu.VMEM((tm, D_MODEL), BF16)],
        compiler_params=pltpu.CompilerParams(
            dimension_semantics=("arbitrary", "arbitrary"), vmem_limit_bytes=VMEM_LIMIT),
        name="ffn",
    )(x, g, wu, wd)


def _ple_kernel(x_ref, pe_ref, wg_ref, wp_ref, g_ref, o_ref):
    x = x_ref[...]
    gate = jax.nn.sigmoid(_mm(x.astype(BF16), wg_ref[...]))
    x = x + gate * _mm(pe_ref[...].astype(BF16), wp_ref[...])
    o_ref[...] = _rms(x, g_ref[...])


def _ple(x, pe, wg, wp, g, tm):
    m = x.shape[0]
    row = lambda i: (i, 0)
    return pl.pallas_call(
        _ple_kernel,
        out_shape=jax.ShapeDtypeStruct((m, D_MODEL), F32),
        grid=(m // tm,),
        in_specs=[
            pl.BlockSpec((tm, D_MODEL), row),
            pl.BlockSpec((tm, PLE_DIM), row),
            _resident((D_MODEL, D_MODEL)),
            _resident((PLE_DIM, D_MODEL)),
            pl.BlockSpec((1, D_MODEL), lambda i: (0, 0)),
        ],
        out_specs=pl.BlockSpec((tm, D_MODEL), row),
        compiler_params=pltpu.CompilerParams(
            dimension_semantics=("arbitrary",), vmem_limit_bytes=VMEM_LIMIT),
        name="ple",
    )(x, pe, wg, wp, g)


def _layer(x, pe, attend, prev0, s0, wts, batch, seq, tiles):
    (g_mix, w_in, wkv_params, wl, bones, wba, wbr, wo, g_ffn, wu, wd, wg, wp, g_fin) = wts
    proj = _proj(x, g_mix, w_in, tiles["proj_m"], tiles["proj_n"])
    ya = attend(proj)
    yr, s_new = _wkv(proj, prev0, s0, wkv_params, wl, bones, batch, seq, tiles["chunk"])
    x = _mix(x, ya, yr, proj, wba, wbr, wo, tiles["mix_m"])
    x = _ffn(x, g_ffn, wu, wd, tiles["ffn_m"], tiles["ffn_k"])
    y = _ple(x, pe, wg, wp, g_fin, tiles["ple_m"])
    return y, proj, s_new


def _shift_out(proj, batch, seq):
    last = proj.reshape(batch, seq, P_COLS)[:, -1]
    return jnp.concatenate([last[:, P_R:P_R + 3 * R_WIDTH], last[:, P_L:P_L + LORA_COLS]], axis=-1)[None]


def kernel(x_prompt, x_sample, cache_k_win, cache_v_win, state_wkv, state_shift, p_prompt, p_sample,
           norm_mix, w_in, attn_sinks, rwkv_mu, rwkv_w0, rwkv_w2, rwkv_a0, rwkv_a2, rwkv_g2,
           rwkv_k_k, rwkv_k_a, rwkv_r_k, rwkv_ln_w, rwkv_ln_b, w_branch_attn, w_branch_rwkv,
           w_out, norm_ffn, w_ff_up, w_ff_down, w_ple_proj, w_ple_gate, norm_final):
    assert w_in.shape[0] == 1, "single-layer step"
    bp, tp = x_prompt.shape[0], x_prompt.shape[1]
    bs, ts = x_sample.shape[0], x_sample.shape[1]
    rw = R_WIDTH

    wi = w_in[0]
    o_k, o_v, o_c = ATTN_WIDTH, ATTN_WIDTH + KV_WIDTH, ATTN_WIDTH + 2 * KV_WIDTH
    o_ga = o_c + RWKV_COLS
    w_in_p = jnp.concatenate([
        wi[:, 0:o_k], wi[:, o_c:o_c + 3 * rw], wi[:, o_ga:o_ga + 2 * D_MODEL],
        wi[:, o_k:o_v], wi[:, o_v:o_c], wi[:, o_c + 3 * rw:o_ga],
        jnp.zeros((D_MODEL, LORA_PAD - LORA_COLS), F32)], axis=1).astype(BF16)
    wl = jnp.zeros((LORA_PAD, 3 * rw), F32)
    wl = wl.at[0:DECAY_LORA, 0:rw].set(rwkv_w2[0])
    wl = wl.at[DECAY_LORA:DECAY_LORA + AAA_LORA, rw:2 * rw].set(rwkv_a2[0])
    wl = wl.at[DECAY_LORA + AAA_LORA:LORA_COLS, 2 * rw:3 * rw].set(rwkv_g2[0])
    wl = wl.astype(BF16)
    seg_id = np.arange(SEG) // R_HEAD
    bones = jnp.asarray(seg_id[:, None] == seg_id[None, :], BF16)
    mu = rwkv_mu[0]
    row = lambda v: v.reshape(1, -1)
    wkv_params = (row(mu[:3 * rw]), row(jnp.pad(mu[3 * rw:], (0, LORA_PAD - LORA_COLS))),
                  row(rwkv_w0[0]), row(rwkv_a0[0]), row(rwkv_k_k[0]), row(rwkv_k_a[0]),
                  row(rwkv_r_k[0]), row(rwkv_ln_w[0]), row(rwkv_ln_b[0]))
    wts = (row(norm_mix[0]), w_in_p, wkv_params, wl, bones,
           w_branch_attn[0].astype(BF16), w_branch_rwkv[0].astype(BF16), w_out[0].astype(BF16),
           row(norm_ffn[0]), w_ff_up[0].astype(BF16), w_ff_down[0].astype(BF16),
           w_ple_gate[0].astype(BF16), w_ple_proj[0].astype(BF16), row(norm_final))
    sinks = attn_sinks[0]

    tiles_p = dict(proj_m=1024, proj_n=1024, chunk=64, mix_m=256, ffn_m=1024, ffn_k=512, ple_m=512)
    yp, proj_p, s_p = _layer(
        x_prompt.reshape(bp * tp, D_MODEL), p_prompt[0].reshape(bp * tp, PLE_DIM),
        lambda pr: _attn_prompt(pr, sinks, bp, tp),
        jnp.zeros((bp, 1, SHIFT_COLS), F32), jnp.zeros((bp, R_HEADS, R_HEAD, R_HEAD), F32),
        wts, bp, tp, tiles_p)
    pp3 = proj_p.reshape(bp, tp, P_COLS)[:, -WINDOW:]
    k_p = pp3[:, :, P_K:P_K + KV_WIDTH].reshape(1, bp, WINDOW, N_KV_HEADS, HEAD_DIM)
    v_p = pp3[:, :, P_V:P_V + KV_WIDTH].reshape(1, bp, WINDOW, N_KV_HEADS, HEAD_DIM)

    ms = bs * ts
    tiles_s = dict(proj_m=ms, proj_n=1024, chunk=ts, mix_m=ms, ffn_m=ms, ffn_k=512, ple_m=ms)
    win_out = {}

    def attend_s(pr):
        ya, nk, nv = _attn_sample(pr, sinks, cache_k_win[0].reshape(bs, WINDOW, KV_WIDTH),
                                  cache_v_win[0].reshape(bs, WINDOW, KV_WIDTH), bs, ts)
        win_out["k"], win_out["v"] = nk, nv
        return ya

    prev_s = jnp.pad(state_shift[0], ((0, 0), (0, LORA_PAD - LORA_COLS))).reshape(bs, 1, SHIFT_COLS)
    ys, proj_s, s_s = _layer(
        x_sample.reshape(ms, D_MODEL), p_sample[0].reshape(ms, PLE_DIM), attend_s,
        prev_s, state_wkv[0], wts, bs, ts, tiles_s)

    return (yp.reshape(bp, tp, D_MODEL), ys.reshape(bs, ts, D_MODEL),
            k_p, v_p, s_p[None], _shift_out(proj_p, bp, tp),
            win_out["k"].reshape(1, bs, WINDOW, N_KV_HEADS, HEAD_DIM),
            win_out["v"].reshape(1, bs, WINDOW, N_KV_HEADS, HEAD_DIM),
            s_s[None], _shift_out(proj_s, bs, ts))
```

```python
import functools

import numpy as np
import jax
import jax.numpy as jnp
from jax import lax
from jax.experimental import pallas as pl
from jax.experimental.pallas import tpu as pltpu

F32 = jnp.float32
BF16 = jnp.bfloat16

D_MODEL = 2048
PLE_DIM = 256
HEAD_DIM = 64
N_Q_HEADS = 16
N_KV_HEADS = 4
GQA_GROUP = 4
ATTN_WIDTH = 1024
KV_WIDTH = 256
WINDOW = 128
ALIBI_MAX = 8.0
R_HEAD = 64
R_WIDTH = 1024
R_HEADS = 16
DECAY_LORA = 64
AAA_LORA = 64
GATE_LORA = 160
LORA_COLS = DECAY_LORA + AAA_LORA + GATE_LORA
LORA_PAD = 512
RWKV_COLS = 3 * R_WIDTH + LORA_COLS
D_FF = 4 * D_MODEL
NORM_EPS = 1e-6
GN_EPS = 64e-5

P_Q = 0
P_R = 1024
P_GA = 4096
P_GR = 6144
P_K = 8192
P_V = 8448
P_L = 8704
P_COLS = 9216
SHIFT_COLS = 3 * R_WIDTH + LORA_PAD

PAIR = 128
N_PAIRS = R_WIDTH // PAIR
SEG = 256

VMEM_LIMIT = 56 * 1024 * 1024


def _mm(a, b):
    return jnp.dot(a, b, preferred_element_type=F32)


def _nt(a, b):
    return lax.dot_general(a, b, (((1,), (1,)), ((), ())), preferred_element_type=F32)


def _tn(a, b):
    return lax.dot_general(a, b, (((0,), (0,)), ((), ())), preferred_element_type=F32)


def _rms(x, g):
    ms = jnp.mean(x * x, axis=-1, keepdims=True)
    return x * lax.rsqrt(ms + NORM_EPS) * g


def _alibi_slope(hq):
    return float(2.0 ** (-ALIBI_MAX * (hq + 1) / N_Q_HEADS))


def _proj_kernel(x_ref, g_ref, w_ref, o_ref, h_ref):
    @pl.when(pl.program_id(1) == 0)
    def _():
        h_ref[...] = _rms(x_ref[...], g_ref[...]).astype(BF16)

    o_ref[...] = _mm(h_ref[...], w_ref[...])


def _proj(x, g, w, tm, tn):
    m = x.shape[0]
    return pl.pallas_call(
        _proj_kernel,
        out_shape=jax.ShapeDtypeStruct((m, P_COLS), F32),
        grid=(m // tm, P_COLS // tn),
        in_specs=[
            pl.BlockSpec((tm, D_MODEL), lambda i, j: (i, 0)),
            pl.BlockSpec((1, D_MODEL), lambda i, j: (0, 0)),
            pl.BlockSpec((D_MODEL, tn), lambda i, j: (0, j)),
        ],
        out_specs=pl.BlockSpec((tm, tn), lambda i, j: (i, j)),
        scratch_shapes=[pltpu.VMEM((tm, D_MODEL), BF16)],
        compiler_params=pltpu.CompilerParams(
            dimension_semantics=("arbitrary", "arbitrary"), vmem_limit_bytes=VMEM_LIMIT),
        name="proj",
    )(x, g, w)


def _attend_heads(scores, values, sinks):
    heads = range(len(scores))
    m = []
    for h in heads:
        mh = sinks[h]
        for s in scores[h]:
            mh = jnp.maximum(mh, jnp.max(s, axis=-1, keepdims=True))
        m.append(mh)
    ps = [[jnp.exp(s - m[h]) for s in scores[h]] for h in heads]
    den = []
    for h in heads:
        dh = jnp.exp(sinks[h] - m[h])
        for p in ps[h]:
            dh = dh + jnp.sum(p, axis=-1, keepdims=True)
        den.append(dh)
    outs = []
    for h in heads:
        o = None
        for p, v in zip(ps[h], values[h]):
            t = _mm(p.astype(BF16), v)
            o = t if o is None else o + t
        outs.append(o * (1.0 / den[h]))
    return outs


def _head_slices():
    q_sl = [slice(hq * HEAD_DIM, (hq + 1) * HEAD_DIM) for hq in range(N_Q_HEADS)]
    kv_sl = [slice((hq // GQA_GROUP) * HEAD_DIM, (hq // GQA_GROUP + 1) * HEAD_DIM) for hq in range(N_Q_HEADS)]
    return q_sl, kv_sl


def _alibi_bias(dist, valid):
    slopes = np.array([_alibi_slope(hq) for hq in range(N_Q_HEADS)], np.float32)
    return np.where(valid[None], -slopes[:, None, None] * dist[None].astype(np.float32), -np.inf).astype(np.float32)


def _attn_prompt_kernel(sink_ref, bias_ref, q_ref, kp_ref, kc_ref, vp_ref, vc_ref, o_ref):
    w = WINDOW
    kj = lax.broadcasted_iota(jnp.int32, (w, 2 * w), 1)
    no_prev = (kj < w) & (pl.program_id(1) == 0)
    q = q_ref[...] * (HEAD_DIM ** -0.5)
    k2 = jnp.concatenate([kp_ref[...], kc_ref[...]], axis=0).astype(BF16)
    v2 = jnp.concatenate([vp_ref[...], vc_ref[...]], axis=0).astype(BF16)
    hqs = range(N_Q_HEADS)
    q_sl, kv_sl = _head_slices()
    qs = [q[:, q_sl[hq]].astype(BF16) for hq in hqs]
    sc = [[jnp.where(no_prev, -jnp.inf, _nt(qs[hq], k2[:, kv_sl[hq]]) + bias_ref[hq])] for hq in hqs]
    outs = _attend_heads(sc, [[v2[:, kv_sl[hq]]] for hq in hqs], [sink_ref[hq] for hq in hqs])
    for hq in hqs:
        o_ref[:, q_sl[hq]] = outs[hq]


def _attn_prompt(proj, sinks, batch, seq):
    nb = seq // WINDOW
    kcol, vcol = P_K // KV_WIDTH, P_V // KV_WIDTH

    def cur(c):
        return lambda b, i: (b * nb + i, c)

    def prev(c):
        return lambda b, i: (b * nb + jnp.maximum(i - 1, 0), c)

    ti = np.arange(WINDOW)[:, None]
    kj = np.arange(2 * WINDOW)[None, :]
    dist = ti - kj + WINDOW
    bias = _alibi_bias(dist, (dist >= 0) & (dist <= WINDOW))
    return pl.pallas_call(
        _attn_prompt_kernel,
        out_shape=jax.ShapeDtypeStruct((batch * seq, ATTN_WIDTH), F32),
        grid=(batch, nb),
        in_specs=[
            pl.BlockSpec(memory_space=pltpu.SMEM),
            _resident((N_Q_HEADS, WINDOW, 2 * WINDOW)),
            pl.BlockSpec((WINDOW, ATTN_WIDTH), cur(P_Q // ATTN_WIDTH)),
            pl.BlockSpec((WINDOW, KV_WIDTH), prev(kcol)),
            pl.BlockSpec((WINDOW, KV_WIDTH), cur(kcol)),
            pl.BlockSpec((WINDOW, KV_WIDTH), prev(vcol)),
            pl.BlockSpec((WINDOW, KV_WIDTH), cur(vcol)),
        ],
        out_specs=pl.BlockSpec((WINDOW, ATTN_WIDTH), lambda b, i: (b * nb + i, 0)),
        compiler_params=pltpu.CompilerParams(dimension_semantics=("arbitrary", "arbitrary")),
        name="attn_prompt",
    )(sinks, jnp.asarray(bias), proj, proj, proj, proj, proj)


def _attn_sample_kernel(sink_ref, q_ref, kn_ref, vn_ref, ck_ref, cv_ref, o_ref, nk_ref, nv_ref):
    t, w = q_ref.shape[0], WINDOW
    kn, vn = kn_ref[...], vn_ref[...]
    ck, cv = ck_ref[0], cv_ref[0]
    nk_ref[0, 0:w - t, :] = ck[t:, :]
    nk_ref[0, w - t:w, :] = kn
    nv_ref[0, 0:w - t, :] = cv[t:, :]
    nv_ref[0, w - t:w, :] = vn
    ti = lax.broadcasted_iota(jnp.int32, (t, w), 0)
    cj = lax.broadcasted_iota(jnp.int32, (t, w), 1)
    dist_c = (ti - cj + w).astype(F32)
    valid_c = cj >= ti
    ti2 = lax.broadcasted_iota(jnp.int32, (t, t), 0)
    tj2 = lax.broadcasted_iota(jnp.int32, (t, t), 1)
    dist_n = (ti2 - tj2).astype(F32)
    valid_n = tj2 <= ti2
    q = q_ref[...] * (HEAD_DIM ** -0.5)
    ckb, cvb, knb, vnb = ck.astype(BF16), cv.astype(BF16), kn.astype(BF16), vn.astype(BF16)
    hqs = range(N_Q_HEADS)
    q_sl, kv_sl = _head_slices()
    qs = [q[:, q_sl[hq]].astype(BF16) for hq in hqs]
    scores = [[_nt(qs[hq], ckb[:, kv_sl[hq]]) + jnp.where(valid_c, -_alibi_slope(hq) * dist_c, -jnp.inf),
               _nt(qs[hq], knb[:, kv_sl[hq]]) + jnp.where(valid_n, -_alibi_slope(hq) * dist_n, -jnp.inf)]
              for hq in hqs]
    outs = _attend_heads(scores, [[cvb[:, kv_sl[hq]], vnb[:, kv_sl[hq]]] for hq in hqs],
                         [sink_ref[hq] for hq in hqs])
    for hq in hqs:
        o_ref[:, q_sl[hq]] = outs[hq]


def _attn_sample(proj, sinks, cache_k, cache_v, batch, seq):
    kcol, vcol = P_K // KV_WIDTH, P_V // KV_WIDTH
    win = jax.ShapeDtypeStruct((batch, WINDOW, KV_WIDTH), F32)
    return pl.pallas_call(
        _attn_sample_kernel,
        out_shape=(jax.ShapeDtypeStruct((batch * seq, ATTN_WIDTH), F32), win, win),
        grid=(batch,),
        in_specs=[
            pl.BlockSpec(memory_space=pltpu.SMEM),
            pl.BlockSpec((seq, ATTN_WIDTH), lambda b: (b, P_Q // ATTN_WIDTH)),
            pl.BlockSpec((seq, KV_WIDTH), lambda b: (b, kcol)),
            pl.BlockSpec((seq, KV_WIDTH), lambda b: (b, vcol)),
            pl.BlockSpec((1, WINDOW, KV_WIDTH), lambda b: (b, 0, 0)),
            pl.BlockSpec((1, WINDOW, KV_WIDTH), lambda b: (b, 0, 0)),
        ],
        out_specs=(
            pl.BlockSpec((seq, ATTN_WIDTH), lambda b: (b, 0)),
            pl.BlockSpec((1, WINDOW, KV_WIDTH), lambda b: (b, 0, 0)),
            pl.BlockSpec((1, WINDOW, KV_WIDTH), lambda b: (b, 0, 0)),
        ),
        compiler_params=pltpu.CompilerParams(dimension_semantics=("arbitrary",)),
        name="attn_sample",
    )(sinks, proj, proj, proj, cache_k, cache_v)


def _seg_sum(x, bones):
    rows = x.shape[0]
    hi = x.astype(BF16).astype(F32)
    lo = x - hi
    groups = [slice(j * SEG, (j + 1) * SEG) for j in range(R_WIDTH // SEG)]
    lhs = jnp.concatenate([t[:, sl] for sl in groups for t in (hi, lo)], axis=0).astype(BF16)
    out = _mm(lhs, bones)
    return jnp.concatenate(
        [out[2 * j * rows:(2 * j + 1) * rows] + out[(2 * j + 1) * rows:(2 * j + 2) * rows]
         for j in range(len(groups))], axis=1)


def _wkv_kernel(pr_ref, pk_ref, pv_ref, pl_ref, prev_ref, s0_ref,
                mu_ref, mul_ref, w0_ref, a0_ref, kk_ref, ka_ref, rk_ref, lnw_ref, lnb_ref,
                wl_ref, bones_ref,
                y_ref, so_ref, s_ref, carry_ref, *, chunk, n_chunks):
    c = pl.program_id(1)
    cs = chunk
    gc = 2 * cs
    hd = R_HEAD

    @pl.when(c == 0)
    def _init():
        carry_ref[...] = prev_ref[0]
        s_ref[...] = jnp.zeros(s_ref.shape, F32)
        for p in range(N_PAIRS):
            s_ref[p, 0:hd, 0:hd] = s0_ref[0, 2 * p]
            s_ref[p, hd:2 * hd, hd:2 * hd] = s0_ref[0, 2 * p + 1]

    row = lax.broadcasted_iota(jnp.int32, (cs, 1), 0)

    def token_shift(x, prev_row, mu):
        shifted = jnp.where(row == 0, prev_row, pltpu.roll(x, 1, axis=0))
        return x + (shifted - x) * mu

    r_raw, k_raw, v_raw, l_raw = pr_ref[...], pk_ref[...], pv_ref[...], pl_ref[...]
    w = R_WIDTH
    xr = token_shift(r_raw, carry_ref[:, 0:w], mu_ref[:, 0:w])
    xk = token_shift(k_raw, carry_ref[:, w:2 * w], mu_ref[:, w:2 * w])
    xv = token_shift(v_raw, carry_ref[:, 2 * w:3 * w], mu_ref[:, 2 * w:3 * w])
    xl = token_shift(l_raw, carry_ref[:, 3 * w:], mul_ref[...])
    carry_ref[:, 0:w] = r_raw[cs - 1:cs, :]
    carry_ref[:, w:2 * w] = k_raw[cs - 1:cs, :]
    carry_ref[:, 2 * w:3 * w] = v_raw[cs - 1:cs, :]
    carry_ref[:, 3 * w:] = l_raw[cs - 1:cs, :]

    lane_l = lax.broadcasted_iota(jnp.int32, (1, LORA_PAD), 1)
    act = jnp.where(lane_l < DECAY_LORA, jnp.tanh(xl),
                    jnp.where(lane_l < DECAY_LORA + AAA_LORA, xl, jax.nn.sigmoid(xl)))
    up = _mm(act.astype(BF16), wl_ref[...])
    z = -(w0_ref[...] + up[:, 0:w])
    softplus = jnp.maximum(z, 0.0) + jnp.log1p(jnp.exp(-jnp.abs(z)))
    lwd = -jnp.exp(-softplus - 0.5)
    a = jax.nn.sigmoid(a0_ref[...] + up[:, w:2 * w])
    gate = up[:, 2 * w:3 * w]

    bones = bones_ref[...]
    kkn = xk * kk_ref[...]
    kk = kkn * lax.rsqrt(jnp.maximum(_seg_sum(kkn * kkn, bones), 1e-24))
    k2 = xk * (1.0 + (a - 1.0) * ka_ref[...])
    b = kk * a

    tri = (lax.broadcasted_iota(jnp.int32, (cs, cs), 0) >= lax.broadcasted_iota(jnp.int32, (cs, cs), 1))
    cum = jnp.dot(tri.astype(F32), lwd, precision=lax.Precision.HIGHEST, preferred_element_type=F32)
    cum_last = cum[cs - 1:cs, :]
    e_inv = jnp.exp(-cum)
    e_last = jnp.exp(cum_last - cum)
    kq = kk * jnp.exp(cum - lwd)
    rq = xr * jnp.exp(cum)
    kd = k2 * e_inv
    bd = b * e_inv
    kdp = k2 * e_last
    bdp = b * e_last
    p_last = jnp.exp(cum_last)

    ri = lax.broadcasted_iota(jnp.int32, (gc, gc), 0)
    ci = lax.broadcasted_iota(jnp.int32, (gc, gc), 1)
    same_head = (ri >= cs) == (ci >= cs)
    strict = same_head & (ci < ri)
    incl = same_head & (ci <= ri)
    eye = (ri == ci).astype(F32)
    head0 = lax.broadcasted_iota(jnp.int32, (1, PAIR), 1) < hd

    def stack(x):
        return jnp.concatenate([jnp.where(head0, x, 0.0), jnp.where(head0, 0.0, x)], axis=0).astype(BF16)

    pairs = range(N_PAIRS)
    lanes = [slice(p * PAIR, (p + 1) * PAIR) for p in pairs]
    kq_s = [stack(kq[:, sl]) for sl in lanes]
    bd_s = [stack(bd[:, sl]) for sl in lanes]
    kd_s = [stack(kd[:, sl]) for sl in lanes]
    rq_s = [stack(rq[:, sl]) for sl in lanes]
    v_s = [stack(xv[:, sl]) for sl in lanes]
    s_old = [s_ref[p] for p in pairs]
    s_b = [s.astype(BF16) for s in s_old]
    probes = [jnp.concatenate([kq_s[p], rq_s[p]], axis=0) for p in pairs]
    state_t = [_nt(probes[p], s_b[p]) for p in pairs]
    if gc % PAIR == 0:
        a_all = [_nt(probes[p], jnp.concatenate([bd_s[p], kd_s[p]], axis=0)) for p in pairs]
        a_kb = [jnp.where(strict, a[0:gc, 0:gc], 0.0) for a in a_all]
        a_kk = [jnp.where(strict, a[0:gc, gc:2 * gc], 0.0).astype(BF16) for a in a_all]
        a_rb = [jnp.where(incl, a[gc:2 * gc, 0:gc], 0.0).astype(BF16) for a in a_all]
        a_rk = [jnp.where(incl, a[gc:2 * gc, gc:2 * gc], 0.0).astype(BF16) for a in a_all]
    else:
        a_kb = [jnp.where(strict, _nt(kq_s[p], bd_s[p]), 0.0) for p in pairs]
        a_kk = [jnp.where(strict, _nt(kq_s[p], kd_s[p]), 0.0).astype(BF16) for p in pairs]
        a_rb = [jnp.where(incl, _nt(rq_s[p], bd_s[p]), 0.0).astype(BF16) for p in pairs]
        a_rk = [jnp.where(incl, _nt(rq_s[p], kd_s[p]), 0.0).astype(BF16) for p in pairs]
    inv = [eye - a for a in a_kb]
    squarings = cs.bit_length() - 2
    if squarings >= 1:
        apow_b = [a.astype(BF16) for a in a_kb]
        apow_b = [_mm(a, a).astype(BF16) for a in apow_b]
        for _ in range(squarings - 1):
            both = [_mm(jnp.concatenate([apow_b[p], inv[p].astype(BF16)], axis=0), apow_b[p]) for p in pairs]
            apow_b = [t[0:gc].astype(BF16) for t in both]
            inv = [inv[p] + both[p][gc:2 * gc] for p in pairs]
        inv = [inv[p] + _mm(inv[p].astype(BF16), apow_b[p]) for p in pairs]
    rhs = [state_t[p][0:gc] + _mm(a_kk[p], v_s[p]) for p in pairs]
    u_b = [_mm(inv[p].astype(BF16), rhs[p].astype(BF16)).astype(BF16) for p in pairs]
    if gc % PAIR == 0:
        y_s = [state_t[p][gc:2 * gc] + _mm(jnp.concatenate([a_rk[p], -a_rb[p]], axis=1),
                                           jnp.concatenate([v_s[p], u_b[p]], axis=0)) for p in pairs]
    else:
        y_s = [state_t[p][gc:2 * gc] + _mm(a_rk[p], v_s[p]) - _mm(a_rb[p], u_b[p]) for p in pairs]
    kdp_s = [stack(kdp[:, sl]) for sl in lanes]
    bdp_s = [stack(bdp[:, sl]) for sl in lanes]
    for p in pairs:
        s_ref[p] = s_old[p] * p_last[:, lanes[p]] + _tn(
            jnp.concatenate([v_s[p], -u_b[p]], axis=0), jnp.concatenate([kdp_s[p], bdp_s[p]], axis=0))

    y = jnp.concatenate([ys[0:cs] + ys[cs:gc] for ys in y_s], axis=1)
    mean = _seg_sum(y, bones) * (1.0 / hd)
    d = y - mean
    var = _seg_sum(d * d, bones) * (1.0 / hd)
    yn = d * lax.rsqrt(var + GN_EPS) * lnw_ref[...] + lnb_ref[...]
    bonus = _seg_sum(xr * k2 * rk_ref[...], bones) * xv
    y_ref[...] = (yn + bonus) * gate

    @pl.when(c == n_chunks - 1)
    def _fin():
        for p in range(N_PAIRS):
            so_ref[0, 2 * p] = s_ref[p, 0:hd, 0:hd]
            so_ref[0, 2 * p + 1] = s_ref[p, hd:2 * hd, hd:2 * hd]


def _wkv(proj, prev0, s0, params, wl, bones, batch, seq, chunk):
    nc = seq // chunk
    rw = R_WIDTH

    def col(cb):
        return lambda b, c: (b * nc + c, cb)

    const2 = lambda b, c: (0, 0)
    vec = lambda n: pl.BlockSpec((1, n), const2)
    mu, mul, w0, a0, k_k, k_a, r_k, ln_w, ln_b = params
    return pl.pallas_call(
        functools.partial(_wkv_kernel, chunk=chunk, n_chunks=nc),
        out_shape=(jax.ShapeDtypeStruct((batch * seq, rw), F32),
                   jax.ShapeDtypeStruct((batch, R_HEADS, R_HEAD, R_HEAD), F32)),
        grid=(batch, nc),
        in_specs=[
            pl.BlockSpec((chunk, rw), col(P_R // rw)),
            pl.BlockSpec((chunk, rw), col(P_R // rw + 1)),
            pl.BlockSpec((chunk, rw), col(P_R // rw + 2)),
            pl.BlockSpec((chunk, LORA_PAD), col(P_L // LORA_PAD)),
            pl.BlockSpec((1, 1, SHIFT_COLS), lambda b, c: (b, 0, 0)),
            pl.BlockSpec((1, R_HEADS, R_HEAD, R_HEAD), lambda b, c: (b, 0, 0, 0)),
            vec(3 * rw), vec(LORA_PAD), vec(rw), vec(rw), vec(rw), vec(rw), vec(rw), vec(rw), vec(rw),
            pl.BlockSpec((LORA_PAD, 3 * rw), const2),
            pl.BlockSpec((SEG, SEG), const2),
        ],
        out_specs=(
            pl.BlockSpec((chunk, rw), lambda b, c: (b * nc + c, 0)),
            pl.BlockSpec((1, R_HEADS, R_HEAD, R_HEAD), lambda b, c: (b, 0, 0, 0)),
        ),
        scratch_shapes=[pltpu.VMEM((N_PAIRS, PAIR, PAIR), F32), pltpu.VMEM((1, SHIFT_COLS), F32)],
        compiler_params=pltpu.CompilerParams(
            dimension_semantics=("arbitrary", "arbitrary"), vmem_limit_bytes=VMEM_LIMIT),
        name="wkv",
    )(proj, proj, proj, proj, prev0, s0, mu, mul, w0, a0, k_k, k_a, r_k, ln_w, ln_b, wl, bones)


def _mix_kernel(x_ref, ya_ref, yr_ref, ga_ref, gr_ref, wba_ref, wbr_ref, wo_ref, o_ref):
    ta = _mm(ya_ref[...].astype(BF16), wba_ref[...])
    tr = _mm(yr_ref[...].astype(BF16), wbr_ref[...])
    mixed = jax.nn.sigmoid(ga_ref[...]) * ta + jax.nn.sigmoid(gr_ref[...]) * tr
    o_ref[...] = x_ref[...] + _mm(mixed.astype(BF16), wo_ref[...])


def _resident(shape):
    return pl.BlockSpec(shape, lambda *_: (0,) * len(shape), pipeline_mode=pl.Buffered(1))


def _mix(x, ya, yr, proj, wba, wbr, wo, tm):
    m = x.shape[0]
    row = lambda i: (i, 0)
    return pl.pallas_call(
        _mix_kernel,
        out_shape=jax.ShapeDtypeStruct((m, D_MODEL), F32),
        grid=(m // tm,),
        in_specs=[
            pl.BlockSpec((tm, D_MODEL), row),
            pl.BlockSpec((tm, ATTN_WIDTH), row),
            pl.BlockSpec((tm, R_WIDTH), row),
            pl.BlockSpec((tm, D_MODEL), lambda i: (i, P_GA // D_MODEL)),
            pl.BlockSpec((tm, D_MODEL), lambda i: (i, P_GR // D_MODEL)),
            _resident((ATTN_WIDTH, D_MODEL)),
            _resident((R_WIDTH, D_MODEL)),
            _resident((D_MODEL, D_MODEL)),
        ],
        out_specs=pl.BlockSpec((tm, D_MODEL), row),
        compiler_params=pltpu.CompilerParams(
            dimension_semantics=("arbitrary",), vmem_limit_bytes=VMEM_LIMIT),
        name="mix",
    )(x, ya, yr, proj, proj, wba, wbr, wo)


def _ffn_kernel(x_ref, g_ref, wu_ref, wd_ref, o_ref, h_ref):
    @pl.when(pl.program_id(1) == 0)
    def _():
        x = x_ref[...]
        h_ref[...] = _rms(x, g_ref[...]).astype(BF16)
        o_ref[...] = x

    u = _mm(h_ref[...], wu_ref[...])
    o_ref[...] += _mm(jnp.square(jnp.maximum(u, 0.0)).astype(BF16), wd_ref[...])


def _ffn(x, g, wu, wd, tm, tk):
    m = x.shape[0]
    return pl.pallas_call(
        _ffn_kernel,
        out_shape=jax.ShapeDtypeStruct((m, D_MODEL), F32),
        grid=(m // tm, D_FF // tk),
        in_specs=[
            pl.BlockSpec((tm, D_MODEL), lambda i, k: (i, 0)),
            pl.BlockSpec((1, D_MODEL), lambda i, k: (0, 0)),
            pl.BlockSpec((D_MODEL, tk), lambda i, k: (0, k)),
            pl.BlockSpec((tk, D_MODEL), lambda i, k: (k, 0)),
        ],
        out_specs=pl.BlockSpec((tm, D_MODEL), lambda i, k: (i, 0)),
        scratch_shapes=[pltpu.VMEM((tm, D_MODEL), BF16)],
        compiler_params=pltpu.CompilerParams(
            dimension_semantics=("arbitrary", "arbitrary"), vmem_limit_bytes=VMEM_LIMIT),
        name="ffn",
    )(x, g, wu, wd)


def _ple_kernel(x_ref, pe_ref, wg_ref, wp_ref, g_ref, o_ref):
    x = x_ref[...]
    gate = jax.nn.sigmoid(_mm(x.astype(BF16), wg_ref[...]))
    x = x + gate * _mm(pe_ref[...].astype(BF16), wp_ref[...])
    o_ref[...] = _rms(x, g_ref[...])


def _ple(x, pe, wg, wp, g, tm):
    m = x.shape[0]
    row = lambda i: (i, 0)
    return pl.pallas_call(
        _ple_kernel,
        out_shape=jax.ShapeDtypeStruct((m, D_MODEL), F32),
        grid=(m // tm,),
        in_specs=[
            pl.BlockSpec((tm, D_MODEL), row),
            pl.BlockSpec((tm, PLE_DIM), row),
            _resident((D_MODEL, D_MODEL)),
            _resident((PLE_DIM, D_MODEL)),
            pl.BlockSpec((1, D_MODEL), lambda i: (0, 0)),
        ],
        out_specs=pl.BlockSpec((tm, D_MODEL), row),
        compiler_params=pltpu.CompilerParams(
            dimension_semantics=("arbitrary",), vmem_limit_bytes=VMEM_LIMIT),
        name="ple",
    )(x, pe, wg, wp, g)


def _layer(x, pe, attend, prev0, s0, wts, batch, seq, tiles):
    (g_mix, w_in, wkv_params, wl, bones, wba, wbr, wo, g_ffn, wu, wd, wg, wp, g_fin) = wts
    proj = _proj(x, g_mix, w_in, tiles["proj_m"], tiles["proj_n"])
    ya = attend(proj)
    yr, s_new = _wkv(proj, prev0, s0, wkv_params, wl, bones, batch, seq, tiles["chunk"])
    x = _mix(x, ya, yr, proj, wba, wbr, wo, tiles["mix_m"])
    x = _ffn(x, g_ffn, wu, wd, tiles["ffn_m"], tiles["ffn_k"])
    y = _ple(x, pe, wg, wp, g_fin, tiles["ple_m"])
    return y, proj, s_new


def _shift_out(proj, batch, seq):
    last = proj.reshape(batch, seq, P_COLS)[:, -1]
    return jnp.concatenate([last[:, P_R:P_R + 3 * R_WIDTH], last[:, P_L:P_L + LORA_COLS]], axis=-1)[None]


def kernel(x_prompt, x_sample, cache_k_win, cache_v_win, state_wkv, state_shift, p_prompt, p_sample,
           norm_mix, w_in, attn_sinks, rwkv_mu, rwkv_w0, rwkv_w2, rwkv_a0, rwkv_a2, rwkv_g2,
           rwkv_k_k, rwkv_k_a, rwkv_r_k, rwkv_ln_w, rwkv_ln_b, w_branch_attn, w_branch_rwkv,
           w_out, norm_ffn, w_ff_up, w_ff_down, w_ple_proj, w_ple_gate, norm_final):
    assert w_in.shape[0] == 1, "single-layer step"
    bp, tp = x_prompt.shape[0], x_prompt.shape[1]
    bs, ts = x_sample.shape[0], x_sample.shape[1]
    rw = R_WIDTH

    wi = w_in[0]
    o_k, o_v, o_c = ATTN_WIDTH, ATTN_WIDTH + KV_WIDTH, ATTN_WIDTH + 2 * KV_WIDTH
    o_ga = o_c + RWKV_COLS
    w_in_p = jnp.concatenate([
        wi[:, 0:o_k], wi[:, o_c:o_c + 3 * rw], wi[:, o_ga:o_ga + 2 * D_MODEL],
        wi[:, o_k:o_v], wi[:, o_v:o_c], wi[:, o_c + 3 * rw:o_ga],
        jnp.zeros((D_MODEL, LORA_PAD - LORA_COLS), F32)], axis=1).astype(BF16)
    wl = jnp.zeros((LORA_PAD, 3 * rw), F32)
    wl = wl.at[0:DECAY_LORA, 0:rw].set(rwkv_w2[0])
    wl = wl.at[DECAY_LORA:DECAY_LORA + AAA_LORA, rw:2 * rw].set(rwkv_a2[0])
    wl = wl.at[DECAY_LORA + AAA_LORA:LORA_COLS, 2 * rw:3 * rw].set(rwkv_g2[0])
    wl = wl.astype(BF16)
    seg_id = np.arange(SEG) // R_HEAD
    bones = jnp.asarray(seg_id[:, None] == seg_id[None, :], BF16)
    mu = rwkv_mu[0]
    row = lambda v: v.reshape(1, -1)
    wkv_params = (row(mu[:3 * rw]), row(jnp.pad(mu[3 * rw:], (0, LORA_PAD - LORA_COLS))),
                  row(rwkv_w0[0]), row(rwkv_a0[0]), row(rwkv_k_k[0]), row(rwkv_k_a[0]),
                  row(rwkv_r_k[0]), row(rwkv_ln_w[0]), row(rwkv_ln_b[0]))
    wts = (row(norm_mix[0]), w_in_p, wkv_params, wl, bones,
           w_branch_attn[0].astype(BF16), w_branch_rwkv[0].astype(BF16), w_out[0].astype(BF16),
           row(norm_ffn[0]), w_ff_up[0].astype(BF16), w_ff_down[0].astype(BF16),
           w_ple_gate[0].astype(BF16), w_ple_proj[0].astype(BF16), row(norm_final))
    sinks = attn_sinks[0]

    tiles_p = dict(proj_m=1024, proj_n=1024, chunk=64, mix_m=256, ffn_m=1024, ffn_k=512, ple_m=512)
    yp, proj_p, s_p = _layer(
        x_prompt.reshape(bp * tp, D_MODEL), p_prompt[0].reshape(bp * tp, PLE_DIM),
        lambda pr: _attn_prompt(pr, sinks, bp, tp),
        jnp.zeros((bp, 1, SHIFT_COLS), F32), jnp.zeros((bp, R_HEADS, R_HEAD, R_HEAD), F32),
        wts, bp, tp, tiles_p)
    pp3 = proj_p.reshape(bp, tp, P_COLS)[:, -WINDOW:]
    k_p = pp3[:, :, P_K:P_K + KV_WIDTH].reshape(1, bp, WINDOW, N_KV_HEADS, HEAD_DIM)
    v_p = pp3[:, :, P_V:P_V + KV_WIDTH].reshape(1, bp, WINDOW, N_KV_HEADS, HEAD_DIM)

    ms = bs * ts
    tiles_s = dict(proj_m=ms, proj_n=1024, chunk=ts, mix_m=ms, ffn_m=ms, ffn_k=512, ple_m=ms)
    win_out = {}

    def attend_s(pr):
        ya, nk, nv = _attn_sample(pr, sinks, cache_k_win[0].reshape(bs, WINDOW, KV_WIDTH),
                                  cache_v_win[0].reshape(bs, WINDOW, KV_WIDTH), bs, ts)
        win_out["k"], win_out["v"] = nk, nv
        return ya

    prev_s = jnp.pad(state_shift[0], ((0, 0), (0, LORA_PAD - LORA_COLS))).reshape(bs, 1, SHIFT_COLS)
    ys, proj_s, s_s = _layer(
        x_sample.reshape(ms, D_MODEL), p_sample[0].reshape(ms, PLE_DIM), attend_s,
        prev_s, state_wkv[0], wts, bs, ts, tiles_s)

    return (yp.reshape(bp, tp, D_MODEL), ys.reshape(bs, ts, D_MODEL),
            k_p, v_p, s_p[None], _shift_out(proj_p, bp, tp),
            win_out["k"].reshape(1, bs, WINDOW, N_KV_HEADS, HEAD_DIM),
            win_out["v"].reshape(1, bs, WINDOW, N_KV_HEADS, HEAD_DIM),
            s_s[None], _shift_out(proj_s, bs, ts))
```

```python
import functools

import numpy as np
import jax
import jax.numpy as jnp
from jax import lax
from jax.experimental import pallas as pl
from jax.experimental.pallas import tpu as pltpu

F32 = jnp.float32
BF16 = jnp.bfloat16

D_MODEL = 2048
PLE_DIM = 256
HEAD_DIM = 64
N_Q_HEADS = 16
N_KV_HEADS = 4
GQA_GROUP = 4
ATTN_WIDTH = 1024
KV_WIDTH = 256
WINDOW = 128
ALIBI_MAX = 8.0
R_HEAD = 64
R_WIDTH = 1024
R_HEADS = 16
DECAY_LORA = 64
AAA_LORA = 64
GATE_LORA = 160
LORA_COLS = DECAY_LORA + AAA_LORA + GATE_LORA
LORA_PAD = 512
RWKV_COLS = 3 * R_WIDTH + LORA_COLS
D_FF = 4 * D_MODEL
NORM_EPS = 1e-6
GN_EPS = 64e-5

P_Q = 0
P_K = ATTN_WIDTH
P_V = ATTN_WIDTH + KV_WIDTH
P_C = ATTN_WIDTH + 2 * KV_WIDTH
P_L = P_C + 3 * R_WIDTH
P_COLS = P_L + LORA_PAD
P_GATES = P_C + RWKV_COLS
C_BLK = P_C
SHIFT_COLS = 3 * R_WIDTH + LORA_PAD

PAIR = 128
N_PAIRS = R_WIDTH // PAIR
SEG = 256
MIX_CHUNK = 512

VMEM_LIMIT = 56 * 1024 * 1024


def _mm(a, b):
    return jnp.dot(a, b, preferred_element_type=F32)


def _nt(a, b):
    return lax.dot_general(a, b, (((1,), (1,)), ((), ())), preferred_element_type=F32)


def _tn(a, b):
    return lax.dot_general(a, b, (((0,), (0,)), ((), ())), preferred_element_type=F32)


def _rms(x, g):
    ms = jnp.mean(x * x, axis=-1, keepdims=True)
    return x * lax.rsqrt(ms + NORM_EPS) * g


def _alibi_slope(hq):
    return float(2.0 ** (-ALIBI_MAX * (hq + 1) / N_Q_HEADS))


def _proj_kernel(x_ref, g_ref, w_ref, o_ref, h_ref):
    @pl.when(pl.program_id(1) == 0)
    def _():
        h_ref[...] = _rms(x_ref[...], g_ref[...]).astype(BF16)

    o_ref[...] = _mm(h_ref[...], w_ref[...])


def _proj(x, g, w, tm, tn):
    m = x.shape[0]
    return pl.pallas_call(
        _proj_kernel,
        out_shape=jax.ShapeDtypeStruct((m, P_COLS), F32),
        grid=(m // tm, P_COLS // tn),
        in_specs=[
            pl.BlockSpec((tm, D_MODEL), lambda i, j: (i, 0)),
            pl.BlockSpec((1, D_MODEL), lambda i, j: (0, 0)),
            pl.BlockSpec((D_MODEL, tn), lambda i, j: (0, j)),
        ],
        out_specs=pl.BlockSpec((tm, tn), lambda i, j: (i, j)),
        scratch_shapes=[pltpu.VMEM((tm, D_MODEL), BF16)],
        compiler_params=pltpu.CompilerParams(
            dimension_semantics=("arbitrary", "arbitrary"), vmem_limit_bytes=VMEM_LIMIT),
        name="proj",
    )(x, g, w)


def _attend_heads(scores, values, sinks):
    heads = range(len(scores))
    m = []
    for h in heads:
        mh = sinks[h]
        for s in scores[h]:
            mh = jnp.maximum(mh, jnp.max(s, axis=-1, keepdims=True))
        m.append(mh)
    ps = [[jnp.exp(s - m[h]) for s in scores[h]] for h in heads]
    den = []
    for h in heads:
        dh = jnp.exp(sinks[h] - m[h])
        for p in ps[h]:
            dh = dh + jnp.sum(p, axis=-1, keepdims=True)
        den.append(dh)
    outs = []
    for h in heads:
        o = None
        for p, v in zip(ps[h], values[h]):
            t = _mm(p.astype(BF16), v)
            o = t if o is None else o + t
        outs.append(o * (1.0 / den[h]))
    return outs


def _head_slices():
    q_sl = [slice(hq * HEAD_DIM, (hq + 1) * HEAD_DIM) for hq in range(N_Q_HEADS)]
    kv_sl = [slice((hq // GQA_GROUP) * HEAD_DIM, (hq // GQA_GROUP + 1) * HEAD_DIM) for hq in range(N_Q_HEADS)]
    return q_sl, kv_sl


def _alibi_bias(dist, valid):
    slopes = np.array([_alibi_slope(hq) for hq in range(N_Q_HEADS)], np.float32)
    return np.where(valid[None], -slopes[:, None, None] * dist[None].astype(np.float32), -np.inf).astype(np.float32)


def _attn_prompt_kernel(sink_ref, bias_ref, q_ref, kp_ref, kc_ref, vp_ref, vc_ref, o_ref):
    w = WINDOW
    kj = lax.broadcasted_iota(jnp.int32, (w, 2 * w), 1)
    no_prev = (kj < w) & (pl.program_id(1) == 0)
    q = q_ref[...] * (HEAD_DIM ** -0.5)
    k2 = jnp.concatenate([kp_ref[...], kc_ref[...]], axis=0).astype(BF16)
    v2 = jnp.concatenate([vp_ref[...], vc_ref[...]], axis=0).astype(BF16)
    hqs = range(N_Q_HEADS)
    q_sl, kv_sl = _head_slices()
    qs = [q[:, q_sl[hq]].astype(BF16) for hq in hqs]
    sc = [[jnp.where(no_prev, -jnp.inf, _nt(qs[hq], k2[:, kv_sl[hq]]) + bias_ref[hq])] for hq in hqs]
    outs = _attend_heads(sc, [[v2[:, kv_sl[hq]]] for hq in hqs], [sink_ref[hq] for hq in hqs])
    for hq in hqs:
        o_ref[:, q_sl[hq]] = outs[hq]


def _attn_prompt(proj, sinks, batch, seq):
    nb = seq // WINDOW
    kcol, vcol = P_K // KV_WIDTH, P_V // KV_WIDTH

    def cur(c):
        return lambda b, i: (b * nb + i, c)

    def prev(c):
        return lambda b, i: (b * nb + jnp.maximum(i - 1, 0), c)

    ti = np.arange(WINDOW)[:, None]
    kj = np.arange(2 * WINDOW)[None, :]
    dist = ti - kj + WINDOW
    bias = _alibi_bias(dist, (dist >= 0) & (dist <= WINDOW))
    return pl.pallas_call(
        _attn_prompt_kernel,
        out_shape=jax.ShapeDtypeStruct((batch * seq, ATTN_WIDTH), F32),
        grid=(batch, nb),
        in_specs=[
            pl.BlockSpec(memory_space=pltpu.SMEM),
            _resident((N_Q_HEADS, WINDOW, 2 * WINDOW)),
            pl.BlockSpec((WINDOW, ATTN_WIDTH), cur(P_Q // ATTN_WIDTH)),
            pl.BlockSpec((WINDOW, KV_WIDTH), prev(kcol)),
            pl.BlockSpec((WINDOW, KV_WIDTH), cur(kcol)),
            pl.BlockSpec((WINDOW, KV_WIDTH), prev(vcol)),
            pl.BlockSpec((WINDOW, KV_WIDTH), cur(vcol)),
        ],
        out_specs=pl.BlockSpec((WINDOW, ATTN_WIDTH), lambda b, i: (b * nb + i, 0)),
        compiler_params=pltpu.CompilerParams(dimension_semantics=("arbitrary", "arbitrary")),
        name="attn_prompt",
    )(sinks, jnp.asarray(bias), proj, proj, proj, proj, proj)


def _attn_sample_kernel(sink_ref, q_ref, kn_ref, vn_ref, ck_ref, cv_ref, o_ref, nk_ref, nv_ref):
    t, w = q_ref.shape[0], WINDOW
    kn, vn = kn_ref[...], vn_ref[...]
    ck, cv = ck_ref[0], cv_ref[0]
    nk_ref[0, 0:w - t, :] = ck[t:, :]
    nk_ref[0, w - t:w, :] = kn
    nv_ref[0, 0:w - t, :] = cv[t:, :]
    nv_ref[0, w - t:w, :] = vn
    ti = lax.broadcasted_iota(jnp.int32, (t, w), 0)
    cj = lax.broadcasted_iota(jnp.int32, (t, w), 1)
    dist_c = (ti - cj + w).astype(F32)
    valid_c = cj >= ti
    ti2 = lax.broadcasted_iota(jnp.int32, (t, t), 0)
    tj2 = lax.broadcasted_iota(jnp.int32, (t, t), 1)
    dist_n = (ti2 - tj2).astype(F32)
    valid_n = tj2 <= ti2
    q = q_ref[...] * (HEAD_DIM ** -0.5)
    ckb, cvb, knb, vnb = ck.astype(BF16), cv.astype(BF16), kn.astype(BF16), vn.astype(BF16)
    hqs = range(N_Q_HEADS)
    q_sl, kv_sl = _head_slices()
    qs = [q[:, q_sl[hq]].astype(BF16) for hq in hqs]
    scores = [[_nt(qs[hq], ckb[:, kv_sl[hq]]) + jnp.where(valid_c, -_alibi_slope(hq) * dist_c, -jnp.inf),
               _nt(qs[hq], knb[:, kv_sl[hq]]) + jnp.where(valid_n, -_alibi_slope(hq) * dist_n, -jnp.inf)]
              for hq in hqs]
    outs = _attend_heads(scores, [[cvb[:, kv_sl[hq]], vnb[:, kv_sl[hq]]] for hq in hqs],
                         [sink_ref[hq] for hq in hqs])
    for hq in hqs:
        o_ref[:, q_sl[hq]] = outs[hq]


def _attn_sample(proj, sinks, cache_k, cache_v, batch, seq):
    kcol, vcol = P_K // KV_WIDTH, P_V // KV_WIDTH
    win = jax.ShapeDtypeStruct((batch, WINDOW, KV_WIDTH), F32)
    return pl.pallas_call(
        _attn_sample_kernel,
        out_shape=(jax.ShapeDtypeStruct((batch * seq, ATTN_WIDTH), F32), win, win),
        grid=(batch,),
        in_specs=[
            pl.BlockSpec(memory_space=pltpu.SMEM),
            pl.BlockSpec((seq, ATTN_WIDTH), lambda b: (b, P_Q // ATTN_WIDTH)),
            pl.BlockSpec((seq, KV_WIDTH), lambda b: (b, kcol)),
            pl.BlockSpec((seq, KV_WIDTH), lambda b: (b, vcol)),
            pl.BlockSpec((1, WINDOW, KV_WIDTH), lambda b: (b, 0, 0)),
            pl.BlockSpec((1, WINDOW, KV_WIDTH), lambda b: (b, 0, 0)),
        ],
        out_specs=(
            pl.BlockSpec((seq, ATTN_WIDTH), lambda b: (b, 0)),
            pl.BlockSpec((1, WINDOW, KV_WIDTH), lambda b: (b, 0, 0)),
            pl.BlockSpec((1, WINDOW, KV_WIDTH), lambda b: (b, 0, 0)),
        ),
        compiler_params=pltpu.CompilerParams(dimension_semantics=("arbitrary",)),
        name="attn_sample",
    )(sinks, proj, proj, proj, cache_k, cache_v)


def _seg_sum(x, bones):
    rows = x.shape[0]
    hi = x.astype(BF16).astype(F32)
    lo = x - hi
    groups = [slice(j * SEG, (j + 1) * SEG) for j in range(R_WIDTH // SEG)]
    lhs = jnp.concatenate([t[:, sl] for sl in groups for t in (hi, lo)], axis=0).astype(BF16)
    out = _mm(lhs, bones)
    return jnp.concatenate(
        [out[2 * j * rows:(2 * j + 1) * rows] + out[(2 * j + 1) * rows:(2 * j + 2) * rows]
         for j in range(len(groups))], axis=1)


def _wkv_kernel(pa_ref, pb_ref, pl_ref, prev_ref, s0_ref,
                mu_ref, mul_ref, w0_ref, a0_ref, kk_ref, ka_ref, rk_ref, lnw_ref, lnb_ref,
                wl_ref, bones_ref,
                y_ref, so_ref, s_ref, carry_ref, *, chunk, n_chunks):
    c = pl.program_id(1)
    cs = chunk
    gc = 2 * cs
    hd = R_HEAD

    @pl.when(c == 0)
    def _init():
        carry_ref[...] = prev_ref[0]
        s_ref[...] = jnp.zeros(s_ref.shape, F32)
        for p in range(N_PAIRS):
            s_ref[p, 0:hd, 0:hd] = s0_ref[0, 2 * p]
            s_ref[p, hd:2 * hd, hd:2 * hd] = s0_ref[0, 2 * p + 1]

    row = lax.broadcasted_iota(jnp.int32, (cs, 1), 0)

    def token_shift(x, prev_row, mu):
        shifted = jnp.where(row == 0, prev_row, pltpu.roll(x, 1, axis=0))
        return x + (shifted - x) * mu

    w = R_WIDTH
    cols = jnp.concatenate([pa_ref[...], pb_ref[...]], axis=1)
    l_raw = pl_ref[...]
    xx = token_shift(cols, carry_ref[:, 0:3 * w], mu_ref[...])
    xr, xk, xv = xx[:, 0:w], xx[:, w:2 * w], xx[:, 2 * w:3 * w]
    xl = token_shift(l_raw, carry_ref[:, 3 * w:], mul_ref[...])
    carry_ref[:, 0:3 * w] = cols[cs - 1:cs, :]
    carry_ref[:, 3 * w:] = l_raw[cs - 1:cs, :]

    lane_l = lax.broadcasted_iota(jnp.int32, (1, LORA_PAD), 1)
    act = jnp.where(lane_l < DECAY_LORA, jnp.tanh(xl),
                    jnp.where(lane_l < DECAY_LORA + AAA_LORA, xl,
                              jnp.where(lane_l < LORA_COLS, jax.nn.sigmoid(xl), 0.0)))
    up = _mm(act.astype(BF16), wl_ref[...])
    z = -(w0_ref[...] + up[:, 0:w])
    softplus = jnp.maximum(z, 0.0) + jnp.log1p(jnp.exp(-jnp.abs(z)))
    lwd = -jnp.exp(-softplus - 0.5)
    a = jax.nn.sigmoid(a0_ref[...] + up[:, w:2 * w])
    gate = up[:, 2 * w:3 * w]

    bones = bones_ref[...]
    kkn = xk * kk_ref[...]
    kk = kkn * lax.rsqrt(jnp.maximum(_seg_sum(kkn * kkn, bones), 1e-24))
    k2 = xk * (1.0 + (a - 1.0) * ka_ref[...])
    b = kk * a

    tri = (lax.broadcasted_iota(jnp.int32, (cs, cs), 0) >= lax.broadcasted_iota(jnp.int32, (cs, cs), 1))
    cum = jnp.dot(tri.astype(F32), lwd, precision=lax.Precision.HIGHEST, preferred_element_type=F32)
    cum_last = cum[cs - 1:cs, :]
    e_inv = jnp.exp(-cum)
    e_last = jnp.exp(cum_last - cum)
    kq = kk * jnp.exp(cum - lwd)
    rq = xr * jnp.exp(cum)
    kd = k2 * e_inv
    bd = b * e_inv
    kdp = k2 * e_last
    bdp = b * e_last
    p_last = jnp.exp(cum_last)

    ri = lax.broadcasted_iota(jnp.int32, (gc, gc), 0)
    ci = lax.broadcasted_iota(jnp.int32, (gc, gc), 1)
    same_head = (ri >= cs) == (ci >= cs)
    strict = same_head & (ci < ri)
    incl = same_head & (ci <= ri)
    eye = (ri == ci).astype(F32)
    head0 = lax.broadcasted_iota(jnp.int32, (1, PAIR), 1) < hd

    def stack(x):
        return jnp.concatenate([jnp.where(head0, x, 0.0), jnp.where(head0, 0.0, x)], axis=0).astype(BF16)

    pairs = range(N_PAIRS)
    lanes = [slice(p * PAIR, (p + 1) * PAIR) for p in pairs]
    kq_s = [stack(kq[:, sl]) for sl in lanes]
    bd_s = [stack(bd[:, sl]) for sl in lanes]
    kd_s = [stack(kd[:, sl]) for sl in lanes]
    rq_s = [stack(rq[:, sl]) for sl in lanes]
    v_s = [stack(xv[:, sl]) for sl in lanes]
    s_old = [s_ref[p] for p in pairs]
    s_b = [s.astype(BF16) for s in s_old]
    probes = [jnp.concatenate([kq_s[p], rq_s[p]], axis=0) for p in pairs]
    state_t = [_nt(probes[p], s_b[p]) for p in pairs]
    if gc % PAIR == 0:
        a_all = [_nt(probes[p], jnp.concatenate([bd_s[p], kd_s[p]], axis=0)) for p in pairs]
        a_kb = [jnp.where(strict, a[0:gc, 0:gc], 0.0) for a in a_all]
        a_kk = [jnp.where(strict, a[0:gc, gc:2 * gc], 0.0).astype(BF16) for a in a_all]
        a_rb = [jnp.where(incl, a[gc:2 * gc, 0:gc], 0.0).astype(BF16) for a in a_all]
        a_rk = [jnp.where(incl, a[gc:2 * gc, gc:2 * gc], 0.0).astype(BF16) for a in a_all]
    else:
        a_kb = [jnp.where(strict, _nt(kq_s[p], bd_s[p]), 0.0) for p in pairs]
        a_kk = [jnp.where(strict, _nt(kq_s[p], kd_s[p]), 0.0).astype(BF16) for p in pairs]
        a_rb = [jnp.where(incl, _nt(rq_s[p], bd_s[p]), 0.0).astype(BF16) for p in pairs]
        a_rk = [jnp.where(incl, _nt(rq_s[p], kd_s[p]), 0.0).astype(BF16) for p in pairs]
    inv = [eye - a for a in a_kb]
    squarings = cs.bit_length() - 2
    if squarings >= 1:
        apow_b = [a.astype(BF16) for a in a_kb]
        apow_b = [_mm(a, a).astype(BF16) for a in apow_b]
        for _ in range(squarings - 1):
            both = [_mm(jnp.concatenate([apow_b[p], inv[p].astype(BF16)], axis=0), apow_b[p]) for p in pairs]
            apow_b = [t[0:gc].astype(BF16) for t in both]
            inv = [inv[p] + both[p][gc:2 * gc] for p in pairs]
        inv = [inv[p] + _mm(inv[p].astype(BF16), apow_b[p]) for p in pairs]
    rhs = [state_t[p][0:gc] + _mm(a_kk[p], v_s[p]) for p in pairs]
    u_b = [_mm(inv[p].astype(BF16), rhs[p].astype(BF16)).astype(BF16) for p in pairs]
    if gc % PAIR == 0:
        y_s = [state_t[p][gc:2 * gc] + _mm(jnp.concatenate([a_rk[p], -a_rb[p]], axis=1),
                                           jnp.concatenate([v_s[p], u_b[p]], axis=0)) for p in pairs]
    else:
        y_s = [state_t[p][gc:2 * gc] + _mm(a_rk[p], v_s[p]) - _mm(a_rb[p], u_b[p]) for p in pairs]
    kdp_s = [stack(kdp[:, sl]) for sl in lanes]
    bdp_s = [stack(bdp[:, sl]) for sl in lanes]
    for p in pairs:
        s_ref[p] = s_old[p] * p_last[:, lanes[p]] + _tn(
            jnp.concatenate([v_s[p], -u_b[p]], axis=0), jnp.concatenate([kdp_s[p], bdp_s[p]], axis=0))

    y = jnp.concatenate([ys[0:cs] + ys[cs:gc] for ys in y_s], axis=1)
    mean = _seg_sum(y, bones) * (1.0 / hd)
    d = y - mean
    var = _seg_sum(d * d, bones) * (1.0 / hd)
    yn = d * lax.rsqrt(var + GN_EPS) * lnw_ref[...] + lnb_ref[...]
    bonus = _seg_sum(xr * k2 * rk_ref[...], bones) * xv
    y_ref[...] = (yn + bonus) * gate

    @pl.when(c == n_chunks - 1)
    def _fin():
        for p in range(N_PAIRS):
            so_ref[0, 2 * p] = s_ref[p, 0:hd, 0:hd]
            so_ref[0, 2 * p + 1] = s_ref[p, hd:2 * hd, hd:2 * hd]


def _wkv(proj, prev0, s0, params, wl, bones, batch, seq, chunk):
    nc = seq // chunk
    rw = R_WIDTH

    def col(cb):
        return lambda b, c: (b * nc + c, cb)

    const2 = lambda b, c: (0, 0)
    vec = lambda n: pl.BlockSpec((1, n), const2)
    mu, mul, w0, a0, k_k, k_a, r_k, ln_w, ln_b = params
    return pl.pallas_call(
        functools.partial(_wkv_kernel, chunk=chunk, n_chunks=nc),
        out_shape=(jax.ShapeDtypeStruct((batch * seq, rw), F32),
                   jax.ShapeDtypeStruct((batch, R_HEADS, R_HEAD, R_HEAD), F32)),
        grid=(batch, nc),
        in_specs=[
            pl.BlockSpec((chunk, C_BLK), col(P_C // C_BLK)),
            pl.BlockSpec((chunk, C_BLK), col(P_C // C_BLK + 1)),
            pl.BlockSpec((chunk, LORA_PAD), col(P_L // LORA_PAD)),
            pl.BlockSpec((1, 1, SHIFT_COLS), lambda b, c: (b, 0, 0)),
            pl.BlockSpec((1, R_HEADS, R_HEAD, R_HEAD), lambda b, c: (b, 0, 0, 0)),
            vec(3 * rw), vec(LORA_PAD), vec(rw), vec(rw), vec(rw), vec(rw), vec(rw), vec(rw), vec(rw),
            pl.BlockSpec((LORA_PAD, 3 * rw), const2),
            pl.BlockSpec((SEG, SEG), const2),
        ],
        out_specs=(
            pl.BlockSpec((chunk, rw), lambda b, c: (b * nc + c, 0)),
            pl.BlockSpec((1, R_HEADS, R_HEAD, R_HEAD), lambda b, c: (b, 0, 0, 0)),
        ),
        scratch_shapes=[pltpu.VMEM((N_PAIRS, PAIR, PAIR), F32), pltpu.VMEM((1, SHIFT_COLS), F32)],
        compiler_params=pltpu.CompilerParams(
            dimension_semantics=("arbitrary", "arbitrary"), vmem_limit_bytes=VMEM_LIMIT),
        name="wkv",
    )(proj, proj, proj, prev0, s0, mu, mul, w0, a0, k_k, k_a, r_k, ln_w, ln_b, wl, bones)


def _mix_kernel(x_ref, ya_ref, yr_ref, g_ref, wg_ref, wba_ref, wbr_ref, wo_ref, o_ref):
    x = x_ref[...]
    h = _rms(x, g_ref[...]).astype(BF16)
    ya = ya_ref[...].astype(BF16)
    yr = yr_ref[...].astype(BF16)
    acc = x
    for c in range(D_MODEL // MIX_CHUNK):
        ca = slice(c * MIX_CHUNK, (c + 1) * MIX_CHUNK)
        cr = slice(D_MODEL + c * MIX_CHUNK, D_MODEL + (c + 1) * MIX_CHUNK)
        mixed = (jax.nn.sigmoid(_mm(h, wg_ref[:, ca])) * _mm(ya, wba_ref[:, ca])
                 + jax.nn.sigmoid(_mm(h, wg_ref[:, cr])) * _mm(yr, wbr_ref[:, ca]))
        acc = acc + _mm(mixed.astype(BF16), wo_ref[ca, :])
    o_ref[...] = acc


def _resident(shape):
    return pl.BlockSpec(shape, lambda *_: (0,) * len(shape), pipeline_mode=pl.Buffered(1))


def _mix(x, ya, yr, g, wg, wba, wbr, wo, tm):
    m = x.shape[0]
    row = lambda i: (i, 0)
    return pl.pallas_call(
        _mix_kernel,
        out_shape=jax.ShapeDtypeStruct((m, D_MODEL), F32),
        grid=(m // tm,),
        in_specs=[
            pl.BlockSpec((tm, D_MODEL), row),
            pl.BlockSpec((tm, ATTN_WIDTH), row),
            pl.BlockSpec((tm, R_WIDTH), row),
            pl.BlockSpec((1, D_MODEL), lambda i: (0, 0)),
            _resident((D_MODEL, 2 * D_MODEL)),
            _resident((ATTN_WIDTH, D_MODEL)),
            _resident((R_WIDTH, D_MODEL)),
            _resident((D_MODEL, D_MODEL)),
        ],
        out_specs=pl.BlockSpec((tm, D_MODEL), row),
        compiler_params=pltpu.CompilerParams(
            dimension_semantics=("arbitrary",), vmem_limit_bytes=VMEM_LIMIT),
        name="mix",
    )(x, ya, yr, g, wg, wba, wbr, wo)


def _ffn_kernel(x_ref, g_ref, wu_ref, wd_ref, o_ref, h_ref):
    @pl.when(pl.program_id(1) == 0)
    def _():
        x = x_ref[...]
        h_ref[...] = _rms(x, g_ref[...]).astype(BF16)
        o_ref[...] = x

    u = _mm(h_ref[...], wu_ref[...])
    o_ref[...] += _mm(jnp.square(jnp.maximum(u, 0.0)).astype(BF16), wd_ref[...])


def _ffn(x, g, wu, wd, tm, tk):
    m = x.shape[0]
    return pl.pallas_call(
        _ffn_kernel,
        out_shape=jax.ShapeDtypeStruct((m, D_MODEL), F32),
        grid=(m // tm, D_FF // tk),
        in_specs=[
            pl.BlockSpec((tm, D_MODEL), lambda i, k: (i, 0)),
            pl.BlockSpec((1, D_MODEL), lambda i, k: (0, 0)),
            pl.BlockSpec((D_MODEL, tk), lambda i, k: (0, k)),
            pl.BlockSpec((tk, D_MODEL), lambda i, k: (k, 0)),
        ],
        out_specs=pl.BlockSpec((tm, D_MODEL), lambda i, k: (i, 0)),
        scratch_shapes=[pltpu.VMEM((tm, D_MODEL), BF16)],
        compiler_params=pltpu.CompilerParams(
            dimension_semantics=("arbitrary", "arbitrary"), vmem_limit_bytes=VMEM_LIMIT),
        name="ffn",
    )(x, g, wu, wd)


def _ple_kernel(x_ref, pe_ref, wg_ref, wp_ref, g_ref, o_ref):
    x = x_ref[...]
    gate = jax.nn.sigmoid(_mm(x.astype(BF16), wg_ref[...]))
    x = x + gate * _mm(pe_ref[...].astype(BF16), wp_ref[...])
    o_ref[...] = _rms(x, g_ref[...])


def _ple(x, pe, wg, wp, g, tm):
    m = x.shape[0]
    row = lambda i: (i, 0)
    return pl.pallas_call(
        _ple_kernel,
        out_shape=jax.ShapeDtypeStruct((m, D_MODEL), F32),
        grid=(m // tm,),
        in_specs=[
            pl.BlockSpec((tm, D_MODEL), row),
            pl.BlockSpec((tm, PLE_DIM), row),
            _resident((D_MODEL, D_MODEL)),
            _resident((PLE_DIM, D_MODEL)),
            pl.BlockSpec((1, D_MODEL), lambda i: (0, 0)),
        ],
        out_specs=pl.BlockSpec((tm, D_MODEL), row),
        compiler_params=pltpu.CompilerParams(
            dimension_semantics=("arbitrary",), vmem_limit_bytes=VMEM_LIMIT),
        name="ple",
    )(x, pe, wg, wp, g)


def _layer(x, pe, attend, prev0, s0, wts, batch, seq, tiles):
    (g_mix, w_in, w_gates, wkv_params, wl, bones, wba, wbr, wo, g_ffn, wu, wd, wg, wp, g_fin) = wts
    proj = _proj(x, g_mix, w_in, tiles["proj_m"], tiles["proj_n"])
    ya = attend(proj)
    yr, s_new = _wkv(proj, prev0, s0, wkv_params, wl, bones, batch, seq, tiles["chunk"])
    x = _mix(x, ya, yr, g_mix, w_gates, wba, wbr, wo, tiles["mix_m"])
    x = _ffn(x, g_ffn, wu, wd, tiles["ffn_m"], tiles["ffn_k"])
    y = _ple(x, pe, wg, wp, g_fin, tiles["ple_m"])
    return y, proj, s_new


def _shift_out(proj, batch, seq):
    last = proj.reshape(batch, seq, P_COLS)[:, -1]
    return last[:, P_C:P_C + RWKV_COLS][None]


def kernel(x_prompt, x_sample, cache_k_win, cache_v_win, state_wkv, state_shift, p_prompt, p_sample,
           norm_mix, w_in, attn_sinks, rwkv_mu, rwkv_w0, rwkv_w2, rwkv_a0, rwkv_a2, rwkv_g2,
           rwkv_k_k, rwkv_k_a, rwkv_r_k, rwkv_ln_w, rwkv_ln_b, w_branch_attn, w_branch_rwkv,
           w_out, norm_ffn, w_ff_up, w_ff_down, w_ple_proj, w_ple_gate, norm_final):
    assert w_in.shape[0] == 1, "single-layer step"
    bp, tp = x_prompt.shape[0], x_prompt.shape[1]
    bs, ts = x_sample.shape[0], x_sample.shape[1]
    rw = R_WIDTH

    w_in_p = w_in[0].astype(BF16)
    w_gates = w_in_p[:, P_GATES:P_GATES + 2 * D_MODEL]
    wl = jnp.zeros((LORA_PAD, 3 * rw), F32)
    wl = wl.at[0:DECAY_LORA, 0:rw].set(rwkv_w2[0])
    wl = wl.at[DECAY_LORA:DECAY_LORA + AAA_LORA, rw:2 * rw].set(rwkv_a2[0])
    wl = wl.at[DECAY_LORA + AAA_LORA:LORA_COLS, 2 * rw:3 * rw].set(rwkv_g2[0])
    wl = wl.astype(BF16)
    seg_id = np.arange(SEG) // R_HEAD
    bones = jnp.asarray(seg_id[:, None] == seg_id[None, :], BF16)
    mu = rwkv_mu[0]
    row = lambda v: v.reshape(1, -1)
    wkv_params = (row(mu[:3 * rw]), row(jnp.pad(mu[3 * rw:], (0, LORA_PAD - LORA_COLS))),
                  row(rwkv_w0[0]), row(rwkv_a0[0]), row(rwkv_k_k[0]), row(rwkv_k_a[0]),
                  row(rwkv_r_k[0]), row(rwkv_ln_w[0]), row(rwkv_ln_b[0]))
    wts = (row(norm_mix[0]), w_in_p, w_gates, wkv_params, wl, bones,
           w_branch_attn[0].astype(BF16), w_branch_rwkv[0].astype(BF16), w_out[0].astype(BF16),
           row(norm_ffn[0]), w_ff_up[0].astype(BF16), w_ff_down[0].astype(BF16),
           w_ple_gate[0].astype(BF16), w_ple_proj[0].astype(BF16), row(norm_final))
    sinks = attn_sinks[0]

    tiles_p = dict(proj_m=1024, proj_n=1024, chunk=64, mix_m=256, ffn_m=1024, ffn_k=512, ple_m=512)
    yp, proj_p, s_p = _layer(
        x_prompt.reshape(bp * tp, D_MODEL), p_prompt[0].reshape(bp * tp, PLE_DIM),
        lambda pr: _attn_prompt(pr, sinks, bp, tp),
        jnp.zeros((bp, 1, SHIFT_COLS), F32), jnp.zeros((bp, R_HEADS, R_HEAD, R_HEAD), F32),
        wts, bp, tp, tiles_p)
    pp3 = proj_p.reshape(bp, tp, P_COLS)[:, -WINDOW:]
    k_p = pp3[:, :, P_K:P_K + KV_WIDTH].reshape(1, bp, WINDOW, N_KV_HEADS, HEAD_DIM)
    v_p = pp3[:, :, P_V:P_V + KV_WIDTH].reshape(1, bp, WINDOW, N_KV_HEADS, HEAD_DIM)

    ms = bs * ts
    tiles_s = dict(proj_m=ms, proj_n=1024, chunk=ts, mix_m=ms, ffn_m=ms, ffn_k=512, ple_m=ms)
    win_out = {}

    def attend_s(pr):
        ya, nk, nv = _attn_sample(pr, sinks, cache_k_win[0].reshape(bs, WINDOW, KV_WIDTH),
                                  cache_v_win[0].reshape(bs, WINDOW, KV_WIDTH), bs, ts)
        win_out["k"], win_out["v"] = nk, nv
        return ya

    prev_s = jnp.pad(state_shift[0], ((0, 0), (0, LORA_PAD - LORA_COLS))).reshape(bs, 1, SHIFT_COLS)
    ys, proj_s, s_s = _layer(
        x_sample.reshape(ms, D_MODEL), p_sample[0].reshape(ms, PLE_DIM), attend_s,
        prev_s, state_wkv[0], wts, bs, ts, tiles_s)

    return (yp.reshape(bp, tp, D_MODEL), ys.reshape(bs, ts, D_MODEL),
            k_p, v_p, s_p[None], _shift_out(proj_p, bp, tp),
            win_out["k"].reshape(1, bs, WINDOW, N_KV_HEADS, HEAD_DIM),
            win_out["v"].reshape(1, bs, WINDOW, N_KV_HEADS, HEAD_DIM),
            s_s[None], _shift_out(proj_s, bs, ts))
```

```python
import functools

import numpy as np
import jax
import jax.numpy as jnp
from jax import lax
from jax.experimental import pallas as pl
from jax.experimental.pallas import tpu as pltpu

F32 = jnp.float32
BF16 = jnp.bfloat16

D_MODEL = 2048
PLE_DIM = 256
HEAD_DIM = 64
N_Q_HEADS = 16
N_KV_HEADS = 4
GQA_GROUP = 4
ATTN_WIDTH = 1024
KV_WIDTH = 256
WINDOW = 128
ALIBI_MAX = 8.0
R_HEAD = 64
R_WIDTH = 1024
R_HEADS = 16
DECAY_LORA = 64
AAA_LORA = 64
GATE_LORA = 160
LORA_COLS = DECAY_LORA + AAA_LORA + GATE_LORA
LORA_PAD = 512
RWKV_COLS = 3 * R_WIDTH + LORA_COLS
D_FF = 4 * D_MODEL
NORM_EPS = 1e-6
GN_EPS = 64e-5

P_Q = 0
P_K = ATTN_WIDTH
P_V = ATTN_WIDTH + KV_WIDTH
P_C = ATTN_WIDTH + 2 * KV_WIDTH
P_L = P_C + 3 * R_WIDTH
P_COLS = P_L + LORA_PAD
P_GATES = P_C + RWKV_COLS
C_BLK = P_C
SHIFT_COLS = 3 * R_WIDTH + LORA_PAD

PAIR = 128
N_PAIRS = R_WIDTH // PAIR
SEG = 256
MIX_CHUNK = 512

VMEM_LIMIT = 56 * 1024 * 1024


def _mm(a, b):
    return jnp.dot(a, b, preferred_element_type=F32)


def _nt(a, b):
    return lax.dot_general(a, b, (((1,), (1,)), ((), ())), preferred_element_type=F32)


def _tn(a, b):
    return lax.dot_general(a, b, (((0,), (0,)), ((), ())), preferred_element_type=F32)


def _rms(x, g):
    ms = jnp.mean(x * x, axis=-1, keepdims=True)
    return x * lax.rsqrt(ms + NORM_EPS) * g


def _alibi_slope(hq):
    return float(2.0 ** (-ALIBI_MAX * (hq + 1) / N_Q_HEADS))


def _proj_kernel(x_ref, g_ref, w_ref, o_ref, h_ref):
    @pl.when(pl.program_id(1) == 0)
    def _():
        h_ref[...] = _rms(x_ref[...], g_ref[...]).astype(BF16)

    o_ref[...] = _mm(h_ref[...], w_ref[...])


def _proj(x, g, w, tm, tn):
    m = x.shape[0]
    return pl.pallas_call(
        _proj_kernel,
        out_shape=jax.ShapeDtypeStruct((m, P_COLS), F32),
        grid=(m // tm, P_COLS // tn),
        in_specs=[
            pl.BlockSpec((tm, D_MODEL), lambda i, j: (i, 0)),
            pl.BlockSpec((1, D_MODEL), lambda i, j: (0, 0)),
            pl.BlockSpec((D_MODEL, tn), lambda i, j: (0, j)),
        ],
        out_specs=pl.BlockSpec((tm, tn), lambda i, j: (i, j)),
        scratch_shapes=[pltpu.VMEM((tm, D_MODEL), BF16)],
        compiler_params=pltpu.CompilerParams(
            dimension_semantics=("arbitrary", "arbitrary"), vmem_limit_bytes=VMEM_LIMIT),
        name="proj",
    )(x, g, w)


def _attend_heads(scores, values, sinks):
    heads = range(len(scores))
    m = []
    for h in heads:
        mh = sinks[h]
        for s in scores[h]:
            mh = jnp.maximum(mh, jnp.max(s, axis=-1, keepdims=True))
        m.append(mh)
    ps = [[jnp.exp(s - m[h]) for s in scores[h]] for h in heads]
    den = []
    for h in heads:
        dh = jnp.exp(sinks[h] - m[h])
        for p in ps[h]:
            dh = dh + jnp.sum(p, axis=-1, keepdims=True)
        den.append(dh)
    outs = []
    for h in heads:
        o = None
        for p, v in zip(ps[h], values[h]):
            t = _mm(p.astype(BF16), v)
            o = t if o is None else o + t
        outs.append(o * (1.0 / den[h]))
    return outs


def _head_slices():
    q_sl = [slice(hq * HEAD_DIM, (hq + 1) * HEAD_DIM) for hq in range(N_Q_HEADS)]
    kv_sl = [slice((hq // GQA_GROUP) * HEAD_DIM, (hq // GQA_GROUP + 1) * HEAD_DIM) for hq in range(N_Q_HEADS)]
    return q_sl, kv_sl


def _alibi_bias(dist, valid):
    slopes = np.array([_alibi_slope(hq) for hq in range(N_Q_HEADS)], np.float32)
    return np.where(valid[None], -slopes[:, None, None] * dist[None].astype(np.float32), -np.inf).astype(np.float32)


def _attn_prompt_kernel(sink_ref, bias_ref, q_ref, kp_ref, kc_ref, vp_ref, vc_ref, o_ref):
    w = WINDOW
    kj = lax.broadcasted_iota(jnp.int32, (w, 2 * w), 1)
    no_prev = (kj < w) & (pl.program_id(1) == 0)
    q = q_ref[...] * (HEAD_DIM ** -0.5)
    k2 = jnp.concatenate([kp_ref[...], kc_ref[...]], axis=0).astype(BF16)
    v2 = jnp.concatenate([vp_ref[...], vc_ref[...]], axis=0).astype(BF16)
    hqs = range(N_Q_HEADS)
    q_sl, kv_sl = _head_slices()
    qs = [q[:, q_sl[hq]].astype(BF16) for hq in hqs]
    sc = [[jnp.where(no_prev, -jnp.inf, _nt(qs[hq], k2[:, kv_sl[hq]]) + bias_ref[hq])] for hq in hqs]
    outs = _attend_heads(sc, [[v2[:, kv_sl[hq]]] for hq in hqs], [sink_ref[hq] for hq in hqs])
    for hq in hqs:
        o_ref[:, q_sl[hq]] = outs[hq]


def _attn_prompt(proj, sinks, batch, seq):
    nb = seq // WINDOW
    kcol, vcol = P_K // KV_WIDTH, P_V // KV_WIDTH

    def cur(c):
        return lambda b, i: (b * nb + i, c)

    def prev(c):
        return lambda b, i: (b * nb + jnp.maximum(i - 1, 0), c)

    ti = np.arange(WINDOW)[:, None]
    kj = np.arange(2 * WINDOW)[None, :]
    dist = ti - kj + WINDOW
    bias = _alibi_bias(dist, (dist >= 0) & (dist <= WINDOW))
    return pl.pallas_call(
        _attn_prompt_kernel,
        out_shape=jax.ShapeDtypeStruct((batch * seq, ATTN_WIDTH), F32),
        grid=(batch, nb),
        in_specs=[
            pl.BlockSpec(memory_space=pltpu.SMEM),
            _resident((N_Q_HEADS, WINDOW, 2 * WINDOW)),
            pl.BlockSpec((WINDOW, ATTN_WIDTH), cur(P_Q // ATTN_WIDTH)),
            pl.BlockSpec((WINDOW, KV_WIDTH), prev(kcol)),
            pl.BlockSpec((WINDOW, KV_WIDTH), cur(kcol)),
            pl.BlockSpec((WINDOW, KV_WIDTH), prev(vcol)),
            pl.BlockSpec((WINDOW, KV_WIDTH), cur(vcol)),
        ],
        out_specs=pl.BlockSpec((WINDOW, ATTN_WIDTH), lambda b, i: (b * nb + i, 0)),
        compiler_params=pltpu.CompilerParams(dimension_semantics=("arbitrary", "arbitrary")),
        name="attn_prompt",
    )(sinks, jnp.asarray(bias), proj, proj, proj, proj, proj)


def _attn_sample_kernel(sink_ref, q_ref, kn_ref, vn_ref, ck_ref, cv_ref, o_ref, nk_ref, nv_ref):
    t, w = q_ref.shape[0], WINDOW
    kn, vn = kn_ref[...], vn_ref[...]
    ck, cv = ck_ref[0], cv_ref[0]
    nk_ref[0, 0:w - t, :] = ck[t:, :]
    nk_ref[0, w - t:w, :] = kn
    nv_ref[0, 0:w - t, :] = cv[t:, :]
    nv_ref[0, w - t:w, :] = vn
    ti = lax.broadcasted_iota(jnp.int32, (t, w), 0)
    cj = lax.broadcasted_iota(jnp.int32, (t, w), 1)
    dist_c = (ti - cj + w).astype(F32)
    valid_c = cj >= ti
    ti2 = lax.broadcasted_iota(jnp.int32, (t, t), 0)
    tj2 = lax.broadcasted_iota(jnp.int32, (t, t), 1)
    dist_n = (ti2 - tj2).astype(F32)
    valid_n = tj2 <= ti2
    q = q_ref[...] * (HEAD_DIM ** -0.5)
    ckb, cvb, knb, vnb = ck.astype(BF16), cv.astype(BF16), kn.astype(BF16), vn.astype(BF16)
    hqs = range(N_Q_HEADS)
    q_sl, kv_sl = _head_slices()
    qs = [q[:, q_sl[hq]].astype(BF16) for hq in hqs]
    scores = [[_nt(qs[hq], ckb[:, kv_sl[hq]]) + jnp.where(valid_c, -_alibi_slope(hq) * dist_c, -jnp.inf),
               _nt(qs[hq], knb[:, kv_sl[hq]]) + jnp.where(valid_n, -_alibi_slope(hq) * dist_n, -jnp.inf)]
              for hq in hqs]
    outs = _attend_heads(scores, [[cvb[:, kv_sl[hq]], vnb[:, kv_sl[hq]]] for hq in hqs],
                         [sink_ref[hq] for hq in hqs])
    for hq in hqs:
        o_ref[:, q_sl[hq]] = outs[hq]


def _attn_sample(proj, sinks, cache_k, cache_v, batch, seq):
    kcol, vcol = P_K // KV_WIDTH, P_V // KV_WIDTH
    win = jax.ShapeDtypeStruct((batch, WINDOW, KV_WIDTH), F32)
    return pl.pallas_call(
        _attn_sample_kernel,
        out_shape=(jax.ShapeDtypeStruct((batch * seq, ATTN_WIDTH), F32), win, win),
        grid=(batch,),
        in_specs=[
            pl.BlockSpec(memory_space=pltpu.SMEM),
            pl.BlockSpec((seq, ATTN_WIDTH), lambda b: (b, P_Q // ATTN_WIDTH)),
            pl.BlockSpec((seq, KV_WIDTH), lambda b: (b, kcol)),
            pl.BlockSpec((seq, KV_WIDTH), lambda b: (b, vcol)),
            pl.BlockSpec((1, WINDOW, KV_WIDTH), lambda b: (b, 0, 0)),
            pl.BlockSpec((1, WINDOW, KV_WIDTH), lambda b: (b, 0, 0)),
        ],
        out_specs=(
            pl.BlockSpec((seq, ATTN_WIDTH), lambda b: (b, 0)),
            pl.BlockSpec((1, WINDOW, KV_WIDTH), lambda b: (b, 0, 0)),
            pl.BlockSpec((1, WINDOW, KV_WIDTH), lambda b: (b, 0, 0)),
        ),
        compiler_params=pltpu.CompilerParams(dimension_semantics=("arbitrary",)),
        name="attn_sample",
    )(sinks, proj, proj, proj, cache_k, cache_v)


def _seg_sum(x, bones):
    rows = x.shape[0]
    hi = x.astype(BF16).astype(F32)
    lo = x - hi
    groups = [slice(j * SEG, (j + 1) * SEG) for j in range(R_WIDTH // SEG)]
    lhs = jnp.concatenate([t[:, sl] for sl in groups for t in (hi, lo)], axis=0).astype(BF16)
    out = _mm(lhs, bones)
    return jnp.concatenate(
        [out[2 * j * rows:(2 * j + 1) * rows] + out[(2 * j + 1) * rows:(2 * j + 2) * rows]
         for j in range(len(groups))], axis=1)


def _wkv_kernel(pa_ref, pb_ref, pl_ref, prev_ref, s0_ref,
                mu_ref, mul_ref, w0_ref, a0_ref, kk_ref, ka_ref, rk_ref, lnw_ref, lnb_ref,
                wl_ref, bones_ref,
                y_ref, so_ref, s_ref, carry_ref, *, chunk, n_sub, n_steps):
    step = pl.program_id(1)
    cs = chunk
    rows = n_sub * cs
    gc = 2 * cs
    hd = R_HEAD
    w = R_WIDTH

    @pl.when(step == 0)
    def _init():
        carry_ref[...] = prev_ref[0]
        s_ref[...] = jnp.zeros(s_ref.shape, F32)
        for p in range(N_PAIRS):
            s_ref[p, 0:hd, 0:hd] = s0_ref[0, 2 * p]
            s_ref[p, hd:2 * hd, hd:2 * hd] = s0_ref[0, 2 * p + 1]

    row = lax.broadcasted_iota(jnp.int32, (rows, 1), 0)

    def token_shift(x, prev_row, mu):
        shifted = jnp.where(row == 0, prev_row, pltpu.roll(x, 1, axis=0))
        return x + (shifted - x) * mu

    cols = jnp.concatenate([pa_ref[...], pb_ref[...]], axis=1)
    l_raw = pl_ref[...]
    xx = token_shift(cols, carry_ref[:, 0:3 * w], mu_ref[...])
    xr, xk, xv = xx[:, 0:w], xx[:, w:2 * w], xx[:, 2 * w:3 * w]
    xl = token_shift(l_raw, carry_ref[:, 3 * w:], mul_ref[...])
    carry_ref[:, 0:3 * w] = cols[rows - 1:rows, :]
    carry_ref[:, 3 * w:] = l_raw[rows - 1:rows, :]

    lane_l = lax.broadcasted_iota(jnp.int32, (1, LORA_PAD), 1)
    act = jnp.where(lane_l < DECAY_LORA, jnp.tanh(xl),
                    jnp.where(lane_l < DECAY_LORA + AAA_LORA, xl,
                              jnp.where(lane_l < LORA_COLS, jax.nn.sigmoid(xl), 0.0)))
    up = _mm(act.astype(BF16), wl_ref[...])
    z = -(w0_ref[...] + up[:, 0:w])
    softplus = jnp.maximum(z, 0.0) + jnp.log1p(jnp.exp(-jnp.abs(z)))
    lwd = -jnp.exp(-softplus - 0.5)
    a = jax.nn.sigmoid(a0_ref[...] + up[:, w:2 * w])
    gate = up[:, 2 * w:3 * w]

    bones = bones_ref[...]
    kkn = xk * kk_ref[...]
    kk = kkn * lax.rsqrt(jnp.maximum(_seg_sum(kkn * kkn, bones), 1e-24))
    k2 = xk * (1.0 + (a - 1.0) * ka_ref[...])
    b = kk * a

    tri = (lax.broadcasted_iota(jnp.int32, (cs, cs), 0)
           >= lax.broadcasted_iota(jnp.int32, (cs, cs), 1)).astype(F32)
    ri = lax.broadcasted_iota(jnp.int32, (gc, gc), 0)
    ci = lax.broadcasted_iota(jnp.int32, (gc, gc), 1)
    same_head = (ri >= cs) == (ci >= cs)
    strict = same_head & (ci < ri)
    incl = same_head & (ci <= ri)
    eye = (ri == ci).astype(F32)
    head0 = lax.broadcasted_iota(jnp.int32, (1, PAIR), 1) < hd

    def stack(x):
        return jnp.concatenate([jnp.where(head0, x, 0.0), jnp.where(head0, 0.0, x)], axis=0).astype(BF16)

    pairs = range(N_PAIRS)
    lanes = [slice(p * PAIR, (p + 1) * PAIR) for p in pairs]
    squarings = cs.bit_length() - 2

    chunks = []
    for sub in range(n_sub):
        rs = slice(sub * cs, (sub + 1) * cs)
        lw_c = lwd[rs]
        cum = jnp.dot(tri, lw_c, precision=lax.Precision.HIGHEST, preferred_element_type=F32)
        cum_last = cum[cs - 1:cs, :]
        e_inv = jnp.exp(-cum)
        e_last = jnp.exp(cum_last - cum)
        kq = kk[rs] * jnp.exp(cum - lw_c)
        rq = xr[rs] * jnp.exp(cum)
        kd = k2[rs] * e_inv
        bd = b[rs] * e_inv
        kdp = k2[rs] * e_last
        bdp = b[rs] * e_last
        xv_c = xv[rs]
        kq_s = [stack(kq[:, sl]) for sl in lanes]
        bd_s = [stack(bd[:, sl]) for sl in lanes]
        kd_s = [stack(kd[:, sl]) for sl in lanes]
        rq_s = [stack(rq[:, sl]) for sl in lanes]
        probes = [jnp.concatenate([kq_s[p], rq_s[p]], axis=0) for p in pairs]
        if gc % PAIR == 0:
            a_all = [_nt(probes[p], jnp.concatenate([bd_s[p], kd_s[p]], axis=0)) for p in pairs]
            a_kb = [jnp.where(strict, t[0:gc, 0:gc], 0.0) for t in a_all]
            a_kk = [jnp.where(strict, t[0:gc, gc:2 * gc], 0.0).astype(BF16) for t in a_all]
            a_rb = [jnp.where(incl, t[gc:2 * gc, 0:gc], 0.0).astype(BF16) for t in a_all]
            a_rk = [jnp.where(incl, t[gc:2 * gc, gc:2 * gc], 0.0).astype(BF16) for t in a_all]
        else:
            a_kb = [jnp.where(strict, _nt(kq_s[p], bd_s[p]), 0.0) for p in pairs]
            a_kk = [jnp.where(strict, _nt(kq_s[p], kd_s[p]), 0.0).astype(BF16) for p in pairs]
            a_rb = [jnp.where(incl, _nt(rq_s[p], bd_s[p]), 0.0).astype(BF16) for p in pairs]
            a_rk = [jnp.where(incl, _nt(rq_s[p], kd_s[p]), 0.0).astype(BF16) for p in pairs]
        inv = [eye - t for t in a_kb]
        if squarings >= 1:
            apow_b = [t.astype(BF16) for t in a_kb]
            apow_b = [_mm(t, t).astype(BF16) for t in apow_b]
            for _ in range(squarings - 1):
                both = [_mm(jnp.concatenate([apow_b[p], inv[p].astype(BF16)], axis=0), apow_b[p]) for p in pairs]
                apow_b = [t[0:gc].astype(BF16) for t in both]
                inv = [inv[p] + both[p][gc:2 * gc] for p in pairs]
            inv = [inv[p] + _mm(inv[p].astype(BF16), apow_b[p]) for p in pairs]
        chunks.append(dict(
            probes=probes, a_kk=a_kk, a_rk=a_rk, a_rb=a_rb, inv=[t.astype(BF16) for t in inv],
            v_s=[stack(xv_c[:, sl]) for sl in lanes],
            upd=[jnp.concatenate([stack(kdp[:, sl]), stack(bdp[:, sl])], axis=0) for sl in lanes],
            p_last=jnp.exp(cum_last)))

    state = [s_ref[p] for p in pairs]
    y_rows = []
    for ch in chunks:
        s_b = [s.astype(BF16) for s in state]
        state_t = [_nt(ch["probes"][p], s_b[p]) for p in pairs]
        rhs = [state_t[p][0:gc] + _mm(ch["a_kk"][p], ch["v_s"][p]) for p in pairs]
        u_b = [_mm(ch["inv"][p], rhs[p].astype(BF16)).astype(BF16) for p in pairs]
        if gc % PAIR == 0:
            y_s = [state_t[p][gc:2 * gc] + _mm(jnp.concatenate([ch["a_rk"][p], -ch["a_rb"][p]], axis=1),
                                               jnp.concatenate([ch["v_s"][p], u_b[p]], axis=0)) for p in pairs]
        else:
            y_s = [state_t[p][gc:2 * gc] + _mm(ch["a_rk"][p], ch["v_s"][p]) - _mm(ch["a_rb"][p], u_b[p])
                   for p in pairs]
        state = [state[p] * ch["p_last"][:, lanes[p]]
                 + _tn(jnp.concatenate([ch["v_s"][p], -u_b[p]], axis=0), ch["upd"][p]) for p in pairs]
        y_rows.append(jnp.concatenate([t[0:cs] + t[cs:gc] for t in y_s], axis=1))
    for p in pairs:
        s_ref[p] = state[p]

    y = jnp.concatenate(y_rows, axis=0) if n_sub > 1 else y_rows[0]
    mean = _seg_sum(y, bones) * (1.0 / hd)
    d = y - mean
    var = _seg_sum(d * d, bones) * (1.0 / hd)
    yn = d * lax.rsqrt(var + GN_EPS) * lnw_ref[...] + lnb_ref[...]
    bonus = _seg_sum(xr * k2 * rk_ref[...], bones) * xv
    y_ref[...] = (yn + bonus) * gate

    @pl.when(step == n_steps - 1)
    def _fin():
        for p in range(N_PAIRS):
            so_ref[0, 2 * p] = s_ref[p, 0:hd, 0:hd]
            so_ref[0, 2 * p + 1] = s_ref[p, hd:2 * hd, hd:2 * hd]


def _wkv(proj, prev0, s0, params, wl, bones, batch, seq, chunk, n_sub):
    rows = chunk * n_sub
    ns = seq // rows
    rw = R_WIDTH

    def col(cb):
        return lambda b, c: (b * ns + c, cb)

    const2 = lambda b, c: (0, 0)
    vec = lambda n: pl.BlockSpec((1, n), const2)
    mu, mul, w0, a0, k_k, k_a, r_k, ln_w, ln_b = params
    return pl.pallas_call(
        functools.partial(_wkv_kernel, chunk=chunk, n_sub=n_sub, n_steps=ns),
        out_shape=(jax.ShapeDtypeStruct((batch * seq, rw), F32),
                   jax.ShapeDtypeStruct((batch, R_HEADS, R_HEAD, R_HEAD), F32)),
        grid=(batch, ns),
        in_specs=[
            pl.BlockSpec((rows, C_BLK), col(P_C // C_BLK)),
            pl.BlockSpec((rows, C_BLK), col(P_C // C_BLK + 1)),
            pl.BlockSpec((rows, LORA_PAD), col(P_L // LORA_PAD)),
            pl.BlockSpec((1, 1, SHIFT_COLS), lambda b, c: (b, 0, 0)),
            pl.BlockSpec((1, R_HEADS, R_HEAD, R_HEAD), lambda b, c: (b, 0, 0, 0)),
            vec(3 * rw), vec(LORA_PAD), vec(rw), vec(rw), vec(rw), vec(rw), vec(rw), vec(rw), vec(rw),
            pl.BlockSpec((LORA_PAD, 3 * rw), const2),
            pl.BlockSpec((SEG, SEG), const2),
        ],
        out_specs=(
            pl.BlockSpec((rows, rw), lambda b, c: (b * ns + c, 0)),
            pl.BlockSpec((1, R_HEADS, R_HEAD, R_HEAD), lambda b, c: (b, 0, 0, 0)),
        ),
        scratch_shapes=[pltpu.VMEM((N_PAIRS, PAIR, PAIR), F32), pltpu.VMEM((1, SHIFT_COLS), F32)],
        compiler_params=pltpu.CompilerParams(
            dimension_semantics=("arbitrary", "arbitrary"), vmem_limit_bytes=VMEM_LIMIT),
        name="wkv",
    )(proj, proj, proj, prev0, s0, mu, mul, w0, a0, k_k, k_a, r_k, ln_w, ln_b, wl, bones)


def _mix_kernel(x_ref, ya_ref, yr_ref, g_ref, wg_ref, wba_ref, wbr_ref, wo_ref, o_ref):
    x = x_ref[...]
    h = _rms(x, g_ref[...]).astype(BF16)
    ya = ya_ref[...].astype(BF16)
    yr = yr_ref[...].astype(BF16)
    acc = x
    for c in range(D_MODEL // MIX_CHUNK):
        ca = slice(c * MIX_CHUNK, (c + 1) * MIX_CHUNK)
        cr = slice(D_MODEL + c * MIX_CHUNK, D_MODEL + (c + 1) * MIX_CHUNK)
        mixed = (jax.nn.sigmoid(_mm(h, wg_ref[:, ca])) * _mm(ya, wba_ref[:, ca])
                 + jax.nn.sigmoid(_mm(h, wg_ref[:, cr])) * _mm(yr, wbr_ref[:, ca]))
        acc = acc + _mm(mixed.astype(BF16), wo_ref[ca, :])
    o_ref[...] = acc


def _resident(shape):
    return pl.BlockSpec(shape, lambda *_: (0,) * len(shape), pipeline_mode=pl.Buffered(1))


def _mix(x, ya, yr, g, wg, wba, wbr, wo, tm):
    m = x.shape[0]
    row = lambda i: (i, 0)
    return pl.pallas_call(
        _mix_kernel,
        out_shape=jax.ShapeDtypeStruct((m, D_MODEL), F32),
        grid=(m // tm,),
        in_specs=[
            pl.BlockSpec((tm, D_MODEL), row),
            pl.BlockSpec((tm, ATTN_WIDTH), row),
            pl.BlockSpec((tm, R_WIDTH), row),
            pl.BlockSpec((1, D_MODEL), lambda i: (0, 0)),
            _resident((D_MODEL, 2 * D_MODEL)),
            _resident((ATTN_WIDTH, D_MODEL)),
            _resident((R_WIDTH, D_MODEL)),
            _resident((D_MODEL, D_MODEL)),
        ],
        out_specs=pl.BlockSpec((tm, D_MODEL), row),
        compiler_params=pltpu.CompilerParams(
            dimension_semantics=("arbitrary",), vmem_limit_bytes=VMEM_LIMIT),
        name="mix",
    )(x, ya, yr, g, wg, wba, wbr, wo)


def _ffn_kernel(x_ref, g_ref, wu_ref, wd_ref, o_ref, h_ref):
    @pl.when(pl.program_id(1) == 0)
    def _():
        x = x_ref[...]
        h_ref[...] = _rms(x, g_ref[...]).astype(BF16)
        o_ref[...] = x

    u = _mm(h_ref[...], wu_ref[...])
    o_ref[...] += _mm(jnp.square(jnp.maximum(u, 0.0)).astype(BF16), wd_ref[...])


def _ffn(x, g, wu, wd, tm, tk):
    m = x.shape[0]
    return pl.pallas_call(
        _ffn_kernel,
        out_shape=jax.ShapeDtypeStruct((m, D_MODEL), F32),
        grid=(m // tm, D_FF // tk),
        in_specs=[
            pl.BlockSpec((tm, D_MODEL), lambda i, k: (i, 0)),
            pl.BlockSpec((1, D_MODEL), lambda i, k: (0, 0)),
            pl.BlockSpec((D_MODEL, tk), lambda i, k: (0, k)),
            pl.BlockSpec((tk, D_MODEL), lambda i, k: (k, 0)),
        ],
        out_specs=pl.BlockSpec((tm, D_MODEL), lambda i, k: (i, 0)),
        scratch_shapes=[pltpu.VMEM((tm, D_MODEL), BF16)],
        compiler_params=pltpu.CompilerParams(
            dimension_semantics=("arbitrary", "arbitrary"), vmem_limit_bytes=VMEM_LIMIT),
        name="ffn",
    )(x, g, wu, wd)


def _ple_kernel(x_ref, pe_ref, wg_ref, wp_ref, g_ref, o_ref):
    x = x_ref[...]
    gate = jax.nn.sigmoid(_mm(x.astype(BF16), wg_ref[...]))
    x = x + gate * _mm(pe_ref[...].astype(BF16), wp_ref[...])
    o_ref[...] = _rms(x, g_ref[...])


def _ple(x, pe, wg, wp, g, tm):
    m = x.shape[0]
    row = lambda i: (i, 0)
    return pl.pallas_call(
        _ple_kernel,
        out_shape=jax.ShapeDtypeStruct((m, D_MODEL), F32),
        grid=(m // tm,),
        in_specs=[
            pl.BlockSpec((tm, D_MODEL), row),
            pl.BlockSpec((tm, PLE_DIM), row),
            _resident((D_MODEL, D_MODEL)),
            _resident((PLE_DIM, D_MODEL)),
            pl.BlockSpec((1, D_MODEL), lambda i: (0, 0)),
        ],
        out_specs=pl.BlockSpec((tm, D_MODEL), row),
        compiler_params=pltpu.CompilerParams(
            dimension_semantics=("arbitrary",), vmem_limit_bytes=VMEM_LIMIT),
        name="ple",
    )(x, pe, wg, wp, g)


def _layer(x, pe, attend, prev0, s0, wts, batch, seq, tiles):
    (g_mix, w_in, w_gates, wkv_params, wl, bones, wba, wbr, wo, g_ffn, wu, wd, wg, wp, g_fin) = wts
    proj = _proj(x, g_mix, w_in, tiles["proj_m"], tiles["proj_n"])
    ya = attend(proj)
    yr, s_new = _wkv(proj, prev0, s0, wkv_params, wl, bones, batch, seq, tiles["chunk"], tiles["n_sub"])
    x = _mix(x, ya, yr, g_mix, w_gates, wba, wbr, wo, tiles["mix_m"])
    x = _ffn(x, g_ffn, wu, wd, tiles["ffn_m"], tiles["ffn_k"])
    y = _ple(x, pe, wg, wp, g_fin, tiles["ple_m"])
    return y, proj, s_new


def _shift_out(proj, batch, seq):
    last = proj.reshape(batch, seq, P_COLS)[:, -1]
    return last[:, P_C:P_C + RWKV_COLS][None]


def kernel(x_prompt, x_sample, cache_k_win, cache_v_win, state_wkv, state_shift, p_prompt, p_sample,
           norm_mix, w_in, attn_sinks, rwkv_mu, rwkv_w0, rwkv_w2, rwkv_a0, rwkv_a2, rwkv_g2,
           rwkv_k_k, rwkv_k_a, rwkv_r_k, rwkv_ln_w, rwkv_ln_b, w_branch_attn, w_branch_rwkv,
           w_out, norm_ffn, w_ff_up, w_ff_down, w_ple_proj, w_ple_gate, norm_final):
    assert w_in.shape[0] == 1, "single-layer step"
    bp, tp = x_prompt.shape[0], x_prompt.shape[1]
    bs, ts = x_sample.shape[0], x_sample.shape[1]
    rw = R_WIDTH

    w_in_p = w_in[0].astype(BF16)
    w_gates = w_in_p[:, P_GATES:P_GATES + 2 * D_MODEL]
    wl = jnp.zeros((LORA_PAD, 3 * rw), F32)
    wl = wl.at[0:DECAY_LORA, 0:rw].set(rwkv_w2[0])
    wl = wl.at[DECAY_LORA:DECAY_LORA + AAA_LORA, rw:2 * rw].set(rwkv_a2[0])
    wl = wl.at[DECAY_LORA + AAA_LORA:LORA_COLS, 2 * rw:3 * rw].set(rwkv_g2[0])
    wl = wl.astype(BF16)
    seg_id = np.arange(SEG) // R_HEAD
    bones = jnp.asarray(seg_id[:, None] == seg_id[None, :], BF16)
    mu = rwkv_mu[0]
    row = lambda v: v.reshape(1, -1)
    wkv_params = (row(mu[:3 * rw]), row(jnp.pad(mu[3 * rw:], (0, LORA_PAD - LORA_COLS))),
                  row(rwkv_w0[0]), row(rwkv_a0[0]), row(rwkv_k_k[0]), row(rwkv_k_a[0]),
                  row(rwkv_r_k[0]), row(rwkv_ln_w[0]), row(rwkv_ln_b[0]))
    wts = (row(norm_mix[0]), w_in_p, w_gates, wkv_params, wl, bones,
           w_branch_attn[0].astype(BF16), w_branch_rwkv[0].astype(BF16), w_out[0].astype(BF16),
           row(norm_ffn[0]), w_ff_up[0].astype(BF16), w_ff_down[0].astype(BF16),
           w_ple_gate[0].astype(BF16), w_ple_proj[0].astype(BF16), row(norm_final))
    sinks = attn_sinks[0]

    tiles_p = dict(proj_m=1024, proj_n=1024, chunk=64, n_sub=4, mix_m=256, ffn_m=1024, ffn_k=512, ple_m=512)
    yp, proj_p, s_p = _layer(
        x_prompt.reshape(bp * tp, D_MODEL), p_prompt[0].reshape(bp * tp, PLE_DIM),
        lambda pr: _attn_prompt(pr, sinks, bp, tp),
        jnp.zeros((bp, 1, SHIFT_COLS), F32), jnp.zeros((bp, R_HEADS, R_HEAD, R_HEAD), F32),
        wts, bp, tp, tiles_p)
    pp3 = proj_p.reshape(bp, tp, P_COLS)[:, -WINDOW:]
    k_p = pp3[:, :, P_K:P_K + KV_WIDTH].reshape(1, bp, WINDOW, N_KV_HEADS, HEAD_DIM)
    v_p = pp3[:, :, P_V:P_V + KV_WIDTH].reshape(1, bp, WINDOW, N_KV_HEADS, HEAD_DIM)

    ms = bs * ts
    tiles_s = dict(proj_m=ms, proj_n=1024, chunk=ts, n_sub=1, mix_m=ms, ffn_m=ms, ffn_k=512, ple_m=ms)
    win_out = {}

    def attend_s(pr):
        ya, nk, nv = _attn_sample(pr, sinks, cache_k_win[0].reshape(bs, WINDOW, KV_WIDTH),
                                  cache_v_win[0].reshape(bs, WINDOW, KV_WIDTH), bs, ts)
        win_out["k"], win_out["v"] = nk, nv
        return ya

    prev_s = jnp.pad(state_shift[0], ((0, 0), (0, LORA_PAD - LORA_COLS))).reshape(bs, 1, SHIFT_COLS)
    ys, proj_s, s_s = _layer(
        x_sample.reshape(ms, D_MODEL), p_sample[0].reshape(ms, PLE_DIM), attend_s,
        prev_s, state_wkv[0], wts, bs, ts, tiles_s)

    return (yp.reshape(bp, tp, D_MODEL), ys.reshape(bs, ts, D_MODEL),
            k_p, v_p, s_p[None], _shift_out(proj_p, bp, tp),
            win_out["k"].reshape(1, bs, WINDOW, N_KV_HEADS, HEAD_DIM),
            win_out["v"].reshape(1, bs, WINDOW, N_KV_HEADS, HEAD_DIM),
            s_s[None], _shift_out(proj_s, bs, ts))
```

```python
import functools

import numpy as np
import jax
import jax.numpy as jnp
from jax import lax
from jax.experimental import pallas as pl
from jax.experimental.pallas import tpu as pltpu

F32 = jnp.float32
BF16 = jnp.bfloat16

D_MODEL = 2048
PLE_DIM = 256
HEAD_DIM = 64
N_Q_HEADS = 16
N_KV_HEADS = 4
GQA_GROUP = 4
ATTN_WIDTH = 1024
KV_WIDTH = 256
WINDOW = 128
ALIBI_MAX = 8.0
R_HEAD = 64
R_WIDTH = 1024
R_HEADS = 16
DECAY_LORA = 64
AAA_LORA = 64
GATE_LORA = 160
LORA_COLS = DECAY_LORA + AAA_LORA + GATE_LORA
LORA_PAD = 512
RWKV_COLS = 3 * R_WIDTH + LORA_COLS
D_FF = 4 * D_MODEL
NORM_EPS = 1e-6
GN_EPS = 64e-5

P_Q = 0
P_K = ATTN_WIDTH
P_V = ATTN_WIDTH + KV_WIDTH
P_C = ATTN_WIDTH + 2 * KV_WIDTH
P_L = P_C + 3 * R_WIDTH
P_COLS = P_L + LORA_PAD
P_GATES = P_C + RWKV_COLS
C_BLK = P_C
SHIFT_COLS = 3 * R_WIDTH + LORA_PAD

PAIR = 128
N_PAIRS = R_WIDTH // PAIR
SEG = 256
MIX_CHUNK = 512

VMEM_LIMIT = 56 * 1024 * 1024


def _mm(a, b):
    return jnp.dot(a, b, preferred_element_type=F32)


def _nt(a, b):
    return lax.dot_general(a, b, (((1,), (1,)), ((), ())), preferred_element_type=F32)


def _tn(a, b):
    return lax.dot_general(a, b, (((0,), (0,)), ((), ())), preferred_element_type=F32)


def _rms(x, g):
    ms = jnp.mean(x * x, axis=-1, keepdims=True)
    return x * lax.rsqrt(ms + NORM_EPS) * g


def _alibi_slope(hq):
    return float(2.0 ** (-ALIBI_MAX * (hq + 1) / N_Q_HEADS))


def _proj_kernel(x_ref, g_ref, w_ref, o_ref, h_ref):
    @pl.when(pl.program_id(1) == 0)
    def _():
        h_ref[...] = _rms(x_ref[...], g_ref[...]).astype(BF16)

    o_ref[...] = _mm(h_ref[...], w_ref[...])


def _proj_first_kernel(x_ref, g_ref, w_ref, o_ref, wb_ref, h_ref):
    @pl.when(pl.program_id(0) == 0)
    def _():
        h_ref[...] = _rms(x_ref[...], g_ref[...]).astype(BF16)

    wb = w_ref[...].astype(BF16)
    wb_ref[...] = wb
    o_ref[...] = _mm(h_ref[...], wb)


def _proj_first(x, g, w_f32, tm, tn):
    m = x.shape[0]
    return pl.pallas_call(
        _proj_first_kernel,
        out_shape=(jax.ShapeDtypeStruct((m, P_COLS), F32), jax.ShapeDtypeStruct((D_MODEL, P_COLS), BF16)),
        grid=(P_COLS // tn,),
        in_specs=[
            pl.BlockSpec((tm, D_MODEL), lambda j: (0, 0), pipeline_mode=pl.Buffered(1)),
            pl.BlockSpec((1, D_MODEL), lambda j: (0, 0)),
            pl.BlockSpec((D_MODEL, tn), lambda j: (0, j)),
        ],
        out_specs=(pl.BlockSpec((tm, tn), lambda j: (0, j)), pl.BlockSpec((D_MODEL, tn), lambda j: (0, j))),
        scratch_shapes=[pltpu.VMEM((tm, D_MODEL), BF16)],
        compiler_params=pltpu.CompilerParams(
            dimension_semantics=("arbitrary",), vmem_limit_bytes=VMEM_LIMIT),
        name="proj_first",
    )(x, g, w_f32)


def _proj_rest_kernel(x_ref, g_ref, w_ref, done_ref, o_ref, h_ref):
    del done_ref
    _proj_kernel(x_ref, g_ref, w_ref, o_ref, h_ref)


def _proj_rest(x, g, w, proj_done, tm, tn):
    m = x.shape[0]
    return pl.pallas_call(
        _proj_rest_kernel,
        out_shape=jax.ShapeDtypeStruct((m, P_COLS), F32),
        grid=(m // tm - 1, P_COLS // tn),
        in_specs=[
            pl.BlockSpec((tm, D_MODEL), lambda i, j: (i + 1, 0)),
            pl.BlockSpec((1, D_MODEL), lambda i, j: (0, 0)),
            pl.BlockSpec((D_MODEL, tn), lambda i, j: (0, j)),
            pl.BlockSpec(memory_space=pl.ANY),
        ],
        out_specs=pl.BlockSpec((tm, tn), lambda i, j: (i + 1, j)),
        scratch_shapes=[pltpu.VMEM((tm, D_MODEL), BF16)],
        input_output_aliases={3: 0},
        compiler_params=pltpu.CompilerParams(
            dimension_semantics=("arbitrary", "arbitrary"), vmem_limit_bytes=VMEM_LIMIT),
        name="proj_rest",
    )(x, g, w, proj_done)


def _proj(x, g, w, tm, tn):
    m = x.shape[0]
    return pl.pallas_call(
        _proj_kernel,
        out_shape=jax.ShapeDtypeStruct((m, P_COLS), F32),
        grid=(m // tm, P_COLS // tn),
        in_specs=[
            pl.BlockSpec((tm, D_MODEL), lambda i, j: (i, 0)),
            pl.BlockSpec((1, D_MODEL), lambda i, j: (0, 0)),
            pl.BlockSpec((D_MODEL, tn), lambda i, j: (0, j)),
        ],
        out_specs=pl.BlockSpec((tm, tn), lambda i, j: (i, j)),
        scratch_shapes=[pltpu.VMEM((tm, D_MODEL), BF16)],
        compiler_params=pltpu.CompilerParams(
            dimension_semantics=("arbitrary", "arbitrary"), vmem_limit_bytes=VMEM_LIMIT),
        name="proj",
    )(x, g, w)


def _attend_heads(scores, values, sinks):
    heads = range(len(scores))
    m = []
    for h in heads:
        mh = sinks[h]
        for s in scores[h]:
            mh = jnp.maximum(mh, jnp.max(s, axis=-1, keepdims=True))
        m.append(mh)
    ps = [[jnp.exp(s - m[h]) for s in scores[h]] for h in heads]
    den = []
    for h in heads:
        dh = jnp.exp(sinks[h] - m[h])
        for p in ps[h]:
            dh = dh + jnp.sum(p, axis=-1, keepdims=True)
        den.append(dh)
    outs = []
    for h in heads:
        o = None
        for p, v in zip(ps[h], values[h]):
            t = _mm(p.astype(BF16), v)
            o = t if o is None else o + t
        outs.append(o * (1.0 / den[h]))
    return outs


def _head_slices():
    q_sl = [slice(hq * HEAD_DIM, (hq + 1) * HEAD_DIM) for hq in range(N_Q_HEADS)]
    kv_sl = [slice((hq // GQA_GROUP) * HEAD_DIM, (hq // GQA_GROUP + 1) * HEAD_DIM) for hq in range(N_Q_HEADS)]
    return q_sl, kv_sl


def _alibi_bias(dist, valid):
    slopes = np.array([_alibi_slope(hq) for hq in range(N_Q_HEADS)], np.float32)
    return np.where(valid[None], -slopes[:, None, None] * dist[None].astype(np.float32), -np.inf).astype(np.float32)


def _attn_prompt_kernel(sink_ref, bias_ref, q_ref, kp_ref, kc_ref, vp_ref, vc_ref, o_ref):
    w = WINDOW
    kj = lax.broadcasted_iota(jnp.int32, (w, 2 * w), 1)
    no_prev = (kj < w) & (pl.program_id(1) == 0)
    q = q_ref[...] * (HEAD_DIM ** -0.5)
    k2 = jnp.concatenate([kp_ref[...], kc_ref[...]], axis=0).astype(BF16)
    v2 = jnp.concatenate([vp_ref[...], vc_ref[...]], axis=0).astype(BF16)
    hqs = range(N_Q_HEADS)
    q_sl, kv_sl = _head_slices()
    qs = [q[:, q_sl[hq]].astype(BF16) for hq in hqs]
    sc = [[jnp.where(no_prev, -jnp.inf, _nt(qs[hq], k2[:, kv_sl[hq]]) + bias_ref[hq])] for hq in hqs]
    outs = _attend_heads(sc, [[v2[:, kv_sl[hq]]] for hq in hqs], [sink_ref[hq] for hq in hqs])
    for hq in hqs:
        o_ref[:, q_sl[hq]] = outs[hq]


def _attn_prompt(proj, sinks, batch, seq):
    nb = seq // WINDOW
    kcol, vcol = P_K // KV_WIDTH, P_V // KV_WIDTH

    def cur(c):
        return lambda b, i: (b * nb + i, c)

    def prev(c):
        return lambda b, i: (b * nb + jnp.maximum(i - 1, 0), c)

    ti = np.arange(WINDOW)[:, None]
    kj = np.arange(2 * WINDOW)[None, :]
    dist = ti - kj + WINDOW
    bias = _alibi_bias(dist, (dist >= 0) & (dist <= WINDOW))
    return pl.pallas_call(
        _attn_prompt_kernel,
        out_shape=jax.ShapeDtypeStruct((batch * seq, ATTN_WIDTH), F32),
        grid=(batch, nb),
        in_specs=[
            pl.BlockSpec(memory_space=pltpu.SMEM),
            _resident((N_Q_HEADS, WINDOW, 2 * WINDOW)),
            pl.BlockSpec((WINDOW, ATTN_WIDTH), cur(P_Q // ATTN_WIDTH)),
            pl.BlockSpec((WINDOW, KV_WIDTH), prev(kcol)),
            pl.BlockSpec((WINDOW, KV_WIDTH), cur(kcol)),
            pl.BlockSpec((WINDOW, KV_WIDTH), prev(vcol)),
            pl.BlockSpec((WINDOW, KV_WIDTH), cur(vcol)),
        ],
        out_specs=pl.BlockSpec((WINDOW, ATTN_WIDTH), lambda b, i: (b * nb + i, 0)),
        compiler_params=pltpu.CompilerParams(dimension_semantics=("arbitrary", "arbitrary")),
        name="attn_prompt",
    )(sinks, jnp.asarray(bias), proj, proj, proj, proj, proj)


def _attn_sample_kernel(sink_ref, q_ref, kn_ref, vn_ref, ck_ref, cv_ref, o_ref, nk_ref, nv_ref):
    t, w = q_ref.shape[0], WINDOW
    kn, vn = kn_ref[...], vn_ref[...]
    ck, cv = ck_ref[0], cv_ref[0]
    nk_ref[0, 0:w - t, :] = ck[t:, :]
    nk_ref[0, w - t:w, :] = kn
    nv_ref[0, 0:w - t, :] = cv[t:, :]
    nv_ref[0, w - t:w, :] = vn
    ti = lax.broadcasted_iota(jnp.int32, (t, w), 0)
    cj = lax.broadcasted_iota(jnp.int32, (t, w), 1)
    dist_c = (ti - cj + w).astype(F32)
    valid_c = cj >= ti
    ti2 = lax.broadcasted_iota(jnp.int32, (t, t), 0)
    tj2 = lax.broadcasted_iota(jnp.int32, (t, t), 1)
    dist_n = (ti2 - tj2).astype(F32)
    valid_n = tj2 <= ti2
    q = q_ref[...] * (HEAD_DIM ** -0.5)
    ckb, cvb, knb, vnb = ck.astype(BF16), cv.astype(BF16), kn.astype(BF16), vn.astype(BF16)
    hqs = range(N_Q_HEADS)
    q_sl, kv_sl = _head_slices()
    qs = [q[:, q_sl[hq]].astype(BF16) for hq in hqs]
    scores = [[_nt(qs[hq], ckb[:, kv_sl[hq]]) + jnp.where(valid_c, -_alibi_slope(hq) * dist_c, -jnp.inf),
               _nt(qs[hq], knb[:, kv_sl[hq]]) + jnp.where(valid_n, -_alibi_slope(hq) * dist_n, -jnp.inf)]
              for hq in hqs]
    outs = _attend_heads(scores, [[cvb[:, kv_sl[hq]], vnb[:, kv_sl[hq]]] for hq in hqs],
                         [sink_ref[hq] for hq in hqs])
    for hq in hqs:
        o_ref[:, q_sl[hq]] = outs[hq]


def _attn_sample(proj, sinks, cache_k, cache_v, batch, seq):
    kcol, vcol = P_K // KV_WIDTH, P_V // KV_WIDTH
    win = jax.ShapeDtypeStruct((batch, WINDOW, KV_WIDTH), F32)
    return pl.pallas_call(
        _attn_sample_kernel,
        out_shape=(jax.ShapeDtypeStruct((batch * seq, ATTN_WIDTH), F32), win, win),
        grid=(batch,),
        in_specs=[
            pl.BlockSpec(memory_space=pltpu.SMEM),
            pl.BlockSpec((seq, ATTN_WIDTH), lambda b: (b, P_Q // ATTN_WIDTH)),
            pl.BlockSpec((seq, KV_WIDTH), lambda b: (b, kcol)),
            pl.BlockSpec((seq, KV_WIDTH), lambda b: (b, vcol)),
            pl.BlockSpec((1, WINDOW, KV_WIDTH), lambda b: (b, 0, 0)),
            pl.BlockSpec((1, WINDOW, KV_WIDTH), lambda b: (b, 0, 0)),
        ],
        out_specs=(
            pl.BlockSpec((seq, ATTN_WIDTH), lambda b: (b, 0)),
            pl.BlockSpec((1, WINDOW, KV_WIDTH), lambda b: (b, 0, 0)),
            pl.BlockSpec((1, WINDOW, KV_WIDTH), lambda b: (b, 0, 0)),
        ),
        compiler_params=pltpu.CompilerParams(dimension_semantics=("arbitrary",)),
        name="attn_sample",
    )(sinks, proj, proj, proj, cache_k, cache_v)


def _seg_sum(x, bones):
    rows = x.shape[0]
    hi = x.astype(BF16).astype(F32)
    lo = x - hi
    groups = [slice(j * SEG, (j + 1) * SEG) for j in range(R_WIDTH // SEG)]
    lhs = jnp.concatenate([t[:, sl] for sl in groups for t in (hi, lo)], axis=0).astype(BF16)
    out = _mm(lhs, bones)
    return jnp.concatenate(
        [out[2 * j * rows:(2 * j + 1) * rows] + out[(2 * j + 1) * rows:(2 * j + 2) * rows]
         for j in range(len(groups))], axis=1)


def _wkv_kernel(pa_ref, pb_ref, pl_ref, prev_ref, s0_ref,
                mu_ref, mul_ref, w0_ref, a0_ref, kk_ref, ka_ref, rk_ref, lnw_ref, lnb_ref,
                wl_ref, bones_ref, *rest, chunk, n_sub, n_steps, cast_lanes):
    n_cast = len(cast_lanes)
    cast_src, (y_ref, so_ref) = rest[:n_cast], rest[n_cast:n_cast + 2]
    cast_dst, (s_ref, carry_ref) = rest[n_cast + 2:2 * n_cast + 2], rest[2 * n_cast + 2:]
    for src, dst, lanes in zip(cast_src, cast_dst, cast_lanes):
        dst[...] = (src[...] if lanes is None else src[:, lanes[0]:lanes[1]]).astype(BF16)

    step = pl.program_id(1)
    cs = chunk
    rows = n_sub * cs
    gc = 2 * cs
    hd = R_HEAD
    w = R_WIDTH

    @pl.when(step == 0)
    def _init():
        carry_ref[...] = prev_ref[0]
        s_ref[...] = jnp.zeros(s_ref.shape, F32)
        for p in range(N_PAIRS):
            s_ref[p, 0:hd, 0:hd] = s0_ref[0, 2 * p]
            s_ref[p, hd:2 * hd, hd:2 * hd] = s0_ref[0, 2 * p + 1]

    row = lax.broadcasted_iota(jnp.int32, (rows, 1), 0)

    def token_shift(x, prev_row, mu):
        shifted = jnp.where(row == 0, prev_row, pltpu.roll(x, 1, axis=0))
        return x + (shifted - x) * mu

    cols = jnp.concatenate([pa_ref[...], pb_ref[...]], axis=1)
    l_raw = pl_ref[...]
    xx = token_shift(cols, carry_ref[:, 0:3 * w], mu_ref[...])
    xr, xk, xv = xx[:, 0:w], xx[:, w:2 * w], xx[:, 2 * w:3 * w]
    xl = token_shift(l_raw, carry_ref[:, 3 * w:], mul_ref[...])
    carry_ref[:, 0:3 * w] = cols[rows - 1:rows, :]
    carry_ref[:, 3 * w:] = l_raw[rows - 1:rows, :]

    lane_l = lax.broadcasted_iota(jnp.int32, (1, LORA_PAD), 1)
    act = jnp.where(lane_l < DECAY_LORA, jnp.tanh(xl),
                    jnp.where(lane_l < DECAY_LORA + AAA_LORA, xl,
                              jnp.where(lane_l < LORA_COLS, jax.nn.sigmoid(xl), 0.0)))
    up = _mm(act.astype(BF16), wl_ref[...])
    z = -(w0_ref[...] + up[:, 0:w])
    softplus = jnp.maximum(z, 0.0) + jnp.log1p(jnp.exp(-jnp.abs(z)))
    lwd = -jnp.exp(-softplus - 0.5)
    a = jax.nn.sigmoid(a0_ref[...] + up[:, w:2 * w])
    gate = up[:, 2 * w:3 * w]

    bones = bones_ref[...]
    kkn = xk * kk_ref[...]
    kk = kkn * lax.rsqrt(jnp.maximum(_seg_sum(kkn * kkn, bones), 1e-24))
    k2 = xk * (1.0 + (a - 1.0) * ka_ref[...])
    b = kk * a

    tri = (lax.broadcasted_iota(jnp.int32, (cs, cs), 0)
           >= lax.broadcasted_iota(jnp.int32, (cs, cs), 1)).astype(F32)
    ri = lax.broadcasted_iota(jnp.int32, (gc, gc), 0)
    ci = lax.broadcasted_iota(jnp.int32, (gc, gc), 1)
    same_head = (ri >= cs) == (ci >= cs)
    strict = same_head & (ci < ri)
    incl = same_head & (ci <= ri)
    eye = (ri == ci).astype(F32)
    head0 = lax.broadcasted_iota(jnp.int32, (1, PAIR), 1) < hd

    def stack(x):
        return jnp.concatenate([jnp.where(head0, x, 0.0), jnp.where(head0, 0.0, x)], axis=0).astype(BF16)

    pairs = range(N_PAIRS)
    lanes = [slice(p * PAIR, (p + 1) * PAIR) for p in pairs]
    squarings = cs.bit_length() - 2

    chunks = []
    for sub in range(n_sub):
        rs = slice(sub * cs, (sub + 1) * cs)
        lw_c = lwd[rs]
        cum = jnp.dot(tri, lw_c, precision=lax.Precision.HIGHEST, preferred_element_type=F32)
        cum_last = cum[cs - 1:cs, :]
        e_inv = jnp.exp(-cum)
        e_last = jnp.exp(cum_last - cum)
        kq = kk[rs] * jnp.exp(cum - lw_c)
        rq = xr[rs] * jnp.exp(cum)
        kd = k2[rs] * e_inv
        bd = b[rs] * e_inv
        kdp = k2[rs] * e_last
        bdp = b[rs] * e_last
        xv_c = xv[rs]
        kq_s = [stack(kq[:, sl]) for sl in lanes]
        bd_s = [stack(bd[:, sl]) for sl in lanes]
        kd_s = [stack(kd[:, sl]) for sl in lanes]
        rq_s = [stack(rq[:, sl]) for sl in lanes]
        probes = [jnp.concatenate([kq_s[p], rq_s[p]], axis=0) for p in pairs]
        if gc % PAIR == 0:
            a_all = [_nt(probes[p], jnp.concatenate([bd_s[p], kd_s[p]], axis=0)) for p in pairs]
            a_kb = [jnp.where(strict, t[0:gc, 0:gc], 0.0) for t in a_all]
            a_kk = [jnp.where(strict, t[0:gc, gc:2 * gc], 0.0).astype(BF16) for t in a_all]
            a_rb = [jnp.where(incl, t[gc:2 * gc, 0:gc], 0.0).astype(BF16) for t in a_all]
            a_rk = [jnp.where(incl, t[gc:2 * gc, gc:2 * gc], 0.0).astype(BF16) for t in a_all]
        else:
            a_kb = [jnp.where(strict, _nt(kq_s[p], bd_s[p]), 0.0) for p in pairs]
            a_kk = [jnp.where(strict, _nt(kq_s[p], kd_s[p]), 0.0).astype(BF16) for p in pairs]
            a_rb = [jnp.where(incl, _nt(rq_s[p], bd_s[p]), 0.0).astype(BF16) for p in pairs]
            a_rk = [jnp.where(incl, _nt(rq_s[p], kd_s[p]), 0.0).astype(BF16) for p in pairs]
        inv = [eye - t for t in a_kb]
        if squarings >= 1:
            apow_b = [t.astype(BF16) for t in a_kb]
            apow_b = [_mm(t, t).astype(BF16) for t in apow_b]
            for _ in range(squarings - 1):
                both = [_mm(jnp.concatenate([apow_b[p], inv[p].astype(BF16)], axis=0), apow_b[p]) for p in pairs]
                apow_b = [t[0:gc].astype(BF16) for t in both]
                inv = [inv[p] + both[p][gc:2 * gc] for p in pairs]
            inv = [inv[p] + _mm(inv[p].astype(BF16), apow_b[p]) for p in pairs]
        chunks.append(dict(
            probes=probes, a_kk=a_kk, a_rk=a_rk, a_rb=a_rb, inv=[t.astype(BF16) for t in inv],
            v_s=[stack(xv_c[:, sl]) for sl in lanes],
            upd=[jnp.concatenate([stack(kdp[:, sl]), stack(bdp[:, sl])], axis=0) for sl in lanes],
            p_last=jnp.exp(cum_last)))

    state = [s_ref[p] for p in pairs]
    y_rows = []
    for ch in chunks:
        s_b = [s.astype(BF16) for s in state]
        state_t = [_nt(ch["probes"][p], s_b[p]) for p in pairs]
        rhs = [state_t[p][0:gc] + _mm(ch["a_kk"][p], ch["v_s"][p]) for p in pairs]
        u_b = [_mm(ch["inv"][p], rhs[p].astype(BF16)).astype(BF16) for p in pairs]
        if gc % PAIR == 0:
            y_s = [state_t[p][gc:2 * gc] + _mm(jnp.concatenate([ch["a_rk"][p], -ch["a_rb"][p]], axis=1),
                                               jnp.concatenate([ch["v_s"][p], u_b[p]], axis=0)) for p in pairs]
        else:
            y_s = [state_t[p][gc:2 * gc] + _mm(ch["a_rk"][p], ch["v_s"][p]) - _mm(ch["a_rb"][p], u_b[p])
                   for p in pairs]
        state = [state[p] * ch["p_last"][:, lanes[p]]
                 + _tn(jnp.concatenate([ch["v_s"][p], -u_b[p]], axis=0), ch["upd"][p]) for p in pairs]
        y_rows.append(jnp.concatenate([t[0:cs] + t[cs:gc] for t in y_s], axis=1))
    for p in pairs:
        s_ref[p] = state[p]

    y = jnp.concatenate(y_rows, axis=0) if n_sub > 1 else y_rows[0]
    mean = _seg_sum(y, bones) * (1.0 / hd)
    d = y - mean
    var = _seg_sum(d * d, bones) * (1.0 / hd)
    yn = d * lax.rsqrt(var + GN_EPS) * lnw_ref[...] + lnb_ref[...]
    bonus = _seg_sum(xr * k2 * rk_ref[...], bones) * xv
    y_ref[...] = (yn + bonus) * gate

    @pl.when(step == n_steps - 1)
    def _fin():
        for p in range(N_PAIRS):
            so_ref[0, 2 * p] = s_ref[p, 0:hd, 0:hd]
            so_ref[0, 2 * p + 1] = s_ref[p, hd:2 * hd, hd:2 * hd]


def _wkv(proj, prev0, s0, params, wl, bones, batch, seq, chunk, n_sub, casts=()):
    rows = chunk * n_sub
    ns = seq // rows
    rw = R_WIDTH

    def col(cb):
        return lambda b, c: (b * ns + c, cb)

    const2 = lambda b, c: (0, 0)
    vec = lambda n: pl.BlockSpec((1, n), const2)
    cast_in, cast_out, cast_shapes = [], [], []
    for wgt, lanes in casts:
        blk = wgt.shape[0] // (batch * ns)
        assert blk * batch * ns == wgt.shape[0] and blk % 16 == 0, wgt.shape
        width = wgt.shape[1] if lanes is None else lanes[1] - lanes[0]
        cast_in.append(pl.BlockSpec((blk, wgt.shape[1]), col(0)))
        cast_out.append(pl.BlockSpec((blk, width), col(0)))
        cast_shapes.append(jax.ShapeDtypeStruct((wgt.shape[0], width), BF16))
    mu, mul, w0, a0, k_k, k_a, r_k, ln_w, ln_b = params
    outs = pl.pallas_call(
        functools.partial(_wkv_kernel, chunk=chunk, n_sub=n_sub, n_steps=ns,
                          cast_lanes=tuple(lanes for _, lanes in casts)),
        out_shape=(jax.ShapeDtypeStruct((batch * seq, rw), F32),
                   jax.ShapeDtypeStruct((batch, R_HEADS, R_HEAD, R_HEAD), F32), *cast_shapes),
        grid=(batch, ns),
        in_specs=[
            pl.BlockSpec((rows, C_BLK), col(P_C // C_BLK)),
            pl.BlockSpec((rows, C_BLK), col(P_C // C_BLK + 1)),
            pl.BlockSpec((rows, LORA_PAD), col(P_L // LORA_PAD)),
            pl.BlockSpec((1, 1, SHIFT_COLS), lambda b, c: (b, 0, 0)),
            pl.BlockSpec((1, R_HEADS, R_HEAD, R_HEAD), lambda b, c: (b, 0, 0, 0)),
            vec(3 * rw), vec(LORA_PAD), vec(rw), vec(rw), vec(rw), vec(rw), vec(rw), vec(rw), vec(rw),
            pl.BlockSpec((LORA_PAD, 3 * rw), const2),
            pl.BlockSpec((SEG, SEG), const2),
            *cast_in,
        ],
        out_specs=(
            pl.BlockSpec((rows, rw), lambda b, c: (b * ns + c, 0)),
            pl.BlockSpec((1, R_HEADS, R_HEAD, R_HEAD), lambda b, c: (b, 0, 0, 0)),
            *cast_out,
        ),
        scratch_shapes=[pltpu.VMEM((N_PAIRS, PAIR, PAIR), F32), pltpu.VMEM((1, SHIFT_COLS), F32)],
        compiler_params=pltpu.CompilerParams(
            dimension_semantics=("arbitrary", "arbitrary"), vmem_limit_bytes=VMEM_LIMIT),
        name="wkv",
    )(proj, proj, proj, prev0, s0, mu, mul, w0, a0, k_k, k_a, r_k, ln_w, ln_b, wl, bones,
      *[wgt for wgt, _ in casts])
    return outs[0], outs[1], outs[2:]


def _mix_kernel(x_ref, ya_ref, yr_ref, g_ref, wg_ref, wba_ref, wbr_ref, wo_ref, o_ref):
    x = x_ref[...]
    h = _rms(x, g_ref[...]).astype(BF16)
    ya = ya_ref[...].astype(BF16)
    yr = yr_ref[...].astype(BF16)
    acc = x
    for c in range(D_MODEL // MIX_CHUNK):
        ca = slice(c * MIX_CHUNK, (c + 1) * MIX_CHUNK)
        cr = slice(D_MODEL + c * MIX_CHUNK, D_MODEL + (c + 1) * MIX_CHUNK)
        mixed = (jax.nn.sigmoid(_mm(h, wg_ref[:, ca])) * _mm(ya, wba_ref[:, ca])
                 + jax.nn.sigmoid(_mm(h, wg_ref[:, cr])) * _mm(yr, wbr_ref[:, ca]))
        acc = acc + _mm(mixed.astype(BF16), wo_ref[ca, :])
    o_ref[...] = acc


def _resident(shape):
    return pl.BlockSpec(shape, lambda *_: (0,) * len(shape), pipeline_mode=pl.Buffered(1))


def _mix(x, ya, yr, g, wg, wba, wbr, wo, tm):
    m = x.shape[0]
    row = lambda i: (i, 0)
    return pl.pallas_call(
        _mix_kernel,
        out_shape=jax.ShapeDtypeStruct((m, D_MODEL), F32),
        grid=(m // tm,),
        in_specs=[
            pl.BlockSpec((tm, D_MODEL), row),
            pl.BlockSpec((tm, ATTN_WIDTH), row),
            pl.BlockSpec((tm, R_WIDTH), row),
            pl.BlockSpec((1, D_MODEL), lambda i: (0, 0)),
            _resident((D_MODEL, 2 * D_MODEL)),
            _resident((ATTN_WIDTH, D_MODEL)),
            _resident((R_WIDTH, D_MODEL)),
            _resident((D_MODEL, D_MODEL)),
        ],
        out_specs=pl.BlockSpec((tm, D_MODEL), row),
        compiler_params=pltpu.CompilerParams(
            dimension_semantics=("arbitrary",), vmem_limit_bytes=VMEM_LIMIT),
        name="mix",
    )(x, ya, yr, g, wg, wba, wbr, wo)


def _ffn_kernel(x_ref, g_ref, wu_ref, wd_ref, o_ref, h_ref):
    @pl.when(pl.program_id(1) == 0)
    def _():
        x = x_ref[...]
        h_ref[...] = _rms(x, g_ref[...]).astype(BF16)
        o_ref[...] = x

    u = _mm(h_ref[...], wu_ref[...])
    o_ref[...] += _mm(jnp.square(jnp.maximum(u, 0.0)).astype(BF16), wd_ref[...])


def _ffn(x, g, wu, wd, tm, tk):
    m = x.shape[0]
    return pl.pallas_call(
        _ffn_kernel,
        out_shape=jax.ShapeDtypeStruct((m, D_MODEL), F32),
        grid=(m // tm, D_FF // tk),
        in_specs=[
            pl.BlockSpec((tm, D_MODEL), lambda i, k: (i, 0)),
            pl.BlockSpec((1, D_MODEL), lambda i, k: (0, 0)),
            pl.BlockSpec((D_MODEL, tk), lambda i, k: (0, k)),
            pl.BlockSpec((tk, D_MODEL), lambda i, k: (k, 0)),
        ],
        out_specs=pl.BlockSpec((tm, D_MODEL), lambda i, k: (i, 0)),
        scratch_shapes=[pltpu.VMEM((tm, D_MODEL), BF16)],
        compiler_params=pltpu.CompilerParams(
            dimension_semantics=("arbitrary", "arbitrary"), vmem_limit_bytes=VMEM_LIMIT),
        name="ffn",
    )(x, g, wu, wd)


def _ple_kernel(x_ref, pe_ref, wg_ref, wp_ref, g_ref, o_ref):
    x = x_ref[...]
    gate = jax.nn.sigmoid(_mm(x.astype(BF16), wg_ref[...]))
    x = x + gate * _mm(pe_ref[...].astype(BF16), wp_ref[...])
    o_ref[...] = _rms(x, g_ref[...])


def _ple(x, pe, wg, wp, g, tm):
    m = x.shape[0]
    row = lambda i: (i, 0)
    return pl.pallas_call(
        _ple_kernel,
        out_shape=jax.ShapeDtypeStruct((m, D_MODEL), F32),
        grid=(m // tm,),
        in_specs=[
            pl.BlockSpec((tm, D_MODEL), row),
            pl.BlockSpec((tm, PLE_DIM), row),
            _resident((D_MODEL, D_MODEL)),
            _resident((PLE_DIM, D_MODEL)),
            pl.BlockSpec((1, D_MODEL), lambda i: (0, 0)),
        ],
        out_specs=pl.BlockSpec((tm, D_MODEL), row),
        compiler_params=pltpu.CompilerParams(
            dimension_semantics=("arbitrary",), vmem_limit_bytes=VMEM_LIMIT),
        name="ple",
    )(x, pe, wg, wp, g)


def _dense_tail(x, ya, yr, pe, wts, tiles):
    (g_mix, w_gates, wba, wbr, wo, g_ffn, wu, wd, wg, wp, g_fin) = wts
    x = _mix(x, ya, yr, g_mix, w_gates, wba, wbr, wo, tiles["mix_m"])
    x = _ffn(x, g_ffn, wu, wd, tiles["ffn_m"], tiles["ffn_k"])
    return _ple(x, pe, wg, wp, g_fin, tiles["ple_m"])


def _shift_out(proj, batch, seq):
    last = proj.reshape(batch, seq, P_COLS)[:, -1]
    return last[:, P_C:P_C + RWKV_COLS][None]


def kernel(x_prompt, x_sample, cache_k_win, cache_v_win, state_wkv, state_shift, p_prompt, p_sample,
           norm_mix, w_in, attn_sinks, rwkv_mu, rwkv_w0, rwkv_w2, rwkv_a0, rwkv_a2, rwkv_g2,
           rwkv_k_k, rwkv_k_a, rwkv_r_k, rwkv_ln_w, rwkv_ln_b, w_branch_attn, w_branch_rwkv,
           w_out, norm_ffn, w_ff_up, w_ff_down, w_ple_proj, w_ple_gate, norm_final):
    assert w_in.shape[0] == 1, "single-layer step"
    bp, tp = x_prompt.shape[0], x_prompt.shape[1]
    bs, ts = x_sample.shape[0], x_sample.shape[1]
    rw = R_WIDTH

    wl = jnp.zeros((LORA_PAD, 3 * rw), F32)
    wl = wl.at[0:DECAY_LORA, 0:rw].set(rwkv_w2[0])
    wl = wl.at[DECAY_LORA:DECAY_LORA + AAA_LORA, rw:2 * rw].set(rwkv_a2[0])
    wl = wl.at[DECAY_LORA + AAA_LORA:LORA_COLS, 2 * rw:3 * rw].set(rwkv_g2[0])
    wl = wl.astype(BF16)
    seg_id = np.arange(SEG) // R_HEAD
    bones = jnp.asarray(seg_id[:, None] == seg_id[None, :], BF16)
    mu = rwkv_mu[0]
    row = lambda v: v.reshape(1, -1)
    wkv_params = (row(mu[:3 * rw]), row(jnp.pad(mu[3 * rw:], (0, LORA_PAD - LORA_COLS))),
                  row(rwkv_w0[0]), row(rwkv_a0[0]), row(rwkv_k_k[0]), row(rwkv_k_a[0]),
                  row(rwkv_r_k[0]), row(rwkv_ln_w[0]), row(rwkv_ln_b[0]))
    g_mix = row(norm_mix[0])
    sinks = attn_sinks[0]

    tiles_p = dict(mix_m=256, ffn_m=1024, ffn_k=512, ple_m=512)
    xp = x_prompt.reshape(bp * tp, D_MODEL)
    proj_p, w_in_b = _proj_first(xp, g_mix, w_in[0], 1024, 1024)
    proj_p = _proj_rest(xp, g_mix, w_in_b, proj_p, 1024, 1024)
    ya_p = _attn_prompt(proj_p, sinks, bp, tp)
    yr_p, s_p, (wu, wd, wo, wg, wba, wbr, w_gates) = _wkv(
        proj_p, jnp.zeros((bp, 1, SHIFT_COLS), F32), jnp.zeros((bp, R_HEADS, R_HEAD, R_HEAD), F32),
        wkv_params, wl, bones, bp, tp, 64, 4,
        casts=((w_ff_up[0], None), (w_ff_down[0], None), (w_out[0], None), (w_ple_gate[0], None),
               (w_branch_attn[0], None), (w_branch_rwkv[0], None), (w_in[0], (P_GATES, P_GATES + 2 * D_MODEL))))
    dense = (g_mix, w_gates, wba, wbr, wo, row(norm_ffn[0]), wu, wd, wg, w_ple_proj[0].astype(BF16),
             row(norm_final))
    yp = _dense_tail(xp, ya_p, yr_p, p_prompt[0].reshape(bp * tp, PLE_DIM), dense, tiles_p)
    pp3 = proj_p.reshape(bp, tp, P_COLS)[:, -WINDOW:]
    k_p = pp3[:, :, P_K:P_K + KV_WIDTH].reshape(1, bp, WINDOW, N_KV_HEADS, HEAD_DIM)
    v_p = pp3[:, :, P_V:P_V + KV_WIDTH].reshape(1, bp, WINDOW, N_KV_HEADS, HEAD_DIM)

    ms = bs * ts
    tiles_s = dict(mix_m=ms, ffn_m=ms, ffn_k=512, ple_m=ms)
    xs = x_sample.reshape(ms, D_MODEL)
    proj_s = _proj(xs, g_mix, w_in_b, ms, 1024)
    ya_s, nk_s, nv_s = _attn_sample(proj_s, sinks, cache_k_win[0].reshape(bs, WINDOW, KV_WIDTH),
                                    cache_v_win[0].reshape(bs, WINDOW, KV_WIDTH), bs, ts)
    prev_s = jnp.pad(state_shift[0], ((0, 0), (0, LORA_PAD - LORA_COLS))).reshape(bs, 1, SHIFT_COLS)
    yr_s, s_s, _ = _wkv(proj_s, prev_s, state_wkv[0], wkv_params, wl, bones, bs, ts, ts, 1)
    ys = _dense_tail(xs, ya_s, yr_s, p_sample[0].reshape(ms, PLE_DIM), dense, tiles_s)

    return (yp.reshape(bp, tp, D_MODEL), ys.reshape(bs, ts, D_MODEL),
            k_p, v_p, s_p[None], _shift_out(proj_p, bp, tp),
            nk_s.reshape(1, bs, WINDOW, N_KV_HEADS, HEAD_DIM), nv_s.reshape(1, bs, WINDOW, N_KV_HEADS, HEAD_DIM),
            s_s[None], _shift_out(proj_s, bs, ts))
```

```python
import functools

import numpy as np
import jax
import jax.numpy as jnp
from jax import lax
from jax.experimental import pallas as pl
from jax.experimental.pallas import tpu as pltpu

F32 = jnp.float32
BF16 = jnp.bfloat16

D_MODEL = 2048
PLE_DIM = 256
HEAD_DIM = 64
N_Q_HEADS = 16
N_KV_HEADS = 4
GQA_GROUP = 4
ATTN_WIDTH = 1024
KV_WIDTH = 256
WINDOW = 128
ALIBI_MAX = 8.0
R_HEAD = 64
R_WIDTH = 1024
R_HEADS = 16
DECAY_LORA = 64
AAA_LORA = 64
GATE_LORA = 160
LORA_COLS = DECAY_LORA + AAA_LORA + GATE_LORA
LORA_PAD = 512
RWKV_COLS = 3 * R_WIDTH + LORA_COLS
D_FF = 4 * D_MODEL
NORM_EPS = 1e-6
GN_EPS = 64e-5

P_Q = 0
P_K = ATTN_WIDTH
P_V = ATTN_WIDTH + KV_WIDTH
P_C = ATTN_WIDTH + 2 * KV_WIDTH
P_L = P_C + 3 * R_WIDTH
P_COLS = P_L + LORA_PAD
P_GATES = P_C + RWKV_COLS
C_BLK = P_C
SHIFT_COLS = 3 * R_WIDTH + LORA_PAD

PAIR = 128
N_PAIRS = R_WIDTH // PAIR
SEG = 256
MIX_CHUNK = 512

VMEM_LIMIT = 56 * 1024 * 1024


def _mm(a, b):
    return jnp.dot(a, b, preferred_element_type=F32)


def _nt(a, b):
    return lax.dot_general(a, b, (((1,), (1,)), ((), ())), preferred_element_type=F32)


def _tn(a, b):
    return lax.dot_general(a, b, (((0,), (0,)), ((), ())), preferred_element_type=F32)


def _rms(x, g):
    ms = jnp.mean(x * x, axis=-1, keepdims=True)
    return x * lax.rsqrt(ms + NORM_EPS) * g


def _alibi_slope(hq):
    return float(2.0 ** (-ALIBI_MAX * (hq + 1) / N_Q_HEADS))


def _proj_kernel(x_ref, g_ref, w_ref, o_ref, h_ref):
    @pl.when(pl.program_id(1) == 0)
    def _():
        h_ref[...] = _rms(x_ref[...], g_ref[...]).astype(BF16)

    o_ref[...] = _nt(h_ref[...], w_ref[...])


def _proj_first_kernel(x_ref, g_ref, w_ref, o_ref, wb_ref, h_ref):
    @pl.when(pl.program_id(0) == 0)
    def _():
        h_ref[...] = _rms(x_ref[...], g_ref[...]).astype(BF16)

    wb = w_ref[...].astype(BF16)
    wb_ref[...] = wb
    o_ref[...] = _nt(h_ref[...], wb)


def _proj_first(x, g, wt_f32, tm, tn):
    m = x.shape[0]
    return pl.pallas_call(
        _proj_first_kernel,
        out_shape=(jax.ShapeDtypeStruct((m, P_COLS), F32), jax.ShapeDtypeStruct((P_COLS, D_MODEL), BF16)),
        grid=(P_COLS // tn,),
        in_specs=[
            pl.BlockSpec((tm, D_MODEL), lambda j: (0, 0), pipeline_mode=pl.Buffered(1)),
            pl.BlockSpec((1, D_MODEL), lambda j: (0, 0)),
            pl.BlockSpec((tn, D_MODEL), lambda j: (j, 0)),
        ],
        out_specs=(pl.BlockSpec((tm, tn), lambda j: (0, j)), pl.BlockSpec((tn, D_MODEL), lambda j: (j, 0))),
        scratch_shapes=[pltpu.VMEM((tm, D_MODEL), BF16)],
        compiler_params=pltpu.CompilerParams(
            dimension_semantics=("arbitrary",), vmem_limit_bytes=VMEM_LIMIT),
        name="proj_first",
    )(x, g, wt_f32)


def _proj_rest_kernel(x_ref, g_ref, w_ref, done_ref, o_ref, h_ref):
    del done_ref
    _proj_kernel(x_ref, g_ref, w_ref, o_ref, h_ref)


def _proj_rest(x, g, w, proj_done, tm, tn):
    m = x.shape[0]
    return pl.pallas_call(
        _proj_rest_kernel,
        out_shape=jax.ShapeDtypeStruct((m, P_COLS), F32),
        grid=(m // tm - 1, P_COLS // tn),
        in_specs=[
            pl.BlockSpec((tm, D_MODEL), lambda i, j: (i + 1, 0)),
            pl.BlockSpec((1, D_MODEL), lambda i, j: (0, 0)),
            pl.BlockSpec((tn, D_MODEL), lambda i, j: (j, 0)),
            pl.BlockSpec(memory_space=pl.ANY),
        ],
        out_specs=pl.BlockSpec((tm, tn), lambda i, j: (i + 1, j)),
        scratch_shapes=[pltpu.VMEM((tm, D_MODEL), BF16)],
        input_output_aliases={3: 0},
        compiler_params=pltpu.CompilerParams(
            dimension_semantics=("arbitrary", "arbitrary"), vmem_limit_bytes=VMEM_LIMIT),
        name="proj_rest",
    )(x, g, w, proj_done)


def _proj(x, g, w, tm, tn):
    m = x.shape[0]
    return pl.pallas_call(
        _proj_kernel,
        out_shape=jax.ShapeDtypeStruct((m, P_COLS), F32),
        grid=(m // tm, P_COLS // tn),
        in_specs=[
            pl.BlockSpec((tm, D_MODEL), lambda i, j: (i, 0)),
            pl.BlockSpec((1, D_MODEL), lambda i, j: (0, 0)),
            pl.BlockSpec((tn, D_MODEL), lambda i, j: (j, 0)),
        ],
        out_specs=pl.BlockSpec((tm, tn), lambda i, j: (i, j)),
        scratch_shapes=[pltpu.VMEM((tm, D_MODEL), BF16)],
        compiler_params=pltpu.CompilerParams(
            dimension_semantics=("arbitrary", "arbitrary"), vmem_limit_bytes=VMEM_LIMIT),
        name="proj",
    )(x, g, w)


def _attend_heads(scores, values, sinks):
    heads = range(len(scores))
    m = []
    for h in heads:
        mh = sinks[h]
        for s in scores[h]:
            mh = jnp.maximum(mh, jnp.max(s, axis=-1, keepdims=True))
        m.append(mh)
    ps = [[jnp.exp(s - m[h]) for s in scores[h]] for h in heads]
    den = []
    for h in heads:
        dh = jnp.exp(sinks[h] - m[h])
        for p in ps[h]:
            dh = dh + jnp.sum(p, axis=-1, keepdims=True)
        den.append(dh)
    outs = []
    for h in heads:
        o = None
        for p, v in zip(ps[h], values[h]):
            t = _mm(p.astype(BF16), v)
            o = t if o is None else o + t
        outs.append(o * (1.0 / den[h]))
    return outs


def _head_slices():
    q_sl = [slice(hq * HEAD_DIM, (hq + 1) * HEAD_DIM) for hq in range(N_Q_HEADS)]
    kv_sl = [slice((hq // GQA_GROUP) * HEAD_DIM, (hq // GQA_GROUP + 1) * HEAD_DIM) for hq in range(N_Q_HEADS)]
    return q_sl, kv_sl


def _alibi_bias(dist, valid):
    slopes = np.array([_alibi_slope(hq) for hq in range(N_Q_HEADS)], np.float32)
    return np.where(valid[None], -slopes[:, None, None] * dist[None].astype(np.float32), -np.inf).astype(np.float32)


def _attn_prompt_kernel(sink_ref, bias_ref, q_ref, kp_ref, kc_ref, vp_ref, vc_ref, o_ref):
    w = WINDOW
    kj = lax.broadcasted_iota(jnp.int32, (w, 2 * w), 1)
    no_prev = (kj < w) & (pl.program_id(1) == 0)
    q = q_ref[...] * (HEAD_DIM ** -0.5)
    k2 = jnp.concatenate([kp_ref[...], kc_ref[...]], axis=0).astype(BF16)
    v2 = jnp.concatenate([vp_ref[...], vc_ref[...]], axis=0).astype(BF16)
    hqs = range(N_Q_HEADS)
    q_sl, kv_sl = _head_slices()
    qs = [q[:, q_sl[hq]].astype(BF16) for hq in hqs]
    sc = [[jnp.where(no_prev, -jnp.inf, _nt(qs[hq], k2[:, kv_sl[hq]]) + bias_ref[hq])] for hq in hqs]
    outs = _attend_heads(sc, [[v2[:, kv_sl[hq]]] for hq in hqs], [sink_ref[hq] for hq in hqs])
    for hq in hqs:
        o_ref[:, q_sl[hq]] = outs[hq]


def _attn_prompt(proj, sinks, batch, seq):
    nb = seq // WINDOW
    kcol, vcol = P_K // KV_WIDTH, P_V // KV_WIDTH

    def cur(c):
        return lambda b, i: (b * nb + i, c)

    def prev(c):
        return lambda b, i: (b * nb + jnp.maximum(i - 1, 0), c)

    ti = np.arange(WINDOW)[:, None]
    kj = np.arange(2 * WINDOW)[None, :]
    dist = ti - kj + WINDOW
    bias = _alibi_bias(dist, (dist >= 0) & (dist <= WINDOW))
    return pl.pallas_call(
        _attn_prompt_kernel,
        out_shape=jax.ShapeDtypeStruct((batch * seq, ATTN_WIDTH), F32),
        grid=(batch, nb),
        in_specs=[
            pl.BlockSpec(memory_space=pltpu.SMEM),
            _resident((N_Q_HEADS, WINDOW, 2 * WINDOW)),
            pl.BlockSpec((WINDOW, ATTN_WIDTH), cur(P_Q // ATTN_WIDTH)),
            pl.BlockSpec((WINDOW, KV_WIDTH), prev(kcol)),
            pl.BlockSpec((WINDOW, KV_WIDTH), cur(kcol)),
            pl.BlockSpec((WINDOW, KV_WIDTH), prev(vcol)),
            pl.BlockSpec((WINDOW, KV_WIDTH), cur(vcol)),
        ],
        out_specs=pl.BlockSpec((WINDOW, ATTN_WIDTH), lambda b, i: (b * nb + i, 0)),
        compiler_params=pltpu.CompilerParams(dimension_semantics=("arbitrary", "arbitrary")),
        name="attn_prompt",
    )(sinks, jnp.asarray(bias), proj, proj, proj, proj, proj)


def _attn_sample_kernel(sink_ref, q_ref, kn_ref, vn_ref, ck_ref, cv_ref, o_ref, nk_ref, nv_ref):
    t, w = q_ref.shape[0], WINDOW
    kn, vn = kn_ref[...], vn_ref[...]
    ck, cv = ck_ref[0], cv_ref[0]
    nk_ref[0, 0:w - t, :] = ck[t:, :]
    nk_ref[0, w - t:w, :] = kn
    nv_ref[0, 0:w - t, :] = cv[t:, :]
    nv_ref[0, w - t:w, :] = vn
    ti = lax.broadcasted_iota(jnp.int32, (t, w), 0)
    cj = lax.broadcasted_iota(jnp.int32, (t, w), 1)
    dist_c = (ti - cj + w).astype(F32)
    valid_c = cj >= ti
    ti2 = lax.broadcasted_iota(jnp.int32, (t, t), 0)
    tj2 = lax.broadcasted_iota(jnp.int32, (t, t), 1)
    dist_n = (ti2 - tj2).astype(F32)
    valid_n = tj2 <= ti2
    q = q_ref[...] * (HEAD_DIM ** -0.5)
    ckb, cvb, knb, vnb = ck.astype(BF16), cv.astype(BF16), kn.astype(BF16), vn.astype(BF16)
    hqs = range(N_Q_HEADS)
    q_sl, kv_sl = _head_slices()
    qs = [q[:, q_sl[hq]].astype(BF16) for hq in hqs]
    scores = [[_nt(qs[hq], ckb[:, kv_sl[hq]]) + jnp.where(valid_c, -_alibi_slope(hq) * dist_c, -jnp.inf),
               _nt(qs[hq], knb[:, kv_sl[hq]]) + jnp.where(valid_n, -_alibi_slope(hq) * dist_n, -jnp.inf)]
              for hq in hqs]
    outs = _attend_heads(scores, [[cvb[:, kv_sl[hq]], vnb[:, kv_sl[hq]]] for hq in hqs],
                         [sink_ref[hq] for hq in hqs])
    for hq in hqs:
        o_ref[:, q_sl[hq]] = outs[hq]


def _attn_sample(proj, sinks, cache_k, cache_v, batch, seq):
    kcol, vcol = P_K // KV_WIDTH, P_V // KV_WIDTH
    win = jax.ShapeDtypeStruct((batch, WINDOW, KV_WIDTH), F32)
    return pl.pallas_call(
        _attn_sample_kernel,
        out_shape=(jax.ShapeDtypeStruct((batch * seq, ATTN_WIDTH), F32), win, win),
        grid=(batch,),
        in_specs=[
            pl.BlockSpec(memory_space=pltpu.SMEM),
            pl.BlockSpec((seq, ATTN_WIDTH), lambda b: (b, P_Q // ATTN_WIDTH)),
            pl.BlockSpec((seq, KV_WIDTH), lambda b: (b, kcol)),
            pl.BlockSpec((seq, KV_WIDTH), lambda b: (b, vcol)),
            pl.BlockSpec((1, WINDOW, KV_WIDTH), lambda b: (b, 0, 0)),
            pl.BlockSpec((1, WINDOW, KV_WIDTH), lambda b: (b, 0, 0)),
        ],
        out_specs=(
            pl.BlockSpec((seq, ATTN_WIDTH), lambda b: (b, 0)),
            pl.BlockSpec((1, WINDOW, KV_WIDTH), lambda b: (b, 0, 0)),
            pl.BlockSpec((1, WINDOW, KV_WIDTH), lambda b: (b, 0, 0)),
        ),
        compiler_params=pltpu.CompilerParams(dimension_semantics=("arbitrary",)),
        name="attn_sample",
    )(sinks, proj, proj, proj, cache_k, cache_v)


def _seg_sum(x, bones):
    rows = x.shape[0]
    hi = x.astype(BF16).astype(F32)
    lo = x - hi
    groups = [slice(j * SEG, (j + 1) * SEG) for j in range(R_WIDTH // SEG)]
    lhs = jnp.concatenate([t[:, sl] for sl in groups for t in (hi, lo)], axis=0).astype(BF16)
    out = _mm(lhs, bones)
    return jnp.concatenate(
        [out[2 * j * rows:(2 * j + 1) * rows] + out[(2 * j + 1) * rows:(2 * j + 2) * rows]
         for j in range(len(groups))], axis=1)


def _wkv_kernel(pa_ref, pb_ref, pl_ref, prev_ref, s0_ref,
                mu_ref, mul_ref, w0_ref, a0_ref, kk_ref, ka_ref, rk_ref, lnw_ref, lnb_ref,
                wl_ref, bones_ref, *rest, chunk, n_sub, n_steps, cast_transposed):
    n_cast = len(cast_transposed)
    cast_src, (y_ref, so_ref) = rest[:n_cast], rest[n_cast:n_cast + 2]
    cast_dst, (s_ref, carry_ref) = rest[n_cast + 2:2 * n_cast + 2], rest[2 * n_cast + 2:]
    for src, dst, transposed in zip(cast_src, cast_dst, cast_transposed):
        dst[...] = (src[...].T if transposed else src[...]).astype(BF16)

    step = pl.program_id(1)
    cs = chunk
    rows = n_sub * cs
    gc = 2 * cs
    hd = R_HEAD
    w = R_WIDTH

    @pl.when(step == 0)
    def _init():
        carry_ref[...] = prev_ref[0]
        s_ref[...] = jnp.zeros(s_ref.shape, F32)
        for p in range(N_PAIRS):
            s_ref[p, 0:hd, 0:hd] = s0_ref[0, 2 * p]
            s_ref[p, hd:2 * hd, hd:2 * hd] = s0_ref[0, 2 * p + 1]

    row = lax.broadcasted_iota(jnp.int32, (rows, 1), 0)

    def token_shift(x, prev_row, mu):
        shifted = jnp.where(row == 0, prev_row, pltpu.roll(x, 1, axis=0))
        return x + (shifted - x) * mu

    cols = jnp.concatenate([pa_ref[...], pb_ref[...]], axis=1)
    l_raw = pl_ref[...]
    xx = token_shift(cols, carry_ref[:, 0:3 * w], mu_ref[...])
    xr, xk, xv = xx[:, 0:w], xx[:, w:2 * w], xx[:, 2 * w:3 * w]
    xl = token_shift(l_raw, carry_ref[:, 3 * w:], mul_ref[...])
    carry_ref[:, 0:3 * w] = cols[rows - 1:rows, :]
    carry_ref[:, 3 * w:] = l_raw[rows - 1:rows, :]

    lane_l = lax.broadcasted_iota(jnp.int32, (1, LORA_PAD), 1)
    act = jnp.where(lane_l < DECAY_LORA, jnp.tanh(xl),
                    jnp.where(lane_l < DECAY_LORA + AAA_LORA, xl,
                              jnp.where(lane_l < LORA_COLS, jax.nn.sigmoid(xl), 0.0)))
    up = _mm(act.astype(BF16), wl_ref[...])
    z = -(w0_ref[...] + up[:, 0:w])
    softplus = jnp.maximum(z, 0.0) + jnp.log1p(jnp.exp(-jnp.abs(z)))
    lwd = -jnp.exp(-softplus - 0.5)
    a = jax.nn.sigmoid(a0_ref[...] + up[:, w:2 * w])
    gate = up[:, 2 * w:3 * w]

    bones = bones_ref[...]
    kkn = xk * kk_ref[...]
    kk = kkn * lax.rsqrt(jnp.maximum(_seg_sum(kkn * kkn, bones), 1e-24))
    k2 = xk * (1.0 + (a - 1.0) * ka_ref[...])
    b = kk * a

    tri = (lax.broadcasted_iota(jnp.int32, (cs, cs), 0)
           >= lax.broadcasted_iota(jnp.int32, (cs, cs), 1)).astype(F32)
    ri = lax.broadcasted_iota(jnp.int32, (gc, gc), 0)
    ci = lax.broadcasted_iota(jnp.int32, (gc, gc), 1)
    same_head = (ri >= cs) == (ci >= cs)
    strict = same_head & (ci < ri)
    incl = same_head & (ci <= ri)
    eye = (ri == ci).astype(F32)
    head0 = lax.broadcasted_iota(jnp.int32, (1, PAIR), 1) < hd

    def stack(x):
        return jnp.concatenate([jnp.where(head0, x, 0.0), jnp.where(head0, 0.0, x)], axis=0).astype(BF16)

    pairs = range(N_PAIRS)
    lanes = [slice(p * PAIR, (p + 1) * PAIR) for p in pairs]
    squarings = cs.bit_length() - 2

    chunks = []
    for sub in range(n_sub):
        rs = slice(sub * cs, (sub + 1) * cs)
        lw_c = lwd[rs]
        cum = jnp.dot(tri, lw_c, precision=lax.Precision.HIGHEST, preferred_element_type=F32)
        cum_last = cum[cs - 1:cs, :]
        e_inv = jnp.exp(-cum)
        e_last = jnp.exp(cum_last - cum)
        kq = kk[rs] * jnp.exp(cum - lw_c)
        rq = xr[rs] * jnp.exp(cum)
        kd = k2[rs] * e_inv
        bd = b[rs] * e_inv
        kdp = k2[rs] * e_last
        bdp = b[rs] * e_last
        xv_c = xv[rs]
        kq_s = [stack(kq[:, sl]) for sl in lanes]
        bd_s = [stack(bd[:, sl]) for sl in lanes]
        kd_s = [stack(kd[:, sl]) for sl in lanes]
        rq_s = [stack(rq[:, sl]) for sl in lanes]
        probes = [jnp.concatenate([kq_s[p], rq_s[p]], axis=0) for p in pairs]
        if gc % PAIR == 0:
            a_all = [_nt(probes[p], jnp.concatenate([bd_s[p], kd_s[p]], axis=0)) for p in pairs]
            a_kb = [jnp.where(strict, t[0:gc, 0:gc], 0.0) for t in a_all]
            a_kk = [jnp.where(strict, t[0:gc, gc:2 * gc], 0.0).astype(BF16) for t in a_all]
            a_rb = [jnp.where(incl, t[gc:2 * gc, 0:gc], 0.0).astype(BF16) for t in a_all]
            a_rk = [jnp.where(incl, t[gc:2 * gc, gc:2 * gc], 0.0).astype(BF16) for t in a_all]
        else:
            a_kb = [jnp.where(strict, _nt(kq_s[p], bd_s[p]), 0.0) for p in pairs]
            a_kk = [jnp.where(strict, _nt(kq_s[p], kd_s[p]), 0.0).astype(BF16) for p in pairs]
            a_rb = [jnp.where(incl, _nt(rq_s[p], bd_s[p]), 0.0).astype(BF16) for p in pairs]
            a_rk = [jnp.where(incl, _nt(rq_s[p], kd_s[p]), 0.0).astype(BF16) for p in pairs]
        inv = [eye - t for t in a_kb]
        if squarings >= 1:
            apow_b = [t.astype(BF16) for t in a_kb]
            apow_b = [_mm(t, t).astype(BF16) for t in apow_b]
            for _ in range(squarings - 1):
                both = [_mm(jnp.concatenate([apow_b[p], inv[p].astype(BF16)], axis=0), apow_b[p]) for p in pairs]
                apow_b = [t[0:gc].astype(BF16) for t in both]
                inv = [inv[p] + both[p][gc:2 * gc] for p in pairs]
            inv = [inv[p] + _mm(inv[p].astype(BF16), apow_b[p]) for p in pairs]
        chunks.append(dict(
            probes=probes, a_kk=a_kk, a_rk=a_rk, a_rb=a_rb, inv=[t.astype(BF16) for t in inv],
            v_s=[stack(xv_c[:, sl]) for sl in lanes],
            upd=[jnp.concatenate([stack(kdp[:, sl]), stack(bdp[:, sl])], axis=0) for sl in lanes],
            p_last=jnp.exp(cum_last)))

    state = [s_ref[p] for p in pairs]
    y_rows = []
    for ch in chunks:
        s_b = [s.astype(BF16) for s in state]
        state_t = [_nt(ch["probes"][p], s_b[p]) for p in pairs]
        rhs = [state_t[p][0:gc] + _mm(ch["a_kk"][p], ch["v_s"][p]) for p in pairs]
        u_b = [_mm(ch["inv"][p], rhs[p].astype(BF16)).astype(BF16) for p in pairs]
        if gc % PAIR == 0:
            y_s = [state_t[p][gc:2 * gc] + _mm(jnp.concatenate([ch["a_rk"][p], -ch["a_rb"][p]], axis=1),
                                               jnp.concatenate([ch["v_s"][p], u_b[p]], axis=0)) for p in pairs]
        else:
            y_s = [state_t[p][gc:2 * gc] + _mm(ch["a_rk"][p], ch["v_s"][p]) - _mm(ch["a_rb"][p], u_b[p])
                   for p in pairs]
        state = [state[p] * ch["p_last"][:, lanes[p]]
                 + _tn(jnp.concatenate([ch["v_s"][p], -u_b[p]], axis=0), ch["upd"][p]) for p in pairs]
        y_rows.append(jnp.concatenate([t[0:cs] + t[cs:gc] for t in y_s], axis=1))
    for p in pairs:
        s_ref[p] = state[p]

    y = jnp.concatenate(y_rows, axis=0) if n_sub > 1 else y_rows[0]
    mean = _seg_sum(y, bones) * (1.0 / hd)
    d = y - mean
    var = _seg_sum(d * d, bones) * (1.0 / hd)
    yn = d * lax.rsqrt(var + GN_EPS) * lnw_ref[...] + lnb_ref[...]
    bonus = _seg_sum(xr * k2 * rk_ref[...], bones) * xv
    y_ref[...] = (yn + bonus) * gate

    @pl.when(step == n_steps - 1)
    def _fin():
        for p in range(N_PAIRS):
            so_ref[0, 2 * p] = s_ref[p, 0:hd, 0:hd]
            so_ref[0, 2 * p + 1] = s_ref[p, hd:2 * hd, hd:2 * hd]


def _wkv(proj, prev0, s0, params, wl, bones, batch, seq, chunk, n_sub, casts=()):
    rows = chunk * n_sub
    ns = seq // rows
    rw = R_WIDTH

    def col(cb):
        return lambda b, c: (b * ns + c, cb)

    const2 = lambda b, c: (0, 0)
    vec = lambda n: pl.BlockSpec((1, n), const2)
    cast_in, cast_out, cast_shapes = [], [], []
    for wgt, span in casts:
        start, count = (0, wgt.shape[0]) if span is None else span
        blk, width = count // (batch * ns), wgt.shape[1]
        assert blk * batch * ns == count, (wgt.shape, span)
        if span is None:
            assert blk % 16 == 0, blk
            cast_in.append(pl.BlockSpec((blk, width), col(0)))
            cast_out.append(pl.BlockSpec((blk, width), col(0)))
            cast_shapes.append(jax.ShapeDtypeStruct((count, width), BF16))
        else:
            assert blk % 128 == 0 and start % 8 == 0, span
            cast_in.append(pl.BlockSpec(
                (pl.Element(blk), pl.Element(width)),
                lambda b, c, start=start, blk=blk: (pl.multiple_of(start + blk * (b * ns + c), 8), 0)))
            cast_out.append(pl.BlockSpec((width, blk), lambda b, c: (0, b * ns + c)))
            cast_shapes.append(jax.ShapeDtypeStruct((width, count), BF16))
    mu, mul, w0, a0, k_k, k_a, r_k, ln_w, ln_b = params
    outs = pl.pallas_call(
        functools.partial(_wkv_kernel, chunk=chunk, n_sub=n_sub, n_steps=ns,
                          cast_transposed=tuple(span is not None for _, span in casts)),
        out_shape=(jax.ShapeDtypeStruct((batch * seq, rw), F32),
                   jax.ShapeDtypeStruct((batch, R_HEADS, R_HEAD, R_HEAD), F32), *cast_shapes),
        grid=(batch, ns),
        in_specs=[
            pl.BlockSpec((rows, C_BLK), col(P_C // C_BLK)),
            pl.BlockSpec((rows, C_BLK), col(P_C // C_BLK + 1)),
            pl.BlockSpec((rows, LORA_PAD), col(P_L // LORA_PAD)),
            pl.BlockSpec((1, 1, SHIFT_COLS), lambda b, c: (b, 0, 0)),
            pl.BlockSpec((1, R_HEADS, R_HEAD, R_HEAD), lambda b, c: (b, 0, 0, 0)),
            vec(3 * rw), vec(LORA_PAD), vec(rw), vec(rw), vec(rw), vec(rw), vec(rw), vec(rw), vec(rw),
            pl.BlockSpec((LORA_PAD, 3 * rw), const2),
            pl.BlockSpec((SEG, SEG), const2),
            *cast_in,
        ],
        out_specs=(
            pl.BlockSpec((rows, rw), lambda b, c: (b * ns + c, 0)),
            pl.BlockSpec((1, R_HEADS, R_HEAD, R_HEAD), lambda b, c: (b, 0, 0, 0)),
            *cast_out,
        ),
        scratch_shapes=[pltpu.VMEM((N_PAIRS, PAIR, PAIR), F32), pltpu.VMEM((1, SHIFT_COLS), F32)],
        compiler_params=pltpu.CompilerParams(
            dimension_semantics=("arbitrary", "arbitrary"), vmem_limit_bytes=VMEM_LIMIT),
        name="wkv",
    )(proj, proj, proj, prev0, s0, mu, mul, w0, a0, k_k, k_a, r_k, ln_w, ln_b, wl, bones,
      *[wgt for wgt, _ in casts])
    return outs[0], outs[1], outs[2:]


def _mix_kernel(x_ref, ya_ref, yr_ref, g_ref, wg_ref, wba_ref, wbr_ref, wo_ref, o_ref):
    x = x_ref[...]
    h = _rms(x, g_ref[...]).astype(BF16)
    ya = ya_ref[...].astype(BF16)
    yr = yr_ref[...].astype(BF16)
    acc = x
    for c in range(D_MODEL // MIX_CHUNK):
        ca = slice(c * MIX_CHUNK, (c + 1) * MIX_CHUNK)
        cr = slice(D_MODEL + c * MIX_CHUNK, D_MODEL + (c + 1) * MIX_CHUNK)
        mixed = (jax.nn.sigmoid(_mm(h, wg_ref[:, ca])) * _mm(ya, wba_ref[:, ca])
                 + jax.nn.sigmoid(_mm(h, wg_ref[:, cr])) * _mm(yr, wbr_ref[:, ca]))
        acc = acc + _mm(mixed.astype(BF16), wo_ref[ca, :])
    o_ref[...] = acc


def _resident(shape):
    return pl.BlockSpec(shape, lambda *_: (0,) * len(shape), pipeline_mode=pl.Buffered(1))


def _mix(x, ya, yr, g, wg, wba, wbr, wo, tm):
    m = x.shape[0]
    row = lambda i: (i, 0)
    return pl.pallas_call(
        _mix_kernel,
        out_shape=jax.ShapeDtypeStruct((m, D_MODEL), F32),
        grid=(m // tm,),
        in_specs=[
            pl.BlockSpec((tm, D_MODEL), row),
            pl.BlockSpec((tm, ATTN_WIDTH), row),
            pl.BlockSpec((tm, R_WIDTH), row),
            pl.BlockSpec((1, D_MODEL), lambda i: (0, 0)),
            _resident((D_MODEL, 2 * D_MODEL)),
            _resident((ATTN_WIDTH, D_MODEL)),
            _resident((R_WIDTH, D_MODEL)),
            _resident((D_MODEL, D_MODEL)),
        ],
        out_specs=pl.BlockSpec((tm, D_MODEL), row),
        compiler_params=pltpu.CompilerParams(
            dimension_semantics=("arbitrary",), vmem_limit_bytes=VMEM_LIMIT),
        name="mix",
    )(x, ya, yr, g, wg, wba, wbr, wo)


def _ffn_kernel(x_ref, g_ref, wu_ref, wd_ref, o_ref, h_ref):
    @pl.when(pl.program_id(1) == 0)
    def _():
        x = x_ref[...]
        h_ref[...] = _rms(x, g_ref[...]).astype(BF16)
        o_ref[...] = x

    u = _mm(h_ref[...], wu_ref[...])
    o_ref[...] += _mm(jnp.square(jnp.maximum(u, 0.0)).astype(BF16), wd_ref[...])


def _ffn(x, g, wu, wd, tm, tk):
    m = x.shape[0]
    return pl.pallas_call(
        _ffn_kernel,
        out_shape=jax.ShapeDtypeStruct((m, D_MODEL), F32),
        grid=(m // tm, D_FF // tk),
        in_specs=[
            pl.BlockSpec((tm, D_MODEL), lambda i, k: (i, 0)),
            pl.BlockSpec((1, D_MODEL), lambda i, k: (0, 0)),
            pl.BlockSpec((D_MODEL, tk), lambda i, k: (0, k)),
            pl.BlockSpec((tk, D_MODEL), lambda i, k: (k, 0)),
        ],
        out_specs=pl.BlockSpec((tm, D_MODEL), lambda i, k: (i, 0)),
        scratch_shapes=[pltpu.VMEM((tm, D_MODEL), BF16)],
        compiler_params=pltpu.CompilerParams(
            dimension_semantics=("arbitrary", "arbitrary"), vmem_limit_bytes=VMEM_LIMIT),
        name="ffn",
    )(x, g, wu, wd)


def _ple_kernel(x_ref, pe_ref, wg_ref, wp_ref, g_ref, o_ref):
    x = x_ref[...]
    gate = jax.nn.sigmoid(_mm(x.astype(BF16), wg_ref[...]))
    x = x + gate * _mm(pe_ref[...].astype(BF16), wp_ref[...])
    o_ref[...] = _rms(x, g_ref[...])


def _ple(x, pe, wg, wp, g, tm):
    m = x.shape[0]
    row = lambda i: (i, 0)
    return pl.pallas_call(
        _ple_kernel,
        out_shape=jax.ShapeDtypeStruct((m, D_MODEL), F32),
        grid=(m // tm,),
        in_specs=[
            pl.BlockSpec((tm, D_MODEL), row),
            pl.BlockSpec((tm, PLE_DIM), row),
            _resident((D_MODEL, D_MODEL)),
            _resident((PLE_DIM, D_MODEL)),
            pl.BlockSpec((1, D_MODEL), lambda i: (0, 0)),
        ],
        out_specs=pl.BlockSpec((tm, D_MODEL), row),
        compiler_params=pltpu.CompilerParams(
            dimension_semantics=("arbitrary",), vmem_limit_bytes=VMEM_LIMIT),
        name="ple",
    )(x, pe, wg, wp, g)


def _dense_tail(x, ya, yr, pe, wts, tiles):
    (g_mix, w_gates, wba, wbr, wo, g_ffn, wu, wd, wg, wp, g_fin) = wts
    x = _mix(x, ya, yr, g_mix, w_gates, wba, wbr, wo, tiles["mix_m"])
    x = _ffn(x, g_ffn, wu, wd, tiles["ffn_m"], tiles["ffn_k"])
    return _ple(x, pe, wg, wp, g_fin, tiles["ple_m"])


def _shift_out(proj, batch, seq):
    last = proj.reshape(batch, seq, P_COLS)[:, -1]
    return last[:, P_C:P_C + RWKV_COLS][None]


def kernel(x_prompt, x_sample, cache_k_win, cache_v_win, state_wkv, state_shift, p_prompt, p_sample,
           norm_mix, w_in, attn_sinks, rwkv_mu, rwkv_w0, rwkv_w2, rwkv_a0, rwkv_a2, rwkv_g2,
           rwkv_k_k, rwkv_k_a, rwkv_r_k, rwkv_ln_w, rwkv_ln_b, w_branch_attn, w_branch_rwkv,
           w_out, norm_ffn, w_ff_up, w_ff_down, w_ple_proj, w_ple_gate, norm_final):
    assert w_in.shape[0] == 1, "single-layer step"
    bp, tp = x_prompt.shape[0], x_prompt.shape[1]
    bs, ts = x_sample.shape[0], x_sample.shape[1]
    rw = R_WIDTH

    wl = jnp.zeros((LORA_PAD, 3 * rw), F32)
    wl = wl.at[0:DECAY_LORA, 0:rw].set(rwkv_w2[0])
    wl = wl.at[DECAY_LORA:DECAY_LORA + AAA_LORA, rw:2 * rw].set(rwkv_a2[0])
    wl = wl.at[DECAY_LORA + AAA_LORA:LORA_COLS, 2 * rw:3 * rw].set(rwkv_g2[0])
    wl = wl.astype(BF16)
    seg_id = np.arange(SEG) // R_HEAD
    bones = jnp.asarray(seg_id[:, None] == seg_id[None, :], BF16)
    mu = rwkv_mu[0]
    row = lambda v: v.reshape(1, -1)
    wkv_params = (row(mu[:3 * rw]), row(jnp.pad(mu[3 * rw:], (0, LORA_PAD - LORA_COLS))),
                  row(rwkv_w0[0]), row(rwkv_a0[0]), row(rwkv_k_k[0]), row(rwkv_k_a[0]),
                  row(rwkv_r_k[0]), row(rwkv_ln_w[0]), row(rwkv_ln_b[0]))
    g_mix = row(norm_mix[0])
    sinks = attn_sinks[0]

    tiles_p = dict(mix_m=256, ffn_m=1024, ffn_k=512, ple_m=512)
    xp = x_prompt.reshape(bp * tp, D_MODEL)
    w_in_t = jnp.transpose(w_in[0])
    proj_p, w_in_b = _proj_first(xp, g_mix, w_in_t, 1024, 1024)
    proj_p = _proj_rest(xp, g_mix, w_in_b, proj_p, 1024, 1024)
    ya_p = _attn_prompt(proj_p, sinks, bp, tp)
    yr_p, s_p, (wu, wd, wo, wg, wba, wbr, w_gates) = _wkv(
        proj_p, jnp.zeros((bp, 1, SHIFT_COLS), F32), jnp.zeros((bp, R_HEADS, R_HEAD, R_HEAD), F32),
        wkv_params, wl, bones, bp, tp, 64, 4,
        casts=((w_ff_up[0], None), (w_ff_down[0], None), (w_out[0], None), (w_ple_gate[0], None),
               (w_branch_attn[0], None), (w_branch_rwkv[0], None), (w_in_t, (P_GATES, 2 * D_MODEL))))
    dense = (g_mix, w_gates, wba, wbr, wo, row(norm_ffn[0]), wu, wd, wg, w_ple_proj[0].astype(BF16),
             row(norm_final))
    yp = _dense_tail(xp, ya_p, yr_p, p_prompt[0].reshape(bp * tp, PLE_DIM), dense, tiles_p)
    pp3 = proj_p.reshape(bp, tp, P_COLS)[:, -WINDOW:]
    k_p = pp3[:, :, P_K:P_K + KV_WIDTH].reshape(1, bp, WINDOW, N_KV_HEADS, HEAD_DIM)
    v_p = pp3[:, :, P_V:P_V + KV_WIDTH].reshape(1, bp, WINDOW, N_KV_HEADS, HEAD_DIM)

    ms = bs * ts
    tiles_s = dict(mix_m=ms, ffn_m=ms, ffn_k=512, ple_m=ms)
    xs = x_sample.reshape(ms, D_MODEL)
    proj_s = _proj(xs, g_mix, w_in_b, ms, 1024)
    ya_s, nk_s, nv_s = _attn_sample(proj_s, sinks, cache_k_win[0].reshape(bs, WINDOW, KV_WIDTH),
                                    cache_v_win[0].reshape(bs, WINDOW, KV_WIDTH), bs, ts)
    prev_s = jnp.pad(state_shift[0], ((0, 0), (0, LORA_PAD - LORA_COLS))).reshape(bs, 1, SHIFT_COLS)
    yr_s, s_s, _ = _wkv(proj_s, prev_s, state_wkv[0], wkv_params, wl, bones, bs, ts, ts, 1)
    ys = _dense_tail(xs, ya_s, yr_s, p_sample[0].reshape(ms, PLE_DIM), dense, tiles_s)

    return (yp.reshape(bp, tp, D_MODEL), ys.reshape(bs, ts, D_MODEL),
            k_p, v_p, s_p[None], _shift_out(proj_p, bp, tp),
            nk_s.reshape(1, bs, WINDOW, N_KV_HEADS, HEAD_DIM), nv_s.reshape(1, bs, WINDOW, N_KV_HEADS, HEAD_DIM),
            s_s[None], _shift_out(proj_s, bs, ts))
```

```python
import functools

import numpy as np
import jax
import jax.numpy as jnp
from jax import lax
from jax.experimental import pallas as pl
from jax.experimental.pallas import tpu as pltpu

F32 = jnp.float32
BF16 = jnp.bfloat16

D_MODEL = 2048
PLE_DIM = 256
HEAD_DIM = 64
N_Q_HEADS = 16
N_KV_HEADS = 4
GQA_GROUP = 4
ATTN_WIDTH = 1024
KV_WIDTH = 256
WINDOW = 128
ALIBI_MAX = 8.0
R_HEAD = 64
R_WIDTH = 1024
R_HEADS = 16
DECAY_LORA = 64
AAA_LORA = 64
GATE_LORA = 160
LORA_COLS = DECAY_LORA + AAA_LORA + GATE_LORA
LORA_PAD = 512
RWKV_COLS = 3 * R_WIDTH + LORA_COLS
D_FF = 4 * D_MODEL
NORM_EPS = 1e-6
GN_EPS = 64e-5

P_Q = 0
P_K = ATTN_WIDTH
P_V = ATTN_WIDTH + KV_WIDTH
P_C = ATTN_WIDTH + 2 * KV_WIDTH
P_L = P_C + 3 * R_WIDTH
P_COLS = P_L + LORA_PAD
P_GATES = P_C + RWKV_COLS
C_BLK = P_C
SHIFT_COLS = 3 * R_WIDTH + LORA_PAD

PAIR = 128
N_PAIRS = R_WIDTH // PAIR
SEG = 256
MIX_CHUNK = 512
SAMPLE_GROUP = 8

VMEM_LIMIT = 56 * 1024 * 1024


def _mm(a, b):
    return jnp.dot(a, b, preferred_element_type=F32)


def _nt(a, b):
    return lax.dot_general(a, b, (((1,), (1,)), ((), ())), preferred_element_type=F32)


def _tn(a, b):
    return lax.dot_general(a, b, (((0,), (0,)), ((), ())), preferred_element_type=F32)


def _rms(x, g):
    ms = jnp.mean(x * x, axis=-1, keepdims=True)
    return x * lax.rsqrt(ms + NORM_EPS) * g


def _alibi_slope(hq):
    return float(2.0 ** (-ALIBI_MAX * (hq + 1) / N_Q_HEADS))


def _proj_kernel(x_ref, g_ref, w_ref, o_ref, h_ref):
    @pl.when(pl.program_id(1) == 0)
    def _():
        h_ref[...] = _rms(x_ref[...], g_ref[...]).astype(BF16)

    o_ref[...] = _nt(h_ref[...], w_ref[...])


def _proj_first_kernel(x_ref, g_ref, w_ref, o_ref, wb_ref, h_ref):
    @pl.when(pl.program_id(0) == 0)
    def _():
        h_ref[...] = _rms(x_ref[...], g_ref[...]).astype(BF16)

    wb = w_ref[...].astype(BF16)
    wb_ref[...] = wb
    o_ref[...] = _nt(h_ref[...], wb)


def _proj_first(x, g, wt_f32, tm, tn):
    m = x.shape[0]
    return pl.pallas_call(
        _proj_first_kernel,
        out_shape=(jax.ShapeDtypeStruct((m, P_COLS), F32), jax.ShapeDtypeStruct((P_COLS, D_MODEL), BF16)),
        grid=(P_COLS // tn,),
        in_specs=[
            pl.BlockSpec((tm, D_MODEL), lambda j: (0, 0), pipeline_mode=pl.Buffered(1)),
            pl.BlockSpec((1, D_MODEL), lambda j: (0, 0)),
            pl.BlockSpec((tn, D_MODEL), lambda j: (j, 0)),
        ],
        out_specs=(pl.BlockSpec((tm, tn), lambda j: (0, j)), pl.BlockSpec((tn, D_MODEL), lambda j: (j, 0))),
        scratch_shapes=[pltpu.VMEM((tm, D_MODEL), BF16)],
        compiler_params=pltpu.CompilerParams(
            dimension_semantics=("arbitrary",), vmem_limit_bytes=VMEM_LIMIT),
        name="proj_first",
    )(x, g, wt_f32)


def _proj_rest_kernel(x_ref, g_ref, w_ref, done_ref, o_ref, h_ref):
    del done_ref
    _proj_kernel(x_ref, g_ref, w_ref, o_ref, h_ref)


def _proj_rest(x, g, w, proj_done, tm, tn):
    m = x.shape[0]
    return pl.pallas_call(
        _proj_rest_kernel,
        out_shape=jax.ShapeDtypeStruct((m, P_COLS), F32),
        grid=(m // tm - 1, P_COLS // tn),
        in_specs=[
            pl.BlockSpec((tm, D_MODEL), lambda i, j: (i + 1, 0)),
            pl.BlockSpec((1, D_MODEL), lambda i, j: (0, 0)),
            pl.BlockSpec((tn, D_MODEL), lambda i, j: (j, 0)),
            pl.BlockSpec(memory_space=pl.ANY),
        ],
        out_specs=pl.BlockSpec((tm, tn), lambda i, j: (i + 1, j)),
        scratch_shapes=[pltpu.VMEM((tm, D_MODEL), BF16)],
        input_output_aliases={3: 0},
        compiler_params=pltpu.CompilerParams(
            dimension_semantics=("arbitrary", "arbitrary"), vmem_limit_bytes=VMEM_LIMIT),
        name="proj_rest",
    )(x, g, w, proj_done)


def _proj(x, g, w, tm, tn):
    m = x.shape[0]
    return pl.pallas_call(
        _proj_kernel,
        out_shape=jax.ShapeDtypeStruct((m, P_COLS), F32),
        grid=(m // tm, P_COLS // tn),
        in_specs=[
            pl.BlockSpec((tm, D_MODEL), lambda i, j: (i, 0)),
            pl.BlockSpec((1, D_MODEL), lambda i, j: (0, 0)),
            pl.BlockSpec((tn, D_MODEL), lambda i, j: (j, 0)),
        ],
        out_specs=pl.BlockSpec((tm, tn), lambda i, j: (i, j)),
        scratch_shapes=[pltpu.VMEM((tm, D_MODEL), BF16)],
        compiler_params=pltpu.CompilerParams(
            dimension_semantics=("arbitrary", "arbitrary"), vmem_limit_bytes=VMEM_LIMIT),
        name="proj",
    )(x, g, w)


def _attend_heads(scores, values, sinks):
    heads = range(len(scores))
    m = []
    for h in heads:
        mh = sinks[h]
        for s in scores[h]:
            mh = jnp.maximum(mh, jnp.max(s, axis=-1, keepdims=True))
        m.append(mh)
    ps = [[jnp.exp(s - m[h]) for s in scores[h]] for h in heads]
    den = []
    for h in heads:
        dh = jnp.exp(sinks[h] - m[h])
        for p in ps[h]:
            dh = dh + jnp.sum(p, axis=-1, keepdims=True)
        den.append(dh)
    outs = []
    for h in heads:
        o = None
        for p, v in zip(ps[h], values[h]):
            t = _mm(p.astype(BF16), v)
            o = t if o is None else o + t
        outs.append(o * (1.0 / den[h]))
    return outs


def _head_slices():
    q_sl = [slice(hq * HEAD_DIM, (hq + 1) * HEAD_DIM) for hq in range(N_Q_HEADS)]
    kv_sl = [slice((hq // GQA_GROUP) * HEAD_DIM, (hq // GQA_GROUP + 1) * HEAD_DIM) for hq in range(N_Q_HEADS)]
    return q_sl, kv_sl


def _alibi_bias(dist, valid):
    slopes = np.array([_alibi_slope(hq) for hq in range(N_Q_HEADS)], np.float32)
    return np.where(valid[None], -slopes[:, None, None] * dist[None].astype(np.float32), -np.inf).astype(np.float32)


def _attn_prompt_kernel(sink_ref, bias_ref, q_ref, kp_ref, kc_ref, vp_ref, vc_ref, o_ref):
    w = WINDOW
    kj = lax.broadcasted_iota(jnp.int32, (w, 2 * w), 1)
    no_prev = (kj < w) & (pl.program_id(1) == 0)
    q = q_ref[...] * (HEAD_DIM ** -0.5)
    k2 = jnp.concatenate([kp_ref[...], kc_ref[...]], axis=0).astype(BF16)
    v2 = jnp.concatenate([vp_ref[...], vc_ref[...]], axis=0).astype(BF16)
    hqs = range(N_Q_HEADS)
    q_sl, kv_sl = _head_slices()
    qs = [q[:, q_sl[hq]].astype(BF16) for hq in hqs]
    sc = [[jnp.where(no_prev, -jnp.inf, _nt(qs[hq], k2[:, kv_sl[hq]]) + bias_ref[hq])] for hq in hqs]
    outs = _attend_heads(sc, [[v2[:, kv_sl[hq]]] for hq in hqs], [sink_ref[hq] for hq in hqs])
    for hq in hqs:
        o_ref[:, q_sl[hq]] = outs[hq]


def _attn_prompt(proj, sinks, batch, seq):
    nb = seq // WINDOW
    kcol, vcol = P_K // KV_WIDTH, P_V // KV_WIDTH

    def cur(c):
        return lambda b, i: (b * nb + i, c)

    def prev(c):
        return lambda b, i: (b * nb + jnp.maximum(i - 1, 0), c)

    ti = np.arange(WINDOW)[:, None]
    kj = np.arange(2 * WINDOW)[None, :]
    dist = ti - kj + WINDOW
    bias = _alibi_bias(dist, (dist >= 0) & (dist <= WINDOW))
    return pl.pallas_call(
        _attn_prompt_kernel,
        out_shape=jax.ShapeDtypeStruct((batch * seq, ATTN_WIDTH), F32),
        grid=(batch, nb),
        in_specs=[
            pl.BlockSpec(memory_space=pltpu.SMEM),
            _resident((N_Q_HEADS, WINDOW, 2 * WINDOW)),
            pl.BlockSpec((WINDOW, ATTN_WIDTH), cur(P_Q // ATTN_WIDTH)),
            pl.BlockSpec((WINDOW, KV_WIDTH), prev(kcol)),
            pl.BlockSpec((WINDOW, KV_WIDTH), cur(kcol)),
            pl.BlockSpec((WINDOW, KV_WIDTH), prev(vcol)),
            pl.BlockSpec((WINDOW, KV_WIDTH), cur(vcol)),
        ],
        out_specs=pl.BlockSpec((WINDOW, ATTN_WIDTH), lambda b, i: (b * nb + i, 0)),
        compiler_params=pltpu.CompilerParams(dimension_semantics=("arbitrary", "arbitrary")),
        name="attn_prompt",
    )(sinks, jnp.asarray(bias), proj, proj, proj, proj, proj)


def _attn_sample_kernel(sink_ref, q_ref, kn_ref, vn_ref, ck_ref, cv_ref, o_ref, nk_ref, nv_ref, *, seq):
    t, w = seq, WINDOW
    n_seq = q_ref.shape[0] // t
    ti = lax.broadcasted_iota(jnp.int32, (t, w), 0)
    cj = lax.broadcasted_iota(jnp.int32, (t, w), 1)
    dist_c = (ti - cj + w).astype(F32)
    valid_c = cj >= ti
    ti2 = lax.broadcasted_iota(jnp.int32, (t, t), 0)
    tj2 = lax.broadcasted_iota(jnp.int32, (t, t), 1)
    dist_n = (ti2 - tj2).astype(F32)
    valid_n = tj2 <= ti2
    hqs = range(N_Q_HEADS)
    q_sl, kv_sl = _head_slices()
    bias_c = [jnp.where(valid_c, -_alibi_slope(hq) * dist_c, -jnp.inf) for hq in hqs]
    bias_n = [jnp.where(valid_n, -_alibi_slope(hq) * dist_n, -jnp.inf) for hq in hqs]
    q_all = q_ref[...] * (HEAD_DIM ** -0.5)
    scores, values = [], []
    for s in range(n_seq):
        rs = slice(s * t, (s + 1) * t)
        kn, vn = kn_ref[rs, :], vn_ref[rs, :]
        ck, cv = ck_ref[s], cv_ref[s]
        nk_ref[s, 0:w - t, :] = ck[t:, :]
        nk_ref[s, w - t:w, :] = kn
        nv_ref[s, 0:w - t, :] = cv[t:, :]
        nv_ref[s, w - t:w, :] = vn
        ckb, cvb, knb, vnb = ck.astype(BF16), cv.astype(BF16), kn.astype(BF16), vn.astype(BF16)
        q = q_all[rs]
        qs = [q[:, q_sl[hq]].astype(BF16) for hq in hqs]
        scores += [[_nt(qs[hq], ckb[:, kv_sl[hq]]) + bias_c[hq], _nt(qs[hq], knb[:, kv_sl[hq]]) + bias_n[hq]]
                   for hq in hqs]
        values += [[cvb[:, kv_sl[hq]], vnb[:, kv_sl[hq]]] for hq in hqs]
    outs = _attend_heads(scores, values, [sink_ref[hq] for _ in range(n_seq) for hq in hqs])
    for s in range(n_seq):
        for hq in hqs:
            o_ref[s * t:(s + 1) * t, q_sl[hq]] = outs[s * N_Q_HEADS + hq]


def _attn_sample(proj, sinks, cache_k, cache_v, batch, seq, n_seq):
    kcol, vcol = P_K // KV_WIDTH, P_V // KV_WIDTH
    win = jax.ShapeDtypeStruct((batch, WINDOW, KV_WIDTH), F32)
    rows = n_seq * seq
    return pl.pallas_call(
        functools.partial(_attn_sample_kernel, seq=seq),
        out_shape=(jax.ShapeDtypeStruct((batch * seq, ATTN_WIDTH), F32), win, win),
        grid=(batch // n_seq,),
        in_specs=[
            pl.BlockSpec(memory_space=pltpu.SMEM),
            pl.BlockSpec((rows, ATTN_WIDTH), lambda b: (b, P_Q // ATTN_WIDTH)),
            pl.BlockSpec((rows, KV_WIDTH), lambda b: (b, kcol)),
            pl.BlockSpec((rows, KV_WIDTH), lambda b: (b, vcol)),
            pl.BlockSpec((n_seq, WINDOW, KV_WIDTH), lambda b: (b, 0, 0)),
            pl.BlockSpec((n_seq, WINDOW, KV_WIDTH), lambda b: (b, 0, 0)),
        ],
        out_specs=(
            pl.BlockSpec((rows, ATTN_WIDTH), lambda b: (b, 0)),
            pl.BlockSpec((n_seq, WINDOW, KV_WIDTH), lambda b: (b, 0, 0)),
            pl.BlockSpec((n_seq, WINDOW, KV_WIDTH), lambda b: (b, 0, 0)),
        ),
        compiler_params=pltpu.CompilerParams(dimension_semantics=("arbitrary",)),
        name="attn_sample",
    )(sinks, proj, proj, proj, cache_k, cache_v)


def _seg_sum(x, bones):
    rows = x.shape[0]
    hi = x.astype(BF16).astype(F32)
    lo = x - hi
    groups = [slice(j * SEG, (j + 1) * SEG) for j in range(R_WIDTH // SEG)]
    lhs = jnp.concatenate([t[:, sl] for sl in groups for t in (hi, lo)], axis=0).astype(BF16)
    out = _mm(lhs, bones)
    return jnp.concatenate(
        [out[2 * j * rows:(2 * j + 1) * rows] + out[(2 * j + 1) * rows:(2 * j + 2) * rows]
         for j in range(len(groups))], axis=1)


def _wkv_kernel(pa_ref, pb_ref, pl_ref, prev_ref, s0_ref,
                mu_ref, mul_ref, w0_ref, a0_ref, kk_ref, ka_ref, rk_ref, lnw_ref, lnb_ref,
                wl_ref, bones_ref, *rest, chunk, n_sub, n_seq, n_steps, cast_transposed):
    n_cast = len(cast_transposed)
    cast_src, (y_ref, so_ref) = rest[:n_cast], rest[n_cast:n_cast + 2]
    cast_dst, (s_ref, carry_ref) = rest[n_cast + 2:2 * n_cast + 2], rest[2 * n_cast + 2:]
    for src, dst, transposed in zip(cast_src, cast_dst, cast_transposed):
        dst[...] = (src[...].T if transposed else src[...]).astype(BF16)

    step = pl.program_id(1)
    cs = chunk
    seq_rows = n_sub * cs
    rows = n_seq * seq_rows
    gc = 2 * cs
    hd = R_HEAD
    w = R_WIDTH
    seqs = range(n_seq)

    @pl.when(step == 0)
    def _init():
        s_ref[...] = jnp.zeros(s_ref.shape, F32)
        for q in seqs:
            carry_ref[q:q + 1, :] = prev_ref[q]
            for p in range(N_PAIRS):
                s_ref[q * N_PAIRS + p, 0:hd, 0:hd] = s0_ref[q, 2 * p]
                s_ref[q * N_PAIRS + p, hd:2 * hd, hd:2 * hd] = s0_ref[q, 2 * p + 1]

    row = lax.broadcasted_iota(jnp.int32, (rows, 1), 0)

    def token_shift(x, lo, hi, mu):
        shifted = pltpu.roll(x, 1, axis=0)
        for q in seqs:
            shifted = jnp.where(row == q * seq_rows, carry_ref[q:q + 1, lo:hi], shifted)
        return x + (shifted - x) * mu

    cols = jnp.concatenate([pa_ref[...], pb_ref[...]], axis=1)
    l_raw = pl_ref[...]
    xx = token_shift(cols, 0, 3 * w, mu_ref[...])
    xr, xk, xv = xx[:, 0:w], xx[:, w:2 * w], xx[:, 2 * w:3 * w]
    xl = token_shift(l_raw, 3 * w, SHIFT_COLS, mul_ref[...])
    for q in seqs:
        last = (q + 1) * seq_rows
        carry_ref[q:q + 1, 0:3 * w] = cols[last - 1:last, :]
        carry_ref[q:q + 1, 3 * w:] = l_raw[last - 1:last, :]

    lane_l = lax.broadcasted_iota(jnp.int32, (1, LORA_PAD), 1)
    act = jnp.where(lane_l < DECAY_LORA, jnp.tanh(xl),
                    jnp.where(lane_l < DECAY_LORA + AAA_LORA, xl,
                              jnp.where(lane_l < LORA_COLS, jax.nn.sigmoid(xl), 0.0)))
    up = _mm(act.astype(BF16), wl_ref[...])
    z = -(w0_ref[...] + up[:, 0:w])
    softplus = jnp.maximum(z, 0.0) + jnp.log1p(jnp.exp(-jnp.abs(z)))
    lwd = -jnp.exp(-softplus - 0.5)
    a = jax.nn.sigmoid(a0_ref[...] + up[:, w:2 * w])
    gate = up[:, 2 * w:3 * w]

    bones = bones_ref[...]
    kkn = xk * kk_ref[...]
    kk = kkn * lax.rsqrt(jnp.maximum(_seg_sum(kkn * kkn, bones), 1e-24))
    k2 = xk * (1.0 + (a - 1.0) * ka_ref[...])
    b = kk * a

    tri = (lax.broadcasted_iota(jnp.int32, (cs, cs), 0)
           >= lax.broadcasted_iota(jnp.int32, (cs, cs), 1)).astype(F32)
    ri = lax.broadcasted_iota(jnp.int32, (gc, gc), 0)
    ci = lax.broadcasted_iota(jnp.int32, (gc, gc), 1)
    same_head = (ri >= cs) == (ci >= cs)
    strict = same_head & (ci < ri)
    incl = same_head & (ci <= ri)
    eye = (ri == ci).astype(F32)
    head0 = lax.broadcasted_iota(jnp.int32, (1, PAIR), 1) < hd

    def stack(x):
        return jnp.concatenate([jnp.where(head0, x, 0.0), jnp.where(head0, 0.0, x)], axis=0).astype(BF16)

    pairs = range(N_PAIRS)
    lanes = [slice(p * PAIR, (p + 1) * PAIR) for p in pairs]
    squarings = cs.bit_length() - 2

    chunks = []
    for sub in range(n_seq * n_sub):
        rs = slice(sub * cs, (sub + 1) * cs)
        lw_c = lwd[rs]
        cum = jnp.dot(tri, lw_c, precision=lax.Precision.HIGHEST, preferred_element_type=F32)
        cum_last = cum[cs - 1:cs, :]
        e_inv = jnp.exp(-cum)
        e_last = jnp.exp(cum_last - cum)
        kq = kk[rs] * jnp.exp(cum - lw_c)
        rq = xr[rs] * jnp.exp(cum)
        kd = k2[rs] * e_inv
        bd = b[rs] * e_inv
        kdp = k2[rs] * e_last
        bdp = b[rs] * e_last
        xv_c = xv[rs]
        kq_s = [stack(kq[:, sl]) for sl in lanes]
        bd_s = [stack(bd[:, sl]) for sl in lanes]
        kd_s = [stack(kd[:, sl]) for sl in lanes]
        rq_s = [stack(rq[:, sl]) for sl in lanes]
        probes = [jnp.concatenate([kq_s[p], rq_s[p]], axis=0) for p in pairs]
        if gc % PAIR == 0:
            a_all = [_nt(probes[p], jnp.concatenate([bd_s[p], kd_s[p]], axis=0)) for p in pairs]
            a_kb = [jnp.where(strict, t[0:gc, 0:gc], 0.0) for t in a_all]
            a_kk = [jnp.where(strict, t[0:gc, gc:2 * gc], 0.0).astype(BF16) for t in a_all]
            a_rb = [jnp.where(incl, t[gc:2 * gc, 0:gc], 0.0).astype(BF16) for t in a_all]
            a_rk = [jnp.where(incl, t[gc:2 * gc, gc:2 * gc], 0.0).astype(BF16) for t in a_all]
        else:
            a_kb = [jnp.where(strict, _nt(kq_s[p], bd_s[p]), 0.0) for p in pairs]
            a_kk = [jnp.where(strict, _nt(kq_s[p], kd_s[p]), 0.0).astype(BF16) for p in pairs]
            a_rb = [jnp.where(incl, _nt(rq_s[p], bd_s[p]), 0.0).astype(BF16) for p in pairs]
            a_rk = [jnp.where(incl, _nt(rq_s[p], kd_s[p]), 0.0).astype(BF16) for p in pairs]
        inv = [eye - t for t in a_kb]
        if squarings >= 1:
            apow_b = [t.astype(BF16) for t in a_kb]
            apow_b = [_mm(t, t).astype(BF16) for t in apow_b]
            for _ in range(squarings - 1):
                both = [_mm(jnp.concatenate([apow_b[p], inv[p].astype(BF16)], axis=0), apow_b[p]) for p in pairs]
                apow_b = [t[0:gc].astype(BF16) for t in both]
                inv = [inv[p] + both[p][gc:2 * gc] for p in pairs]
            inv = [inv[p] + _mm(inv[p].astype(BF16), apow_b[p]) for p in pairs]
        chunks.append(dict(
            probes=probes, a_kk=a_kk, a_rk=a_rk, a_rb=a_rb, inv=[t.astype(BF16) for t in inv],
            v_s=[stack(xv_c[:, sl]) for sl in lanes],
            upd=[jnp.concatenate([stack(kdp[:, sl]), stack(bdp[:, sl])], axis=0) for sl in lanes],
            p_last=jnp.exp(cum_last)))

    sp = [(q, p) for q in seqs for p in pairs]
    state = {(q, p): s_ref[q * N_PAIRS + p] for q, p in sp}
    y_rows = [None] * (n_seq * n_sub)
    for level in range(n_sub):
        ch = {q: chunks[q * n_sub + level] for q in seqs}
        s_b = {k: state[k].astype(BF16) for k in sp}
        state_t = {(q, p): _nt(ch[q]["probes"][p], s_b[q, p]) for q, p in sp}
        rhs = {(q, p): state_t[q, p][0:gc] + _mm(ch[q]["a_kk"][p], ch[q]["v_s"][p]) for q, p in sp}
        u_b = {(q, p): _mm(ch[q]["inv"][p], rhs[q, p].astype(BF16)).astype(BF16) for q, p in sp}
        if gc % PAIR == 0:
            y_s = {(q, p): state_t[q, p][gc:2 * gc]
                   + _mm(jnp.concatenate([ch[q]["a_rk"][p], -ch[q]["a_rb"][p]], axis=1),
                         jnp.concatenate([ch[q]["v_s"][p], u_b[q, p]], axis=0)) for q, p in sp}
        else:
            y_s = {(q, p): state_t[q, p][gc:2 * gc] + _mm(ch[q]["a_rk"][p], ch[q]["v_s"][p])
                   - _mm(ch[q]["a_rb"][p], u_b[q, p]) for q, p in sp}
        state = {(q, p): state[q, p] * ch[q]["p_last"][:, lanes[p]]
                 + _tn(jnp.concatenate([ch[q]["v_s"][p], -u_b[q, p]], axis=0), ch[q]["upd"][p]) for q, p in sp}
        for q in seqs:
            y_rows[q * n_sub + level] = jnp.concatenate(
                [y_s[q, p][0:cs] + y_s[q, p][cs:gc] for p in pairs], axis=1)
    for q, p in sp:
        s_ref[q * N_PAIRS + p] = state[q, p]

    y = jnp.concatenate(y_rows, axis=0) if len(y_rows) > 1 else y_rows[0]
    mean = _seg_sum(y, bones) * (1.0 / hd)
    d = y - mean
    var = _seg_sum(d * d, bones) * (1.0 / hd)
    yn = d * lax.rsqrt(var + GN_EPS) * lnw_ref[...] + lnb_ref[...]
    bonus = _seg_sum(xr * k2 * rk_ref[...], bones) * xv
    y_ref[...] = (yn + bonus) * gate

    @pl.when(step == n_steps - 1)
    def _fin():
        for q, p in sp:
            so_ref[q, 2 * p] = s_ref[q * N_PAIRS + p, 0:hd, 0:hd]
            so_ref[q, 2 * p + 1] = s_ref[q * N_PAIRS + p, hd:2 * hd, hd:2 * hd]


def _wkv(proj, prev0, s0, params, wl, bones, batch, seq, chunk, n_sub, n_seq=1, casts=()):
    ns = seq // (chunk * n_sub)
    assert n_seq == 1 or ns == 1, "several sequences per step only when a step covers them whole"
    rows = n_seq * chunk * n_sub
    nb = batch // n_seq
    rw = R_WIDTH

    def col(cb):
        return lambda b, c: (b * ns + c, cb)

    const2 = lambda b, c: (0, 0)
    vec = lambda n: pl.BlockSpec((1, n), const2)
    cast_in, cast_out, cast_shapes = [], [], []
    for wgt, span in casts:
        start, count = (0, wgt.shape[0]) if span is None else span
        blk, width = count // (nb * ns), wgt.shape[1]
        assert blk * nb * ns == count, (wgt.shape, span)
        if span is None:
            assert blk % 16 == 0, blk
            cast_in.append(pl.BlockSpec((blk, width), col(0)))
            cast_out.append(pl.BlockSpec((blk, width), col(0)))
            cast_shapes.append(jax.ShapeDtypeStruct((count, width), BF16))
        else:
            assert blk % 128 == 0 and start % 8 == 0, span
            cast_in.append(pl.BlockSpec(
                (pl.Element(blk), pl.Element(width)),
                lambda b, c, start=start, blk=blk: (pl.multiple_of(start + blk * (b * ns + c), 8), 0)))
            cast_out.append(pl.BlockSpec((width, blk), lambda b, c: (0, b * ns + c)))
            cast_shapes.append(jax.ShapeDtypeStruct((width, count), BF16))
    mu, mul, w0, a0, k_k, k_a, r_k, ln_w, ln_b = params
    outs = pl.pallas_call(
        functools.partial(_wkv_kernel, chunk=chunk, n_sub=n_sub, n_seq=n_seq, n_steps=ns,
                          cast_transposed=tuple(span is not None for _, span in casts)),
        out_shape=(jax.ShapeDtypeStruct((batch * seq, rw), F32),
                   jax.ShapeDtypeStruct((batch, R_HEADS, R_HEAD, R_HEAD), F32), *cast_shapes),
        grid=(nb, ns),
        in_specs=[
            pl.BlockSpec((rows, C_BLK), col(P_C // C_BLK)),
            pl.BlockSpec((rows, C_BLK), col(P_C // C_BLK + 1)),
            pl.BlockSpec((rows, LORA_PAD), col(P_L // LORA_PAD)),
            pl.BlockSpec((n_seq, 1, SHIFT_COLS), lambda b, c: (b, 0, 0)),
            pl.BlockSpec((n_seq, R_HEADS, R_HEAD, R_HEAD), lambda b, c: (b, 0, 0, 0)),
            vec(3 * rw), vec(LORA_PAD), vec(rw), vec(rw), vec(rw), vec(rw), vec(rw), vec(rw), vec(rw),
            pl.BlockSpec((LORA_PAD, 3 * rw), const2),
            pl.BlockSpec((SEG, SEG), const2),
            *cast_in,
        ],
        out_specs=(
            pl.BlockSpec((rows, rw), lambda b, c: (b * ns + c, 0)),
            pl.BlockSpec((n_seq, R_HEADS, R_HEAD, R_HEAD), lambda b, c: (b, 0, 0, 0)),
            *cast_out,
        ),
        scratch_shapes=[pltpu.VMEM((n_seq * N_PAIRS, PAIR, PAIR), F32), pltpu.VMEM((n_seq, SHIFT_COLS), F32)],
        compiler_params=pltpu.CompilerParams(
            dimension_semantics=("arbitrary", "arbitrary"), vmem_limit_bytes=VMEM_LIMIT),
        name="wkv",
    )(proj, proj, proj, prev0, s0, mu, mul, w0, a0, k_k, k_a, r_k, ln_w, ln_b, wl, bones,
      *[wgt for wgt, _ in casts])
    return outs[0], outs[1], outs[2:]


def _mix_kernel(x_ref, ya_ref, yr_ref, g_ref, wg_ref, wba_ref, wbr_ref, wo_ref, o_ref):
    x = x_ref[...]
    h = _rms(x, g_ref[...]).astype(BF16)
    ya = ya_ref[...].astype(BF16)
    yr = yr_ref[...].astype(BF16)
    acc = x
    for c in range(D_MODEL // MIX_CHUNK):
        ca = slice(c * MIX_CHUNK, (c + 1) * MIX_CHUNK)
        cr = slice(D_MODEL + c * MIX_CHUNK, D_MODEL + (c + 1) * MIX_CHUNK)
        mixed = (jax.nn.sigmoid(_mm(h, wg_ref[:, ca])) * _mm(ya, wba_ref[:, ca])
                 + jax.nn.sigmoid(_mm(h, wg_ref[:, cr])) * _mm(yr, wbr_ref[:, ca]))
        acc = acc + _mm(mixed.astype(BF16), wo_ref[ca, :])
    o_ref[...] = acc


def _resident(shape):
    return pl.BlockSpec(shape, lambda *_: (0,) * len(shape), pipeline_mode=pl.Buffered(1))


def _mix(x, ya, yr, g, wg, wba, wbr, wo, tm):
    m = x.shape[0]
    row = lambda i: (i, 0)
    return pl.pallas_call(
        _mix_kernel,
        out_shape=jax.ShapeDtypeStruct((m, D_MODEL), F32),
        grid=(m // tm,),
        in_specs=[
            pl.BlockSpec((tm, D_MODEL), row),
            pl.BlockSpec((tm, ATTN_WIDTH), row),
            pl.BlockSpec((tm, R_WIDTH), row),
            pl.BlockSpec((1, D_MODEL), lambda i: (0, 0)),
            _resident((D_MODEL, 2 * D_MODEL)),
            _resident((ATTN_WIDTH, D_MODEL)),
            _resident((R_WIDTH, D_MODEL)),
            _resident((D_MODEL, D_MODEL)),
        ],
        out_specs=pl.BlockSpec((tm, D_MODEL), row),
        compiler_params=pltpu.CompilerParams(
            dimension_semantics=("arbitrary",), vmem_limit_bytes=VMEM_LIMIT),
        name="mix",
    )(x, ya, yr, g, wg, wba, wbr, wo)


def _ffn_kernel(x_ref, g_ref, wu_ref, wd_ref, o_ref, h_ref):
    @pl.when(pl.program_id(1) == 0)
    def _():
        x = x_ref[...]
        h_ref[...] = _rms(x, g_ref[...]).astype(BF16)
        o_ref[...] = x

    u = _mm(h_ref[...], wu_ref[...])
    o_ref[...] += _mm(jnp.square(jnp.maximum(u, 0.0)).astype(BF16), wd_ref[...])


def _ffn(x, g, wu, wd, tm, tk):
    m = x.shape[0]
    return pl.pallas_call(
        _ffn_kernel,
        out_shape=jax.ShapeDtypeStruct((m, D_MODEL), F32),
        grid=(m // tm, D_FF // tk),
        in_specs=[
            pl.BlockSpec((tm, D_MODEL), lambda i, k: (i, 0)),
            pl.BlockSpec((1, D_MODEL), lambda i, k: (0, 0)),
            pl.BlockSpec((D_MODEL, tk), lambda i, k: (0, k)),
            pl.BlockSpec((tk, D_MODEL), lambda i, k: (k, 0)),
        ],
        out_specs=pl.BlockSpec((tm, D_MODEL), lambda i, k: (i, 0)),
        scratch_shapes=[pltpu.VMEM((tm, D_MODEL), BF16)],
        compiler_params=pltpu.CompilerParams(
            dimension_semantics=("arbitrary", "arbitrary"), vmem_limit_bytes=VMEM_LIMIT),
        name="ffn",
    )(x, g, wu, wd)


def _ple_kernel(x_ref, pe_ref, wg_ref, wp_ref, g_ref, o_ref):
    x = x_ref[...]
    gate = jax.nn.sigmoid(_mm(x.astype(BF16), wg_ref[...]))
    x = x + gate * _mm(pe_ref[...].astype(BF16), wp_ref[...])
    o_ref[...] = _rms(x, g_ref[...])


def _ple(x, pe, wg, wp, g, tm):
    m = x.shape[0]
    row = lambda i: (i, 0)
    return pl.pallas_call(
        _ple_kernel,
        out_shape=jax.ShapeDtypeStruct((m, D_MODEL), F32),
        grid=(m // tm,),
        in_specs=[
            pl.BlockSpec((tm, D_MODEL), row),
            pl.BlockSpec((tm, PLE_DIM), row),
            _resident((D_MODEL, D_MODEL)),
            _resident((PLE_DIM, D_MODEL)),
            pl.BlockSpec((1, D_MODEL), lambda i: (0, 0)),
        ],
        out_specs=pl.BlockSpec((tm, D_MODEL), row),
        compiler_params=pltpu.CompilerParams(
            dimension_semantics=("arbitrary",), vmem_limit_bytes=VMEM_LIMIT),
        name="ple",
    )(x, pe, wg, wp, g)


def _dense_tail(x, ya, yr, pe, wts, tiles):
    (g_mix, w_gates, wba, wbr, wo, g_ffn, wu, wd, wg, wp, g_fin) = wts
    x = _mix(x, ya, yr, g_mix, w_gates, wba, wbr, wo, tiles["mix_m"])
    x = _ffn(x, g_ffn, wu, wd, tiles["ffn_m"], tiles["ffn_k"])
    return _ple(x, pe, wg, wp, g_fin, tiles["ple_m"])


def _shift_out(proj, batch, seq):
    last = proj.reshape(batch, seq, P_COLS)[:, -1]
    return last[:, P_C:P_C + RWKV_COLS][None]


def kernel(x_prompt, x_sample, cache_k_win, cache_v_win, state_wkv, state_shift, p_prompt, p_sample,
           norm_mix, w_in, attn_sinks, rwkv_mu, rwkv_w0, rwkv_w2, rwkv_a0, rwkv_a2, rwkv_g2,
           rwkv_k_k, rwkv_k_a, rwkv_r_k, rwkv_ln_w, rwkv_ln_b, w_branch_attn, w_branch_rwkv,
           w_out, norm_ffn, w_ff_up, w_ff_down, w_ple_proj, w_ple_gate, norm_final):
    assert w_in.shape[0] == 1, "single-layer step"
    bp, tp = x_prompt.shape[0], x_prompt.shape[1]
    bs, ts = x_sample.shape[0], x_sample.shape[1]
    rw = R_WIDTH

    wl = jnp.zeros((LORA_PAD, 3 * rw), F32)
    wl = wl.at[0:DECAY_LORA, 0:rw].set(rwkv_w2[0])
    wl = wl.at[DECAY_LORA:DECAY_LORA + AAA_LORA, rw:2 * rw].set(rwkv_a2[0])
    wl = wl.at[DECAY_LORA + AAA_LORA:LORA_COLS, 2 * rw:3 * rw].set(rwkv_g2[0])
    wl = wl.astype(BF16)
    seg_id = np.arange(SEG) // R_HEAD
    bones = jnp.asarray(seg_id[:, None] == seg_id[None, :], BF16)
    mu = rwkv_mu[0]
    row = lambda v: v.reshape(1, -1)
    wkv_params = (row(mu[:3 * rw]), row(jnp.pad(mu[3 * rw:], (0, LORA_PAD - LORA_COLS))),
                  row(rwkv_w0[0]), row(rwkv_a0[0]), row(rwkv_k_k[0]), row(rwkv_k_a[0]),
                  row(rwkv_r_k[0]), row(rwkv_ln_w[0]), row(rwkv_ln_b[0]))
    g_mix = row(norm_mix[0])
    sinks = attn_sinks[0]

    tiles_p = dict(mix_m=256, ffn_m=1024, ffn_k=512, ple_m=512)
    xp = x_prompt.reshape(bp * tp, D_MODEL)
    w_in_t = jnp.transpose(w_in[0])
    proj_p, w_in_b = _proj_first(xp, g_mix, w_in_t, 1024, 1024)
    proj_p = _proj_rest(xp, g_mix, w_in_b, proj_p, 1024, 1024)
    ya_p = _attn_prompt(proj_p, sinks, bp, tp)
    yr_p, s_p, (wu, wd, wo, wg, wba, wbr, w_gates) = _wkv(
        proj_p, jnp.zeros((bp, 1, SHIFT_COLS), F32), jnp.zeros((bp, R_HEADS, R_HEAD, R_HEAD), F32),
        wkv_params, wl, bones, bp, tp, 64, 4,
        casts=((w_ff_up[0], None), (w_ff_down[0], None), (w_out[0], None), (w_ple_gate[0], None),
               (w_branch_attn[0], None), (w_branch_rwkv[0], None), (w_in_t, (P_GATES, 2 * D_MODEL))))
    dense = (g_mix, w_gates, wba, wbr, wo, row(norm_ffn[0]), wu, wd, wg, w_ple_proj[0].astype(BF16),
             row(norm_final))
    yp = _dense_tail(xp, ya_p, yr_p, p_prompt[0].reshape(bp * tp, PLE_DIM), dense, tiles_p)
    pp3 = proj_p.reshape(bp, tp, P_COLS)[:, -WINDOW:]
    k_p = pp3[:, :, P_K:P_K + KV_WIDTH].reshape(1, bp, WINDOW, N_KV_HEADS, HEAD_DIM)
    v_p = pp3[:, :, P_V:P_V + KV_WIDTH].reshape(1, bp, WINDOW, N_KV_HEADS, HEAD_DIM)

    ms = bs * ts
    tiles_s = dict(mix_m=ms, ffn_m=ms, ffn_k=512, ple_m=ms)
    xs = x_sample.reshape(ms, D_MODEL)
    proj_s = _proj(xs, g_mix, w_in_b, ms, 1024)
    ya_s, nk_s, nv_s = _attn_sample(proj_s, sinks, cache_k_win[0].reshape(bs, WINDOW, KV_WIDTH),
                                    cache_v_win[0].reshape(bs, WINDOW, KV_WIDTH), bs, ts, SAMPLE_GROUP)
    prev_s = jnp.pad(state_shift[0], ((0, 0), (0, LORA_PAD - LORA_COLS))).reshape(bs, 1, SHIFT_COLS)
    yr_s, s_s, _ = _wkv(proj_s, prev_s, state_wkv[0], wkv_params, wl, bones, bs, ts, ts, 1, n_seq=SAMPLE_GROUP)
    ys = _dense_tail(xs, ya_s, yr_s, p_sample[0].reshape(ms, PLE_DIM), dense, tiles_s)

    return (yp.reshape(bp, tp, D_MODEL), ys.reshape(bs, ts, D_MODEL),
            k_p, v_p, s_p[None], _shift_out(proj_p, bp, tp),
            nk_s.reshape(1, bs, WINDOW, N_KV_HEADS, HEAD_DIM), nv_s.reshape(1, bs, WINDOW, N_KV_HEADS, HEAD_DIM),
            s_s[None], _shift_out(proj_s, bs, ts))
```

```python
import functools

import numpy as np
import jax
import jax.numpy as jnp
from jax import lax
from jax.experimental import pallas as pl
from jax.experimental.pallas import tpu as pltpu

F32 = jnp.float32
BF16 = jnp.bfloat16

D_MODEL = 2048
PLE_DIM = 256
HEAD_DIM = 64
N_Q_HEADS = 16
N_KV_HEADS = 4
GQA_GROUP = 4
ATTN_WIDTH = 1024
KV_WIDTH = 256
WINDOW = 128
ALIBI_MAX = 8.0
R_HEAD = 64
R_WIDTH = 1024
R_HEADS = 16
DECAY_LORA = 64
AAA_LORA = 64
GATE_LORA = 160
LORA_COLS = DECAY_LORA + AAA_LORA + GATE_LORA
LORA_PAD = 512
RWKV_COLS = 3 * R_WIDTH + LORA_COLS
D_FF = 4 * D_MODEL
NORM_EPS = 1e-6
GN_EPS = 64e-5

P_Q = 0
P_K = ATTN_WIDTH
P_V = ATTN_WIDTH + KV_WIDTH
P_C = ATTN_WIDTH + 2 * KV_WIDTH
P_L = P_C + 3 * R_WIDTH
P_COLS = P_L + LORA_PAD
P_GATES = P_C + RWKV_COLS
C_BLK = P_C
SHIFT_COLS = 3 * R_WIDTH + LORA_PAD

PAIR = 128
N_PAIRS = R_WIDTH // PAIR
SEG = 256
MIX_CHUNK = 512
SAMPLE_GROUP = 8

VMEM_LIMIT = 56 * 1024 * 1024


def _mm(a, b):
    return jnp.dot(a, b, preferred_element_type=F32)


def _nt(a, b):
    return lax.dot_general(a, b, (((1,), (1,)), ((), ())), preferred_element_type=F32)


def _tn(a, b):
    return lax.dot_general(a, b, (((0,), (0,)), ((), ())), preferred_element_type=F32)


def _rms(x, g):
    ms = jnp.mean(x * x, axis=-1, keepdims=True)
    return x * lax.rsqrt(ms + NORM_EPS) * g


def _alibi_slope(hq):
    return float(2.0 ** (-ALIBI_MAX * (hq + 1) / N_Q_HEADS))


def _proj_kernel(x_ref, g_ref, w_ref, o_ref, h_ref):
    @pl.when(pl.program_id(1) == 0)
    def _():
        h_ref[...] = _rms(x_ref[...], g_ref[...]).astype(BF16)

    o_ref[...] = _nt(h_ref[...], w_ref[...])


def _proj_cast_kernel(x_ref, g_ref, w_ref, o_ref, wb_ref, h_ref):
    @pl.when(pl.program_id(0) == 0)
    def _():
        h_ref[...] = _rms(x_ref[...], g_ref[...]).astype(BF16)

    wb = w_ref[...].astype(BF16)
    wb_ref[...] = wb
    o_ref[...] = _nt(h_ref[...], wb)


def _proj_cast(x, g, wt_f32, tn):
    m = x.shape[0]
    return pl.pallas_call(
        _proj_cast_kernel,
        out_shape=(jax.ShapeDtypeStruct((m, P_COLS), F32), jax.ShapeDtypeStruct((P_COLS, D_MODEL), BF16)),
        grid=(P_COLS // tn,),
        in_specs=[
            pl.BlockSpec((m, D_MODEL), lambda j: (0, 0)),
            pl.BlockSpec((1, D_MODEL), lambda j: (0, 0)),
            pl.BlockSpec((tn, D_MODEL), lambda j: (j, 0)),
        ],
        out_specs=(pl.BlockSpec((m, tn), lambda j: (0, j)), pl.BlockSpec((tn, D_MODEL), lambda j: (j, 0))),
        scratch_shapes=[pltpu.VMEM((m, D_MODEL), BF16)],
        compiler_params=pltpu.CompilerParams(
            dimension_semantics=("arbitrary",), vmem_limit_bytes=VMEM_LIMIT),
        name="proj_cast",
    )(x, g, wt_f32)


def _proj(x, g, w, tm, tn):
    m = x.shape[0]
    return pl.pallas_call(
        _proj_kernel,
        out_shape=jax.ShapeDtypeStruct((m, P_COLS), F32),
        grid=(m // tm, P_COLS // tn),
        in_specs=[
            pl.BlockSpec((tm, D_MODEL), lambda i, j: (i, 0)),
            pl.BlockSpec((1, D_MODEL), lambda i, j: (0, 0)),
            pl.BlockSpec((tn, D_MODEL), lambda i, j: (j, 0)),
        ],
        out_specs=pl.BlockSpec((tm, tn), lambda i, j: (i, j)),
        scratch_shapes=[pltpu.VMEM((tm, D_MODEL), BF16)],
        compiler_params=pltpu.CompilerParams(
            dimension_semantics=("arbitrary", "arbitrary"), vmem_limit_bytes=VMEM_LIMIT),
        name="proj",
    )(x, g, w)


def _attend_heads(scores, values, sinks):
    heads = range(len(scores))
    m = []
    for h in heads:
        mh = sinks[h]
        for s in scores[h]:
            mh = jnp.maximum(mh, jnp.max(s, axis=-1, keepdims=True))
        m.append(mh)
    ps = [[jnp.exp(s - m[h]) for s in scores[h]] for h in heads]
    den = []
    for h in heads:
        dh = jnp.exp(sinks[h] - m[h])
        for p in ps[h]:
            dh = dh + jnp.sum(p, axis=-1, keepdims=True)
        den.append(dh)
    outs = []
    for h in heads:
        o = None
        for p, v in zip(ps[h], values[h]):
            t = _mm(p.astype(BF16), v)
            o = t if o is None else o + t
        outs.append(o * (1.0 / den[h]))
    return outs


def _head_slices():
    q_sl = [slice(hq * HEAD_DIM, (hq + 1) * HEAD_DIM) for hq in range(N_Q_HEADS)]
    kv_sl = [slice((hq // GQA_GROUP) * HEAD_DIM, (hq // GQA_GROUP + 1) * HEAD_DIM) for hq in range(N_Q_HEADS)]
    return q_sl, kv_sl


def _alibi_bias(dist, valid):
    slopes = np.array([_alibi_slope(hq) for hq in range(N_Q_HEADS)], np.float32)
    return np.where(valid[None], -slopes[:, None, None] * dist[None].astype(np.float32), -np.inf).astype(np.float32)


def _attn_prompt_kernel(sink_ref, bias_ref, q_ref, kp_ref, kc_ref, vp_ref, vc_ref, o_ref):
    w = WINDOW
    kj = lax.broadcasted_iota(jnp.int32, (w, 2 * w), 1)
    no_prev = (kj < w) & (pl.program_id(1) == 0)
    q = q_ref[...] * (HEAD_DIM ** -0.5)
    k2 = jnp.concatenate([kp_ref[...], kc_ref[...]], axis=0).astype(BF16)
    v2 = jnp.concatenate([vp_ref[...], vc_ref[...]], axis=0).astype(BF16)
    hqs = range(N_Q_HEADS)
    q_sl, kv_sl = _head_slices()
    qs = [q[:, q_sl[hq]].astype(BF16) for hq in hqs]
    sc = [[jnp.where(no_prev, -jnp.inf, _nt(qs[hq], k2[:, kv_sl[hq]]) + bias_ref[hq])] for hq in hqs]
    outs = _attend_heads(sc, [[v2[:, kv_sl[hq]]] for hq in hqs], [sink_ref[hq] for hq in hqs])
    for hq in hqs:
        o_ref[:, q_sl[hq]] = outs[hq]


def _attn_prompt(proj, sinks, batch, seq):
    nb = seq // WINDOW
    kcol, vcol = P_K // KV_WIDTH, P_V // KV_WIDTH

    def cur(c):
        return lambda b, i: (b * nb + i, c)

    def prev(c):
        return lambda b, i: (b * nb + jnp.maximum(i - 1, 0), c)

    ti = np.arange(WINDOW)[:, None]
    kj = np.arange(2 * WINDOW)[None, :]
    dist = ti - kj + WINDOW
    bias = _alibi_bias(dist, (dist >= 0) & (dist <= WINDOW))
    return pl.pallas_call(
        _attn_prompt_kernel,
        out_shape=jax.ShapeDtypeStruct((batch * seq, ATTN_WIDTH), F32),
        grid=(batch, nb),
        in_specs=[
            pl.BlockSpec(memory_space=pltpu.SMEM),
            _resident((N_Q_HEADS, WINDOW, 2 * WINDOW)),
            pl.BlockSpec((WINDOW, ATTN_WIDTH), cur(P_Q // ATTN_WIDTH)),
            pl.BlockSpec((WINDOW, KV_WIDTH), prev(kcol)),
            pl.BlockSpec((WINDOW, KV_WIDTH), cur(kcol)),
            pl.BlockSpec((WINDOW, KV_WIDTH), prev(vcol)),
            pl.BlockSpec((WINDOW, KV_WIDTH), cur(vcol)),
        ],
        out_specs=pl.BlockSpec((WINDOW, ATTN_WIDTH), lambda b, i: (b * nb + i, 0)),
        compiler_params=pltpu.CompilerParams(dimension_semantics=("arbitrary", "arbitrary")),
        name="attn_prompt",
    )(sinks, jnp.asarray(bias), proj, proj, proj, proj, proj)


def _attn_sample_kernel(sink_ref, q_ref, kn_ref, vn_ref, ck_ref, cv_ref, o_ref, nk_ref, nv_ref, *, seq):
    t, w = seq, WINDOW
    n_seq = q_ref.shape[0] // t
    ti = lax.broadcasted_iota(jnp.int32, (t, w), 0)
    cj = lax.broadcasted_iota(jnp.int32, (t, w), 1)
    dist_c = (ti - cj + w).astype(F32)
    valid_c = cj >= ti
    ti2 = lax.broadcasted_iota(jnp.int32, (t, t), 0)
    tj2 = lax.broadcasted_iota(jnp.int32, (t, t), 1)
    dist_n = (ti2 - tj2).astype(F32)
    valid_n = tj2 <= ti2
    hqs = range(N_Q_HEADS)
    q_sl, kv_sl = _head_slices()
    bias_c = [jnp.where(valid_c, -_alibi_slope(hq) * dist_c, -jnp.inf) for hq in hqs]
    bias_n = [jnp.where(valid_n, -_alibi_slope(hq) * dist_n, -jnp.inf) for hq in hqs]
    q_all = q_ref[...] * (HEAD_DIM ** -0.5)
    scores, values = [], []
    for s in range(n_seq):
        rs = slice(s * t, (s + 1) * t)
        kn, vn = kn_ref[rs, :], vn_ref[rs, :]
        ck, cv = ck_ref[s], cv_ref[s]
        nk_ref[s, 0:w - t, :] = ck[t:, :]
        nk_ref[s, w - t:w, :] = kn
        nv_ref[s, 0:w - t, :] = cv[t:, :]
        nv_ref[s, w - t:w, :] = vn
        ckb, cvb, knb, vnb = ck.astype(BF16), cv.astype(BF16), kn.astype(BF16), vn.astype(BF16)
        q = q_all[rs]
        qs = [q[:, q_sl[hq]].astype(BF16) for hq in hqs]
        scores += [[_nt(qs[hq], ckb[:, kv_sl[hq]]) + bias_c[hq], _nt(qs[hq], knb[:, kv_sl[hq]]) + bias_n[hq]]
                   for hq in hqs]
        values += [[cvb[:, kv_sl[hq]], vnb[:, kv_sl[hq]]] for hq in hqs]
    outs = _attend_heads(scores, values, [sink_ref[hq] for _ in range(n_seq) for hq in hqs])
    for s in range(n_seq):
        for hq in hqs:
            o_ref[s * t:(s + 1) * t, q_sl[hq]] = outs[s * N_Q_HEADS + hq]


def _attn_sample(proj, sinks, cache_k, cache_v, batch, seq, n_seq):
    kcol, vcol = P_K // KV_WIDTH, P_V // KV_WIDTH
    win = jax.ShapeDtypeStruct((batch, WINDOW, KV_WIDTH), F32)
    rows = n_seq * seq
    return pl.pallas_call(
        functools.partial(_attn_sample_kernel, seq=seq),
        out_shape=(jax.ShapeDtypeStruct((batch * seq, ATTN_WIDTH), F32), win, win),
        grid=(batch // n_seq,),
        in_specs=[
            pl.BlockSpec(memory_space=pltpu.SMEM),
            pl.BlockSpec((rows, ATTN_WIDTH), lambda b: (b, P_Q // ATTN_WIDTH)),
            pl.BlockSpec((rows, KV_WIDTH), lambda b: (b, kcol)),
            pl.BlockSpec((rows, KV_WIDTH), lambda b: (b, vcol)),
            pl.BlockSpec((n_seq, WINDOW, KV_WIDTH), lambda b: (b, 0, 0)),
            pl.BlockSpec((n_seq, WINDOW, KV_WIDTH), lambda b: (b, 0, 0)),
        ],
        out_specs=(
            pl.BlockSpec((rows, ATTN_WIDTH), lambda b: (b, 0)),
            pl.BlockSpec((n_seq, WINDOW, KV_WIDTH), lambda b: (b, 0, 0)),
            pl.BlockSpec((n_seq, WINDOW, KV_WIDTH), lambda b: (b, 0, 0)),
        ),
        compiler_params=pltpu.CompilerParams(dimension_semantics=("arbitrary",)),
        name="attn_sample",
    )(sinks, proj, proj, proj, cache_k, cache_v)


def _seg_sum(x, bones):
    rows = x.shape[0]
    hi = x.astype(BF16).astype(F32)
    lo = x - hi
    groups = [slice(j * SEG, (j + 1) * SEG) for j in range(R_WIDTH // SEG)]
    lhs = jnp.concatenate([t[:, sl] for sl in groups for t in (hi, lo)], axis=0).astype(BF16)
    out = _mm(lhs, bones)
    return jnp.concatenate(
        [out[2 * j * rows:(2 * j + 1) * rows] + out[(2 * j + 1) * rows:(2 * j + 2) * rows]
         for j in range(len(groups))], axis=1)


def _wkv_kernel(pa_ref, pb_ref, pl_ref, prev_ref, s0_ref,
                mu_ref, mul_ref, w0_ref, a0_ref, kk_ref, ka_ref, rk_ref, lnw_ref, lnb_ref,
                wl_ref, bones_ref, *rest, chunk, n_sub, n_seq, n_steps, cast_transposed):
    n_cast = len(cast_transposed)
    cast_src, (y_ref, so_ref) = rest[:n_cast], rest[n_cast:n_cast + 2]
    cast_dst, (s_ref, carry_ref) = rest[n_cast + 2:2 * n_cast + 2], rest[2 * n_cast + 2:]
    for src, dst, transposed in zip(cast_src, cast_dst, cast_transposed):
        dst[...] = (src[...].T if transposed else src[...]).astype(BF16)

    step = pl.program_id(1)
    cs = chunk
    seq_rows = n_sub * cs
    rows = n_seq * seq_rows
    gc = 2 * cs
    hd = R_HEAD
    w = R_WIDTH
    seqs = range(n_seq)

    @pl.when(step == 0)
    def _init():
        s_ref[...] = jnp.zeros(s_ref.shape, F32)
        for q in seqs:
            carry_ref[q:q + 1, :] = prev_ref[q]
            for p in range(N_PAIRS):
                s_ref[q * N_PAIRS + p, 0:hd, 0:hd] = s0_ref[q, 2 * p]
                s_ref[q * N_PAIRS + p, hd:2 * hd, hd:2 * hd] = s0_ref[q, 2 * p + 1]

    row = lax.broadcasted_iota(jnp.int32, (rows, 1), 0)

    def token_shift(x, lo, hi, mu):
        shifted = pltpu.roll(x, 1, axis=0)
        for q in seqs:
            shifted = jnp.where(row == q * seq_rows, carry_ref[q:q + 1, lo:hi], shifted)
        return x + (shifted - x) * mu

    cols = jnp.concatenate([pa_ref[...], pb_ref[...]], axis=1)
    l_raw = pl_ref[...]
    xx = token_shift(cols, 0, 3 * w, mu_ref[...])
    xr, xk, xv = xx[:, 0:w], xx[:, w:2 * w], xx[:, 2 * w:3 * w]
    xl = token_shift(l_raw, 3 * w, SHIFT_COLS, mul_ref[...])
    for q in seqs:
        last = (q + 1) * seq_rows
        carry_ref[q:q + 1, 0:3 * w] = cols[last - 1:last, :]
        carry_ref[q:q + 1, 3 * w:] = l_raw[last - 1:last, :]

    lane_l = lax.broadcasted_iota(jnp.int32, (1, LORA_PAD), 1)
    act = jnp.where(lane_l < DECAY_LORA, jnp.tanh(xl),
                    jnp.where(lane_l < DECAY_LORA + AAA_LORA, xl,
                              jnp.where(lane_l < LORA_COLS, jax.nn.sigmoid(xl), 0.0)))
    up = _mm(act.astype(BF16), wl_ref[...])
    z = -(w0_ref[...] + up[:, 0:w])
    softplus = jnp.maximum(z, 0.0) + jnp.log1p(jnp.exp(-jnp.abs(z)))
    lwd = -jnp.exp(-softplus - 0.5)
    a = jax.nn.sigmoid(a0_ref[...] + up[:, w:2 * w])
    gate = up[:, 2 * w:3 * w]

    bones = bones_ref[...]
    kkn = xk * kk_ref[...]
    kk = kkn * lax.rsqrt(jnp.maximum(_seg_sum(kkn * kkn, bones), 1e-24))
    k2 = xk * (1.0 + (a - 1.0) * ka_ref[...])
    b = kk * a

    tri = (lax.broadcasted_iota(jnp.int32, (cs, cs), 0)
           >= lax.broadcasted_iota(jnp.int32, (cs, cs), 1)).astype(BF16)
    ri = lax.broadcasted_iota(jnp.int32, (gc, gc), 0)
    ci = lax.broadcasted_iota(jnp.int32, (gc, gc), 1)
    same_head = (ri >= cs) == (ci >= cs)
    strict = same_head & (ci < ri)
    incl = same_head & (ci <= ri)
    eye = (ri == ci).astype(F32)
    head0 = lax.broadcasted_iota(jnp.int32, (1, PAIR), 1) < hd

    def stack(x):
        return jnp.concatenate([jnp.where(head0, x, 0.0), jnp.where(head0, 0.0, x)], axis=0).astype(BF16)

    pairs = range(N_PAIRS)
    lanes = [slice(p * PAIR, (p + 1) * PAIR) for p in pairs]
    squarings = cs.bit_length() - 2

    chunks = []
    for sub in range(n_seq * n_sub):
        rs = slice(sub * cs, (sub + 1) * cs)
        lw_c = lwd[rs]
        lw_hi = lw_c.astype(BF16)
        lw_r = lw_c - lw_hi.astype(F32)
        lw_mid = lw_r.astype(BF16)
        cum = _mm(tri, lw_hi) + _mm(tri, lw_mid) + _mm(tri, (lw_r - lw_mid.astype(F32)).astype(BF16))
        cum_last = cum[cs - 1:cs, :]
        e_inv = jnp.exp(-cum)
        e_last = jnp.exp(cum_last - cum)
        kq = kk[rs] * jnp.exp(cum - lw_c)
        rq = xr[rs] * jnp.exp(cum)
        kd = k2[rs] * e_inv
        bd = b[rs] * e_inv
        kdp = k2[rs] * e_last
        bdp = b[rs] * e_last
        xv_c = xv[rs]
        kq_s = [stack(kq[:, sl]) for sl in lanes]
        bd_s = [stack(bd[:, sl]) for sl in lanes]
        kd_s = [stack(kd[:, sl]) for sl in lanes]
        rq_s = [stack(rq[:, sl]) for sl in lanes]
        probes = [jnp.concatenate([kq_s[p], rq_s[p]], axis=0) for p in pairs]
        if gc % PAIR == 0:
            a_all = [_nt(probes[p], jnp.concatenate([bd_s[p], kd_s[p]], axis=0)) for p in pairs]
            a_kb = [jnp.where(strict, t[0:gc, 0:gc], 0.0) for t in a_all]
            a_kk = [jnp.where(strict, t[0:gc, gc:2 * gc], 0.0).astype(BF16) for t in a_all]
            a_rb = [jnp.where(incl, t[gc:2 * gc, 0:gc], 0.0).astype(BF16) for t in a_all]
            a_rk = [jnp.where(incl, t[gc:2 * gc, gc:2 * gc], 0.0).astype(BF16) for t in a_all]
        else:
            a_kb = [jnp.where(strict, _nt(kq_s[p], bd_s[p]), 0.0) for p in pairs]
            a_kk = [jnp.where(strict, _nt(kq_s[p], kd_s[p]), 0.0).astype(BF16) for p in pairs]
            a_rb = [jnp.where(incl, _nt(rq_s[p], bd_s[p]), 0.0).astype(BF16) for p in pairs]
            a_rk = [jnp.where(incl, _nt(rq_s[p], kd_s[p]), 0.0).astype(BF16) for p in pairs]
        inv = [eye - t for t in a_kb]
        if squarings >= 1:
            apow_b = [t.astype(BF16) for t in a_kb]
            apow_b = [_mm(t, t).astype(BF16) for t in apow_b]
            for _ in range(squarings - 1):
                both = [_mm(jnp.concatenate([apow_b[p], inv[p].astype(BF16)], axis=0), apow_b[p]) for p in pairs]
                apow_b = [t[0:gc].astype(BF16) for t in both]
                inv = [inv[p] + both[p][gc:2 * gc] for p in pairs]
            inv = [inv[p] + _mm(inv[p].astype(BF16), apow_b[p]) for p in pairs]
        chunks.append(dict(
            probes=probes, a_kk=a_kk, a_rk=a_rk, a_rb=a_rb, inv=[t.astype(BF16) for t in inv],
            v_s=[stack(xv_c[:, sl]) for sl in lanes],
            upd=[jnp.concatenate([stack(kdp[:, sl]), stack(bdp[:, sl])], axis=0) for sl in lanes],
            p_last=jnp.exp(cum_last)))

    sp = [(q, p) for q in seqs for p in pairs]
    state = {(q, p): s_ref[q * N_PAIRS + p] for q, p in sp}
    y_rows = [None] * (n_seq * n_sub)
    for level in range(n_sub):
        ch = {q: chunks[q * n_sub + level] for q in seqs}
        s_b = {k: state[k].astype(BF16) for k in sp}
        state_t = {(q, p): _nt(ch[q]["probes"][p], s_b[q, p]) for q, p in sp}
        rhs = {(q, p): state_t[q, p][0:gc] + _mm(ch[q]["a_kk"][p], ch[q]["v_s"][p]) for q, p in sp}
        u_b = {(q, p): _mm(ch[q]["inv"][p], rhs[q, p].astype(BF16)).astype(BF16) for q, p in sp}
        if gc % PAIR == 0:
            y_s = {(q, p): state_t[q, p][gc:2 * gc]
                   + _mm(jnp.concatenate([ch[q]["a_rk"][p], -ch[q]["a_rb"][p]], axis=1),
                         jnp.concatenate([ch[q]["v_s"][p], u_b[q, p]], axis=0)) for q, p in sp}
        else:
            y_s = {(q, p): state_t[q, p][gc:2 * gc] + _mm(ch[q]["a_rk"][p], ch[q]["v_s"][p])
                   - _mm(ch[q]["a_rb"][p], u_b[q, p]) for q, p in sp}
        state = {(q, p): state[q, p] * ch[q]["p_last"][:, lanes[p]]
                 + _tn(jnp.concatenate([ch[q]["v_s"][p], -u_b[q, p]], axis=0), ch[q]["upd"][p]) for q, p in sp}
        for q in seqs:
            y_rows[q * n_sub + level] = jnp.concatenate(
                [y_s[q, p][0:cs] + y_s[q, p][cs:gc] for p in pairs], axis=1)
    for q, p in sp:
        s_ref[q * N_PAIRS + p] = state[q, p]

    y = jnp.concatenate(y_rows, axis=0) if len(y_rows) > 1 else y_rows[0]
    mean = _seg_sum(y, bones) * (1.0 / hd)
    d = y - mean
    var = _seg_sum(d * d, bones) * (1.0 / hd)
    yn = d * lax.rsqrt(var + GN_EPS) * lnw_ref[...] + lnb_ref[...]
    bonus = _seg_sum(xr * k2 * rk_ref[...], bones) * xv
    y_ref[...] = (yn + bonus) * gate

    @pl.when(step == n_steps - 1)
    def _fin():
        for q, p in sp:
            so_ref[q, 2 * p] = s_ref[q * N_PAIRS + p, 0:hd, 0:hd]
            so_ref[q, 2 * p + 1] = s_ref[q * N_PAIRS + p, hd:2 * hd, hd:2 * hd]


def _wkv(proj, prev0, s0, params, wl, bones, batch, seq, chunk, n_sub, n_seq=1, casts=()):
    ns = seq // (chunk * n_sub)
    assert n_seq == 1 or ns == 1, "several sequences per step only when a step covers them whole"
    rows = n_seq * chunk * n_sub
    nb = batch // n_seq
    rw = R_WIDTH

    def col(cb):
        return lambda b, c: (b * ns + c, cb)

    const2 = lambda b, c: (0, 0)
    vec = lambda n: pl.BlockSpec((1, n), const2)
    cast_in, cast_out, cast_shapes = [], [], []
    for wgt, span in casts:
        start, count = (0, wgt.shape[0]) if span is None else span
        blk, width = count // (nb * ns), wgt.shape[1]
        assert blk * nb * ns == count, (wgt.shape, span)
        if span is None:
            assert blk % 16 == 0, blk
            cast_in.append(pl.BlockSpec((blk, width), col(0)))
            cast_out.append(pl.BlockSpec((blk, width), col(0)))
            cast_shapes.append(jax.ShapeDtypeStruct((count, width), BF16))
        else:
            assert blk % 128 == 0 and start % 8 == 0, span
            cast_in.append(pl.BlockSpec(
                (pl.Element(blk), pl.Element(width)),
                lambda b, c, start=start, blk=blk: (pl.multiple_of(start + blk * (b * ns + c), 8), 0)))
            cast_out.append(pl.BlockSpec((width, blk), lambda b, c: (0, b * ns + c)))
            cast_shapes.append(jax.ShapeDtypeStruct((width, count), BF16))
    mu, mul, w0, a0, k_k, k_a, r_k, ln_w, ln_b = params
    outs = pl.pallas_call(
        functools.partial(_wkv_kernel, chunk=chunk, n_sub=n_sub, n_seq=n_seq, n_steps=ns,
                          cast_transposed=tuple(span is not None for _, span in casts)),
        out_shape=(jax.ShapeDtypeStruct((batch * seq, rw), F32),
                   jax.ShapeDtypeStruct((batch, R_HEADS, R_HEAD, R_HEAD), F32), *cast_shapes),
        grid=(nb, ns),
        in_specs=[
            pl.BlockSpec((rows, C_BLK), col(P_C // C_BLK)),
            pl.BlockSpec((rows, C_BLK), col(P_C // C_BLK + 1)),
            pl.BlockSpec((rows, LORA_PAD), col(P_L // LORA_PAD)),
            pl.BlockSpec((n_seq, 1, SHIFT_COLS), lambda b, c: (b, 0, 0)),
            pl.BlockSpec((n_seq, R_HEADS, R_HEAD, R_HEAD), lambda b, c: (b, 0, 0, 0)),
            vec(3 * rw), vec(LORA_PAD), vec(rw), vec(rw), vec(rw), vec(rw), vec(rw), vec(rw), vec(rw),
            pl.BlockSpec((LORA_PAD, 3 * rw), const2),
            pl.BlockSpec((SEG, SEG), const2),
            *cast_in,
        ],
        out_specs=(
            pl.BlockSpec((rows, rw), lambda b, c: (b * ns + c, 0)),
            pl.BlockSpec((n_seq, R_HEADS, R_HEAD, R_HEAD), lambda b, c: (b, 0, 0, 0)),
            *cast_out,
        ),
        scratch_shapes=[pltpu.VMEM((n_seq * N_PAIRS, PAIR, PAIR), F32), pltpu.VMEM((n_seq, SHIFT_COLS), F32)],
        compiler_params=pltpu.CompilerParams(
            dimension_semantics=("arbitrary", "arbitrary"), vmem_limit_bytes=VMEM_LIMIT),
        name="wkv",
    )(proj, proj, proj, prev0, s0, mu, mul, w0, a0, k_k, k_a, r_k, ln_w, ln_b, wl, bones,
      *[wgt for wgt, _ in casts])
    return outs[0], outs[1], outs[2:]


def _mix_kernel(x_ref, ya_ref, yr_ref, g_ref, wg_ref, wba_ref, wbr_ref, wo_ref, o_ref):
    x = x_ref[...]
    h = _rms(x, g_ref[...]).astype(BF16)
    ya = ya_ref[...].astype(BF16)
    yr = yr_ref[...].astype(BF16)
    acc = x
    for c in range(D_MODEL // MIX_CHUNK):
        ca = slice(c * MIX_CHUNK, (c + 1) * MIX_CHUNK)
        cr = slice(D_MODEL + c * MIX_CHUNK, D_MODEL + (c + 1) * MIX_CHUNK)
        mixed = (jax.nn.sigmoid(_mm(h, wg_ref[:, ca])) * _mm(ya, wba_ref[:, ca])
                 + jax.nn.sigmoid(_mm(h, wg_ref[:, cr])) * _mm(yr, wbr_ref[:, ca]))
        acc = acc + _mm(mixed.astype(BF16), wo_ref[ca, :])
    o_ref[...] = acc


def _resident(shape):
    return pl.BlockSpec(shape, lambda *_: (0,) * len(shape), pipeline_mode=pl.Buffered(1))


def _mix(x, ya, yr, g, wg, wba, wbr, wo, tm):
    m = x.shape[0]
    row = lambda i: (i, 0)
    return pl.pallas_call(
        _mix_kernel,
        out_shape=jax.ShapeDtypeStruct((m, D_MODEL), F32),
        grid=(m // tm,),
        in_specs=[
            pl.BlockSpec((tm, D_MODEL), row),
            pl.BlockSpec((tm, ATTN_WIDTH), row),
            pl.BlockSpec((tm, R_WIDTH), row),
            pl.BlockSpec((1, D_MODEL), lambda i: (0, 0)),
            _resident((D_MODEL, 2 * D_MODEL)),
            _resident((ATTN_WIDTH, D_MODEL)),
            _resident((R_WIDTH, D_MODEL)),
            _resident((D_MODEL, D_MODEL)),
        ],
        out_specs=pl.BlockSpec((tm, D_MODEL), row),
        compiler_params=pltpu.CompilerParams(
            dimension_semantics=("arbitrary",), vmem_limit_bytes=VMEM_LIMIT),
        name="mix",
    )(x, ya, yr, g, wg, wba, wbr, wo)


def _ffn_kernel(x_ref, g_ref, wu_ref, wd_ref, o_ref, h_ref):
    @pl.when(pl.program_id(1) == 0)
    def _():
        x = x_ref[...]
        h_ref[...] = _rms(x, g_ref[...]).astype(BF16)
        o_ref[...] = x

    u = _mm(h_ref[...], wu_ref[...])
    o_ref[...] += _mm(jnp.square(jnp.maximum(u, 0.0)).astype(BF16), wd_ref[...])


def _ffn(x, g, wu, wd, tm, tk):
    m = x.shape[0]
    return pl.pallas_call(
        _ffn_kernel,
        out_shape=jax.ShapeDtypeStruct((m, D_MODEL), F32),
        grid=(m // tm, D_FF // tk),
        in_specs=[
            pl.BlockSpec((tm, D_MODEL), lambda i, k: (i, 0)),
            pl.BlockSpec((1, D_MODEL), lambda i, k: (0, 0)),
            pl.BlockSpec((D_MODEL, tk), lambda i, k: (0, k)),
            pl.BlockSpec((tk, D_MODEL), lambda i, k: (k, 0)),
        ],
        out_specs=pl.BlockSpec((tm, D_MODEL), lambda i, k: (i, 0)),
        scratch_shapes=[pltpu.VMEM((tm, D_MODEL), BF16)],
        compiler_params=pltpu.CompilerParams(
            dimension_semantics=("arbitrary", "arbitrary"), vmem_limit_bytes=VMEM_LIMIT),
        name="ffn",
    )(x, g, wu, wd)


def _ple_kernel(x_ref, pe_ref, wg_ref, wp_ref, g_ref, o_ref):
    x = x_ref[...]
    gate = jax.nn.sigmoid(_mm(x.astype(BF16), wg_ref[...]))
    x = x + gate * _mm(pe_ref[...].astype(BF16), wp_ref[...])
    o_ref[...] = _rms(x, g_ref[...])


def _ple(x, pe, wg, wp, g, tm):
    m = x.shape[0]
    row = lambda i: (i, 0)
    return pl.pallas_call(
        _ple_kernel,
        out_shape=jax.ShapeDtypeStruct((m, D_MODEL), F32),
        grid=(m // tm,),
        in_specs=[
            pl.BlockSpec((tm, D_MODEL), row),
            pl.BlockSpec((tm, PLE_DIM), row),
            _resident((D_MODEL, D_MODEL)),
            _resident((PLE_DIM, D_MODEL)),
            pl.BlockSpec((1, D_MODEL), lambda i: (0, 0)),
        ],
        out_specs=pl.BlockSpec((tm, D_MODEL), row),
        compiler_params=pltpu.CompilerParams(
            dimension_semantics=("arbitrary",), vmem_limit_bytes=VMEM_LIMIT),
        name="ple",
    )(x, pe, wg, wp, g)


def _dense_tail(x, ya, yr, pe, wts, tiles):
    (g_mix, w_gates, wba, wbr, wo, g_ffn, wu, wd, wg, wp, g_fin) = wts
    x = _mix(x, ya, yr, g_mix, w_gates, wba, wbr, wo, tiles["mix_m"])
    x = _ffn(x, g_ffn, wu, wd, tiles["ffn_m"], tiles["ffn_k"])
    return _ple(x, pe, wg, wp, g_fin, tiles["ple_m"])


def _shift_out(proj, batch, seq):
    last = proj.reshape(batch, seq, P_COLS)[:, -1]
    return last[:, P_C:P_C + RWKV_COLS][None]


def kernel(x_prompt, x_sample, cache_k_win, cache_v_win, state_wkv, state_shift, p_prompt, p_sample,
           norm_mix, w_in, attn_sinks, rwkv_mu, rwkv_w0, rwkv_w2, rwkv_a0, rwkv_a2, rwkv_g2,
           rwkv_k_k, rwkv_k_a, rwkv_r_k, rwkv_ln_w, rwkv_ln_b, w_branch_attn, w_branch_rwkv,
           w_out, norm_ffn, w_ff_up, w_ff_down, w_ple_proj, w_ple_gate, norm_final):
    assert w_in.shape[0] == 1, "single-layer step"
    bp, tp = x_prompt.shape[0], x_prompt.shape[1]
    bs, ts = x_sample.shape[0], x_sample.shape[1]
    rw = R_WIDTH

    wl = jnp.zeros((LORA_PAD, 3 * rw), F32)
    wl = wl.at[0:DECAY_LORA, 0:rw].set(rwkv_w2[0])
    wl = wl.at[DECAY_LORA:DECAY_LORA + AAA_LORA, rw:2 * rw].set(rwkv_a2[0])
    wl = wl.at[DECAY_LORA + AAA_LORA:LORA_COLS, 2 * rw:3 * rw].set(rwkv_g2[0])
    wl = wl.astype(BF16)
    seg_id = np.arange(SEG) // R_HEAD
    bones = jnp.asarray(seg_id[:, None] == seg_id[None, :], BF16)
    mu = rwkv_mu[0]
    row = lambda v: v.reshape(1, -1)
    wkv_params = (row(mu[:3 * rw]), row(jnp.pad(mu[3 * rw:], (0, LORA_PAD - LORA_COLS))),
                  row(rwkv_w0[0]), row(rwkv_a0[0]), row(rwkv_k_k[0]), row(rwkv_k_a[0]),
                  row(rwkv_r_k[0]), row(rwkv_ln_w[0]), row(rwkv_ln_b[0]))
    g_mix = row(norm_mix[0])
    sinks = attn_sinks[0]

    ms = bs * ts
    xs = x_sample.reshape(ms, D_MODEL)
    w_in_t = jnp.transpose(w_in[0])
    proj_s, w_in_b = _proj_cast(xs, g_mix, w_in_t, 1024)

    tiles_p = dict(mix_m=256, ffn_m=1024, ffn_k=512, ple_m=512)
    xp = x_prompt.reshape(bp * tp, D_MODEL)
    proj_p = _proj(xp, g_mix, w_in_b, 1024, 1024)
    ya_p = _attn_prompt(proj_p, sinks, bp, tp)
    yr_p, s_p, (wu, wd, wo, wg, wba, wbr, w_gates) = _wkv(
        proj_p, jnp.zeros((bp, 1, SHIFT_COLS), F32), jnp.zeros((bp, R_HEADS, R_HEAD, R_HEAD), F32),
        wkv_params, wl, bones, bp, tp, 64, 4,
        casts=((w_ff_up[0], None), (w_ff_down[0], None), (w_out[0], None), (w_ple_gate[0], None),
               (w_branch_attn[0], None), (w_branch_rwkv[0], None), (w_in_t, (P_GATES, 2 * D_MODEL))))
    dense = (g_mix, w_gates, wba, wbr, wo, row(norm_ffn[0]), wu, wd, wg, w_ple_proj[0].astype(BF16),
             row(norm_final))
    yp = _dense_tail(xp, ya_p, yr_p, p_prompt[0].reshape(bp * tp, PLE_DIM), dense, tiles_p)
    pp3 = proj_p.reshape(bp, tp, P_COLS)[:, -WINDOW:]
    k_p = pp3[:, :, P_K:P_K + KV_WIDTH].reshape(1, bp, WINDOW, N_KV_HEADS, HEAD_DIM)
    v_p = pp3[:, :, P_V:P_V + KV_WIDTH].reshape(1, bp, WINDOW, N_KV_HEADS, HEAD_DIM)

    tiles_s = dict(mix_m=ms, ffn_m=ms, ffn_k=512, ple_m=ms)
    ya_s, nk_s, nv_s = _attn_sample(proj_s, sinks, cache_k_win[0].reshape(bs, WINDOW, KV_WIDTH),
                                    cache_v_win[0].reshape(bs, WINDOW, KV_WIDTH), bs, ts, SAMPLE_GROUP)
    prev_s = jnp.pad(state_shift[0], ((0, 0), (0, LORA_PAD - LORA_COLS))).reshape(bs, 1, SHIFT_COLS)
    yr_s, s_s, _ = _wkv(proj_s, prev_s, state_wkv[0], wkv_params, wl, bones, bs, ts, ts, 1, n_seq=SAMPLE_GROUP)
    ys = _dense_tail(xs, ya_s, yr_s, p_sample[0].reshape(ms, PLE_DIM), dense, tiles_s)

    return (yp.reshape(bp, tp, D_MODEL), ys.reshape(bs, ts, D_MODEL),
            k_p, v_p, s_p[None], _shift_out(proj_p, bp, tp),
            nk_s.reshape(1, bs, WINDOW, N_KV_HEADS, HEAD_DIM), nv_s.reshape(1, bs, WINDOW, N_KV_HEADS, HEAD_DIM),
            s_s[None], _shift_out(proj_s, bs, ts))
```

```python
import functools

import numpy as np
import jax
import jax.numpy as jnp
from jax import lax
from jax.experimental import pallas as pl
from jax.experimental.pallas import tpu as pltpu

F32 = jnp.float32
BF16 = jnp.bfloat16

D_MODEL = 2048
PLE_DIM = 256
HEAD_DIM = 64
N_Q_HEADS = 16
N_KV_HEADS = 4
GQA_GROUP = 4
ATTN_WIDTH = 1024
KV_WIDTH = 256
WINDOW = 128
ALIBI_MAX = 8.0
R_HEAD = 64
R_WIDTH = 1024
R_HEADS = 16
DECAY_LORA = 64
AAA_LORA = 64
GATE_LORA = 160
LORA_COLS = DECAY_LORA + AAA_LORA + GATE_LORA
LORA_PAD = 512
RWKV_COLS = 3 * R_WIDTH + LORA_COLS
D_FF = 4 * D_MODEL
NORM_EPS = 1e-6
GN_EPS = 64e-5

P_Q = 0
P_K = ATTN_WIDTH
P_V = ATTN_WIDTH + KV_WIDTH
P_C = ATTN_WIDTH + 2 * KV_WIDTH
P_L = P_C + 3 * R_WIDTH
P_COLS = P_L + LORA_PAD
P_GATES = P_C + RWKV_COLS
C_BLK = P_C
SHIFT_COLS = 3 * R_WIDTH + LORA_PAD

GROUP = 4
GROUP_W = GROUP * R_HEAD
N_GROUPS = R_HEADS // GROUP
SEG = 256
MIX_CHUNK = 512
SAMPLE_GROUP = 8

VMEM_LIMIT = 56 * 1024 * 1024


def _mm(a, b):
    return jnp.dot(a, b, preferred_element_type=F32)


def _nt(a, b):
    return lax.dot_general(a, b, (((1,), (1,)), ((), ())), preferred_element_type=F32)


def _tn(a, b):
    return lax.dot_general(a, b, (((0,), (0,)), ((), ())), preferred_element_type=F32)


def _rms(x, g):
    ms = jnp.mean(x * x, axis=-1, keepdims=True)
    return x * lax.rsqrt(ms + NORM_EPS) * g


def _alibi_slope(hq):
    return float(2.0 ** (-ALIBI_MAX * (hq + 1) / N_Q_HEADS))


def _proj_kernel(x_ref, g_ref, w_ref, o_ref, h_ref):
    @pl.when(pl.program_id(1) == 0)
    def _():
        h_ref[...] = _rms(x_ref[...], g_ref[...]).astype(BF16)

    o_ref[...] = _nt(h_ref[...], w_ref[...])


def _proj_cast_kernel(x_ref, g_ref, w_ref, o_ref, wb_ref, h_ref):
    @pl.when(pl.program_id(0) == 0)
    def _():
        h_ref[...] = _rms(x_ref[...], g_ref[...]).astype(BF16)

    wb = w_ref[...].astype(BF16)
    wb_ref[...] = wb
    o_ref[...] = _nt(h_ref[...], wb)


def _proj_cast(x, g, wt_f32, tn):
    m = x.shape[0]
    return pl.pallas_call(
        _proj_cast_kernel,
        out_shape=(jax.ShapeDtypeStruct((m, P_COLS), F32), jax.ShapeDtypeStruct((P_COLS, D_MODEL), BF16)),
        grid=(P_COLS // tn,),
        in_specs=[
            pl.BlockSpec((m, D_MODEL), lambda j: (0, 0)),
            pl.BlockSpec((1, D_MODEL), lambda j: (0, 0)),
            pl.BlockSpec((tn, D_MODEL), lambda j: (j, 0)),
        ],
        out_specs=(pl.BlockSpec((m, tn), lambda j: (0, j)), pl.BlockSpec((tn, D_MODEL), lambda j: (j, 0))),
        scratch_shapes=[pltpu.VMEM((m, D_MODEL), BF16)],
        compiler_params=pltpu.CompilerParams(
            dimension_semantics=("arbitrary",), vmem_limit_bytes=VMEM_LIMIT),
        name="proj_cast",
    )(x, g, wt_f32)


def _proj(x, g, w, tm, tn):
    m = x.shape[0]
    return pl.pallas_call(
        _proj_kernel,
        out_shape=jax.ShapeDtypeStruct((m, P_COLS), F32),
        grid=(m // tm, P_COLS // tn),
        in_specs=[
            pl.BlockSpec((tm, D_MODEL), lambda i, j: (i, 0)),
            pl.BlockSpec((1, D_MODEL), lambda i, j: (0, 0)),
            pl.BlockSpec((tn, D_MODEL), lambda i, j: (j, 0)),
        ],
        out_specs=pl.BlockSpec((tm, tn), lambda i, j: (i, j)),
        scratch_shapes=[pltpu.VMEM((tm, D_MODEL), BF16)],
        compiler_params=pltpu.CompilerParams(
            dimension_semantics=("arbitrary", "arbitrary"), vmem_limit_bytes=VMEM_LIMIT),
        name="proj",
    )(x, g, w)


def _attend_heads(scores, values, sinks):
    heads = range(len(scores))
    m = []
    for h in heads:
        mh = sinks[h]
        for s in scores[h]:
            mh = jnp.maximum(mh, jnp.max(s, axis=-1, keepdims=True))
        m.append(mh)
    ps = [[jnp.exp(s - m[h]) for s in scores[h]] for h in heads]
    den = []
    for h in heads:
        dh = jnp.exp(sinks[h] - m[h])
        for p in ps[h]:
            dh = dh + jnp.sum(p, axis=-1, keepdims=True)
        den.append(dh)
    outs = []
    for h in heads:
        o = None
        for p, v in zip(ps[h], values[h]):
            t = _mm(p.astype(BF16), v)
            o = t if o is None else o + t
        outs.append(o * (1.0 / den[h]))
    return outs


def _head_slices():
    q_sl = [slice(hq * HEAD_DIM, (hq + 1) * HEAD_DIM) for hq in range(N_Q_HEADS)]
    kv_sl = [slice((hq // GQA_GROUP) * HEAD_DIM, (hq // GQA_GROUP + 1) * HEAD_DIM) for hq in range(N_Q_HEADS)]
    return q_sl, kv_sl


def _alibi_bias(dist, valid):
    slopes = np.array([_alibi_slope(hq) for hq in range(N_Q_HEADS)], np.float32)
    return np.where(valid[None], -slopes[:, None, None] * dist[None].astype(np.float32), -np.inf).astype(np.float32)


def _attn_prompt_kernel(sink_ref, bias_ref, q_ref, kp_ref, kc_ref, vp_ref, vc_ref, o_ref):
    w = WINDOW
    kj = lax.broadcasted_iota(jnp.int32, (w, 2 * w), 1)
    no_prev = (kj < w) & (pl.program_id(1) == 0)
    q = q_ref[...] * (HEAD_DIM ** -0.5)
    k2 = jnp.concatenate([kp_ref[...], kc_ref[...]], axis=0).astype(BF16)
    v2 = jnp.concatenate([vp_ref[...], vc_ref[...]], axis=0).astype(BF16)
    hqs = range(N_Q_HEADS)
    q_sl, kv_sl = _head_slices()
    qs = [q[:, q_sl[hq]].astype(BF16) for hq in hqs]
    sc = [[jnp.where(no_prev, -jnp.inf, _nt(qs[hq], k2[:, kv_sl[hq]]) + bias_ref[hq])] for hq in hqs]
    outs = _attend_heads(sc, [[v2[:, kv_sl[hq]]] for hq in hqs], [sink_ref[hq] for hq in hqs])
    for hq in hqs:
        o_ref[:, q_sl[hq]] = outs[hq]


def _attn_prompt(proj, sinks, batch, seq):
    nb = seq // WINDOW
    kcol, vcol = P_K // KV_WIDTH, P_V // KV_WIDTH

    def cur(c):
        return lambda b, i: (b * nb + i, c)

    def prev(c):
        return lambda b, i: (b * nb + jnp.maximum(i - 1, 0), c)

    ti = np.arange(WINDOW)[:, None]
    kj = np.arange(2 * WINDOW)[None, :]
    dist = ti - kj + WINDOW
    bias = _alibi_bias(dist, (dist >= 0) & (dist <= WINDOW))
    return pl.pallas_call(
        _attn_prompt_kernel,
        out_shape=jax.ShapeDtypeStruct((batch * seq, ATTN_WIDTH), F32),
        grid=(batch, nb),
        in_specs=[
            pl.BlockSpec(memory_space=pltpu.SMEM),
            _resident((N_Q_HEADS, WINDOW, 2 * WINDOW)),
            pl.BlockSpec((WINDOW, ATTN_WIDTH), cur(P_Q // ATTN_WIDTH)),
            pl.BlockSpec((WINDOW, KV_WIDTH), prev(kcol)),
            pl.BlockSpec((WINDOW, KV_WIDTH), cur(kcol)),
            pl.BlockSpec((WINDOW, KV_WIDTH), prev(vcol)),
            pl.BlockSpec((WINDOW, KV_WIDTH), cur(vcol)),
        ],
        out_specs=pl.BlockSpec((WINDOW, ATTN_WIDTH), lambda b, i: (b * nb + i, 0)),
        compiler_params=pltpu.CompilerParams(dimension_semantics=("arbitrary", "arbitrary")),
        name="attn_prompt",
    )(sinks, jnp.asarray(bias), proj, proj, proj, proj, proj)


def _attn_sample_kernel(sink_ref, q_ref, kn_ref, vn_ref, ck_ref, cv_ref, o_ref, nk_ref, nv_ref, *, seq):
    t, w = seq, WINDOW
    n_seq = q_ref.shape[0] // t
    ti = lax.broadcasted_iota(jnp.int32, (t, w), 0)
    cj = lax.broadcasted_iota(jnp.int32, (t, w), 1)
    dist_c = (ti - cj + w).astype(F32)
    valid_c = cj >= ti
    ti2 = lax.broadcasted_iota(jnp.int32, (t, t), 0)
    tj2 = lax.broadcasted_iota(jnp.int32, (t, t), 1)
    dist_n = (ti2 - tj2).astype(F32)
    valid_n = tj2 <= ti2
    hqs = range(N_Q_HEADS)
    q_sl, kv_sl = _head_slices()
    bias_c = [jnp.where(valid_c, -_alibi_slope(hq) * dist_c, -jnp.inf) for hq in hqs]
    bias_n = [jnp.where(valid_n, -_alibi_slope(hq) * dist_n, -jnp.inf) for hq in hqs]
    q_all = q_ref[...] * (HEAD_DIM ** -0.5)
    scores, values = [], []
    for s in range(n_seq):
        rs = slice(s * t, (s + 1) * t)
        kn, vn = kn_ref[rs, :], vn_ref[rs, :]
        ck, cv = ck_ref[s], cv_ref[s]
        nk_ref[s, 0:w - t, :] = ck[t:, :]
        nk_ref[s, w - t:w, :] = kn
        nv_ref[s, 0:w - t, :] = cv[t:, :]
        nv_ref[s, w - t:w, :] = vn
        ckb, cvb, knb, vnb = ck.astype(BF16), cv.astype(BF16), kn.astype(BF16), vn.astype(BF16)
        q = q_all[rs]
        qs = [q[:, q_sl[hq]].astype(BF16) for hq in hqs]
        scores += [[_nt(qs[hq], ckb[:, kv_sl[hq]]) + bias_c[hq], _nt(qs[hq], knb[:, kv_sl[hq]]) + bias_n[hq]]
                   for hq in hqs]
        values += [[cvb[:, kv_sl[hq]], vnb[:, kv_sl[hq]]] for hq in hqs]
    outs = _attend_heads(scores, values, [sink_ref[hq] for _ in range(n_seq) for hq in hqs])
    for s in range(n_seq):
        for hq in hqs:
            o_ref[s * t:(s + 1) * t, q_sl[hq]] = outs[s * N_Q_HEADS + hq]


def _attn_sample(proj, sinks, cache_k, cache_v, batch, seq, n_seq):
    kcol, vcol = P_K // KV_WIDTH, P_V // KV_WIDTH
    win = jax.ShapeDtypeStruct((batch, WINDOW, KV_WIDTH), F32)
    rows = n_seq * seq
    return pl.pallas_call(
        functools.partial(_attn_sample_kernel, seq=seq),
        out_shape=(jax.ShapeDtypeStruct((batch * seq, ATTN_WIDTH), F32), win, win),
        grid=(batch // n_seq,),
        in_specs=[
            pl.BlockSpec(memory_space=pltpu.SMEM),
            pl.BlockSpec((rows, ATTN_WIDTH), lambda b: (b, P_Q // ATTN_WIDTH)),
            pl.BlockSpec((rows, KV_WIDTH), lambda b: (b, kcol)),
            pl.BlockSpec((rows, KV_WIDTH), lambda b: (b, vcol)),
            pl.BlockSpec((n_seq, WINDOW, KV_WIDTH), lambda b: (b, 0, 0)),
            pl.BlockSpec((n_seq, WINDOW, KV_WIDTH), lambda b: (b, 0, 0)),
        ],
        out_specs=(
            pl.BlockSpec((rows, ATTN_WIDTH), lambda b: (b, 0)),
            pl.BlockSpec((n_seq, WINDOW, KV_WIDTH), lambda b: (b, 0, 0)),
            pl.BlockSpec((n_seq, WINDOW, KV_WIDTH), lambda b: (b, 0, 0)),
        ),
        compiler_params=pltpu.CompilerParams(dimension_semantics=("arbitrary",)),
        name="attn_sample",
    )(sinks, proj, proj, proj, cache_k, cache_v)


def _seg_sum(x, bones):
    rows = x.shape[0]
    hi = x.astype(BF16).astype(F32)
    lo = x - hi
    groups = [slice(j * SEG, (j + 1) * SEG) for j in range(R_WIDTH // SEG)]
    lhs = jnp.concatenate([t[:, sl] for sl in groups for t in (hi, lo)], axis=0).astype(BF16)
    out = _mm(lhs, bones)
    return jnp.concatenate(
        [out[2 * j * rows:(2 * j + 1) * rows] + out[(2 * j + 1) * rows:(2 * j + 2) * rows]
         for j in range(len(groups))], axis=1)


def _wkv_kernel(pa_ref, pb_ref, pl_ref, prev_ref, s0_ref,
                mu_ref, mul_ref, w0_ref, a0_ref, kk_ref, ka_ref, rk_ref, lnw_ref, lnb_ref,
                wl_ref, bones_ref, *rest, chunk, n_sub, n_seq, n_steps, cast_transposed):
    n_cast = len(cast_transposed)
    cast_src, (y_ref, so_ref) = rest[:n_cast], rest[n_cast:n_cast + 2]
    cast_dst, (s_ref, carry_ref) = rest[n_cast + 2:2 * n_cast + 2], rest[2 * n_cast + 2:]
    for src, dst, transposed in zip(cast_src, cast_dst, cast_transposed):
        dst[...] = (src[...].T if transposed else src[...]).astype(BF16)

    step = pl.program_id(1)
    cs = chunk
    seq_rows = n_sub * cs
    rows = n_seq * seq_rows
    hd = R_HEAD
    w = R_WIDTH
    seqs = range(n_seq)

    @pl.when(step == 0)
    def _init():
        s_ref[...] = jnp.zeros(s_ref.shape, F32)
        for q in seqs:
            carry_ref[q:q + 1, :] = prev_ref[q]
            for h in range(R_HEADS):
                o = (h % GROUP) * hd
                s_ref[q * N_GROUPS + h // GROUP, o:o + hd, o:o + hd] = s0_ref[q, h]

    row = lax.broadcasted_iota(jnp.int32, (rows, 1), 0)

    def token_shift(x, lo, hi, mu):
        shifted = pltpu.roll(x, 1, axis=0)
        for q in seqs:
            shifted = jnp.where(row == q * seq_rows, carry_ref[q:q + 1, lo:hi], shifted)
        return x + (shifted - x) * mu

    cols = jnp.concatenate([pa_ref[...], pb_ref[...]], axis=1)
    l_raw = pl_ref[...]
    xx = token_shift(cols, 0, 3 * w, mu_ref[...])
    xr, xk, xv = xx[:, 0:w], xx[:, w:2 * w], xx[:, 2 * w:3 * w]
    xl = token_shift(l_raw, 3 * w, SHIFT_COLS, mul_ref[...])
    for q in seqs:
        last = (q + 1) * seq_rows
        carry_ref[q:q + 1, 0:3 * w] = cols[last - 1:last, :]
        carry_ref[q:q + 1, 3 * w:] = l_raw[last - 1:last, :]

    lane_l = lax.broadcasted_iota(jnp.int32, (1, LORA_PAD), 1)
    act = jnp.where(lane_l < DECAY_LORA, jnp.tanh(xl),
                    jnp.where(lane_l < DECAY_LORA + AAA_LORA, xl,
                              jnp.where(lane_l < LORA_COLS, jax.nn.sigmoid(xl), 0.0)))
    up = _mm(act.astype(BF16), wl_ref[...])
    z = -(w0_ref[...] + up[:, 0:w])
    softplus = jnp.maximum(z, 0.0) + jnp.log1p(jnp.exp(-jnp.abs(z)))
    lwd = -jnp.exp(-softplus - 0.5)
    a = jax.nn.sigmoid(a0_ref[...] + up[:, w:2 * w])
    gate = up[:, 2 * w:3 * w]

    bones = bones_ref[...]
    kkn = xk * kk_ref[...]
    kk = kkn * lax.rsqrt(jnp.maximum(_seg_sum(kkn * kkn, bones), 1e-24))
    k2 = xk * (1.0 + (a - 1.0) * ka_ref[...])
    b = kk * a

    tri = (lax.broadcasted_iota(jnp.int32, (cs, cs), 0)
           >= lax.broadcasted_iota(jnp.int32, (cs, cs), 1)).astype(BF16)
    gcs = GROUP * cs
    tok = lax.broadcasted_iota(jnp.int32, (cs, gcs), 0)
    src = lax.broadcasted_iota(jnp.int32, (cs, gcs), 1) % cs
    strict = src < tok
    incl = src <= tok
    eye = (src == tok).astype(F32)
    lane_k = lax.broadcasted_iota(jnp.int32, (1, GROUP_W), 1) // hd
    lane_t = lax.broadcasted_iota(jnp.int32, (1, gcs), 1) // cs
    state_mask = (lax.broadcasted_iota(jnp.int32, (GROUP_W, GROUP_W), 0) // hd
                  == lax.broadcasted_iota(jnp.int32, (GROUP_W, GROUP_W), 1) // hd)

    def block_diag(x, lane_head):
        xb = x.astype(BF16)
        zero = jnp.zeros_like(xb)
        return jnp.concatenate([jnp.where(lane_head == g, xb, zero) for g in range(GROUP)], axis=0)

    groups = range(N_GROUPS)
    lanes = [slice(g * GROUP_W, (g + 1) * GROUP_W) for g in groups]
    squarings = cs.bit_length() - 2
    aligned = gcs % 128 == 0

    def chunk_setup(sub):
        rs = slice(sub * cs, (sub + 1) * cs)
        lw_c = lwd[rs]
        lw_hi = lw_c.astype(BF16)
        lw_r = lw_c - lw_hi.astype(F32)
        lw_mid = lw_r.astype(BF16)
        cum = _mm(tri, lw_hi) + _mm(tri, lw_mid) + _mm(tri, (lw_r - lw_mid.astype(F32)).astype(BF16))
        cum_last = cum[cs - 1:cs, :]
        e_inv = jnp.exp(-cum)
        e_last = jnp.exp(cum_last - cum)
        kq = kk[rs] * jnp.exp(cum - lw_c)
        rq = xr[rs] * jnp.exp(cum)
        kd = k2[rs] * e_inv
        bd = b[rs] * e_inv
        kdp = k2[rs] * e_last
        bdp = b[rs] * e_last
        xv_c = xv[rs]
        probes = [jnp.concatenate([kq[:, sl], rq[:, sl]], axis=0).astype(BF16) for sl in lanes]
        bd_d = [block_diag(bd[:, sl], lane_k) for sl in lanes]
        kd_d = [block_diag(kd[:, sl], lane_k) for sl in lanes]
        if aligned:
            a_all = [_nt(probes[g], jnp.concatenate([bd_d[g], kd_d[g]], axis=0)) for g in groups]
            a_kb = [jnp.where(strict, t[0:cs, 0:gcs], 0.0) for t in a_all]
            a_kk = [jnp.where(strict, t[0:cs, gcs:2 * gcs], 0.0).astype(BF16) for t in a_all]
            a_rb = [jnp.where(incl, t[cs:2 * cs, 0:gcs], 0.0).astype(BF16) for t in a_all]
            a_rk = [jnp.where(incl, t[cs:2 * cs, gcs:2 * gcs], 0.0).astype(BF16) for t in a_all]
        else:
            t_b = [_nt(probes[g], bd_d[g]) for g in groups]
            t_k = [_nt(probes[g], kd_d[g]) for g in groups]
            a_kb = [jnp.where(strict, t[0:cs], 0.0) for t in t_b]
            a_kk = [jnp.where(strict, t[0:cs], 0.0).astype(BF16) for t in t_k]
            a_rb = [jnp.where(incl, t[cs:2 * cs], 0.0).astype(BF16) for t in t_b]
            a_rk = [jnp.where(incl, t[cs:2 * cs], 0.0).astype(BF16) for t in t_k]
        inv = [eye - t for t in a_kb]
        if squarings >= 1:
            apow = [_mm(t.astype(BF16), block_diag(t, lane_t)) for t in a_kb]
            for _ in range(squarings - 1):
                both = [_mm(jnp.concatenate([apow[g], inv[g]], axis=0).astype(BF16), block_diag(apow[g], lane_t))
                        for g in groups]
                apow = [t[0:cs] for t in both]
                inv = [inv[g] + both[g][cs:2 * cs] for g in groups]
            inv = [inv[g] + _mm(inv[g].astype(BF16), block_diag(apow[g], lane_t)) for g in groups]
        return dict(
            probes=probes, a_kk=a_kk, a_rk=a_rk, a_rb=a_rb, inv=[t.astype(BF16) for t in inv],
            v=[xv_c[:, sl] for sl in lanes], v_d=[block_diag(xv_c[:, sl], lane_k) for sl in lanes],
            upd=[jnp.concatenate([kdp[:, sl], bdp[:, sl]], axis=0).astype(BF16) for sl in lanes],
            p_last=jnp.exp(cum_last))

    sg = [(q, g) for q in seqs for g in groups]
    state = {(q, g): s_ref[q * N_GROUPS + g] for q, g in sg}
    y_rows = [None] * (n_seq * n_sub)
    chunks = {(q, 0): chunk_setup(q * n_sub) for q in seqs}
    for level in range(n_sub):
        if level + 1 < n_sub:
            chunks.update({(q, level + 1): chunk_setup(q * n_sub + level + 1) for q in seqs})
        ch = {q: chunks.pop((q, level)) for q in seqs}
        s_b = {k: state[k].astype(BF16) for k in sg}
        state_t = {(q, g): _nt(ch[q]["probes"][g], s_b[q, g]) for q, g in sg}
        rhs = {(q, g): state_t[q, g][0:cs] + _mm(ch[q]["a_kk"][g], ch[q]["v_d"][g]) for q, g in sg}
        u = {(q, g): _mm(ch[q]["inv"][g], block_diag(rhs[q, g], lane_k)) for q, g in sg}
        u_d = {k: block_diag(u[k], lane_k) for k in sg}
        if aligned:
            y_g = {(q, g): state_t[q, g][cs:2 * cs]
                   + _mm(jnp.concatenate([ch[q]["a_rk"][g], -ch[q]["a_rb"][g]], axis=1),
                         jnp.concatenate([ch[q]["v_d"][g], u_d[q, g]], axis=0)) for q, g in sg}
        else:
            y_g = {(q, g): state_t[q, g][cs:2 * cs] + _mm(ch[q]["a_rk"][g], ch[q]["v_d"][g])
                   - _mm(ch[q]["a_rb"][g], u_d[q, g]) for q, g in sg}
        state = {(q, g): state[q, g] * ch[q]["p_last"][:, lanes[g]] + jnp.where(
            state_mask, _tn(jnp.concatenate([ch[q]["v"][g], -u[q, g]], axis=0).astype(BF16), ch[q]["upd"][g]), 0.0)
            for q, g in sg}
        for q in seqs:
            y_rows[q * n_sub + level] = jnp.concatenate([y_g[q, g] for g in groups], axis=1)
    for q, g in sg:
        s_ref[q * N_GROUPS + g] = state[q, g]

    y = jnp.concatenate(y_rows, axis=0) if len(y_rows) > 1 else y_rows[0]
    mean = _seg_sum(y, bones) * (1.0 / hd)
    d = y - mean
    var = _seg_sum(d * d, bones) * (1.0 / hd)
    yn = d * lax.rsqrt(var + GN_EPS) * lnw_ref[...] + lnb_ref[...]
    bonus = _seg_sum(xr * k2 * rk_ref[...], bones) * xv
    y_ref[...] = (yn + bonus) * gate

    @pl.when(step == n_steps - 1)
    def _fin():
        for q in seqs:
            for h in range(R_HEADS):
                o = (h % GROUP) * hd
                so_ref[q, h] = s_ref[q * N_GROUPS + h // GROUP, o:o + hd, o:o + hd]


def _wkv(proj, prev0, s0, params, wl, bones, batch, seq, chunk, n_sub, n_seq=1, casts=()):
    ns = seq // (chunk * n_sub)
    assert n_seq == 1 or ns == 1, "several sequences per step only when a step covers them whole"
    rows = n_seq * chunk * n_sub
    nb = batch // n_seq
    rw = R_WIDTH

    def col(cb):
        return lambda b, c: (b * ns + c, cb)

    const2 = lambda b, c: (0, 0)
    vec = lambda n: pl.BlockSpec((1, n), const2)
    cast_in, cast_out, cast_shapes = [], [], []
    for wgt, span in casts:
        start, count = (0, wgt.shape[0]) if span is None else span
        blk, width = count // (nb * ns), wgt.shape[1]
        assert blk * nb * ns == count, (wgt.shape, span)
        if span is None:
            assert blk % 16 == 0, blk
            cast_in.append(pl.BlockSpec((blk, width), col(0)))
            cast_out.append(pl.BlockSpec((blk, width), col(0)))
            cast_shapes.append(jax.ShapeDtypeStruct((count, width), BF16))
        else:
            assert blk % 128 == 0 and start % 8 == 0, span
            cast_in.append(pl.BlockSpec(
                (pl.Element(blk), pl.Element(width)),
                lambda b, c, start=start, blk=blk: (pl.multiple_of(start + blk * (b * ns + c), 8), 0)))
            cast_out.append(pl.BlockSpec((width, blk), lambda b, c: (0, b * ns + c)))
            cast_shapes.append(jax.ShapeDtypeStruct((width, count), BF16))
    mu, mul, w0, a0, k_k, k_a, r_k, ln_w, ln_b = params
    outs = pl.pallas_call(
        functools.partial(_wkv_kernel, chunk=chunk, n_sub=n_sub, n_seq=n_seq, n_steps=ns,
                          cast_transposed=tuple(span is not None for _, span in casts)),
        out_shape=(jax.ShapeDtypeStruct((batch * seq, rw), F32),
                   jax.ShapeDtypeStruct((batch, R_HEADS, R_HEAD, R_HEAD), F32), *cast_shapes),
        grid=(nb, ns),
        in_specs=[
            pl.BlockSpec((rows, C_BLK), col(P_C // C_BLK)),
            pl.BlockSpec((rows, C_BLK), col(P_C // C_BLK + 1)),
            pl.BlockSpec((rows, LORA_PAD), col(P_L // LORA_PAD)),
            pl.BlockSpec((n_seq, 1, SHIFT_COLS), lambda b, c: (b, 0, 0)),
            pl.BlockSpec((n_seq, R_HEADS, R_HEAD, R_HEAD), lambda b, c: (b, 0, 0, 0)),
            vec(3 * rw), vec(LORA_PAD), vec(rw), vec(rw), vec(rw), vec(rw), vec(rw), vec(rw), vec(rw),
            pl.BlockSpec((LORA_PAD, 3 * rw), const2),
            pl.BlockSpec((SEG, SEG), const2),
            *cast_in,
        ],
        out_specs=(
            pl.BlockSpec((rows, rw), lambda b, c: (b * ns + c, 0)),
            pl.BlockSpec((n_seq, R_HEADS, R_HEAD, R_HEAD), lambda b, c: (b, 0, 0, 0)),
            *cast_out,
        ),
        scratch_shapes=[pltpu.VMEM((n_seq * N_GROUPS, GROUP_W, GROUP_W), F32),
                        pltpu.VMEM((n_seq, SHIFT_COLS), F32)],
        compiler_params=pltpu.CompilerParams(
            dimension_semantics=("arbitrary", "arbitrary"), vmem_limit_bytes=VMEM_LIMIT),
        name="wkv",
    )(proj, proj, proj, prev0, s0, mu, mul, w0, a0, k_k, k_a, r_k, ln_w, ln_b, wl, bones,
      *[wgt for wgt, _ in casts])
    return outs[0], outs[1], outs[2:]


def _mix_kernel(x_ref, ya_ref, yr_ref, g_ref, wg_ref, wba_ref, wbr_ref, wo_ref, o_ref):
    x = x_ref[...]
    h = _rms(x, g_ref[...]).astype(BF16)
    ya = ya_ref[...].astype(BF16)
    yr = yr_ref[...].astype(BF16)
    acc = x
    for c in range(D_MODEL // MIX_CHUNK):
        ca = slice(c * MIX_CHUNK, (c + 1) * MIX_CHUNK)
        cr = slice(D_MODEL + c * MIX_CHUNK, D_MODEL + (c + 1) * MIX_CHUNK)
        mixed = (jax.nn.sigmoid(_mm(h, wg_ref[:, ca])) * _mm(ya, wba_ref[:, ca])
                 + jax.nn.sigmoid(_mm(h, wg_ref[:, cr])) * _mm(yr, wbr_ref[:, ca]))
        acc = acc + _mm(mixed.astype(BF16), wo_ref[ca, :])
    o_ref[...] = acc


def _resident(shape):
    return pl.BlockSpec(shape, lambda *_: (0,) * len(shape), pipeline_mode=pl.Buffered(1))


def _mix(x, ya, yr, g, wg, wba, wbr, wo, tm):
    m = x.shape[0]
    row = lambda i: (i, 0)
    return pl.pallas_call(
        _mix_kernel,
        out_shape=jax.ShapeDtypeStruct((m, D_MODEL), F32),
        grid=(m // tm,),
        in_specs=[
            pl.BlockSpec((tm, D_MODEL), row),
            pl.BlockSpec((tm, ATTN_WIDTH), row),
            pl.BlockSpec((tm, R_WIDTH), row),
            pl.BlockSpec((1, D_MODEL), lambda i: (0, 0)),
            _resident((D_MODEL, 2 * D_MODEL)),
            _resident((ATTN_WIDTH, D_MODEL)),
            _resident((R_WIDTH, D_MODEL)),
            _resident((D_MODEL, D_MODEL)),
        ],
        out_specs=pl.BlockSpec((tm, D_MODEL), row),
        compiler_params=pltpu.CompilerParams(
            dimension_semantics=("arbitrary",), vmem_limit_bytes=VMEM_LIMIT),
        name="mix",
    )(x, ya, yr, g, wg, wba, wbr, wo)


def _ffn_kernel(x_ref, g_ref, wu_ref, wd_ref, o_ref, h_ref):
    @pl.when(pl.program_id(1) == 0)
    def _():
        x = x_ref[...]
        h_ref[...] = _rms(x, g_ref[...]).astype(BF16)
        o_ref[...] = x

    u = _mm(h_ref[...], wu_ref[...])
    o_ref[...] += _mm(jnp.square(jnp.maximum(u, 0.0)).astype(BF16), wd_ref[...])


def _ffn(x, g, wu, wd, tm, tk):
    m = x.shape[0]
    return pl.pallas_call(
        _ffn_kernel,
        out_shape=jax.ShapeDtypeStruct((m, D_MODEL), F32),
        grid=(m // tm, D_FF // tk),
        in_specs=[
            pl.BlockSpec((tm, D_MODEL), lambda i, k: (i, 0)),
            pl.BlockSpec((1, D_MODEL), lambda i, k: (0, 0)),
            pl.BlockSpec((D_MODEL, tk), lambda i, k: (0, k)),
            pl.BlockSpec((tk, D_MODEL), lambda i, k: (k, 0)),
        ],
        out_specs=pl.BlockSpec((tm, D_MODEL), lambda i, k: (i, 0)),
        scratch_shapes=[pltpu.VMEM((tm, D_MODEL), BF16)],
        compiler_params=pltpu.CompilerParams(
            dimension_semantics=("arbitrary", "arbitrary"), vmem_limit_bytes=VMEM_LIMIT),
        name="ffn",
    )(x, g, wu, wd)


def _ple_kernel(x_ref, pe_ref, wg_ref, wp_ref, g_ref, o_ref):
    x = x_ref[...]
    gate = jax.nn.sigmoid(_mm(x.astype(BF16), wg_ref[...]))
    x = x + gate * _mm(pe_ref[...].astype(BF16), wp_ref[...])
    o_ref[...] = _rms(x, g_ref[...])


def _ple(x, pe, wg, wp, g, tm):
    m = x.shape[0]
    row = lambda i: (i, 0)
    return pl.pallas_call(
        _ple_kernel,
        out_shape=jax.ShapeDtypeStruct((m, D_MODEL), F32),
        grid=(m // tm,),
        in_specs=[
            pl.BlockSpec((tm, D_MODEL), row),
            pl.BlockSpec((tm, PLE_DIM), row),
            _resident((D_MODEL, D_MODEL)),
            _resident((PLE_DIM, D_MODEL)),
            pl.BlockSpec((1, D_MODEL), lambda i: (0, 0)),
        ],
        out_specs=pl.BlockSpec((tm, D_MODEL), row),
        compiler_params=pltpu.CompilerParams(
            dimension_semantics=("arbitrary",), vmem_limit_bytes=VMEM_LIMIT),
        name="ple",
    )(x, pe, wg, wp, g)


def _dense_tail(x, ya, yr, pe, wts, tiles):
    (g_mix, w_gates, wba, wbr, wo, g_ffn, wu, wd, wg, wp, g_fin) = wts
    x = _mix(x, ya, yr, g_mix, w_gates, wba, wbr, wo, tiles["mix_m"])
    x = _ffn(x, g_ffn, wu, wd, tiles["ffn_m"], tiles["ffn_k"])
    return _ple(x, pe, wg, wp, g_fin, tiles["ple_m"])


def _shift_out(proj, batch, seq):
    last = proj.reshape(batch, seq, P_COLS)[:, -1]
    return last[:, P_C:P_C + RWKV_COLS][None]


def kernel(x_prompt, x_sample, cache_k_win, cache_v_win, state_wkv, state_shift, p_prompt, p_sample,
           norm_mix, w_in, attn_sinks, rwkv_mu, rwkv_w0, rwkv_w2, rwkv_a0, rwkv_a2, rwkv_g2,
           rwkv_k_k, rwkv_k_a, rwkv_r_k, rwkv_ln_w, rwkv_ln_b, w_branch_attn, w_branch_rwkv,
           w_out, norm_ffn, w_ff_up, w_ff_down, w_ple_proj, w_ple_gate, norm_final):
    assert w_in.shape[0] == 1, "single-layer step"
    bp, tp = x_prompt.shape[0], x_prompt.shape[1]
    bs, ts = x_sample.shape[0], x_sample.shape[1]
    rw = R_WIDTH

    wl = jnp.zeros((LORA_PAD, 3 * rw), F32)
    wl = wl.at[0:DECAY_LORA, 0:rw].set(rwkv_w2[0])
    wl = wl.at[DECAY_LORA:DECAY_LORA + AAA_LORA, rw:2 * rw].set(rwkv_a2[0])
    wl = wl.at[DECAY_LORA + AAA_LORA:LORA_COLS, 2 * rw:3 * rw].set(rwkv_g2[0])
    wl = wl.astype(BF16)
    seg_id = np.arange(SEG) // R_HEAD
    bones = jnp.asarray(seg_id[:, None] == seg_id[None, :], BF16)
    mu = rwkv_mu[0]
    row = lambda v: v.reshape(1, -1)
    wkv_params = (row(mu[:3 * rw]), row(jnp.pad(mu[3 * rw:], (0, LORA_PAD - LORA_COLS))),
                  row(rwkv_w0[0]), row(rwkv_a0[0]), row(rwkv_k_k[0]), row(rwkv_k_a[0]),
                  row(rwkv_r_k[0]), row(rwkv_ln_w[0]), row(rwkv_ln_b[0]))
    g_mix = row(norm_mix[0])
    sinks = attn_sinks[0]

    ms = bs * ts
    xs = x_sample.reshape(ms, D_MODEL)
    w_in_t = jnp.transpose(w_in[0])
    proj_s, w_in_b = _proj_cast(xs, g_mix, w_in_t, 1024)

    tiles_p = dict(mix_m=256, ffn_m=1024, ffn_k=512, ple_m=512)
    xp = x_prompt.reshape(bp * tp, D_MODEL)
    proj_p = _proj(xp, g_mix, w_in_b, 1024, 1024)
    ya_p = _attn_prompt(proj_p, sinks, bp, tp)
    yr_p, s_p, (wu, wd, wo, wg, wba, wbr, w_gates) = _wkv(
        proj_p, jnp.zeros((bp, 1, SHIFT_COLS), F32), jnp.zeros((bp, R_HEADS, R_HEAD, R_HEAD), F32),
        wkv_params, wl, bones, bp, tp, 64, 4,
        casts=((w_ff_up[0], None), (w_ff_down[0], None), (w_out[0], None), (w_ple_gate[0], None),
               (w_branch_attn[0], None), (w_branch_rwkv[0], None), (w_in_t, (P_GATES, 2 * D_MODEL))))
    dense = (g_mix, w_gates, wba, wbr, wo, row(norm_ffn[0]), wu, wd, wg, w_ple_proj[0].astype(BF16),
             row(norm_final))
    yp = _dense_tail(xp, ya_p, yr_p, p_prompt[0].reshape(bp * tp, PLE_DIM), dense, tiles_p)
    pp3 = proj_p.reshape(bp, tp, P_COLS)[:, -WINDOW:]
    k_p = pp3[:, :, P_K:P_K + KV_WIDTH].reshape(1, bp, WINDOW, N_KV_HEADS, HEAD_DIM)
    v_p = pp3[:, :, P_V:P_V + KV_WIDTH].reshape(1, bp, WINDOW, N_KV_HEADS, HEAD_DIM)

    tiles_s = dict(mix_m=ms, ffn_m=ms, ffn_k=512, ple_m=ms)
    ya_s, nk_s, nv_s = _attn_sample(proj_s, sinks, cache_k_win[0].reshape(bs, WINDOW, KV_WIDTH),
                                    cache_v_win[0].reshape(bs, WINDOW, KV_WIDTH), bs, ts, SAMPLE_GROUP)
    prev_s = jnp.pad(state_shift[0], ((0, 0), (0, LORA_PAD - LORA_COLS))).reshape(bs, 1, SHIFT_COLS)
    yr_s, s_s, _ = _wkv(proj_s, prev_s, state_wkv[0], wkv_params, wl, bones, bs, ts, ts, 1, n_seq=SAMPLE_GROUP)
    ys = _dense_tail(xs, ya_s, yr_s, p_sample[0].reshape(ms, PLE_DIM), dense, tiles_s)

    return (yp.reshape(bp, tp, D_MODEL), ys.reshape(bs, ts, D_MODEL),
            k_p, v_p, s_p[None], _shift_out(proj_p, bp, tp),
            nk_s.reshape(1, bs, WINDOW, N_KV_HEADS, HEAD_DIM), nv_s.reshape(1, bs, WINDOW, N_KV_HEADS, HEAD_DIM),
            s_s[None], _shift_out(proj_s, bs, ts))
```

```python
import functools

import numpy as np
import jax
import jax.numpy as jnp
from jax import lax
from jax.experimental import pallas as pl
from jax.experimental.pallas import tpu as pltpu

F32 = jnp.float32
BF16 = jnp.bfloat16

D_MODEL = 2048
PLE_DIM = 256
HEAD_DIM = 64
N_Q_HEADS = 16
N_KV_HEADS = 4
GQA_GROUP = 4
ATTN_WIDTH = 1024
KV_WIDTH = 256
WINDOW = 128
ALIBI_MAX = 8.0
R_HEAD = 64
R_WIDTH = 1024
R_HEADS = 16
DECAY_LORA = 64
AAA_LORA = 64
GATE_LORA = 160
LORA_COLS = DECAY_LORA + AAA_LORA + GATE_LORA
LORA_PAD = 512
RWKV_COLS = 3 * R_WIDTH + LORA_COLS
D_FF = 4 * D_MODEL
NORM_EPS = 1e-6
GN_EPS = 64e-5

P_Q = 0
P_K = ATTN_WIDTH
P_V = ATTN_WIDTH + KV_WIDTH
P_C = ATTN_WIDTH + 2 * KV_WIDTH
P_L = P_C + 3 * R_WIDTH
P_COLS = P_L + LORA_PAD
P_GATES = P_C + RWKV_COLS
C_BLK = P_C
SHIFT_COLS = 3 * R_WIDTH + LORA_PAD

PAIR = 128
N_PAIRS = R_WIDTH // PAIR
SEG = 256
MIX_CHUNK = 512
SAMPLE_GROUP = 8

VMEM_LIMIT = 56 * 1024 * 1024


def _mm(a, b):
    return jnp.dot(a, b, preferred_element_type=F32)


def _nt(a, b):
    return lax.dot_general(a, b, (((1,), (1,)), ((), ())), preferred_element_type=F32)


def _tn(a, b):
    return lax.dot_general(a, b, (((0,), (0,)), ((), ())), preferred_element_type=F32)


def _rms(x, g):
    ms = jnp.mean(x * x, axis=-1, keepdims=True)
    return x * lax.rsqrt(ms + NORM_EPS) * g


def _alibi_slope(hq):
    return float(2.0 ** (-ALIBI_MAX * (hq + 1) / N_Q_HEADS))


def _proj_kernel(x_ref, g_ref, w_ref, o_ref, h_ref):
    @pl.when(pl.program_id(1) == 0)
    def _():
        h_ref[...] = _rms(x_ref[...], g_ref[...]).astype(BF16)

    o_ref[...] = _nt(h_ref[...], w_ref[...])


def _proj_cast_kernel(x_ref, g_ref, w_ref, o_ref, wb_ref, h_ref):
    @pl.when(pl.program_id(0) == 0)
    def _():
        h_ref[...] = _rms(x_ref[...], g_ref[...]).astype(BF16)

    wb = w_ref[...].astype(BF16)
    wb_ref[...] = wb
    o_ref[...] = _nt(h_ref[...], wb)


def _proj_cast(x, g, wt_f32, tn):
    m = x.shape[0]
    return pl.pallas_call(
        _proj_cast_kernel,
        out_shape=(jax.ShapeDtypeStruct((m, P_COLS), F32), jax.ShapeDtypeStruct((P_COLS, D_MODEL), BF16)),
        grid=(P_COLS // tn,),
        in_specs=[
            pl.BlockSpec((m, D_MODEL), lambda j: (0, 0)),
            pl.BlockSpec((1, D_MODEL), lambda j: (0, 0)),
            pl.BlockSpec((tn, D_MODEL), lambda j: (j, 0)),
        ],
        out_specs=(pl.BlockSpec((m, tn), lambda j: (0, j)), pl.BlockSpec((tn, D_MODEL), lambda j: (j, 0))),
        scratch_shapes=[pltpu.VMEM((m, D_MODEL), BF16)],
        compiler_params=pltpu.CompilerParams(
            dimension_semantics=("arbitrary",), vmem_limit_bytes=VMEM_LIMIT),
        name="proj_cast",
    )(x, g, wt_f32)


def _proj(x, g, w, tm, tn):
    m = x.shape[0]
    return pl.pallas_call(
        _proj_kernel,
        out_shape=jax.ShapeDtypeStruct((m, P_COLS), F32),
        grid=(m // tm, P_COLS // tn),
        in_specs=[
            pl.BlockSpec((tm, D_MODEL), lambda i, j: (i, 0)),
            pl.BlockSpec((1, D_MODEL), lambda i, j: (0, 0)),
            pl.BlockSpec((tn, D_MODEL), lambda i, j: (j, 0)),
        ],
        out_specs=pl.BlockSpec((tm, tn), lambda i, j: (i, j)),
        scratch_shapes=[pltpu.VMEM((tm, D_MODEL), BF16)],
        compiler_params=pltpu.CompilerParams(
            dimension_semantics=("arbitrary", "arbitrary"), vmem_limit_bytes=VMEM_LIMIT),
        name="proj",
    )(x, g, w)


def _attend_heads(scores, values, sinks):
    heads = range(len(scores))
    m = []
    for h in heads:
        mh = sinks[h]
        for s in scores[h]:
            mh = jnp.maximum(mh, jnp.max(s, axis=-1, keepdims=True))
        m.append(mh)
    ps = [[jnp.exp(s - m[h]) for s in scores[h]] for h in heads]
    den = []
    for h in heads:
        dh = jnp.exp(sinks[h] - m[h])
        for p in ps[h]:
            dh = dh + jnp.sum(p, axis=-1, keepdims=True)
        den.append(dh)
    outs = []
    for h in heads:
        o = None
        for p, v in zip(ps[h], values[h]):
            t = _mm(p.astype(BF16), v)
            o = t if o is None else o + t
        outs.append(o * (1.0 / den[h]))
    return outs


def _head_slices():
    q_sl = [slice(hq * HEAD_DIM, (hq + 1) * HEAD_DIM) for hq in range(N_Q_HEADS)]
    kv_sl = [slice((hq // GQA_GROUP) * HEAD_DIM, (hq // GQA_GROUP + 1) * HEAD_DIM) for hq in range(N_Q_HEADS)]
    return q_sl, kv_sl


def _alibi_bias(dist, valid):
    slopes = np.array([_alibi_slope(hq) for hq in range(N_Q_HEADS)], np.float32)
    return np.where(valid[None], -slopes[:, None, None] * dist[None].astype(np.float32), -np.inf).astype(np.float32)


def _attn_prompt_kernel(sink_ref, bias_ref, q_ref, kp_ref, kc_ref, vp_ref, vc_ref, o_ref):
    q = q_ref[...] * (HEAD_DIM ** -0.5)
    k2 = jnp.concatenate([kp_ref[...], kc_ref[...]], axis=0).astype(BF16)
    v2 = jnp.concatenate([vp_ref[...], vc_ref[...]], axis=0).astype(BF16)
    hqs = range(N_Q_HEADS)
    q_sl, kv_sl = _head_slices()
    qs = [q[:, q_sl[hq]].astype(BF16) for hq in hqs]
    sc = [[_nt(qs[hq], k2[:, kv_sl[hq]]) + bias_ref[0, hq]] for hq in hqs]
    outs = _attend_heads(sc, [[v2[:, kv_sl[hq]]] for hq in hqs], [sink_ref[hq] for hq in hqs])
    for hq in hqs:
        o_ref[:, q_sl[hq]] = outs[hq]


def _attn_prompt(proj, sinks, batch, seq):
    nb = seq // WINDOW
    kcol, vcol = P_K // KV_WIDTH, P_V // KV_WIDTH

    def cur(c):
        return lambda b, i: (b * nb + i, c)

    def prev(c):
        return lambda b, i: (b * nb + jnp.maximum(i - 1, 0), c)

    ti = np.arange(WINDOW)[:, None]
    kj = np.arange(2 * WINDOW)[None, :]
    dist = ti - kj + WINDOW
    valid = (dist >= 0) & (dist <= WINDOW)
    bias = np.stack([_alibi_bias(dist, valid), _alibi_bias(dist, valid & (kj >= WINDOW))])
    return pl.pallas_call(
        _attn_prompt_kernel,
        out_shape=jax.ShapeDtypeStruct((batch * seq, ATTN_WIDTH), F32),
        grid=(batch, nb),
        in_specs=[
            pl.BlockSpec(memory_space=pltpu.SMEM),
            pl.BlockSpec((1, N_Q_HEADS, WINDOW, 2 * WINDOW), lambda b, i: (jnp.where(i == 0, 1, 0), 0, 0, 0)),
            pl.BlockSpec((WINDOW, ATTN_WIDTH), cur(P_Q // ATTN_WIDTH)),
            pl.BlockSpec((WINDOW, KV_WIDTH), prev(kcol)),
            pl.BlockSpec((WINDOW, KV_WIDTH), cur(kcol)),
            pl.BlockSpec((WINDOW, KV_WIDTH), prev(vcol)),
            pl.BlockSpec((WINDOW, KV_WIDTH), cur(vcol)),
        ],
        out_specs=pl.BlockSpec((WINDOW, ATTN_WIDTH), lambda b, i: (b * nb + i, 0)),
        compiler_params=pltpu.CompilerParams(dimension_semantics=("arbitrary", "arbitrary")),
        name="attn_prompt",
    )(sinks, jnp.asarray(bias), proj, proj, proj, proj, proj)


def _attn_sample_kernel(sink_ref, q_ref, kn_ref, vn_ref, ck_ref, cv_ref, o_ref, nk_ref, nv_ref, *, seq):
    t, w = seq, WINDOW
    n_seq = q_ref.shape[0] // t
    ti = lax.broadcasted_iota(jnp.int32, (t, w), 0)
    cj = lax.broadcasted_iota(jnp.int32, (t, w), 1)
    dist_c = (ti - cj + w).astype(F32)
    valid_c = cj >= ti
    ti2 = lax.broadcasted_iota(jnp.int32, (t, t), 0)
    tj2 = lax.broadcasted_iota(jnp.int32, (t, t), 1)
    dist_n = (ti2 - tj2).astype(F32)
    valid_n = tj2 <= ti2
    hqs = range(N_Q_HEADS)
    q_sl, kv_sl = _head_slices()
    bias_c = [jnp.where(valid_c, -_alibi_slope(hq) * dist_c, -jnp.inf) for hq in hqs]
    bias_n = [jnp.where(valid_n, -_alibi_slope(hq) * dist_n, -jnp.inf) for hq in hqs]
    q_all = q_ref[...] * (HEAD_DIM ** -0.5)
    scores, values = [], []
    for s in range(n_seq):
        rs = slice(s * t, (s + 1) * t)
        kn, vn = kn_ref[rs, :], vn_ref[rs, :]
        ck, cv = ck_ref[s], cv_ref[s]
        nk_ref[s, 0:w - t, :] = ck[t:, :]
        nk_ref[s, w - t:w, :] = kn
        nv_ref[s, 0:w - t, :] = cv[t:, :]
        nv_ref[s, w - t:w, :] = vn
        ckb, cvb, knb, vnb = ck.astype(BF16), cv.astype(BF16), kn.astype(BF16), vn.astype(BF16)
        q = q_all[rs]
        qs = [q[:, q_sl[hq]].astype(BF16) for hq in hqs]
        scores += [[_nt(qs[hq], ckb[:, kv_sl[hq]]) + bias_c[hq], _nt(qs[hq], knb[:, kv_sl[hq]]) + bias_n[hq]]
                   for hq in hqs]
        values += [[cvb[:, kv_sl[hq]], vnb[:, kv_sl[hq]]] for hq in hqs]
    outs = _attend_heads(scores, values, [sink_ref[hq] for _ in range(n_seq) for hq in hqs])
    for s in range(n_seq):
        for hq in hqs:
            o_ref[s * t:(s + 1) * t, q_sl[hq]] = outs[s * N_Q_HEADS + hq]


def _attn_sample(proj, sinks, cache_k, cache_v, batch, seq, n_seq):
    kcol, vcol = P_K // KV_WIDTH, P_V // KV_WIDTH
    win = jax.ShapeDtypeStruct((batch, WINDOW, KV_WIDTH), F32)
    rows = n_seq * seq
    return pl.pallas_call(
        functools.partial(_attn_sample_kernel, seq=seq),
        out_shape=(jax.ShapeDtypeStruct((batch * seq, ATTN_WIDTH), F32), win, win),
        grid=(batch // n_seq,),
        in_specs=[
            pl.BlockSpec(memory_space=pltpu.SMEM),
            pl.BlockSpec((rows, ATTN_WIDTH), lambda b: (b, P_Q // ATTN_WIDTH)),
            pl.BlockSpec((rows, KV_WIDTH), lambda b: (b, kcol)),
            pl.BlockSpec((rows, KV_WIDTH), lambda b: (b, vcol)),
            pl.BlockSpec((n_seq, WINDOW, KV_WIDTH), lambda b: (b, 0, 0)),
            pl.BlockSpec((n_seq, WINDOW, KV_WIDTH), lambda b: (b, 0, 0)),
        ],
        out_specs=(
            pl.BlockSpec((rows, ATTN_WIDTH), lambda b: (b, 0)),
            pl.BlockSpec((n_seq, WINDOW, KV_WIDTH), lambda b: (b, 0, 0)),
            pl.BlockSpec((n_seq, WINDOW, KV_WIDTH), lambda b: (b, 0, 0)),
        ),
        compiler_params=pltpu.CompilerParams(dimension_semantics=("arbitrary",)),
        name="attn_sample",
    )(sinks, proj, proj, proj, cache_k, cache_v)


def _seg_sum(x, bones):
    rows = x.shape[0]
    hi = x.astype(BF16).astype(F32)
    lo = x - hi
    groups = [slice(j * SEG, (j + 1) * SEG) for j in range(R_WIDTH // SEG)]
    lhs = jnp.concatenate([t[:, sl] for sl in groups for t in (hi, lo)], axis=0).astype(BF16)
    out = _mm(lhs, bones)
    return jnp.concatenate(
        [out[2 * j * rows:(2 * j + 1) * rows] + out[(2 * j + 1) * rows:(2 * j + 2) * rows]
         for j in range(len(groups))], axis=1)


def _wkv_kernel(pa_ref, pb_ref, pl_ref, prev_ref, s0_ref,
                mu_ref, mul_ref, w0_ref, a0_ref, kk_ref, ka_ref, rk_ref, lnw_ref, lnb_ref,
                wl_ref, bones_ref, *rest, chunk, n_sub, n_seq, n_steps, cast_transposed):
    n_cast = len(cast_transposed)
    cast_src, (y_ref, so_ref) = rest[:n_cast], rest[n_cast:n_cast + 2]
    cast_dst, (s_ref, carry_ref) = rest[n_cast + 2:2 * n_cast + 2], rest[2 * n_cast + 2:]
    for src, dst, transposed in zip(cast_src, cast_dst, cast_transposed):
        dst[...] = (src[...].T if transposed else src[...]).astype(BF16)

    step = pl.program_id(1)
    cs = chunk
    seq_rows = n_sub * cs
    rows = n_seq * seq_rows
    gc = 2 * cs
    hd = R_HEAD
    w = R_WIDTH
    seqs = range(n_seq)

    @pl.when(step == 0)
    def _init():
        s_ref[...] = jnp.zeros(s_ref.shape, F32)
        for q in seqs:
            carry_ref[q:q + 1, :] = prev_ref[q]
            for p in range(N_PAIRS):
                s_ref[q * N_PAIRS + p, 0:hd, 0:hd] = s0_ref[q, 2 * p]
                s_ref[q * N_PAIRS + p, hd:2 * hd, hd:2 * hd] = s0_ref[q, 2 * p + 1]

    row = lax.broadcasted_iota(jnp.int32, (rows, 1), 0)

    def token_shift(x, lo, hi, mu):
        shifted = pltpu.roll(x, 1, axis=0)
        for q in seqs:
            shifted = jnp.where(row == q * seq_rows, carry_ref[q:q + 1, lo:hi], shifted)
        return x + (shifted - x) * mu

    cols = jnp.concatenate([pa_ref[...], pb_ref[...]], axis=1)
    l_raw = pl_ref[...]
    xx = token_shift(cols, 0, 3 * w, mu_ref[...])
    xr, xk, xv = xx[:, 0:w], xx[:, w:2 * w], xx[:, 2 * w:3 * w]
    xl = token_shift(l_raw, 3 * w, SHIFT_COLS, mul_ref[...])
    for q in seqs:
        last = (q + 1) * seq_rows
        carry_ref[q:q + 1, 0:3 * w] = cols[last - 1:last, :]
        carry_ref[q:q + 1, 3 * w:] = l_raw[last - 1:last, :]

    lane_l = lax.broadcasted_iota(jnp.int32, (1, LORA_PAD), 1)
    act = jnp.where(lane_l < DECAY_LORA, jnp.tanh(xl),
                    jnp.where(lane_l < DECAY_LORA + AAA_LORA, xl,
                              jnp.where(lane_l < LORA_COLS, jax.nn.sigmoid(xl), 0.0)))
    up = _mm(act.astype(BF16), wl_ref[...])
    z = -(w0_ref[...] + up[:, 0:w])
    softplus = jnp.maximum(z, 0.0) + jnp.log1p(jnp.exp(-jnp.abs(z)))
    lwd = -jnp.exp(-softplus - 0.5)
    a = jax.nn.sigmoid(a0_ref[...] + up[:, w:2 * w])
    gate = up[:, 2 * w:3 * w]

    bones = bones_ref[...]
    kkn = xk * kk_ref[...]
    kk = kkn * lax.rsqrt(jnp.maximum(_seg_sum(kkn * kkn, bones), 1e-24))
    k2 = xk * (1.0 + (a - 1.0) * ka_ref[...])
    b = kk * a

    tri = (lax.broadcasted_iota(jnp.int32, (cs, cs), 0)
           >= lax.broadcasted_iota(jnp.int32, (cs, cs), 1)).astype(BF16)
    ri = lax.broadcasted_iota(jnp.int32, (gc, gc), 0)
    ci = lax.broadcasted_iota(jnp.int32, (gc, gc), 1)
    same_head = (ri >= cs) == (ci >= cs)
    strict = same_head & (ci < ri)
    incl = same_head & (ci <= ri)
    eye = (ri == ci).astype(F32)
    head0 = lax.broadcasted_iota(jnp.int32, (1, PAIR), 1) < hd

    def stack(x):
        return jnp.concatenate([jnp.where(head0, x, 0.0), jnp.where(head0, 0.0, x)], axis=0).astype(BF16)

    pairs = range(N_PAIRS)
    lanes = [slice(p * PAIR, (p + 1) * PAIR) for p in pairs]
    squarings = cs.bit_length() - 2

    chunks = []
    for sub in range(n_seq * n_sub):
        rs = slice(sub * cs, (sub + 1) * cs)
        lw_c = lwd[rs]
        lw_hi = lw_c.astype(BF16)
        lw_r = lw_c - lw_hi.astype(F32)
        lw_mid = lw_r.astype(BF16)
        cum = _mm(tri, lw_hi) + _mm(tri, lw_mid) + _mm(tri, (lw_r - lw_mid.astype(F32)).astype(BF16))
        cum_last = cum[cs - 1:cs, :]
        e_inv = jnp.exp(-cum)
        e_last = jnp.exp(cum_last - cum)
        kq = kk[rs] * jnp.exp(cum - lw_c)
        rq = xr[rs] * jnp.exp(cum)
        kd = k2[rs] * e_inv
        bd = b[rs] * e_inv
        kdp = k2[rs] * e_last
        bdp = b[rs] * e_last
        xv_c = xv[rs]
        kq_s = [stack(kq[:, sl]) for sl in lanes]
        bd_s = [stack(bd[:, sl]) for sl in lanes]
        kd_s = [stack(kd[:, sl]) for sl in lanes]
        rq_s = [stack(rq[:, sl]) for sl in lanes]
        probes = [jnp.concatenate([kq_s[p], rq_s[p]], axis=0) for p in pairs]
        if gc % PAIR == 0:
            a_all = [_nt(probes[p], jnp.concatenate([bd_s[p], kd_s[p]], axis=0)) for p in pairs]
            a_kb = [jnp.where(strict, t[0:gc, 0:gc], 0.0) for t in a_all]
            a_kk = [jnp.where(strict, t[0:gc, gc:2 * gc], 0.0).astype(BF16) for t in a_all]
            a_rb = [jnp.where(incl, t[gc:2 * gc, 0:gc], 0.0).astype(BF16) for t in a_all]
            a_rk = [jnp.where(incl, t[gc:2 * gc, gc:2 * gc], 0.0).astype(BF16) for t in a_all]
        else:
            a_kb = [jnp.where(strict, _nt(kq_s[p], bd_s[p]), 0.0) for p in pairs]
            a_kk = [jnp.where(strict, _nt(kq_s[p], kd_s[p]), 0.0).astype(BF16) for p in pairs]
            a_rb = [jnp.where(incl, _nt(rq_s[p], bd_s[p]), 0.0).astype(BF16) for p in pairs]
            a_rk = [jnp.where(incl, _nt(rq_s[p], kd_s[p]), 0.0).astype(BF16) for p in pairs]
        inv = [eye - t for t in a_kb]
        if squarings >= 1:
            apow_b = [t.astype(BF16) for t in a_kb]
            apow_b = [_mm(t, t).astype(BF16) for t in apow_b]
            for _ in range(squarings - 1):
                both = [_mm(jnp.concatenate([apow_b[p], inv[p].astype(BF16)], axis=0), apow_b[p]) for p in pairs]
                apow_b = [t[0:gc].astype(BF16) for t in both]
                inv = [inv[p] + both[p][gc:2 * gc] for p in pairs]
            inv = [inv[p] + _mm(inv[p].astype(BF16), apow_b[p]) for p in pairs]
        chunks.append(dict(
            probes=probes, a_kk=a_kk, a_rk=a_rk, a_rb=a_rb, inv=[t.astype(BF16) for t in inv],
            v_s=[stack(xv_c[:, sl]) for sl in lanes],
            upd=[jnp.concatenate([stack(kdp[:, sl]), stack(bdp[:, sl])], axis=0) for sl in lanes],
            p_last=jnp.exp(cum_last)))

    sp = [(q, p) for q in seqs for p in pairs]
    state = {(q, p): s_ref[q * N_PAIRS + p] for q, p in sp}
    y_rows = [None] * (n_seq * n_sub)
    for level in range(n_sub):
        ch = {q: chunks[q * n_sub + level] for q in seqs}
        s_b = {k: state[k].astype(BF16) for k in sp}
        state_t = {(q, p): _nt(ch[q]["probes"][p], s_b[q, p]) for q, p in sp}
        rhs = {(q, p): state_t[q, p][0:gc] + _mm(ch[q]["a_kk"][p], ch[q]["v_s"][p]) for q, p in sp}
        u_b = {(q, p): _mm(ch[q]["inv"][p], rhs[q, p].astype(BF16)).astype(BF16) for q, p in sp}
        if gc % PAIR == 0:
            y_s = {(q, p): state_t[q, p][gc:2 * gc]
                   + _mm(jnp.concatenate([ch[q]["a_rk"][p], -ch[q]["a_rb"][p]], axis=1),
                         jnp.concatenate([ch[q]["v_s"][p], u_b[q, p]], axis=0)) for q, p in sp}
        else:
            y_s = {(q, p): state_t[q, p][gc:2 * gc] + _mm(ch[q]["a_rk"][p], ch[q]["v_s"][p])
                   - _mm(ch[q]["a_rb"][p], u_b[q, p]) for q, p in sp}
        state = {(q, p): state[q, p] * ch[q]["p_last"][:, lanes[p]]
                 + _tn(jnp.concatenate([ch[q]["v_s"][p], -u_b[q, p]], axis=0), ch[q]["upd"][p]) for q, p in sp}
        for q in seqs:
            y_rows[q * n_sub + level] = jnp.concatenate(
                [y_s[q, p][0:cs] + y_s[q, p][cs:gc] for p in pairs], axis=1)
    for q, p in sp:
        s_ref[q * N_PAIRS + p] = state[q, p]

    y = jnp.concatenate(y_rows, axis=0) if len(y_rows) > 1 else y_rows[0]
    mean = _seg_sum(y, bones) * (1.0 / hd)
    d = y - mean
    var = _seg_sum(d * d, bones) * (1.0 / hd)
    yn = d * lax.rsqrt(var + GN_EPS) * lnw_ref[...] + lnb_ref[...]
    bonus = _seg_sum(xr * k2 * rk_ref[...], bones) * xv
    y_ref[...] = (yn + bonus) * gate

    @pl.when(step == n_steps - 1)
    def _fin():
        for q, p in sp:
            so_ref[q, 2 * p] = s_ref[q * N_PAIRS + p, 0:hd, 0:hd]
            so_ref[q, 2 * p + 1] = s_ref[q * N_PAIRS + p, hd:2 * hd, hd:2 * hd]


def _wkv(proj, prev0, s0, params, wl, bones, batch, seq, chunk, n_sub, n_seq=1, casts=()):
    ns = seq // (chunk * n_sub)
    assert n_seq == 1 or ns == 1, "several sequences per step only when a step covers them whole"
    rows = n_seq * chunk * n_sub
    nb = batch // n_seq
    rw = R_WIDTH

    def col(cb):
        return lambda b, c: (b * ns + c, cb)

    const2 = lambda b, c: (0, 0)
    vec = lambda n: pl.BlockSpec((1, n), const2)
    cast_in, cast_out, cast_shapes = [], [], []
    for wgt, span in casts:
        start, count = (0, wgt.shape[0]) if span is None else span
        blk, width = count // (nb * ns), wgt.shape[1]
        assert blk * nb * ns == count, (wgt.shape, span)
        if span is None:
            assert blk % 16 == 0, blk
            cast_in.append(pl.BlockSpec((blk, width), col(0)))
            cast_out.append(pl.BlockSpec((blk, width), col(0)))
            cast_shapes.append(jax.ShapeDtypeStruct((count, width), BF16))
        else:
            assert blk % 128 == 0 and start % 8 == 0, span
            cast_in.append(pl.BlockSpec(
                (pl.Element(blk), pl.Element(width)),
                lambda b, c, start=start, blk=blk: (pl.multiple_of(start + blk * (b * ns + c), 8), 0)))
            cast_out.append(pl.BlockSpec((width, blk), lambda b, c: (0, b * ns + c)))
            cast_shapes.append(jax.ShapeDtypeStruct((width, count), BF16))
    mu, mul, w0, a0, k_k, k_a, r_k, ln_w, ln_b = params
    outs = pl.pallas_call(
        functools.partial(_wkv_kernel, chunk=chunk, n_sub=n_sub, n_seq=n_seq, n_steps=ns,
                          cast_transposed=tuple(span is not None for _, span in casts)),
        out_shape=(jax.ShapeDtypeStruct((batch * seq, rw), F32),
                   jax.ShapeDtypeStruct((batch, R_HEADS, R_HEAD, R_HEAD), F32), *cast_shapes),
        grid=(nb, ns),
        in_specs=[
            pl.BlockSpec((rows, C_BLK), col(P_C // C_BLK)),
            pl.BlockSpec((rows, C_BLK), col(P_C // C_BLK + 1)),
            pl.BlockSpec((rows, LORA_PAD), col(P_L // LORA_PAD)),
            pl.BlockSpec((n_seq, 1, SHIFT_COLS), lambda b, c: (b, 0, 0)),
            pl.BlockSpec((n_seq, R_HEADS, R_HEAD, R_HEAD), lambda b, c: (b, 0, 0, 0)),
            vec(3 * rw), vec(LORA_PAD), vec(rw), vec(rw), vec(rw), vec(rw), vec(rw), vec(rw), vec(rw),
            pl.BlockSpec((LORA_PAD, 3 * rw), const2),
            pl.BlockSpec((SEG, SEG), const2),
            *cast_in,
        ],
        out_specs=(
            pl.BlockSpec((rows, rw), lambda b, c: (b * ns + c, 0)),
            pl.BlockSpec((n_seq, R_HEADS, R_HEAD, R_HEAD), lambda b, c: (b, 0, 0, 0)),
            *cast_out,
        ),
        scratch_shapes=[pltpu.VMEM((n_seq * N_PAIRS, PAIR, PAIR), F32), pltpu.VMEM((n_seq, SHIFT_COLS), F32)],
        compiler_params=pltpu.CompilerParams(
            dimension_semantics=("arbitrary", "arbitrary"), vmem_limit_bytes=VMEM_LIMIT),
        name="wkv",
    )(proj, proj, proj, prev0, s0, mu, mul, w0, a0, k_k, k_a, r_k, ln_w, ln_b, wl, bones,
      *[wgt for wgt, _ in casts])
    return outs[0], outs[1], outs[2:]


def _mix_kernel(x_ref, ya_ref, yr_ref, g_ref, wga_ref, wgr_ref, wba_ref, wbr_ref, wo_ref, o_ref,
                h_ref, yab_ref, yrb_ref):
    @pl.when(pl.program_id(1) == 0)
    def _():
        x = x_ref[...]
        h_ref[...] = _rms(x, g_ref[...]).astype(BF16)
        yab_ref[...] = ya_ref[...].astype(BF16)
        yrb_ref[...] = yr_ref[...].astype(BF16)
        o_ref[...] = x

    h = h_ref[...]
    mixed = (jax.nn.sigmoid(_mm(h, wga_ref[...])) * _mm(yab_ref[...], wba_ref[...])
             + jax.nn.sigmoid(_mm(h, wgr_ref[...])) * _mm(yrb_ref[...], wbr_ref[...]))
    o_ref[...] += _mm(mixed.astype(BF16), wo_ref[...])


def _resident(shape):
    return pl.BlockSpec(shape, lambda *_: (0,) * len(shape), pipeline_mode=pl.Buffered(1))


def _mix(x, ya, yr, g, wg, wba, wbr, wo, tm):
    m = x.shape[0]
    nc = D_MODEL // MIX_CHUNK
    row = lambda i, c: (i, 0)
    return pl.pallas_call(
        _mix_kernel,
        out_shape=jax.ShapeDtypeStruct((m, D_MODEL), F32),
        grid=(m // tm, nc),
        in_specs=[
            pl.BlockSpec((tm, D_MODEL), row),
            pl.BlockSpec((tm, ATTN_WIDTH), row),
            pl.BlockSpec((tm, R_WIDTH), row),
            pl.BlockSpec((1, D_MODEL), lambda i, c: (0, 0)),
            pl.BlockSpec((D_MODEL, MIX_CHUNK), lambda i, c: (0, c)),
            pl.BlockSpec((D_MODEL, MIX_CHUNK), lambda i, c: (0, nc + c)),
            pl.BlockSpec((ATTN_WIDTH, MIX_CHUNK), lambda i, c: (0, c)),
            pl.BlockSpec((R_WIDTH, MIX_CHUNK), lambda i, c: (0, c)),
            pl.BlockSpec((MIX_CHUNK, D_MODEL), lambda i, c: (c, 0)),
        ],
        out_specs=pl.BlockSpec((tm, D_MODEL), row),
        scratch_shapes=[pltpu.VMEM((tm, D_MODEL), BF16), pltpu.VMEM((tm, ATTN_WIDTH), BF16),
                        pltpu.VMEM((tm, R_WIDTH), BF16)],
        compiler_params=pltpu.CompilerParams(
            dimension_semantics=("arbitrary", "arbitrary"), vmem_limit_bytes=VMEM_LIMIT),
        name="mix",
    )(x, ya, yr, g, wg, wg, wba, wbr, wo)


def _ffn_kernel(x_ref, g_ref, wu_ref, wd_ref, o_ref, h_ref):
    @pl.when(pl.program_id(1) == 0)
    def _():
        x = x_ref[...]
        h_ref[...] = _rms(x, g_ref[...]).astype(BF16)
        o_ref[...] = x

    u = _mm(h_ref[...], wu_ref[...])
    o_ref[...] += _mm(jnp.square(jnp.maximum(u, 0.0)).astype(BF16), wd_ref[...])


def _ffn(x, g, wu, wd, tm, tk):
    m = x.shape[0]
    return pl.pallas_call(
        _ffn_kernel,
        out_shape=jax.ShapeDtypeStruct((m, D_MODEL), F32),
        grid=(m // tm, D_FF // tk),
        in_specs=[
            pl.BlockSpec((tm, D_MODEL), lambda i, k: (i, 0)),
            pl.BlockSpec((1, D_MODEL), lambda i, k: (0, 0)),
            pl.BlockSpec((D_MODEL, tk), lambda i, k: (0, k)),
            pl.BlockSpec((tk, D_MODEL), lambda i, k: (k, 0)),
        ],
        out_specs=pl.BlockSpec((tm, D_MODEL), lambda i, k: (i, 0)),
        scratch_shapes=[pltpu.VMEM((tm, D_MODEL), BF16)],
        compiler_params=pltpu.CompilerParams(
            dimension_semantics=("arbitrary", "arbitrary"), vmem_limit_bytes=VMEM_LIMIT),
        name="ffn",
    )(x, g, wu, wd)


def _ple_kernel(x_ref, pe_ref, wg_ref, wp_ref, g_ref, o_ref):
    x = x_ref[...]
    gate = jax.nn.sigmoid(_mm(x.astype(BF16), wg_ref[...]))
    x = x + gate * _mm(pe_ref[...].astype(BF16), wp_ref[...])
    o_ref[...] = _rms(x, g_ref[...])


def _ple(x, pe, wg, wp, g, tm):
    m = x.shape[0]
    row = lambda i: (i, 0)
    return pl.pallas_call(
        _ple_kernel,
        out_shape=jax.ShapeDtypeStruct((m, D_MODEL), F32),
        grid=(m // tm,),
        in_specs=[
            pl.BlockSpec((tm, D_MODEL), row),
            pl.BlockSpec((tm, PLE_DIM), row),
            _resident((D_MODEL, D_MODEL)),
            _resident((PLE_DIM, D_MODEL)),
            pl.BlockSpec((1, D_MODEL), lambda i: (0, 0)),
        ],
        out_specs=pl.BlockSpec((tm, D_MODEL), row),
        compiler_params=pltpu.CompilerParams(
            dimension_semantics=("arbitrary",), vmem_limit_bytes=VMEM_LIMIT),
        name="ple",
    )(x, pe, wg, wp, g)


def _dense_tail(x, ya, yr, pe, wts, tiles):
    (g_mix, w_gates, wba, wbr, wo, g_ffn, wu, wd, wg, wp, g_fin) = wts
    x = _mix(x, ya, yr, g_mix, w_gates, wba, wbr, wo, tiles["mix_m"])
    x = _ffn(x, g_ffn, wu, wd, tiles["ffn_m"], tiles["ffn_k"])
    return _ple(x, pe, wg, wp, g_fin, tiles["ple_m"])


def _shift_out(proj, batch, seq):
    last = proj.reshape(batch, seq, P_COLS)[:, -1]
    return last[:, P_C:P_C + RWKV_COLS][None]


def kernel(x_prompt, x_sample, cache_k_win, cache_v_win, state_wkv, state_shift, p_prompt, p_sample,
           norm_mix, w_in, attn_sinks, rwkv_mu, rwkv_w0, rwkv_w2, rwkv_a0, rwkv_a2, rwkv_g2,
           rwkv_k_k, rwkv_k_a, rwkv_r_k, rwkv_ln_w, rwkv_ln_b, w_branch_attn, w_branch_rwkv,
           w_out, norm_ffn, w_ff_up, w_ff_down, w_ple_proj, w_ple_gate, norm_final):
    assert w_in.shape[0] == 1, "single-layer step"
    bp, tp = x_prompt.shape[0], x_prompt.shape[1]
    bs, ts = x_sample.shape[0], x_sample.shape[1]
    rw = R_WIDTH

    wl = jnp.zeros((LORA_PAD, 3 * rw), F32)
    wl = wl.at[0:DECAY_LORA, 0:rw].set(rwkv_w2[0])
    wl = wl.at[DECAY_LORA:DECAY_LORA + AAA_LORA, rw:2 * rw].set(rwkv_a2[0])
    wl = wl.at[DECAY_LORA + AAA_LORA:LORA_COLS, 2 * rw:3 * rw].set(rwkv_g2[0])
    wl = wl.astype(BF16)
    seg_id = np.arange(SEG) // R_HEAD
    bones = jnp.asarray(seg_id[:, None] == seg_id[None, :], BF16)
    mu = rwkv_mu[0]
    row = lambda v: v.reshape(1, -1)
    wkv_params = (row(mu[:3 * rw]), row(jnp.pad(mu[3 * rw:], (0, LORA_PAD - LORA_COLS))),
                  row(rwkv_w0[0]), row(rwkv_a0[0]), row(rwkv_k_k[0]), row(rwkv_k_a[0]),
                  row(rwkv_r_k[0]), row(rwkv_ln_w[0]), row(rwkv_ln_b[0]))
    g_mix = row(norm_mix[0])
    sinks = attn_sinks[0]

    ms = bs * ts
    xs = x_sample.reshape(ms, D_MODEL)
    w_in_t = jnp.transpose(w_in[0])
    proj_s, w_in_b = _proj_cast(xs, g_mix, w_in_t, 1024)

    tiles_p = dict(mix_m=512, ffn_m=1024, ffn_k=512, ple_m=512)
    xp = x_prompt.reshape(bp * tp, D_MODEL)
    proj_p = _proj(xp, g_mix, w_in_b, 1024, 1024)
    ya_p = _attn_prompt(proj_p, sinks, bp, tp)
    yr_p, s_p, (wu, wd, wo, wg, wba, wbr, w_gates) = _wkv(
        proj_p, jnp.zeros((bp, 1, SHIFT_COLS), F32), jnp.zeros((bp, R_HEADS, R_HEAD, R_HEAD), F32),
        wkv_params, wl, bones, bp, tp, 64, 4,
        casts=((w_ff_up[0], None), (w_ff_down[0], None), (w_out[0], None), (w_ple_gate[0], None),
               (w_branch_attn[0], None), (w_branch_rwkv[0], None), (w_in_t, (P_GATES, 2 * D_MODEL))))
    dense = (g_mix, w_gates, wba, wbr, wo, row(norm_ffn[0]), wu, wd, wg, w_ple_proj[0].astype(BF16),
             row(norm_final))
    yp = _dense_tail(xp, ya_p, yr_p, p_prompt[0].reshape(bp * tp, PLE_DIM), dense, tiles_p)
    pp3 = proj_p.reshape(bp, tp, P_COLS)[:, -WINDOW:]
    k_p = pp3[:, :, P_K:P_K + KV_WIDTH].reshape(1, bp, WINDOW, N_KV_HEADS, HEAD_DIM)
    v_p = pp3[:, :, P_V:P_V + KV_WIDTH].reshape(1, bp, WINDOW, N_KV_HEADS, HEAD_DIM)

    tiles_s = dict(mix_m=ms, ffn_m=ms, ffn_k=2048, ple_m=ms)
    ya_s, nk_s, nv_s = _attn_sample(proj_s, sinks, cache_k_win[0].reshape(bs, WINDOW, KV_WIDTH),
                                    cache_v_win[0].reshape(bs, WINDOW, KV_WIDTH), bs, ts, SAMPLE_GROUP)
    prev_s = jnp.pad(state_shift[0], ((0, 0), (0, LORA_PAD - LORA_COLS))).reshape(bs, 1, SHIFT_COLS)
    yr_s, s_s, _ = _wkv(proj_s, prev_s, state_wkv[0], wkv_params, wl, bones, bs, ts, ts, 1, n_seq=SAMPLE_GROUP)
    ys = _dense_tail(xs, ya_s, yr_s, p_sample[0].reshape(ms, PLE_DIM), dense, tiles_s)

    return (yp.reshape(bp, tp, D_MODEL), ys.reshape(bs, ts, D_MODEL),
            k_p, v_p, s_p[None], _shift_out(proj_p, bp, tp),
            nk_s.reshape(1, bs, WINDOW, N_KV_HEADS, HEAD_DIM), nv_s.reshape(1, bs, WINDOW, N_KV_HEADS, HEAD_DIM),
            s_s[None], _shift_out(proj_s, bs, ts))
```

```python
import functools

import numpy as np
import jax
import jax.numpy as jnp
from jax import lax
from jax.experimental import pallas as pl
from jax.experimental.pallas import tpu as pltpu

F32 = jnp.float32
BF16 = jnp.bfloat16

D_MODEL = 2048
PLE_DIM = 256
HEAD_DIM = 64
N_Q_HEADS = 16
N_KV_HEADS = 4
GQA_GROUP = 4
ATTN_WIDTH = 1024
KV_WIDTH = 256
WINDOW = 128
ALIBI_MAX = 8.0
R_HEAD = 64
R_WIDTH = 1024
R_HEADS = 16
DECAY_LORA = 64
AAA_LORA = 64
GATE_LORA = 160
LORA_COLS = DECAY_LORA + AAA_LORA + GATE_LORA
LORA_PAD = 512
RWKV_COLS = 3 * R_WIDTH + LORA_COLS
D_FF = 4 * D_MODEL
NORM_EPS = 1e-6
GN_EPS = 64e-5

P_Q = 0
P_K = ATTN_WIDTH
P_V = ATTN_WIDTH + KV_WIDTH
P_C = ATTN_WIDTH + 2 * KV_WIDTH
P_L = P_C + 3 * R_WIDTH
P_COLS = P_L + LORA_PAD
P_GATES = P_C + RWKV_COLS
C_BLK = P_C
SHIFT_COLS = 3 * R_WIDTH + LORA_PAD

PAIR = 128
N_PAIRS = R_WIDTH // PAIR
SEG = 256
MIX_CHUNK = 512
SAMPLE_GROUP = 8
PROMPT_BLOCKS = 4

VMEM_LIMIT = 56 * 1024 * 1024


def _mm(a, b):
    return jnp.dot(a, b, preferred_element_type=F32)


def _nt(a, b):
    return lax.dot_general(a, b, (((1,), (1,)), ((), ())), preferred_element_type=F32)


def _tn(a, b):
    return lax.dot_general(a, b, (((0,), (0,)), ((), ())), preferred_element_type=F32)


def _rms(x, g):
    ms = jnp.mean(x * x, axis=-1, keepdims=True)
    return x * lax.rsqrt(ms + NORM_EPS) * g


def _alibi_slope(hq):
    return float(2.0 ** (-ALIBI_MAX * (hq + 1) / N_Q_HEADS))


def _proj_kernel(x_ref, g_ref, w_ref, o_ref, h_ref):
    @pl.when(pl.program_id(1) == 0)
    def _():
        h_ref[...] = _rms(x_ref[...], g_ref[...]).astype(BF16)

    o_ref[...] = _nt(h_ref[...], w_ref[...])


def _proj_cast_kernel(x_ref, g_ref, w_ref, o_ref, wb_ref, h_ref):
    @pl.when(pl.program_id(0) == 0)
    def _():
        h_ref[...] = _rms(x_ref[...], g_ref[...]).astype(BF16)

    wb = w_ref[...].astype(BF16)
    wb_ref[...] = wb
    o_ref[...] = _nt(h_ref[...], wb)


def _proj_cast(x, g, wt_f32, tn):
    m = x.shape[0]
    return pl.pallas_call(
        _proj_cast_kernel,
        out_shape=(jax.ShapeDtypeStruct((m, P_COLS), F32), jax.ShapeDtypeStruct((P_COLS, D_MODEL), BF16)),
        grid=(P_COLS // tn,),
        in_specs=[
            pl.BlockSpec((m, D_MODEL), lambda j: (0, 0)),
            pl.BlockSpec((1, D_MODEL), lambda j: (0, 0)),
            pl.BlockSpec((tn, D_MODEL), lambda j: (j, 0)),
        ],
        out_specs=(pl.BlockSpec((m, tn), lambda j: (0, j)), pl.BlockSpec((tn, D_MODEL), lambda j: (j, 0))),
        scratch_shapes=[pltpu.VMEM((m, D_MODEL), BF16)],
        compiler_params=pltpu.CompilerParams(
            dimension_semantics=("arbitrary",), vmem_limit_bytes=VMEM_LIMIT),
        name="proj_cast",
    )(x, g, wt_f32)


def _proj(x, g, w, tm, tn):
    m = x.shape[0]
    return pl.pallas_call(
        _proj_kernel,
        out_shape=jax.ShapeDtypeStruct((m, P_COLS), F32),
        grid=(m // tm, P_COLS // tn),
        in_specs=[
            pl.BlockSpec((tm, D_MODEL), lambda i, j: (i, 0)),
            pl.BlockSpec((1, D_MODEL), lambda i, j: (0, 0)),
            pl.BlockSpec((tn, D_MODEL), lambda i, j: (j, 0)),
        ],
        out_specs=pl.BlockSpec((tm, tn), lambda i, j: (i, j)),
        scratch_shapes=[pltpu.VMEM((tm, D_MODEL), BF16)],
        compiler_params=pltpu.CompilerParams(
            dimension_semantics=("arbitrary", "arbitrary"), vmem_limit_bytes=VMEM_LIMIT),
        name="proj",
    )(x, g, w)


def _attend_heads(scores, values, sinks):
    heads = range(len(scores))
    m = []
    for h in heads:
        mh = sinks[h]
        for s in scores[h]:
            mh = jnp.maximum(mh, jnp.max(s, axis=-1, keepdims=True))
        m.append(mh)
    ps = [[jnp.exp(s - m[h]) for s in scores[h]] for h in heads]
    den = []
    for h in heads:
        dh = jnp.exp(sinks[h] - m[h])
        for p in ps[h]:
            dh = dh + jnp.sum(p, axis=-1, keepdims=True)
        den.append(dh)
    outs = []
    for h in heads:
        o = None
        for p, v in zip(ps[h], values[h]):
            t = _mm(p.astype(BF16), v)
            o = t if o is None else o + t
        outs.append(o * (1.0 / den[h]))
    return outs


def _head_slices():
    q_sl = [slice(hq * HEAD_DIM, (hq + 1) * HEAD_DIM) for hq in range(N_Q_HEADS)]
    kv_sl = [slice((hq // GQA_GROUP) * HEAD_DIM, (hq // GQA_GROUP + 1) * HEAD_DIM) for hq in range(N_Q_HEADS)]
    return q_sl, kv_sl


def _alibi_bias(dist, valid):
    slopes = np.array([_alibi_slope(hq) for hq in range(N_Q_HEADS)], np.float32)
    return np.where(valid[None], -slopes[:, None, None] * dist[None].astype(np.float32), -np.inf).astype(np.float32)


def _attn_prompt_kernel(sink_ref, bias_ref, q_ref, kp_ref, kc_ref, vp_ref, vc_ref, o_ref):
    w = WINDOW
    n_blk = q_ref.shape[0] // w
    kj = lax.broadcasted_iota(jnp.int32, (w, 2 * w), 1)
    no_prev = (kj < w) & (pl.program_id(1) == 0)
    q = q_ref[...] * (HEAD_DIM ** -0.5)
    kc, vc = kc_ref[...].astype(BF16), vc_ref[...].astype(BF16)
    ks = jnp.concatenate([kp_ref[...].astype(BF16), kc], axis=0)
    vs = jnp.concatenate([vp_ref[...].astype(BF16), vc], axis=0)
    hqs = range(N_Q_HEADS)
    q_sl, kv_sl = _head_slices()
    scores, values = [], []
    for s in range(n_blk):
        k2, v2 = ks[s * w:(s + 2) * w], vs[s * w:(s + 2) * w]
        qb = q[s * w:(s + 1) * w]
        for hq in hqs:
            sc = _nt(qb[:, q_sl[hq]].astype(BF16), k2[:, kv_sl[hq]]) + bias_ref[hq]
            scores.append([jnp.where(no_prev, -jnp.inf, sc) if s == 0 else sc])
            values.append([v2[:, kv_sl[hq]]])
    outs = _attend_heads(scores, values, [sink_ref[hq] for _ in range(n_blk) for hq in hqs])
    for s in range(n_blk):
        for hq in hqs:
            o_ref[s * w:(s + 1) * w, q_sl[hq]] = outs[s * N_Q_HEADS + hq]


def _attn_prompt(proj, sinks, batch, seq):
    rows = PROMPT_BLOCKS * WINDOW
    ns = seq // rows
    kcol, vcol = P_K // KV_WIDTH, P_V // KV_WIDTH

    def cur(c):
        return lambda b, i: (b * ns + i, c)

    def prev(c):
        return lambda b, i: (b * ns * PROMPT_BLOCKS + jnp.maximum(i * PROMPT_BLOCKS - 1, 0), c)

    ti = np.arange(WINDOW)[:, None]
    kj = np.arange(2 * WINDOW)[None, :]
    dist = ti - kj + WINDOW
    bias = _alibi_bias(dist, (dist >= 0) & (dist <= WINDOW))
    return pl.pallas_call(
        _attn_prompt_kernel,
        out_shape=jax.ShapeDtypeStruct((batch * seq, ATTN_WIDTH), F32),
        grid=(batch, ns),
        in_specs=[
            pl.BlockSpec(memory_space=pltpu.SMEM),
            _resident((N_Q_HEADS, WINDOW, 2 * WINDOW)),
            pl.BlockSpec((rows, ATTN_WIDTH), cur(P_Q // ATTN_WIDTH)),
            pl.BlockSpec((WINDOW, KV_WIDTH), prev(kcol)),
            pl.BlockSpec((rows, KV_WIDTH), cur(kcol)),
            pl.BlockSpec((WINDOW, KV_WIDTH), prev(vcol)),
            pl.BlockSpec((rows, KV_WIDTH), cur(vcol)),
        ],
        out_specs=pl.BlockSpec((rows, ATTN_WIDTH), lambda b, i: (b * ns + i, 0)),
        compiler_params=pltpu.CompilerParams(dimension_semantics=("arbitrary", "arbitrary")),
        name="attn_prompt",
    )(sinks, jnp.asarray(bias), proj, proj, proj, proj, proj)


def _attn_sample_kernel(sink_ref, q_ref, kn_ref, vn_ref, ck_ref, cv_ref, o_ref, nk_ref, nv_ref, *, seq):
    t, w = seq, WINDOW
    n_seq = q_ref.shape[0] // t
    ti = lax.broadcasted_iota(jnp.int32, (t, w), 0)
    cj = lax.broadcasted_iota(jnp.int32, (t, w), 1)
    dist_c = (ti - cj + w).astype(F32)
    valid_c = cj >= ti
    ti2 = lax.broadcasted_iota(jnp.int32, (t, t), 0)
    tj2 = lax.broadcasted_iota(jnp.int32, (t, t), 1)
    dist_n = (ti2 - tj2).astype(F32)
    valid_n = tj2 <= ti2
    hqs = range(N_Q_HEADS)
    q_sl, kv_sl = _head_slices()
    bias_c = [jnp.where(valid_c, -_alibi_slope(hq) * dist_c, -jnp.inf) for hq in hqs]
    bias_n = [jnp.where(valid_n, -_alibi_slope(hq) * dist_n, -jnp.inf) for hq in hqs]
    q_all = q_ref[...] * (HEAD_DIM ** -0.5)
    scores, values = [], []
    for s in range(n_seq):
        rs = slice(s * t, (s + 1) * t)
        kn, vn = kn_ref[rs, :], vn_ref[rs, :]
        ck, cv = ck_ref[s], cv_ref[s]
        nk_ref[s, 0:w - t, :] = ck[t:, :]
        nk_ref[s, w - t:w, :] = kn
        nv_ref[s, 0:w - t, :] = cv[t:, :]
        nv_ref[s, w - t:w, :] = vn
        ckb, cvb, knb, vnb = ck.astype(BF16), cv.astype(BF16), kn.astype(BF16), vn.astype(BF16)
        q = q_all[rs]
        qs = [q[:, q_sl[hq]].astype(BF16) for hq in hqs]
        scores += [[_nt(qs[hq], ckb[:, kv_sl[hq]]) + bias_c[hq], _nt(qs[hq], knb[:, kv_sl[hq]]) + bias_n[hq]]
                   for hq in hqs]
        values += [[cvb[:, kv_sl[hq]], vnb[:, kv_sl[hq]]] for hq in hqs]
    outs = _attend_heads(scores, values, [sink_ref[hq] for _ in range(n_seq) for hq in hqs])
    for s in range(n_seq):
        for hq in hqs:
            o_ref[s * t:(s + 1) * t, q_sl[hq]] = outs[s * N_Q_HEADS + hq]


def _attn_sample(proj, sinks, cache_k, cache_v, batch, seq, n_seq):
    kcol, vcol = P_K // KV_WIDTH, P_V // KV_WIDTH
    win = jax.ShapeDtypeStruct((batch, WINDOW, KV_WIDTH), F32)
    rows = n_seq * seq
    return pl.pallas_call(
        functools.partial(_attn_sample_kernel, seq=seq),
        out_shape=(jax.ShapeDtypeStruct((batch * seq, ATTN_WIDTH), F32), win, win),
        grid=(batch // n_seq,),
        in_specs=[
            pl.BlockSpec(memory_space=pltpu.SMEM),
            pl.BlockSpec((rows, ATTN_WIDTH), lambda b: (b, P_Q // ATTN_WIDTH)),
            pl.BlockSpec((rows, KV_WIDTH), lambda b: (b, kcol)),
            pl.BlockSpec((rows, KV_WIDTH), lambda b: (b, vcol)),
            pl.BlockSpec((n_seq, WINDOW, KV_WIDTH), lambda b: (b, 0, 0)),
            pl.BlockSpec((n_seq, WINDOW, KV_WIDTH), lambda b: (b, 0, 0)),
        ],
        out_specs=(
            pl.BlockSpec((rows, ATTN_WIDTH), lambda b: (b, 0)),
            pl.BlockSpec((n_seq, WINDOW, KV_WIDTH), lambda b: (b, 0, 0)),
            pl.BlockSpec((n_seq, WINDOW, KV_WIDTH), lambda b: (b, 0, 0)),
        ),
        compiler_params=pltpu.CompilerParams(dimension_semantics=("arbitrary",)),
        name="attn_sample",
    )(sinks, proj, proj, proj, cache_k, cache_v)


def _seg_sum(x, bones):
    rows = x.shape[0]
    hi = x.astype(BF16).astype(F32)
    lo = x - hi
    groups = [slice(j * SEG, (j + 1) * SEG) for j in range(R_WIDTH // SEG)]
    lhs = jnp.concatenate([t[:, sl] for sl in groups for t in (hi, lo)], axis=0).astype(BF16)
    out = _mm(lhs, bones)
    return jnp.concatenate(
        [out[2 * j * rows:(2 * j + 1) * rows] + out[(2 * j + 1) * rows:(2 * j + 2) * rows]
         for j in range(len(groups))], axis=1)


def _wkv_kernel(pa_ref, pb_ref, pl_ref, prev_ref, s0_ref,
                mu_ref, mul_ref, w0_ref, a0_ref, kk_ref, ka_ref, rk_ref, lnw_ref, lnb_ref,
                wl_ref, bones_ref, *rest, chunk, n_sub, n_seq, n_steps, cast_transposed):
    n_cast = len(cast_transposed)
    cast_src, (y_ref, so_ref) = rest[:n_cast], rest[n_cast:n_cast + 2]
    cast_dst, (s_ref, carry_ref) = rest[n_cast + 2:2 * n_cast + 2], rest[2 * n_cast + 2:]
    for src, dst, transposed in zip(cast_src, cast_dst, cast_transposed):
        dst[...] = (src[...].T if transposed else src[...]).astype(BF16)

    step = pl.program_id(1)
    cs = chunk
    seq_rows = n_sub * cs
    rows = n_seq * seq_rows
    gc = 2 * cs
    hd = R_HEAD
    w = R_WIDTH
    seqs = range(n_seq)

    @pl.when(step == 0)
    def _init():
        s_ref[...] = jnp.zeros(s_ref.shape, F32)
        for q in seqs:
            carry_ref[q:q + 1, :] = prev_ref[q]
            for p in range(N_PAIRS):
                s_ref[q * N_PAIRS + p, 0:hd, 0:hd] = s0_ref[q, 2 * p]
                s_ref[q * N_PAIRS + p, hd:2 * hd, hd:2 * hd] = s0_ref[q, 2 * p + 1]

    row = lax.broadcasted_iota(jnp.int32, (rows, 1), 0)

    def token_shift(x, lo, hi, mu):
        shifted = pltpu.roll(x, 1, axis=0)
        for q in seqs:
            shifted = jnp.where(row == q * seq_rows, carry_ref[q:q + 1, lo:hi], shifted)
        return x + (shifted - x) * mu

    cols = jnp.concatenate([pa_ref[...], pb_ref[...]], axis=1)
    l_raw = pl_ref[...]
    xx = token_shift(cols, 0, 3 * w, mu_ref[...])
    xr, xk, xv = xx[:, 0:w], xx[:, w:2 * w], xx[:, 2 * w:3 * w]
    xl = token_shift(l_raw, 3 * w, SHIFT_COLS, mul_ref[...])
    for q in seqs:
        last = (q + 1) * seq_rows
        carry_ref[q:q + 1, 0:3 * w] = cols[last - 1:last, :]
        carry_ref[q:q + 1, 3 * w:] = l_raw[last - 1:last, :]

    lane_l = lax.broadcasted_iota(jnp.int32, (1, LORA_PAD), 1)
    act = jnp.where(lane_l < DECAY_LORA, jnp.tanh(xl),
                    jnp.where(lane_l < DECAY_LORA + AAA_LORA, xl,
                              jnp.where(lane_l < LORA_COLS, jax.nn.sigmoid(xl), 0.0)))
    up = _mm(act.astype(BF16), wl_ref[...])
    z = -(w0_ref[...] + up[:, 0:w])
    softplus = jnp.maximum(z, 0.0) + jnp.log1p(jnp.exp(-jnp.abs(z)))
    lwd = -jnp.exp(-softplus - 0.5)
    a = jax.nn.sigmoid(a0_ref[...] + up[:, w:2 * w])
    gate = up[:, 2 * w:3 * w]

    bones = bones_ref[...]
    kkn = xk * kk_ref[...]
    kk = kkn * lax.rsqrt(jnp.maximum(_seg_sum(kkn * kkn, bones), 1e-24))
    k2 = xk * (1.0 + (a - 1.0) * ka_ref[...])
    b = kk * a

    tri = (lax.broadcasted_iota(jnp.int32, (cs, cs), 0)
           >= lax.broadcasted_iota(jnp.int32, (cs, cs), 1)).astype(BF16)
    ri = lax.broadcasted_iota(jnp.int32, (gc, gc), 0)
    ci = lax.broadcasted_iota(jnp.int32, (gc, gc), 1)
    same_head = (ri >= cs) == (ci >= cs)
    strict = same_head & (ci < ri)
    incl = same_head & (ci <= ri)
    eye = (ri == ci).astype(F32)
    head0 = lax.broadcasted_iota(jnp.int32, (1, PAIR), 1) < hd

    def stack(x):
        return jnp.concatenate([jnp.where(head0, x, 0.0), jnp.where(head0, 0.0, x)], axis=0).astype(BF16)

    pairs = range(N_PAIRS)
    lanes = [slice(p * PAIR, (p + 1) * PAIR) for p in pairs]
    squarings = cs.bit_length() - 2

    chunks = []
    for sub in range(n_seq * n_sub):
        rs = slice(sub * cs, (sub + 1) * cs)
        lw_c = lwd[rs]
        lw_hi = lw_c.astype(BF16)
        lw_r = lw_c - lw_hi.astype(F32)
        lw_mid = lw_r.astype(BF16)
        cum = _mm(tri, lw_hi) + _mm(tri, lw_mid) + _mm(tri, (lw_r - lw_mid.astype(F32)).astype(BF16))
        cum_last = cum[cs - 1:cs, :]
        e_inv = jnp.exp(-cum)
        e_last = jnp.exp(cum_last - cum)
        kq = kk[rs] * jnp.exp(cum - lw_c)
        rq = xr[rs] * jnp.exp(cum)
        kd = k2[rs] * e_inv
        bd = b[rs] * e_inv
        kdp = k2[rs] * e_last
        bdp = b[rs] * e_last
        xv_c = xv[rs]
        kq_s = [stack(kq[:, sl]) for sl in lanes]
        bd_s = [stack(bd[:, sl]) for sl in lanes]
        kd_s = [stack(kd[:, sl]) for sl in lanes]
        rq_s = [stack(rq[:, sl]) for sl in lanes]
        probes = [jnp.concatenate([kq_s[p], rq_s[p]], axis=0) for p in pairs]
        if gc % PAIR == 0:
            a_all = [_nt(probes[p], jnp.concatenate([bd_s[p], kd_s[p]], axis=0)) for p in pairs]
            a_kb = [jnp.where(strict, t[0:gc, 0:gc], 0.0) for t in a_all]
            a_kk = [jnp.where(strict, t[0:gc, gc:2 * gc], 0.0).astype(BF16) for t in a_all]
            a_rb = [jnp.where(incl, t[gc:2 * gc, 0:gc], 0.0).astype(BF16) for t in a_all]
            a_rk = [jnp.where(incl, t[gc:2 * gc, gc:2 * gc], 0.0).astype(BF16) for t in a_all]
        else:
            a_kb = [jnp.where(strict, _nt(kq_s[p], bd_s[p]), 0.0) for p in pairs]
            a_kk = [jnp.where(strict, _nt(kq_s[p], kd_s[p]), 0.0).astype(BF16) for p in pairs]
            a_rb = [jnp.where(incl, _nt(rq_s[p], bd_s[p]), 0.0).astype(BF16) for p in pairs]
            a_rk = [jnp.where(incl, _nt(rq_s[p], kd_s[p]), 0.0).astype(BF16) for p in pairs]
        inv = [eye - t for t in a_kb]
        if squarings >= 1:
            apow_b = [t.astype(BF16) for t in a_kb]
            apow_b = [_mm(t, t).astype(BF16) for t in apow_b]
            for _ in range(squarings - 1):
                both = [_mm(jnp.concatenate([apow_b[p], inv[p].astype(BF16)], axis=0), apow_b[p]) for p in pairs]
                apow_b = [t[0:gc].astype(BF16) for t in both]
                inv = [inv[p] + both[p][gc:2 * gc] for p in pairs]
            inv = [inv[p] + _mm(inv[p].astype(BF16), apow_b[p]) for p in pairs]
        chunks.append(dict(
            probes=probes, a_kk=a_kk, a_rk=a_rk, a_rb=a_rb, inv=[t.astype(BF16) for t in inv],
            v_s=[stack(xv_c[:, sl]) for sl in lanes],
            upd=[jnp.concatenate([stack(kdp[:, sl]), stack(bdp[:, sl])], axis=0) for sl in lanes],
            p_last=jnp.exp(cum_last)))

    sp = [(q, p) for q in seqs for p in pairs]
    state = {(q, p): s_ref[q * N_PAIRS + p] for q, p in sp}
    y_rows = [None] * (n_seq * n_sub)
    for level in range(n_sub):
        ch = {q: chunks[q * n_sub + level] for q in seqs}
        s_b = {k: state[k].astype(BF16) for k in sp}
        state_t = {(q, p): _nt(ch[q]["probes"][p], s_b[q, p]) for q, p in sp}
        rhs = {(q, p): state_t[q, p][0:gc] + _mm(ch[q]["a_kk"][p], ch[q]["v_s"][p]) for q, p in sp}
        u_b = {(q, p): _mm(ch[q]["inv"][p], rhs[q, p].astype(BF16)).astype(BF16) for q, p in sp}
        if gc % PAIR == 0:
            y_s = {(q, p): state_t[q, p][gc:2 * gc]
                   + _mm(jnp.concatenate([ch[q]["a_rk"][p], -ch[q]["a_rb"][p]], axis=1),
                         jnp.concatenate([ch[q]["v_s"][p], u_b[q, p]], axis=0)) for q, p in sp}
        else:
            y_s = {(q, p): state_t[q, p][gc:2 * gc] + _mm(ch[q]["a_rk"][p], ch[q]["v_s"][p])
                   - _mm(ch[q]["a_rb"][p], u_b[q, p]) for q, p in sp}
        state = {(q, p): state[q, p] * ch[q]["p_last"][:, lanes[p]]
                 + _tn(jnp.concatenate([ch[q]["v_s"][p], -u_b[q, p]], axis=0), ch[q]["upd"][p]) for q, p in sp}
        for q in seqs:
            y_rows[q * n_sub + level] = jnp.concatenate(
                [y_s[q, p][0:cs] + y_s[q, p][cs:gc] for p in pairs], axis=1)
    for q, p in sp:
        s_ref[q * N_PAIRS + p] = state[q, p]

    y = jnp.concatenate(y_rows, axis=0) if len(y_rows) > 1 else y_rows[0]
    mean = _seg_sum(y, bones) * (1.0 / hd)
    d = y - mean
    var = _seg_sum(d * d, bones) * (1.0 / hd)
    yn = d * lax.rsqrt(var + GN_EPS) * lnw_ref[...] + lnb_ref[...]
    bonus = _seg_sum(xr * k2 * rk_ref[...], bones) * xv
    y_ref[...] = (yn + bonus) * gate

    @pl.when(step == n_steps - 1)
    def _fin():
        for q, p in sp:
            so_ref[q, 2 * p] = s_ref[q * N_PAIRS + p, 0:hd, 0:hd]
            so_ref[q, 2 * p + 1] = s_ref[q * N_PAIRS + p, hd:2 * hd, hd:2 * hd]


def _wkv(proj, prev0, s0, params, wl, bones, batch, seq, chunk, n_sub, n_seq=1, casts=()):
    ns = seq // (chunk * n_sub)
    assert n_seq == 1 or ns == 1, "several sequences per step only when a step covers them whole"
    rows = n_seq * chunk * n_sub
    nb = batch // n_seq
    rw = R_WIDTH

    def col(cb):
        return lambda b, c: (b * ns + c, cb)

    const2 = lambda b, c: (0, 0)
    vec = lambda n: pl.BlockSpec((1, n), const2)
    cast_in, cast_out, cast_shapes = [], [], []
    for wgt, span in casts:
        start, count = (0, wgt.shape[0]) if span is None else span
        blk, width = count // (nb * ns), wgt.shape[1]
        assert blk * nb * ns == count, (wgt.shape, span)
        if span is None:
            assert blk % 16 == 0, blk
            cast_in.append(pl.BlockSpec((blk, width), col(0)))
            cast_out.append(pl.BlockSpec((blk, width), col(0)))
            cast_shapes.append(jax.ShapeDtypeStruct((count, width), BF16))
        else:
            assert blk % 128 == 0 and start % 8 == 0, span
            cast_in.append(pl.BlockSpec(
                (pl.Element(blk), pl.Element(width)),
                lambda b, c, start=start, blk=blk: (pl.multiple_of(start + blk * (b * ns + c), 8), 0)))
            cast_out.append(pl.BlockSpec((width, blk), lambda b, c: (0, b * ns + c)))
            cast_shapes.append(jax.ShapeDtypeStruct((width, count), BF16))
    mu, mul, w0, a0, k_k, k_a, r_k, ln_w, ln_b = params
    outs = pl.pallas_call(
        functools.partial(_wkv_kernel, chunk=chunk, n_sub=n_sub, n_seq=n_seq, n_steps=ns,
                          cast_transposed=tuple(span is not None for _, span in casts)),
        out_shape=(jax.ShapeDtypeStruct((batch * seq, rw), F32),
                   jax.ShapeDtypeStruct((batch, R_HEADS, R_HEAD, R_HEAD), F32), *cast_shapes),
        grid=(nb, ns),
        in_specs=[
            pl.BlockSpec((rows, C_BLK), col(P_C // C_BLK)),
            pl.BlockSpec((rows, C_BLK), col(P_C // C_BLK + 1)),
            pl.BlockSpec((rows, LORA_PAD), col(P_L // LORA_PAD)),
            pl.BlockSpec((n_seq, 1, SHIFT_COLS), lambda b, c: (b, 0, 0)),
            pl.BlockSpec((n_seq, R_HEADS, R_HEAD, R_HEAD), lambda b, c: (b, 0, 0, 0)),
            vec(3 * rw), vec(LORA_PAD), vec(rw), vec(rw), vec(rw), vec(rw), vec(rw), vec(rw), vec(rw),
            pl.BlockSpec((LORA_PAD, 3 * rw), const2),
            pl.BlockSpec((SEG, SEG), const2),
            *cast_in,
        ],
        out_specs=(
            pl.BlockSpec((rows, rw), lambda b, c: (b * ns + c, 0)),
            pl.BlockSpec((n_seq, R_HEADS, R_HEAD, R_HEAD), lambda b, c: (b, 0, 0, 0)),
            *cast_out,
        ),
        scratch_shapes=[pltpu.VMEM((n_seq * N_PAIRS, PAIR, PAIR), F32), pltpu.VMEM((n_seq, SHIFT_COLS), F32)],
        compiler_params=pltpu.CompilerParams(
            dimension_semantics=("arbitrary", "arbitrary"), vmem_limit_bytes=VMEM_LIMIT),
        name="wkv",
    )(proj, proj, proj, prev0, s0, mu, mul, w0, a0, k_k, k_a, r_k, ln_w, ln_b, wl, bones,
      *[wgt for wgt, _ in casts])
    return outs[0], outs[1], outs[2:]


def _mix_kernel(x_ref, ya_ref, yr_ref, g_ref, wg_ref, wba_ref, wbr_ref, wo_ref, o_ref):
    x = x_ref[...]
    h = _rms(x, g_ref[...]).astype(BF16)
    ya = ya_ref[...].astype(BF16)
    yr = yr_ref[...].astype(BF16)
    acc = x
    for c in range(D_MODEL // MIX_CHUNK):
        ca = slice(c * MIX_CHUNK, (c + 1) * MIX_CHUNK)
        cr = slice(D_MODEL + c * MIX_CHUNK, D_MODEL + (c + 1) * MIX_CHUNK)
        mixed = (jax.nn.sigmoid(_mm(h, wg_ref[:, ca])) * _mm(ya, wba_ref[:, ca])
                 + jax.nn.sigmoid(_mm(h, wg_ref[:, cr])) * _mm(yr, wbr_ref[:, ca]))
        acc = acc + _mm(mixed.astype(BF16), wo_ref[ca, :])
    o_ref[...] = acc


def _resident(shape):
    return pl.BlockSpec(shape, lambda *_: (0,) * len(shape), pipeline_mode=pl.Buffered(1))


def _mix(x, ya, yr, g, wg, wba, wbr, wo, tm):
    m = x.shape[0]
    row = lambda i: (i, 0)
    return pl.pallas_call(
        _mix_kernel,
        out_shape=jax.ShapeDtypeStruct((m, D_MODEL), F32),
        grid=(m // tm,),
        in_specs=[
            pl.BlockSpec((tm, D_MODEL), row),
            pl.BlockSpec((tm, ATTN_WIDTH), row),
            pl.BlockSpec((tm, R_WIDTH), row),
            pl.BlockSpec((1, D_MODEL), lambda i: (0, 0)),
            _resident((D_MODEL, 2 * D_MODEL)),
            _resident((ATTN_WIDTH, D_MODEL)),
            _resident((R_WIDTH, D_MODEL)),
            _resident((D_MODEL, D_MODEL)),
        ],
        out_specs=pl.BlockSpec((tm, D_MODEL), row),
        compiler_params=pltpu.CompilerParams(
            dimension_semantics=("arbitrary",), vmem_limit_bytes=VMEM_LIMIT),
        name="mix",
    )(x, ya, yr, g, wg, wba, wbr, wo)


def _ffn_kernel(x_ref, g_ref, wu_ref, wd_ref, o_ref, h_ref):
    @pl.when(pl.program_id(1) == 0)
    def _():
        x = x_ref[...]
        h_ref[...] = _rms(x, g_ref[...]).astype(BF16)
        o_ref[...] = x

    u = _mm(h_ref[...], wu_ref[...])
    o_ref[...] += _mm(jnp.square(jnp.maximum(u, 0.0)).astype(BF16), wd_ref[...])


def _ffn(x, g, wu, wd, tm, tk):
    m = x.shape[0]
    return pl.pallas_call(
        _ffn_kernel,
        out_shape=jax.ShapeDtypeStruct((m, D_MODEL), F32),
        grid=(m // tm, D_FF // tk),
        in_specs=[
            pl.BlockSpec((tm, D_MODEL), lambda i, k: (i, 0)),
            pl.BlockSpec((1, D_MODEL), lambda i, k: (0, 0)),
            pl.BlockSpec((D_MODEL, tk), lambda i, k: (0, k)),
            pl.BlockSpec((tk, D_MODEL), lambda i, k: (k, 0)),
        ],
        out_specs=pl.BlockSpec((tm, D_MODEL), lambda i, k: (i, 0)),
        scratch_shapes=[pltpu.VMEM((tm, D_MODEL), BF16)],
        compiler_params=pltpu.CompilerParams(
            dimension_semantics=("arbitrary", "arbitrary"), vmem_limit_bytes=VMEM_LIMIT),
        name="ffn",
    )(x, g, wu, wd)


def _ple_kernel(x_ref, pe_ref, wg_ref, wp_ref, g_ref, o_ref):
    x = x_ref[...]
    gate = jax.nn.sigmoid(_mm(x.astype(BF16), wg_ref[...]))
    x = x + gate * _mm(pe_ref[...].astype(BF16), wp_ref[...])
    o_ref[...] = _rms(x, g_ref[...])


def _ple(x, pe, wg, wp, g, tm):
    m = x.shape[0]
    row = lambda i: (i, 0)
    return pl.pallas_call(
        _ple_kernel,
        out_shape=jax.ShapeDtypeStruct((m, D_MODEL), F32),
        grid=(m // tm,),
        in_specs=[
            pl.BlockSpec((tm, D_MODEL), row),
            pl.BlockSpec((tm, PLE_DIM), row),
            _resident((D_MODEL, D_MODEL)),
            _resident((PLE_DIM, D_MODEL)),
            pl.BlockSpec((1, D_MODEL), lambda i: (0, 0)),
        ],
        out_specs=pl.BlockSpec((tm, D_MODEL), row),
        compiler_params=pltpu.CompilerParams(
            dimension_semantics=("arbitrary",), vmem_limit_bytes=VMEM_LIMIT),
        name="ple",
    )(x, pe, wg, wp, g)


def _dense_tail(x, ya, yr, pe, wts, tiles):
    (g_mix, w_gates, wba, wbr, wo, g_ffn, wu, wd, wg, wp, g_fin) = wts
    x = _mix(x, ya, yr, g_mix, w_gates, wba, wbr, wo, tiles["mix_m"])
    x = _ffn(x, g_ffn, wu, wd, tiles["ffn_m"], tiles["ffn_k"])
    return _ple(x, pe, wg, wp, g_fin, tiles["ple_m"])


def _shift_out(proj, batch, seq):
    last = proj.reshape(batch, seq, P_COLS)[:, -1]
    return last[:, P_C:P_C + RWKV_COLS][None]


def kernel(x_prompt, x_sample, cache_k_win, cache_v_win, state_wkv, state_shift, p_prompt, p_sample,
           norm_mix, w_in, attn_sinks, rwkv_mu, rwkv_w0, rwkv_w2, rwkv_a0, rwkv_a2, rwkv_g2,
           rwkv_k_k, rwkv_k_a, rwkv_r_k, rwkv_ln_w, rwkv_ln_b, w_branch_attn, w_branch_rwkv,
           w_out, norm_ffn, w_ff_up, w_ff_down, w_ple_proj, w_ple_gate, norm_final):
    assert w_in.shape[0] == 1, "single-layer step"
    bp, tp = x_prompt.shape[0], x_prompt.shape[1]
    bs, ts = x_sample.shape[0], x_sample.shape[1]
    rw = R_WIDTH

    wl = jnp.zeros((LORA_PAD, 3 * rw), F32)
    wl = wl.at[0:DECAY_LORA, 0:rw].set(rwkv_w2[0])
    wl = wl.at[DECAY_LORA:DECAY_LORA + AAA_LORA, rw:2 * rw].set(rwkv_a2[0])
    wl = wl.at[DECAY_LORA + AAA_LORA:LORA_COLS, 2 * rw:3 * rw].set(rwkv_g2[0])
    wl = wl.astype(BF16)
    seg_id = np.arange(SEG) // R_HEAD
    bones = jnp.asarray(seg_id[:, None] == seg_id[None, :], BF16)
    mu = rwkv_mu[0]
    row = lambda v: v.reshape(1, -1)
    wkv_params = (row(mu[:3 * rw]), row(jnp.pad(mu[3 * rw:], (0, LORA_PAD - LORA_COLS))),
                  row(rwkv_w0[0]), row(rwkv_a0[0]), row(rwkv_k_k[0]), row(rwkv_k_a[0]),
                  row(rwkv_r_k[0]), row(rwkv_ln_w[0]), row(rwkv_ln_b[0]))
    g_mix = row(norm_mix[0])
    sinks = attn_sinks[0]

    ms = bs * ts
    xs = x_sample.reshape(ms, D_MODEL)
    w_in_t = jnp.transpose(w_in[0])
    proj_s, w_in_b = _proj_cast(xs, g_mix, w_in_t, 1024)

    tiles_p = dict(mix_m=256, ffn_m=1024, ffn_k=512, ple_m=512)
    xp = x_prompt.reshape(bp * tp, D_MODEL)
    proj_p = _proj(xp, g_mix, w_in_b, 1024, 1024)
    ya_p = _attn_prompt(proj_p, sinks, bp, tp)
    yr_p, s_p, (wu, wd, wo, wg, wba, wbr, w_gates) = _wkv(
        proj_p, jnp.zeros((bp, 1, SHIFT_COLS), F32), jnp.zeros((bp, R_HEADS, R_HEAD, R_HEAD), F32),
        wkv_params, wl, bones, bp, tp, 64, 4,
        casts=((w_ff_up[0], None), (w_ff_down[0], None), (w_out[0], None), (w_ple_gate[0], None),
               (w_branch_attn[0], None), (w_branch_rwkv[0], None), (w_in_t, (P_GATES, 2 * D_MODEL))))
    dense = (g_mix, w_gates, wba, wbr, wo, row(norm_ffn[0]), wu, wd, wg, w_ple_proj[0].astype(BF16),
             row(norm_final))
    yp = _dense_tail(xp, ya_p, yr_p, p_prompt[0].reshape(bp * tp, PLE_DIM), dense, tiles_p)
    pp3 = proj_p.reshape(bp, tp, P_COLS)[:, -WINDOW:]
    k_p = pp3[:, :, P_K:P_K + KV_WIDTH].reshape(1, bp, WINDOW, N_KV_HEADS, HEAD_DIM)
    v_p = pp3[:, :, P_V:P_V + KV_WIDTH].reshape(1, bp, WINDOW, N_KV_HEADS, HEAD_DIM)

    tiles_s = dict(mix_m=ms, ffn_m=ms, ffn_k=2048, ple_m=ms)
    ya_s, nk_s, nv_s = _attn_sample(proj_s, sinks, cache_k_win[0].reshape(bs, WINDOW, KV_WIDTH),
                                    cache_v_win[0].reshape(bs, WINDOW, KV_WIDTH), bs, ts, SAMPLE_GROUP)
    prev_s = jnp.pad(state_shift[0], ((0, 0), (0, LORA_PAD - LORA_COLS))).reshape(bs, 1, SHIFT_COLS)
    yr_s, s_s, _ = _wkv(proj_s, prev_s, state_wkv[0], wkv_params, wl, bones, bs, ts, ts, 1, n_seq=SAMPLE_GROUP)
    ys = _dense_tail(xs, ya_s, yr_s, p_sample[0].reshape(ms, PLE_DIM), dense, tiles_s)

    return (yp.reshape(bp, tp, D_MODEL), ys.reshape(bs, ts, D_MODEL),
            k_p, v_p, s_p[None], _shift_out(proj_p, bp, tp),
            nk_s.reshape(1, bs, WINDOW, N_KV_HEADS, HEAD_DIM), nv_s.reshape(1, bs, WINDOW, N_KV_HEADS, HEAD_DIM),
            s_s[None], _shift_out(proj_s, bs, ts))
```

```python
import functools

import numpy as np
import jax
import jax.numpy as jnp
from jax import lax
from jax.experimental import pallas as pl
from jax.experimental.pallas import tpu as pltpu

F32 = jnp.float32
BF16 = jnp.bfloat16

D_MODEL = 2048
PLE_DIM = 256
HEAD_DIM = 64
N_Q_HEADS = 16
N_KV_HEADS = 4
GQA_GROUP = 4
ATTN_WIDTH = 1024
KV_WIDTH = 256
WINDOW = 128
ALIBI_MAX = 8.0
R_HEAD = 64
R_WIDTH = 1024
R_HEADS = 16
DECAY_LORA = 64
AAA_LORA = 64
GATE_LORA = 160
LORA_COLS = DECAY_LORA + AAA_LORA + GATE_LORA
LORA_PAD = 512
RWKV_COLS = 3 * R_WIDTH + LORA_COLS
D_FF = 4 * D_MODEL
NORM_EPS = 1e-6
GN_EPS = 64e-5

P_Q = 0
P_K = ATTN_WIDTH
P_V = ATTN_WIDTH + KV_WIDTH
P_C = ATTN_WIDTH + 2 * KV_WIDTH
P_L = P_C + 3 * R_WIDTH
P_COLS = P_L + LORA_PAD
P_GATES = P_C + RWKV_COLS
C_BLK = P_C
SHIFT_COLS = 3 * R_WIDTH + LORA_PAD

PAIR = 128
N_PAIRS = R_WIDTH // PAIR
SEG = 256
MIX_CHUNK = 512
SAMPLE_GROUP = 8
PROMPT_BLOCKS = 4

VMEM_LIMIT = 56 * 1024 * 1024


def _mm(a, b):
    return jnp.dot(a, b, preferred_element_type=F32)


def _nt(a, b):
    return lax.dot_general(a, b, (((1,), (1,)), ((), ())), preferred_element_type=F32)


def _tn(a, b):
    return lax.dot_general(a, b, (((0,), (0,)), ((), ())), preferred_element_type=F32)


def _rms(x, g):
    ms = jnp.mean(x * x, axis=-1, keepdims=True)
    return x * lax.rsqrt(ms + NORM_EPS) * g


def _alibi_slope(hq):
    return float(2.0 ** (-ALIBI_MAX * (hq + 1) / N_Q_HEADS))


def _proj_kernel(x_ref, g_ref, w_ref, o_ref, h_ref):
    @pl.when(pl.program_id(1) == 0)
    def _():
        h_ref[...] = _rms(x_ref[...], g_ref[...]).astype(BF16)

    o_ref[...] = _nt(h_ref[...], w_ref[...])


def _proj_cast_kernel(x_ref, g_ref, w_ref, o_ref, wb_ref, h_ref):
    @pl.when(pl.program_id(0) == 0)
    def _():
        h_ref[...] = _rms(x_ref[...], g_ref[...]).astype(BF16)

    wb = w_ref[...].astype(BF16)
    wb_ref[...] = wb
    o_ref[...] = _nt(h_ref[...], wb)


def _proj_cast(x, g, wt_f32, tn):
    m = x.shape[0]
    return pl.pallas_call(
        _proj_cast_kernel,
        out_shape=(jax.ShapeDtypeStruct((m, P_COLS), F32), jax.ShapeDtypeStruct((P_COLS, D_MODEL), BF16)),
        grid=(P_COLS // tn,),
        in_specs=[
            pl.BlockSpec((m, D_MODEL), lambda j: (0, 0)),
            pl.BlockSpec((1, D_MODEL), lambda j: (0, 0)),
            pl.BlockSpec((tn, D_MODEL), lambda j: (j, 0)),
        ],
        out_specs=(pl.BlockSpec((m, tn), lambda j: (0, j)), pl.BlockSpec((tn, D_MODEL), lambda j: (j, 0))),
        scratch_shapes=[pltpu.VMEM((m, D_MODEL), BF16)],
        compiler_params=pltpu.CompilerParams(
            dimension_semantics=("arbitrary",), vmem_limit_bytes=VMEM_LIMIT),
        name="proj_cast",
    )(x, g, wt_f32)


def _proj(x, g, w, tm, tn):
    m = x.shape[0]
    return pl.pallas_call(
        _proj_kernel,
        out_shape=jax.ShapeDtypeStruct((m, P_COLS), F32),
        grid=(m // tm, P_COLS // tn),
        in_specs=[
            pl.BlockSpec((tm, D_MODEL), lambda i, j: (i, 0)),
            pl.BlockSpec((1, D_MODEL), lambda i, j: (0, 0)),
            pl.BlockSpec((tn, D_MODEL), lambda i, j: (j, 0)),
        ],
        out_specs=pl.BlockSpec((tm, tn), lambda i, j: (i, j)),
        scratch_shapes=[pltpu.VMEM((tm, D_MODEL), BF16)],
        compiler_params=pltpu.CompilerParams(
            dimension_semantics=("arbitrary", "arbitrary"), vmem_limit_bytes=VMEM_LIMIT),
        name="proj",
    )(x, g, w)


def _attend_heads(scores, apply_values, sinks):
    heads = range(len(scores))
    m = []
    for h in heads:
        mh = sinks[h]
        for s in scores[h]:
            mh = jnp.maximum(mh, jnp.max(s, axis=-1, keepdims=True))
        m.append(mh)
    ps = [[jnp.exp(s - m[h]) for s in scores[h]] for h in heads]
    den = []
    for h in heads:
        dh = jnp.exp(sinks[h] - m[h])
        for p in ps[h]:
            dh = dh + jnp.sum(p, axis=-1, keepdims=True)
        den.append(dh)
    outs = []
    for h in heads:
        o = None
        for p, pv in zip(ps[h], apply_values[h]):
            t = pv(p.astype(BF16))
            o = t if o is None else o + t
        outs.append(o * (1.0 / den[h]))
    return outs


def _head_slices():
    q_sl = [slice(hq * HEAD_DIM, (hq + 1) * HEAD_DIM) for hq in range(N_Q_HEADS)]
    kv_sl = [slice((hq // GQA_GROUP) * HEAD_DIM, (hq // GQA_GROUP + 1) * HEAD_DIM) for hq in range(N_Q_HEADS)]
    return q_sl, kv_sl


def _alibi_bias(dist, valid):
    slopes = np.array([_alibi_slope(hq) for hq in range(N_Q_HEADS)], np.float32)
    return np.where(valid[None], -slopes[:, None, None] * dist[None].astype(np.float32), -np.inf).astype(np.float32)


def _attn_prompt_kernel(sink_ref, bias_ref, q_ref, kp_ref, kc_ref, vp_ref, vc_ref, o_ref):
    w = WINDOW
    n_blk = q_ref.shape[0] // w
    kj = lax.broadcasted_iota(jnp.int32, (w, 2 * w), 1)
    no_prev = (kj < w) & (pl.program_id(1) == 0)
    q = q_ref[...] * (HEAD_DIM ** -0.5)
    kc, vc = kc_ref[...].astype(BF16), vc_ref[...].astype(BF16)
    ks = jnp.concatenate([kp_ref[...].astype(BF16), kc], axis=0)
    vs = jnp.concatenate([vp_ref[...].astype(BF16), vc], axis=0)
    hqs = range(N_Q_HEADS)
    q_sl, kv_sl = _head_slices()
    scores, values = [], []
    for s in range(n_blk):
        k2, v2 = ks[s * w:(s + 2) * w], vs[s * w:(s + 2) * w]
        qb = q[s * w:(s + 1) * w]
        for hq in hqs:
            sc = _nt(qb[:, q_sl[hq]].astype(BF16), k2[:, kv_sl[hq]]) + bias_ref[hq]
            scores.append([jnp.where(no_prev, -jnp.inf, sc) if s == 0 else sc])
            values.append([functools.partial(_mm, b=v2[:, kv_sl[hq]])])
    outs = _attend_heads(scores, values, [sink_ref[hq] for _ in range(n_blk) for hq in hqs])
    for s in range(n_blk):
        for hq in hqs:
            o_ref[s * w:(s + 1) * w, q_sl[hq]] = outs[s * N_Q_HEADS + hq]


def _attn_prompt(proj, sinks, batch, seq):
    rows = PROMPT_BLOCKS * WINDOW
    ns = seq // rows
    kcol, vcol = P_K // KV_WIDTH, P_V // KV_WIDTH

    def cur(c):
        return lambda b, i: (b * ns + i, c)

    def prev(c):
        return lambda b, i: (b * ns * PROMPT_BLOCKS + jnp.maximum(i * PROMPT_BLOCKS - 1, 0), c)

    ti = np.arange(WINDOW)[:, None]
    kj = np.arange(2 * WINDOW)[None, :]
    dist = ti - kj + WINDOW
    bias = _alibi_bias(dist, (dist >= 0) & (dist <= WINDOW))
    return pl.pallas_call(
        _attn_prompt_kernel,
        out_shape=jax.ShapeDtypeStruct((batch * seq, ATTN_WIDTH), F32),
        grid=(batch, ns),
        in_specs=[
            pl.BlockSpec(memory_space=pltpu.SMEM),
            _resident((N_Q_HEADS, WINDOW, 2 * WINDOW)),
            pl.BlockSpec((rows, ATTN_WIDTH), cur(P_Q // ATTN_WIDTH)),
            pl.BlockSpec((WINDOW, KV_WIDTH), prev(kcol)),
            pl.BlockSpec((rows, KV_WIDTH), cur(kcol)),
            pl.BlockSpec((WINDOW, KV_WIDTH), prev(vcol)),
            pl.BlockSpec((rows, KV_WIDTH), cur(vcol)),
        ],
        out_specs=pl.BlockSpec((rows, ATTN_WIDTH), lambda b, i: (b * ns + i, 0)),
        compiler_params=pltpu.CompilerParams(dimension_semantics=("arbitrary", "arbitrary")),
        name="attn_prompt",
    )(sinks, jnp.asarray(bias), proj, proj, proj, proj, proj)


def _attn_sample_kernel(sink_ref, q_ref, kn_ref, vn_ref, ckt_ref, cvt_ref, o_ref, *, seq):
    t, w = seq, WINDOW
    n_seq = q_ref.shape[0] // t
    ti = lax.broadcasted_iota(jnp.int32, (t, w), 0)
    cj = lax.broadcasted_iota(jnp.int32, (t, w), 1)
    dist_c = (ti - cj + w).astype(F32)
    valid_c = cj >= ti
    ti2 = lax.broadcasted_iota(jnp.int32, (t, t), 0)
    tj2 = lax.broadcasted_iota(jnp.int32, (t, t), 1)
    dist_n = (ti2 - tj2).astype(F32)
    valid_n = tj2 <= ti2
    hqs = range(N_Q_HEADS)
    q_sl, kv_sl = _head_slices()
    bias_c = [jnp.where(valid_c, -_alibi_slope(hq) * dist_c, -jnp.inf) for hq in hqs]
    bias_n = [jnp.where(valid_n, -_alibi_slope(hq) * dist_n, -jnp.inf) for hq in hqs]
    q_all = q_ref[...] * (HEAD_DIM ** -0.5)
    scores, values = [], []
    for s in range(n_seq):
        rs = slice(s * t, (s + 1) * t)
        cktb, cvtb = ckt_ref[s].astype(BF16), cvt_ref[s].astype(BF16)
        knb, vnb = kn_ref[rs, :].astype(BF16), vn_ref[rs, :].astype(BF16)
        q = q_all[rs]
        qs = [q[:, q_sl[hq]].astype(BF16) for hq in hqs]
        scores += [[_mm(qs[hq], cktb[kv_sl[hq], :]) + bias_c[hq], _nt(qs[hq], knb[:, kv_sl[hq]]) + bias_n[hq]]
                   for hq in hqs]
        values += [[functools.partial(_nt, b=cvtb[kv_sl[hq], :]), functools.partial(_mm, b=vnb[:, kv_sl[hq]])]
                   for hq in hqs]
    outs = _attend_heads(scores, values, [sink_ref[hq] for _ in range(n_seq) for hq in hqs])
    for s in range(n_seq):
        for hq in hqs:
            o_ref[s * t:(s + 1) * t, q_sl[hq]] = outs[s * N_Q_HEADS + hq]


def _attn_sample(proj, sinks, cache_kt, cache_vt, batch, seq, n_seq):
    kcol, vcol = P_K // KV_WIDTH, P_V // KV_WIDTH
    rows = n_seq * seq
    return pl.pallas_call(
        functools.partial(_attn_sample_kernel, seq=seq),
        out_shape=jax.ShapeDtypeStruct((batch * seq, ATTN_WIDTH), F32),
        grid=(batch // n_seq,),
        in_specs=[
            pl.BlockSpec(memory_space=pltpu.SMEM),
            pl.BlockSpec((rows, ATTN_WIDTH), lambda b: (b, P_Q // ATTN_WIDTH)),
            pl.BlockSpec((rows, KV_WIDTH), lambda b: (b, kcol)),
            pl.BlockSpec((rows, KV_WIDTH), lambda b: (b, vcol)),
            pl.BlockSpec((n_seq, KV_WIDTH, WINDOW), lambda b: (b, 0, 0)),
            pl.BlockSpec((n_seq, KV_WIDTH, WINDOW), lambda b: (b, 0, 0)),
        ],
        out_specs=pl.BlockSpec((rows, ATTN_WIDTH), lambda b: (b, 0)),
        compiler_params=pltpu.CompilerParams(dimension_semantics=("arbitrary",)),
        name="attn_sample",
    )(sinks, proj, proj, proj, cache_kt, cache_vt)


def _roll_window(cache_t, new, batch, seq):
    new_t = jnp.transpose(new.reshape(batch, seq, KV_WIDTH), (0, 2, 1))
    win_t = jnp.concatenate([cache_t[:, :, seq:], new_t], axis=2)
    return jnp.transpose(win_t.reshape(batch, N_KV_HEADS, HEAD_DIM, WINDOW), (0, 3, 1, 2))[None]


def _seg_sum(x, bones):
    rows = x.shape[0]
    hi = x.astype(BF16).astype(F32)
    lo = x - hi
    groups = [slice(j * SEG, (j + 1) * SEG) for j in range(R_WIDTH // SEG)]
    lhs = jnp.concatenate([t[:, sl] for sl in groups for t in (hi, lo)], axis=0).astype(BF16)
    out = _mm(lhs, bones)
    return jnp.concatenate(
        [out[2 * j * rows:(2 * j + 1) * rows] + out[(2 * j + 1) * rows:(2 * j + 2) * rows]
         for j in range(len(groups))], axis=1)


def _wkv_kernel(pa_ref, pb_ref, pl_ref, prev_ref, s0_ref,
                mu_ref, mul_ref, w0_ref, a0_ref, kk_ref, ka_ref, rk_ref, lnw_ref, lnb_ref,
                wl_ref, bones_ref, *rest, chunk, n_sub, n_seq, n_steps, cast_transposed):
    n_cast = len(cast_transposed)
    cast_src, (y_ref, so_ref) = rest[:n_cast], rest[n_cast:n_cast + 2]
    cast_dst, (s_ref, carry_ref) = rest[n_cast + 2:2 * n_cast + 2], rest[2 * n_cast + 2:]
    for src, dst, transposed in zip(cast_src, cast_dst, cast_transposed):
        dst[...] = (src[...].T if transposed else src[...]).astype(BF16)

    step = pl.program_id(1)
    cs = chunk
    seq_rows = n_sub * cs
    rows = n_seq * seq_rows
    gc = 2 * cs
    hd = R_HEAD
    w = R_WIDTH
    seqs = range(n_seq)

    @pl.when(step == 0)
    def _init():
        s_ref[...] = jnp.zeros(s_ref.shape, F32)
        for q in seqs:
            carry_ref[q:q + 1, :] = prev_ref[q]
            for p in range(N_PAIRS):
                s_ref[q * N_PAIRS + p, 0:hd, 0:hd] = s0_ref[q, 2 * p]
                s_ref[q * N_PAIRS + p, hd:2 * hd, hd:2 * hd] = s0_ref[q, 2 * p + 1]

    row = lax.broadcasted_iota(jnp.int32, (rows, 1), 0)

    def token_shift(x, lo, hi, mu):
        shifted = pltpu.roll(x, 1, axis=0)
        for q in seqs:
            shifted = jnp.where(row == q * seq_rows, carry_ref[q:q + 1, lo:hi], shifted)
        return x + (shifted - x) * mu

    cols = jnp.concatenate([pa_ref[...], pb_ref[...]], axis=1)
    l_raw = pl_ref[...]
    xx = token_shift(cols, 0, 3 * w, mu_ref[...])
    xr, xk, xv = xx[:, 0:w], xx[:, w:2 * w], xx[:, 2 * w:3 * w]
    xl = token_shift(l_raw, 3 * w, SHIFT_COLS, mul_ref[...])
    for q in seqs:
        last = (q + 1) * seq_rows
        carry_ref[q:q + 1, 0:3 * w] = cols[last - 1:last, :]
        carry_ref[q:q + 1, 3 * w:] = l_raw[last - 1:last, :]

    lane_l = lax.broadcasted_iota(jnp.int32, (1, LORA_PAD), 1)
    act = jnp.where(lane_l < DECAY_LORA, jnp.tanh(xl),
                    jnp.where(lane_l < DECAY_LORA + AAA_LORA, xl,
                              jnp.where(lane_l < LORA_COLS, jax.nn.sigmoid(xl), 0.0)))
    up = _mm(act.astype(BF16), wl_ref[...])
    z = -(w0_ref[...] + up[:, 0:w])
    softplus = jnp.maximum(z, 0.0) + jnp.log1p(jnp.exp(-jnp.abs(z)))
    lwd = -jnp.exp(-softplus - 0.5)
    a = jax.nn.sigmoid(a0_ref[...] + up[:, w:2 * w])
    gate = up[:, 2 * w:3 * w]

    bones = bones_ref[...]
    kkn = xk * kk_ref[...]
    kk = kkn * lax.rsqrt(jnp.maximum(_seg_sum(kkn * kkn, bones), 1e-24))
    k2 = xk * (1.0 + (a - 1.0) * ka_ref[...])
    b = kk * a

    tri = (lax.broadcasted_iota(jnp.int32, (cs, cs), 0)
           >= lax.broadcasted_iota(jnp.int32, (cs, cs), 1)).astype(BF16)
    ri = lax.broadcasted_iota(jnp.int32, (gc, gc), 0)
    ci = lax.broadcasted_iota(jnp.int32, (gc, gc), 1)
    same_head = (ri >= cs) == (ci >= cs)
    strict = same_head & (ci < ri)
    incl = same_head & (ci <= ri)
    eye = (ri == ci).astype(F32)
    head0 = lax.broadcasted_iota(jnp.int32, (1, PAIR), 1) < hd

    def stack(x):
        return jnp.concatenate([jnp.where(head0, x, 0.0), jnp.where(head0, 0.0, x)], axis=0).astype(BF16)

    pairs = range(N_PAIRS)
    lanes = [slice(p * PAIR, (p + 1) * PAIR) for p in pairs]
    squarings = cs.bit_length() - 2

    chunks = []
    for sub in range(n_seq * n_sub):
        rs = slice(sub * cs, (sub + 1) * cs)
        lw_c = lwd[rs]
        lw_hi = lw_c.astype(BF16)
        lw_r = lw_c - lw_hi.astype(F32)
        lw_mid = lw_r.astype(BF16)
        cum = _mm(tri, lw_hi) + _mm(tri, lw_mid) + _mm(tri, (lw_r - lw_mid.astype(F32)).astype(BF16))
        cum_last = cum[cs - 1:cs, :]
        e_inv = jnp.exp(-cum)
        e_last = jnp.exp(cum_last - cum)
        kq = kk[rs] * jnp.exp(cum - lw_c)
        rq = xr[rs] * jnp.exp(cum)
        kd = k2[rs] * e_inv
        bd = b[rs] * e_inv
        kdp = k2[rs] * e_last
        bdp = b[rs] * e_last
        xv_c = xv[rs]
        kq_s = [stack(kq[:, sl]) for sl in lanes]
        bd_s = [stack(bd[:, sl]) for sl in lanes]
        kd_s = [stack(kd[:, sl]) for sl in lanes]
        rq_s = [stack(rq[:, sl]) for sl in lanes]
        probes = [jnp.concatenate([kq_s[p], rq_s[p]], axis=0) for p in pairs]
        if gc % PAIR == 0:
            a_all = [_nt(probes[p], jnp.concatenate([bd_s[p], kd_s[p]], axis=0)) for p in pairs]
            a_kb = [jnp.where(strict, t[0:gc, 0:gc], 0.0) for t in a_all]
            a_kk = [jnp.where(strict, t[0:gc, gc:2 * gc], 0.0).astype(BF16) for t in a_all]
            a_rb = [jnp.where(incl, t[gc:2 * gc, 0:gc], 0.0).astype(BF16) for t in a_all]
            a_rk = [jnp.where(incl, t[gc:2 * gc, gc:2 * gc], 0.0).astype(BF16) for t in a_all]
        else:
            a_kb = [jnp.where(strict, _nt(kq_s[p], bd_s[p]), 0.0) for p in pairs]
            a_kk = [jnp.where(strict, _nt(kq_s[p], kd_s[p]), 0.0).astype(BF16) for p in pairs]
            a_rb = [jnp.where(incl, _nt(rq_s[p], bd_s[p]), 0.0).astype(BF16) for p in pairs]
            a_rk = [jnp.where(incl, _nt(rq_s[p], kd_s[p]), 0.0).astype(BF16) for p in pairs]
        inv = [eye - t for t in a_kb]
        if squarings >= 1:
            apow_b = [t.astype(BF16) for t in a_kb]
            apow_b = [_mm(t, t).astype(BF16) for t in apow_b]
            for _ in range(squarings - 1):
                both = [_mm(jnp.concatenate([apow_b[p], inv[p].astype(BF16)], axis=0), apow_b[p]) for p in pairs]
                apow_b = [t[0:gc].astype(BF16) for t in both]
                inv = [inv[p] + both[p][gc:2 * gc] for p in pairs]
            inv = [inv[p] + _mm(inv[p].astype(BF16), apow_b[p]) for p in pairs]
        chunks.append(dict(
            probes=probes, a_kk=a_kk, a_rk=a_rk, a_rb=a_rb, inv=[t.astype(BF16) for t in inv],
            v_s=[stack(xv_c[:, sl]) for sl in lanes],
            upd=[jnp.concatenate([stack(kdp[:, sl]), stack(bdp[:, sl])], axis=0) for sl in lanes],
            p_last=jnp.exp(cum_last)))

    sp = [(q, p) for q in seqs for p in pairs]
    state = {(q, p): s_ref[q * N_PAIRS + p] for q, p in sp}
    y_rows = [None] * (n_seq * n_sub)
    for level in range(n_sub):
        ch = {q: chunks[q * n_sub + level] for q in seqs}
        s_b = {k: state[k].astype(BF16) for k in sp}
        state_t = {(q, p): _nt(ch[q]["probes"][p], s_b[q, p]) for q, p in sp}
        rhs = {(q, p): state_t[q, p][0:gc] + _mm(ch[q]["a_kk"][p], ch[q]["v_s"][p]) for q, p in sp}
        u_b = {(q, p): _mm(ch[q]["inv"][p], rhs[q, p].astype(BF16)).astype(BF16) for q, p in sp}
        if gc % PAIR == 0:
            y_s = {(q, p): state_t[q, p][gc:2 * gc]
                   + _mm(jnp.concatenate([ch[q]["a_rk"][p], -ch[q]["a_rb"][p]], axis=1),
                         jnp.concatenate([ch[q]["v_s"][p], u_b[q, p]], axis=0)) for q, p in sp}
        else:
            y_s = {(q, p): state_t[q, p][gc:2 * gc] + _mm(ch[q]["a_rk"][p], ch[q]["v_s"][p])
                   - _mm(ch[q]["a_rb"][p], u_b[q, p]) for q, p in sp}
        state = {(q, p): state[q, p] * ch[q]["p_last"][:, lanes[p]]
                 + _tn(jnp.concatenate([ch[q]["v_s"][p], -u_b[q, p]], axis=0), ch[q]["upd"][p]) for q, p in sp}
        for q in seqs:
            y_rows[q * n_sub + level] = jnp.concatenate(
                [y_s[q, p][0:cs] + y_s[q, p][cs:gc] for p in pairs], axis=1)
    for q, p in sp:
        s_ref[q * N_PAIRS + p] = state[q, p]

    y = jnp.concatenate(y_rows, axis=0) if len(y_rows) > 1 else y_rows[0]
    mean = _seg_sum(y, bones) * (1.0 / hd)
    d = y - mean
    var = _seg_sum(d * d, bones) * (1.0 / hd)
    yn = d * lax.rsqrt(var + GN_EPS) * lnw_ref[...] + lnb_ref[...]
    bonus = _seg_sum(xr * k2 * rk_ref[...], bones) * xv
    y_ref[...] = (yn + bonus) * gate

    @pl.when(step == n_steps - 1)
    def _fin():
        for q, p in sp:
            so_ref[q, 2 * p] = s_ref[q * N_PAIRS + p, 0:hd, 0:hd]
            so_ref[q, 2 * p + 1] = s_ref[q * N_PAIRS + p, hd:2 * hd, hd:2 * hd]


def _wkv(proj, prev0, s0, params, wl, bones, batch, seq, chunk, n_sub, n_seq=1, casts=()):
    ns = seq // (chunk * n_sub)
    assert n_seq == 1 or ns == 1, "several sequences per step only when a step covers them whole"
    rows = n_seq * chunk * n_sub
    nb = batch // n_seq
    rw = R_WIDTH

    def col(cb):
        return lambda b, c: (b * ns + c, cb)

    const2 = lambda b, c: (0, 0)
    vec = lambda n: pl.BlockSpec((1, n), const2)
    cast_in, cast_out, cast_shapes = [], [], []
    for wgt, span in casts:
        start, count = (0, wgt.shape[0]) if span is None else span
        blk, width = count // (nb * ns), wgt.shape[1]
        assert blk * nb * ns == count, (wgt.shape, span)
        if span is None:
            assert blk % 16 == 0, blk
            cast_in.append(pl.BlockSpec((blk, width), col(0)))
            cast_out.append(pl.BlockSpec((blk, width), col(0)))
            cast_shapes.append(jax.ShapeDtypeStruct((count, width), BF16))
        else:
            assert blk % 128 == 0 and start % 8 == 0, span
            cast_in.append(pl.BlockSpec(
                (pl.Element(blk), pl.Element(width)),
                lambda b, c, start=start, blk=blk: (pl.multiple_of(start + blk * (b * ns + c), 8), 0)))
            cast_out.append(pl.BlockSpec((width, blk), lambda b, c: (0, b * ns + c)))
            cast_shapes.append(jax.ShapeDtypeStruct((width, count), BF16))
    mu, mul, w0, a0, k_k, k_a, r_k, ln_w, ln_b = params
    outs = pl.pallas_call(
        functools.partial(_wkv_kernel, chunk=chunk, n_sub=n_sub, n_seq=n_seq, n_steps=ns,
                          cast_transposed=tuple(span is not None for _, span in casts)),
        out_shape=(jax.ShapeDtypeStruct((batch * seq, rw), F32),
                   jax.ShapeDtypeStruct((batch, R_HEADS, R_HEAD, R_HEAD), F32), *cast_shapes),
        grid=(nb, ns),
        in_specs=[
            pl.BlockSpec((rows, C_BLK), col(P_C // C_BLK)),
            pl.BlockSpec((rows, C_BLK), col(P_C // C_BLK + 1)),
            pl.BlockSpec((rows, LORA_PAD), col(P_L // LORA_PAD)),
            pl.BlockSpec((n_seq, 1, SHIFT_COLS), lambda b, c: (b, 0, 0)),
            pl.BlockSpec((n_seq, R_HEADS, R_HEAD, R_HEAD), lambda b, c: (b, 0, 0, 0)),
            vec(3 * rw), vec(LORA_PAD), vec(rw), vec(rw), vec(rw), vec(rw), vec(rw), vec(rw), vec(rw),
            pl.BlockSpec((LORA_PAD, 3 * rw), const2),
            pl.BlockSpec((SEG, SEG), const2),
            *cast_in,
        ],
        out_specs=(
            pl.BlockSpec((rows, rw), lambda b, c: (b * ns + c, 0)),
            pl.BlockSpec((n_seq, R_HEADS, R_HEAD, R_HEAD), lambda b, c: (b, 0, 0, 0)),
            *cast_out,
        ),
        scratch_shapes=[pltpu.VMEM((n_seq * N_PAIRS, PAIR, PAIR), F32), pltpu.VMEM((n_seq, SHIFT_COLS), F32)],
        compiler_params=pltpu.CompilerParams(
            dimension_semantics=("arbitrary", "arbitrary"), vmem_limit_bytes=VMEM_LIMIT),
        name="wkv",
    )(proj, proj, proj, prev0, s0, mu, mul, w0, a0, k_k, k_a, r_k, ln_w, ln_b, wl, bones,
      *[wgt for wgt, _ in casts])
    return outs[0], outs[1], outs[2:]


def _mix_kernel(x_ref, ya_ref, yr_ref, g_ref, wg_ref, wba_ref, wbr_ref, wo_ref, o_ref):
    x = x_ref[...]
    h = _rms(x, g_ref[...]).astype(BF16)
    ya = ya_ref[...].astype(BF16)
    yr = yr_ref[...].astype(BF16)
    acc = x
    for c in range(D_MODEL // MIX_CHUNK):
        ca = slice(c * MIX_CHUNK, (c + 1) * MIX_CHUNK)
        cr = slice(D_MODEL + c * MIX_CHUNK, D_MODEL + (c + 1) * MIX_CHUNK)
        mixed = (jax.nn.sigmoid(_mm(h, wg_ref[:, ca])) * _mm(ya, wba_ref[:, ca])
                 + jax.nn.sigmoid(_mm(h, wg_ref[:, cr])) * _mm(yr, wbr_ref[:, ca]))
        acc = acc + _mm(mixed.astype(BF16), wo_ref[ca, :])
    o_ref[...] = acc


def _resident(shape):
    return pl.BlockSpec(shape, lambda *_: (0,) * len(shape), pipeline_mode=pl.Buffered(1))


def _mix(x, ya, yr, g, wg, wba, wbr, wo, tm):
    m = x.shape[0]
    row = lambda i: (i, 0)
    return pl.pallas_call(
        _mix_kernel,
        out_shape=jax.ShapeDtypeStruct((m, D_MODEL), F32),
        grid=(m // tm,),
        in_specs=[
            pl.BlockSpec((tm, D_MODEL), row),
            pl.BlockSpec((tm, ATTN_WIDTH), row),
            pl.BlockSpec((tm, R_WIDTH), row),
            pl.BlockSpec((1, D_MODEL), lambda i: (0, 0)),
            _resident((D_MODEL, 2 * D_MODEL)),
            _resident((ATTN_WIDTH, D_MODEL)),
            _resident((R_WIDTH, D_MODEL)),
            _resident((D_MODEL, D_MODEL)),
        ],
        out_specs=pl.BlockSpec((tm, D_MODEL), row),
        compiler_params=pltpu.CompilerParams(
            dimension_semantics=("arbitrary",), vmem_limit_bytes=VMEM_LIMIT),
        name="mix",
    )(x, ya, yr, g, wg, wba, wbr, wo)


def _ffn_kernel(x_ref, g_ref, wu_ref, wd_ref, o_ref, h_ref):
    @pl.when(pl.program_id(1) == 0)
    def _():
        x = x_ref[...]
        h_ref[...] = _rms(x, g_ref[...]).astype(BF16)
        o_ref[...] = x

    u = _mm(h_ref[...], wu_ref[...])
    o_ref[...] += _mm(jnp.square(jnp.maximum(u, 0.0)).astype(BF16), wd_ref[...])


def _ffn(x, g, wu, wd, tm, tk):
    m = x.shape[0]
    return pl.pallas_call(
        _ffn_kernel,
        out_shape=jax.ShapeDtypeStruct((m, D_MODEL), F32),
        grid=(m // tm, D_FF // tk),
        in_specs=[
            pl.BlockSpec((tm, D_MODEL), lambda i, k: (i, 0)),
            pl.BlockSpec((1, D_MODEL), lambda i, k: (0, 0)),
            pl.BlockSpec((D_MODEL, tk), lambda i, k: (0, k)),
            pl.BlockSpec((tk, D_MODEL), lambda i, k: (k, 0)),
        ],
        out_specs=pl.BlockSpec((tm, D_MODEL), lambda i, k: (i, 0)),
        scratch_shapes=[pltpu.VMEM((tm, D_MODEL), BF16)],
        compiler_params=pltpu.CompilerParams(
            dimension_semantics=("arbitrary", "arbitrary"), vmem_limit_bytes=VMEM_LIMIT),
        name="ffn",
    )(x, g, wu, wd)


def _ple_kernel(x_ref, pe_ref, wg_ref, wp_ref, g_ref, o_ref):
    x = x_ref[...]
    gate = jax.nn.sigmoid(_mm(x.astype(BF16), wg_ref[...]))
    x = x + gate * _mm(pe_ref[...].astype(BF16), wp_ref[...])
    o_ref[...] = _rms(x, g_ref[...])


def _ple(x, pe, wg, wp, g, tm):
    m = x.shape[0]
    row = lambda i: (i, 0)
    return pl.pallas_call(
        _ple_kernel,
        out_shape=jax.ShapeDtypeStruct((m, D_MODEL), F32),
        grid=(m // tm,),
        in_specs=[
            pl.BlockSpec((tm, D_MODEL), row),
            pl.BlockSpec((tm, PLE_DIM), row),
            _resident((D_MODEL, D_MODEL)),
            _resident((PLE_DIM, D_MODEL)),
            pl.BlockSpec((1, D_MODEL), lambda i: (0, 0)),
        ],
        out_specs=pl.BlockSpec((tm, D_MODEL), row),
        compiler_params=pltpu.CompilerParams(
            dimension_semantics=("arbitrary",), vmem_limit_bytes=VMEM_LIMIT),
        name="ple",
    )(x, pe, wg, wp, g)


def _dense_tail(x, ya, yr, pe, wts, tiles):
    (g_mix, w_gates, wba, wbr, wo, g_ffn, wu, wd, wg, wp, g_fin) = wts
    x = _mix(x, ya, yr, g_mix, w_gates, wba, wbr, wo, tiles["mix_m"])
    x = _ffn(x, g_ffn, wu, wd, tiles["ffn_m"], tiles["ffn_k"])
    return _ple(x, pe, wg, wp, g_fin, tiles["ple_m"])


def _shift_out(proj, batch, seq):
    last = proj.reshape(batch, seq, P_COLS)[:, -1]
    return last[:, P_C:P_C + RWKV_COLS][None]


def kernel(x_prompt, x_sample, cache_k_win, cache_v_win, state_wkv, state_shift, p_prompt, p_sample,
           norm_mix, w_in, attn_sinks, rwkv_mu, rwkv_w0, rwkv_w2, rwkv_a0, rwkv_a2, rwkv_g2,
           rwkv_k_k, rwkv_k_a, rwkv_r_k, rwkv_ln_w, rwkv_ln_b, w_branch_attn, w_branch_rwkv,
           w_out, norm_ffn, w_ff_up, w_ff_down, w_ple_proj, w_ple_gate, norm_final):
    assert w_in.shape[0] == 1, "single-layer step"
    bp, tp = x_prompt.shape[0], x_prompt.shape[1]
    bs, ts = x_sample.shape[0], x_sample.shape[1]
    rw = R_WIDTH

    wl = jnp.zeros((LORA_PAD, 3 * rw), F32)
    wl = wl.at[0:DECAY_LORA, 0:rw].set(rwkv_w2[0])
    wl = wl.at[DECAY_LORA:DECAY_LORA + AAA_LORA, rw:2 * rw].set(rwkv_a2[0])
    wl = wl.at[DECAY_LORA + AAA_LORA:LORA_COLS, 2 * rw:3 * rw].set(rwkv_g2[0])
    wl = wl.astype(BF16)
    seg_id = np.arange(SEG) // R_HEAD
    bones = jnp.asarray(seg_id[:, None] == seg_id[None, :], BF16)
    mu = rwkv_mu[0]
    row = lambda v: v.reshape(1, -1)
    wkv_params = (row(mu[:3 * rw]), row(jnp.pad(mu[3 * rw:], (0, LORA_PAD - LORA_COLS))),
                  row(rwkv_w0[0]), row(rwkv_a0[0]), row(rwkv_k_k[0]), row(rwkv_k_a[0]),
                  row(rwkv_r_k[0]), row(rwkv_ln_w[0]), row(rwkv_ln_b[0]))
    g_mix = row(norm_mix[0])
    sinks = attn_sinks[0]

    ms = bs * ts
    xs = x_sample.reshape(ms, D_MODEL)
    w_in_t = jnp.transpose(w_in[0])
    proj_s, w_in_b = _proj_cast(xs, g_mix, w_in_t, 1024)

    tiles_p = dict(mix_m=256, ffn_m=1024, ffn_k=512, ple_m=512)
    xp = x_prompt.reshape(bp * tp, D_MODEL)
    proj_p = _proj(xp, g_mix, w_in_b, 1024, 1024)
    ya_p = _attn_prompt(proj_p, sinks, bp, tp)
    yr_p, s_p, (wu, wd, wo, wg, wba, wbr, w_gates) = _wkv(
        proj_p, jnp.zeros((bp, 1, SHIFT_COLS), F32), jnp.zeros((bp, R_HEADS, R_HEAD, R_HEAD), F32),
        wkv_params, wl, bones, bp, tp, 64, 4,
        casts=((w_ff_up[0], None), (w_ff_down[0], None), (w_out[0], None), (w_ple_gate[0], None),
               (w_branch_attn[0], None), (w_branch_rwkv[0], None), (w_in_t, (P_GATES, 2 * D_MODEL))))
    dense = (g_mix, w_gates, wba, wbr, wo, row(norm_ffn[0]), wu, wd, wg, w_ple_proj[0].astype(BF16),
             row(norm_final))
    yp = _dense_tail(xp, ya_p, yr_p, p_prompt[0].reshape(bp * tp, PLE_DIM), dense, tiles_p)
    pp3 = proj_p.reshape(bp, tp, P_COLS)[:, -WINDOW:]
    k_p = pp3[:, :, P_K:P_K + KV_WIDTH].reshape(1, bp, WINDOW, N_KV_HEADS, HEAD_DIM)
    v_p = pp3[:, :, P_V:P_V + KV_WIDTH].reshape(1, bp, WINDOW, N_KV_HEADS, HEAD_DIM)

    tiles_s = dict(mix_m=ms, ffn_m=ms, ffn_k=2048, ple_m=ms)
    ck_t = jnp.transpose(cache_k_win[0], (0, 2, 3, 1)).reshape(bs, KV_WIDTH, WINDOW)
    cv_t = jnp.transpose(cache_v_win[0], (0, 2, 3, 1)).reshape(bs, KV_WIDTH, WINDOW)
    ya_s = _attn_sample(proj_s, sinks, ck_t, cv_t, bs, ts, SAMPLE_GROUP)
    prev_s = jnp.pad(state_shift[0], ((0, 0), (0, LORA_PAD - LORA_COLS))).reshape(bs, 1, SHIFT_COLS)
    yr_s, s_s, _ = _wkv(proj_s, prev_s, state_wkv[0], wkv_params, wl, bones, bs, ts, ts, 1, n_seq=SAMPLE_GROUP)
    ys = _dense_tail(xs, ya_s, yr_s, p_sample[0].reshape(ms, PLE_DIM), dense, tiles_s)

    return (yp.reshape(bp, tp, D_MODEL), ys.reshape(bs, ts, D_MODEL),
            k_p, v_p, s_p[None], _shift_out(proj_p, bp, tp),
            _roll_window(ck_t, proj_s[:, P_K:P_K + KV_WIDTH], bs, ts),
            _roll_window(cv_t, proj_s[:, P_V:P_V + KV_WIDTH], bs, ts),
            s_s[None], _shift_out(proj_s, bs, ts))
```

```python
import functools

import numpy as np
import jax
import jax.numpy as jnp
from jax import lax
from jax.experimental import pallas as pl
from jax.experimental.pallas import tpu as pltpu

F32 = jnp.float32
BF16 = jnp.bfloat16

D_MODEL = 2048
PLE_DIM = 256
HEAD_DIM = 64
N_Q_HEADS = 16
N_KV_HEADS = 4
GQA_GROUP = 4
ATTN_WIDTH = 1024
KV_WIDTH = 256
WINDOW = 128
ALIBI_MAX = 8.0
R_HEAD = 64
R_WIDTH = 1024
R_HEADS = 16
DECAY_LORA = 64
AAA_LORA = 64
GATE_LORA = 160
LORA_COLS = DECAY_LORA + AAA_LORA + GATE_LORA
LORA_PAD = 512
RWKV_COLS = 3 * R_WIDTH + LORA_COLS
D_FF = 4 * D_MODEL
NORM_EPS = 1e-6
GN_EPS = 64e-5

P_Q = 0
P_K = ATTN_WIDTH
P_V = ATTN_WIDTH + KV_WIDTH
P_C = ATTN_WIDTH + 2 * KV_WIDTH
P_L = P_C + 3 * R_WIDTH
P_COLS = P_L + LORA_PAD
P_GATES = P_C + RWKV_COLS
C_BLK = P_C
SHIFT_COLS = 3 * R_WIDTH + LORA_PAD

PAIR = 128
N_PAIRS = R_WIDTH // PAIR
SEG = 256
MIX_CHUNK = 512
SAMPLE_GROUP = 8
PROMPT_BLOCKS = 4

VMEM_LIMIT = 56 * 1024 * 1024


def _mm(a, b):
    return jnp.dot(a, b, preferred_element_type=F32)


def _nt(a, b):
    return lax.dot_general(a, b, (((1,), (1,)), ((), ())), preferred_element_type=F32)


def _tn(a, b):
    return lax.dot_general(a, b, (((0,), (0,)), ((), ())), preferred_element_type=F32)


def _rms(x, g):
    ms = jnp.mean(x * x, axis=-1, keepdims=True)
    return x * lax.rsqrt(ms + NORM_EPS) * g


def _alibi_slope(hq):
    return float(2.0 ** (-ALIBI_MAX * (hq + 1) / N_Q_HEADS))


def _proj_kernel(x_ref, g_ref, w_ref, o_ref, h_ref):
    @pl.when(pl.program_id(1) == 0)
    def _():
        h_ref[...] = _rms(x_ref[...], g_ref[...]).astype(BF16)

    o_ref[...] = _nt(h_ref[...], w_ref[...])


def _proj_cast_kernel(x_ref, g_ref, w_ref, o_ref, wb_ref, h_ref):
    @pl.when(pl.program_id(0) == 0)
    def _():
        h_ref[...] = _rms(x_ref[...], g_ref[...]).astype(BF16)

    wb = w_ref[...].astype(BF16)
    wb_ref[...] = wb
    o_ref[...] = _nt(h_ref[...], wb)


def _proj_cast(x, g, wt_f32, tn):
    m = x.shape[0]
    return pl.pallas_call(
        _proj_cast_kernel,
        out_shape=(jax.ShapeDtypeStruct((m, P_COLS), F32), jax.ShapeDtypeStruct((P_COLS, D_MODEL), BF16)),
        grid=(P_COLS // tn,),
        in_specs=[
            pl.BlockSpec((m, D_MODEL), lambda j: (0, 0)),
            pl.BlockSpec((1, D_MODEL), lambda j: (0, 0)),
            pl.BlockSpec((tn, D_MODEL), lambda j: (j, 0)),
        ],
        out_specs=(pl.BlockSpec((m, tn), lambda j: (0, j)), pl.BlockSpec((tn, D_MODEL), lambda j: (j, 0))),
        scratch_shapes=[pltpu.VMEM((m, D_MODEL), BF16)],
        compiler_params=pltpu.CompilerParams(
            dimension_semantics=("arbitrary",), vmem_limit_bytes=VMEM_LIMIT),
        name="proj_cast",
    )(x, g, wt_f32)


def _proj(x, g, w, tm, tn):
    m = x.shape[0]
    return pl.pallas_call(
        _proj_kernel,
        out_shape=jax.ShapeDtypeStruct((m, P_COLS), F32),
        grid=(m // tm, P_COLS // tn),
        in_specs=[
            pl.BlockSpec((tm, D_MODEL), lambda i, j: (i, 0)),
            pl.BlockSpec((1, D_MODEL), lambda i, j: (0, 0)),
            pl.BlockSpec((tn, D_MODEL), lambda i, j: (j, 0)),
        ],
        out_specs=pl.BlockSpec((tm, tn), lambda i, j: (i, j)),
        scratch_shapes=[pltpu.VMEM((tm, D_MODEL), BF16)],
        compiler_params=pltpu.CompilerParams(
            dimension_semantics=("arbitrary", "arbitrary"), vmem_limit_bytes=VMEM_LIMIT),
        name="proj",
    )(x, g, w)


def _attend_heads(scores, apply_values, sinks):
    heads = range(len(scores))
    m = []
    for h in heads:
        mh = sinks[h]
        for s in scores[h]:
            mh = jnp.maximum(mh, jnp.max(s, axis=-1, keepdims=True))
        m.append(mh)
    ps = [[jnp.exp(s - m[h]) for s in scores[h]] for h in heads]
    den = []
    for h in heads:
        dh = jnp.exp(sinks[h] - m[h])
        for p in ps[h]:
            dh = dh + jnp.sum(p, axis=-1, keepdims=True)
        den.append(dh)
    outs = []
    for h in heads:
        o = None
        for p, pv in zip(ps[h], apply_values[h]):
            t = pv(p.astype(BF16))
            o = t if o is None else o + t
        outs.append(o * (1.0 / den[h]))
    return outs


def _head_slices():
    q_sl = [slice(hq * HEAD_DIM, (hq + 1) * HEAD_DIM) for hq in range(N_Q_HEADS)]
    kv_sl = [slice((hq // GQA_GROUP) * HEAD_DIM, (hq // GQA_GROUP + 1) * HEAD_DIM) for hq in range(N_Q_HEADS)]
    return q_sl, kv_sl


def _alibi_bias(dist, valid):
    slopes = np.array([_alibi_slope(hq) for hq in range(N_Q_HEADS)], np.float32)
    return np.where(valid[None], -slopes[:, None, None] * dist[None].astype(np.float32), -np.inf).astype(np.float32)


def _attn_prompt_kernel(sink_ref, bias_ref, q_ref, kp_ref, kc_ref, vp_ref, vc_ref, o_ref):
    w = WINDOW
    n_blk = q_ref.shape[0] // w
    kj = lax.broadcasted_iota(jnp.int32, (w, 2 * w), 1)
    no_prev = (kj < w) & (pl.program_id(1) == 0)
    q = q_ref[...] * (HEAD_DIM ** -0.5)
    kc, vc = kc_ref[...].astype(BF16), vc_ref[...].astype(BF16)
    ks = jnp.concatenate([kp_ref[...].astype(BF16), kc], axis=0)
    vs = jnp.concatenate([vp_ref[...].astype(BF16), vc], axis=0)
    hqs = range(N_Q_HEADS)
    q_sl, kv_sl = _head_slices()
    scores, values = [], []
    for s in range(n_blk):
        k2, v2 = ks[s * w:(s + 2) * w], vs[s * w:(s + 2) * w]
        qb = q[s * w:(s + 1) * w]
        for hq in hqs:
            sc = _nt(qb[:, q_sl[hq]].astype(BF16), k2[:, kv_sl[hq]]) + bias_ref[hq]
            scores.append([jnp.where(no_prev, -jnp.inf, sc) if s == 0 else sc])
            values.append([functools.partial(_mm, b=v2[:, kv_sl[hq]])])
    outs = _attend_heads(scores, values, [sink_ref[hq] for _ in range(n_blk) for hq in hqs])
    for s in range(n_blk):
        for hq in hqs:
            o_ref[s * w:(s + 1) * w, q_sl[hq]] = outs[s * N_Q_HEADS + hq]


def _attn_prompt(proj, sinks, batch, seq):
    rows = PROMPT_BLOCKS * WINDOW
    ns = seq // rows
    kcol, vcol = P_K // KV_WIDTH, P_V // KV_WIDTH

    def cur(c):
        return lambda b, i: (b * ns + i, c)

    def prev(c):
        return lambda b, i: (b * ns * PROMPT_BLOCKS + jnp.maximum(i * PROMPT_BLOCKS - 1, 0), c)

    ti = np.arange(WINDOW)[:, None]
    kj = np.arange(2 * WINDOW)[None, :]
    dist = ti - kj + WINDOW
    bias = _alibi_bias(dist, (dist >= 0) & (dist <= WINDOW))
    return pl.pallas_call(
        _attn_prompt_kernel,
        out_shape=jax.ShapeDtypeStruct((batch * seq, ATTN_WIDTH), F32),
        grid=(batch, ns),
        in_specs=[
            pl.BlockSpec(memory_space=pltpu.SMEM),
            _resident((N_Q_HEADS, WINDOW, 2 * WINDOW)),
            pl.BlockSpec((rows, ATTN_WIDTH), cur(P_Q // ATTN_WIDTH)),
            pl.BlockSpec((WINDOW, KV_WIDTH), prev(kcol)),
            pl.BlockSpec((rows, KV_WIDTH), cur(kcol)),
            pl.BlockSpec((WINDOW, KV_WIDTH), prev(vcol)),
            pl.BlockSpec((rows, KV_WIDTH), cur(vcol)),
        ],
        out_specs=pl.BlockSpec((rows, ATTN_WIDTH), lambda b, i: (b * ns + i, 0)),
        compiler_params=pltpu.CompilerParams(dimension_semantics=("arbitrary", "arbitrary")),
        name="attn_prompt",
    )(sinks, jnp.asarray(bias), proj, proj, proj, proj, proj)


def _attn_sample_kernel(sink_ref, q_ref, kn_ref, vn_ref, ckt_ref, cvt_ref, o_ref, nkt_ref, nvt_ref, *, seq):
    t, w = seq, WINDOW
    n_seq = q_ref.shape[0] // t
    ti = lax.broadcasted_iota(jnp.int32, (t, w), 0)
    cj = lax.broadcasted_iota(jnp.int32, (t, w), 1)
    dist_c = (ti - cj + w).astype(F32)
    valid_c = cj >= ti
    ti2 = lax.broadcasted_iota(jnp.int32, (t, t), 0)
    tj2 = lax.broadcasted_iota(jnp.int32, (t, t), 1)
    dist_n = (ti2 - tj2).astype(F32)
    valid_n = tj2 <= ti2
    hqs = range(N_Q_HEADS)
    q_sl, kv_sl = _head_slices()
    bias_c = [jnp.where(valid_c, -_alibi_slope(hq) * dist_c, -jnp.inf) for hq in hqs]
    bias_n = [jnp.where(valid_n, -_alibi_slope(hq) * dist_n, -jnp.inf) for hq in hqs]
    q_all = q_ref[...] * (HEAD_DIM ** -0.5)
    scores, values = [], []
    for s in range(n_seq):
        rs = slice(s * t, (s + 1) * t)
        ckt, cvt = ckt_ref[s], cvt_ref[s]
        nkt_ref[s] = pltpu.roll(ckt, w - t, axis=1)
        nvt_ref[s] = pltpu.roll(cvt, w - t, axis=1)
        cktb, cvtb = ckt.astype(BF16), cvt.astype(BF16)
        knb, vnb = kn_ref[rs, :].astype(BF16), vn_ref[rs, :].astype(BF16)
        q = q_all[rs]
        qs = [q[:, q_sl[hq]].astype(BF16) for hq in hqs]
        scores += [[_mm(qs[hq], cktb[kv_sl[hq], :]) + bias_c[hq], _nt(qs[hq], knb[:, kv_sl[hq]]) + bias_n[hq]]
                   for hq in hqs]
        values += [[functools.partial(_nt, b=cvtb[kv_sl[hq], :]), functools.partial(_mm, b=vnb[:, kv_sl[hq]])]
                   for hq in hqs]
    outs = _attend_heads(scores, values, [sink_ref[hq] for _ in range(n_seq) for hq in hqs])
    for s in range(n_seq):
        for hq in hqs:
            o_ref[s * t:(s + 1) * t, q_sl[hq]] = outs[s * N_Q_HEADS + hq]


def _attn_sample(proj, sinks, cache_kt, cache_vt, batch, seq, n_seq):
    kcol, vcol = P_K // KV_WIDTH, P_V // KV_WIDTH
    rows = n_seq * seq
    win = jax.ShapeDtypeStruct((batch, KV_WIDTH, WINDOW), F32)
    return pl.pallas_call(
        functools.partial(_attn_sample_kernel, seq=seq),
        out_shape=(jax.ShapeDtypeStruct((batch * seq, ATTN_WIDTH), F32), win, win),
        grid=(batch // n_seq,),
        in_specs=[
            pl.BlockSpec(memory_space=pltpu.SMEM),
            pl.BlockSpec((rows, ATTN_WIDTH), lambda b: (b, P_Q // ATTN_WIDTH)),
            pl.BlockSpec((rows, KV_WIDTH), lambda b: (b, kcol)),
            pl.BlockSpec((rows, KV_WIDTH), lambda b: (b, vcol)),
            pl.BlockSpec((n_seq, KV_WIDTH, WINDOW), lambda b: (b, 0, 0)),
            pl.BlockSpec((n_seq, KV_WIDTH, WINDOW), lambda b: (b, 0, 0)),
        ],
        out_specs=(
            pl.BlockSpec((rows, ATTN_WIDTH), lambda b: (b, 0)),
            pl.BlockSpec((n_seq, KV_WIDTH, WINDOW), lambda b: (b, 0, 0)),
            pl.BlockSpec((n_seq, KV_WIDTH, WINDOW), lambda b: (b, 0, 0)),
        ),
        compiler_params=pltpu.CompilerParams(dimension_semantics=("arbitrary",)),
        name="attn_sample",
    )(sinks, proj, proj, proj, cache_kt, cache_vt)


def _finish_window(moved_t, new, batch, seq):
    new_t = jnp.transpose(new.reshape(batch, seq, KV_WIDTH), (0, 2, 1))
    win_t = lax.dynamic_update_slice(moved_t, new_t, (0, 0, WINDOW - seq))
    return jnp.transpose(win_t.reshape(batch, N_KV_HEADS, HEAD_DIM, WINDOW), (0, 3, 1, 2))[None]


def _seg_sum(x, bones):
    rows = x.shape[0]
    hi = x.astype(BF16).astype(F32)
    lo = x - hi
    groups = [slice(j * SEG, (j + 1) * SEG) for j in range(R_WIDTH // SEG)]
    lhs = jnp.concatenate([t[:, sl] for sl in groups for t in (hi, lo)], axis=0).astype(BF16)
    out = _mm(lhs, bones)
    return jnp.concatenate(
        [out[2 * j * rows:(2 * j + 1) * rows] + out[(2 * j + 1) * rows:(2 * j + 2) * rows]
         for j in range(len(groups))], axis=1)


def _wkv_kernel(pa_ref, pb_ref, pl_ref, prev_ref, s0_ref,
                mu_ref, mul_ref, w0_ref, a0_ref, kk_ref, ka_ref, rk_ref, lnw_ref, lnb_ref,
                wl_ref, bones_ref, *rest, chunk, n_sub, n_seq, n_steps, cast_transposed):
    n_cast = len(cast_transposed)
    cast_src, (y_ref, so_ref) = rest[:n_cast], rest[n_cast:n_cast + 2]
    cast_dst, (s_ref, carry_ref) = rest[n_cast + 2:2 * n_cast + 2], rest[2 * n_cast + 2:]
    for src, dst, transposed in zip(cast_src, cast_dst, cast_transposed):
        dst[...] = (src[...].T if transposed else src[...]).astype(BF16)

    step = pl.program_id(1)
    cs = chunk
    seq_rows = n_sub * cs
    rows = n_seq * seq_rows
    gc = 2 * cs
    hd = R_HEAD
    w = R_WIDTH
    seqs = range(n_seq)

    @pl.when(step == 0)
    def _init():
        s_ref[...] = jnp.zeros(s_ref.shape, F32)
        for q in seqs:
            carry_ref[q:q + 1, :] = prev_ref[q]
            for p in range(N_PAIRS):
                s_ref[q * N_PAIRS + p, 0:hd, 0:hd] = s0_ref[q, 2 * p]
                s_ref[q * N_PAIRS + p, hd:2 * hd, hd:2 * hd] = s0_ref[q, 2 * p + 1]

    row = lax.broadcasted_iota(jnp.int32, (rows, 1), 0)

    def token_shift(x, lo, hi, mu):
        shifted = pltpu.roll(x, 1, axis=0)
        for q in seqs:
            shifted = jnp.where(row == q * seq_rows, carry_ref[q:q + 1, lo:hi], shifted)
        return x + (shifted - x) * mu

    cols = jnp.concatenate([pa_ref[...], pb_ref[...]], axis=1)
    l_raw = pl_ref[...]
    xx = token_shift(cols, 0, 3 * w, mu_ref[...])
    xr, xk, xv = xx[:, 0:w], xx[:, w:2 * w], xx[:, 2 * w:3 * w]
    xl = token_shift(l_raw, 3 * w, SHIFT_COLS, mul_ref[...])
    for q in seqs:
        last = (q + 1) * seq_rows
        carry_ref[q:q + 1, 0:3 * w] = cols[last - 1:last, :]
        carry_ref[q:q + 1, 3 * w:] = l_raw[last - 1:last, :]

    lane_l = lax.broadcasted_iota(jnp.int32, (1, LORA_PAD), 1)
    act = jnp.where(lane_l < DECAY_LORA, jnp.tanh(xl),
                    jnp.where(lane_l < DECAY_LORA + AAA_LORA, xl,
                              jnp.where(lane_l < LORA_COLS, jax.nn.sigmoid(xl), 0.0)))
    up = _mm(act.astype(BF16), wl_ref[...])
    z = -(w0_ref[...] + up[:, 0:w])
    softplus = jnp.maximum(z, 0.0) + jnp.log1p(jnp.exp(-jnp.abs(z)))
    lwd = -jnp.exp(-softplus - 0.5)
    a = jax.nn.sigmoid(a0_ref[...] + up[:, w:2 * w])
    gate = up[:, 2 * w:3 * w]

    bones = bones_ref[...]
    kkn = xk * kk_ref[...]
    kk = kkn * lax.rsqrt(jnp.maximum(_seg_sum(kkn * kkn, bones), 1e-24))
    k2 = xk * (1.0 + (a - 1.0) * ka_ref[...])
    b = kk * a

    tri = (lax.broadcasted_iota(jnp.int32, (cs, cs), 0)
           >= lax.broadcasted_iota(jnp.int32, (cs, cs), 1)).astype(BF16)
    ri = lax.broadcasted_iota(jnp.int32, (gc, gc), 0)
    ci = lax.broadcasted_iota(jnp.int32, (gc, gc), 1)
    same_head = (ri >= cs) == (ci >= cs)
    strict = same_head & (ci < ri)
    incl = same_head & (ci <= ri)
    eye = (ri == ci).astype(F32)
    head0 = lax.broadcasted_iota(jnp.int32, (1, PAIR), 1) < hd

    def stack(x):
        return jnp.concatenate([jnp.where(head0, x, 0.0), jnp.where(head0, 0.0, x)], axis=0).astype(BF16)

    pairs = range(N_PAIRS)
    lanes = [slice(p * PAIR, (p + 1) * PAIR) for p in pairs]
    squarings = cs.bit_length() - 2

    chunks = []
    for sub in range(n_seq * n_sub):
        rs = slice(sub * cs, (sub + 1) * cs)
        lw_c = lwd[rs]
        lw_hi = lw_c.astype(BF16)
        lw_r = lw_c - lw_hi.astype(F32)
        lw_mid = lw_r.astype(BF16)
        cum = _mm(tri, lw_hi) + _mm(tri, lw_mid) + _mm(tri, (lw_r - lw_mid.astype(F32)).astype(BF16))
        cum_last = cum[cs - 1:cs, :]
        e_inv = jnp.exp(-cum)
        e_last = jnp.exp(cum_last - cum)
        kq = kk[rs] * jnp.exp(cum - lw_c)
        rq = xr[rs] * jnp.exp(cum)
        kd = k2[rs] * e_inv
        bd = b[rs] * e_inv
        kdp = k2[rs] * e_last
        bdp = b[rs] * e_last
        xv_c = xv[rs]
        kq_s = [stack(kq[:, sl]) for sl in lanes]
        bd_s = [stack(bd[:, sl]) for sl in lanes]
        kd_s = [stack(kd[:, sl]) for sl in lanes]
        rq_s = [stack(rq[:, sl]) for sl in lanes]
        probes = [jnp.concatenate([kq_s[p], rq_s[p]], axis=0) for p in pairs]
        if gc % PAIR == 0:
            a_all = [_nt(probes[p], jnp.concatenate([bd_s[p], kd_s[p]], axis=0)) for p in pairs]
            a_kb = [jnp.where(strict, t[0:gc, 0:gc], 0.0) for t in a_all]
            a_kk = [jnp.where(strict, t[0:gc, gc:2 * gc], 0.0).astype(BF16) for t in a_all]
            a_rb = [jnp.where(incl, t[gc:2 * gc, 0:gc], 0.0).astype(BF16) for t in a_all]
            a_rk = [jnp.where(incl, t[gc:2 * gc, gc:2 * gc], 0.0).astype(BF16) for t in a_all]
        else:
            a_kb = [jnp.where(strict, _nt(kq_s[p], bd_s[p]), 0.0) for p in pairs]
            a_kk = [jnp.where(strict, _nt(kq_s[p], kd_s[p]), 0.0).astype(BF16) for p in pairs]
            a_rb = [jnp.where(incl, _nt(rq_s[p], bd_s[p]), 0.0).astype(BF16) for p in pairs]
            a_rk = [jnp.where(incl, _nt(rq_s[p], kd_s[p]), 0.0).astype(BF16) for p in pairs]
        inv = [eye - t for t in a_kb]
        if squarings >= 1:
            apow_b = [t.astype(BF16) for t in a_kb]
            apow_b = [_mm(t, t).astype(BF16) for t in apow_b]
            for _ in range(squarings - 1):
                both = [_mm(jnp.concatenate([apow_b[p], inv[p].astype(BF16)], axis=0), apow_b[p]) for p in pairs]
                apow_b = [t[0:gc].astype(BF16) for t in both]
                inv = [inv[p] + both[p][gc:2 * gc] for p in pairs]
            inv = [inv[p] + _mm(inv[p].astype(BF16), apow_b[p]) for p in pairs]
        chunks.append(dict(
            probes=probes, a_kk=a_kk, a_rk=a_rk, a_rb=a_rb, inv=[t.astype(BF16) for t in inv],
            v_s=[stack(xv_c[:, sl]) for sl in lanes],
            upd=[jnp.concatenate([stack(kdp[:, sl]), stack(bdp[:, sl])], axis=0) for sl in lanes],
            p_last=jnp.exp(cum_last)))

    sp = [(q, p) for q in seqs for p in pairs]
    state = {(q, p): s_ref[q * N_PAIRS + p] for q, p in sp}
    y_rows = [None] * (n_seq * n_sub)
    for level in range(n_sub):
        ch = {q: chunks[q * n_sub + level] for q in seqs}
        s_b = {k: state[k].astype(BF16) for k in sp}
        state_t = {(q, p): _nt(ch[q]["probes"][p], s_b[q, p]) for q, p in sp}
        rhs = {(q, p): state_t[q, p][0:gc] + _mm(ch[q]["a_kk"][p], ch[q]["v_s"][p]) for q, p in sp}
        u_b = {(q, p): _mm(ch[q]["inv"][p], rhs[q, p].astype(BF16)).astype(BF16) for q, p in sp}
        if gc % PAIR == 0:
            y_s = {(q, p): state_t[q, p][gc:2 * gc]
                   + _mm(jnp.concatenate([ch[q]["a_rk"][p], -ch[q]["a_rb"][p]], axis=1),
                         jnp.concatenate([ch[q]["v_s"][p], u_b[q, p]], axis=0)) for q, p in sp}
        else:
            y_s = {(q, p): state_t[q, p][gc:2 * gc] + _mm(ch[q]["a_rk"][p], ch[q]["v_s"][p])
                   - _mm(ch[q]["a_rb"][p], u_b[q, p]) for q, p in sp}
        state = {(q, p): state[q, p] * ch[q]["p_last"][:, lanes[p]]
                 + _tn(jnp.concatenate([ch[q]["v_s"][p], -u_b[q, p]], axis=0), ch[q]["upd"][p]) for q, p in sp}
        for q in seqs:
            y_rows[q * n_sub + level] = jnp.concatenate(
                [y_s[q, p][0:cs] + y_s[q, p][cs:gc] for p in pairs], axis=1)
    for q, p in sp:
        s_ref[q * N_PAIRS + p] = state[q, p]

    y = jnp.concatenate(y_rows, axis=0) if len(y_rows) > 1 else y_rows[0]
    mean = _seg_sum(y, bones) * (1.0 / hd)
    d = y - mean
    var = _seg_sum(d * d, bones) * (1.0 / hd)
    yn = d * lax.rsqrt(var + GN_EPS) * lnw_ref[...] + lnb_ref[...]
    bonus = _seg_sum(xr * k2 * rk_ref[...], bones) * xv
    y_ref[...] = (yn + bonus) * gate

    @pl.when(step == n_steps - 1)
    def _fin():
        for q, p in sp:
            so_ref[q, 2 * p] = s_ref[q * N_PAIRS + p, 0:hd, 0:hd]
            so_ref[q, 2 * p + 1] = s_ref[q * N_PAIRS + p, hd:2 * hd, hd:2 * hd]


def _wkv(proj, prev0, s0, params, wl, bones, batch, seq, chunk, n_sub, n_seq=1, casts=()):
    ns = seq // (chunk * n_sub)
    assert n_seq == 1 or ns == 1, "several sequences per step only when a step covers them whole"
    rows = n_seq * chunk * n_sub
    nb = batch // n_seq
    rw = R_WIDTH

    def col(cb):
        return lambda b, c: (b * ns + c, cb)

    const2 = lambda b, c: (0, 0)
    vec = lambda n: pl.BlockSpec((1, n), const2)
    cast_in, cast_out, cast_shapes = [], [], []
    for wgt, span in casts:
        start, count = (0, wgt.shape[0]) if span is None else span
        blk, width = count // (nb * ns), wgt.shape[1]
        assert blk * nb * ns == count, (wgt.shape, span)
        if span is None:
            assert blk % 16 == 0, blk
            cast_in.append(pl.BlockSpec((blk, width), col(0)))
            cast_out.append(pl.BlockSpec((blk, width), col(0)))
            cast_shapes.append(jax.ShapeDtypeStruct((count, width), BF16))
        else:
            assert blk % 128 == 0 and start % 8 == 0, span
            cast_in.append(pl.BlockSpec(
                (pl.Element(blk), pl.Element(width)),
                lambda b, c, start=start, blk=blk: (pl.multiple_of(start + blk * (b * ns + c), 8), 0)))
            cast_out.append(pl.BlockSpec((width, blk), lambda b, c: (0, b * ns + c)))
            cast_shapes.append(jax.ShapeDtypeStruct((width, count), BF16))
    mu, mul, w0, a0, k_k, k_a, r_k, ln_w, ln_b = params
    outs = pl.pallas_call(
        functools.partial(_wkv_kernel, chunk=chunk, n_sub=n_sub, n_seq=n_seq, n_steps=ns,
                          cast_transposed=tuple(span is not None for _, span in casts)),
        out_shape=(jax.ShapeDtypeStruct((batch * seq, rw), F32),
                   jax.ShapeDtypeStruct((batch, R_HEADS, R_HEAD, R_HEAD), F32), *cast_shapes),
        grid=(nb, ns),
        in_specs=[
            pl.BlockSpec((rows, C_BLK), col(P_C // C_BLK)),
            pl.BlockSpec((rows, C_BLK), col(P_C // C_BLK + 1)),
            pl.BlockSpec((rows, LORA_PAD), col(P_L // LORA_PAD)),
            pl.BlockSpec((n_seq, 1, SHIFT_COLS), lambda b, c: (b, 0, 0)),
            pl.BlockSpec((n_seq, R_HEADS, R_HEAD, R_HEAD), lambda b, c: (b, 0, 0, 0)),
            vec(3 * rw), vec(LORA_PAD), vec(rw), vec(rw), vec(rw), vec(rw), vec(rw), vec(rw), vec(rw),
            pl.BlockSpec((LORA_PAD, 3 * rw), const2),
            pl.BlockSpec((SEG, SEG), const2),
            *cast_in,
        ],
        out_specs=(
            pl.BlockSpec((rows, rw), lambda b, c: (b * ns + c, 0)),
            pl.BlockSpec((n_seq, R_HEADS, R_HEAD, R_HEAD), lambda b, c: (b, 0, 0, 0)),
            *cast_out,
        ),
        scratch_shapes=[pltpu.VMEM((n_seq * N_PAIRS, PAIR, PAIR), F32), pltpu.VMEM((n_seq, SHIFT_COLS), F32)],
        compiler_params=pltpu.CompilerParams(
            dimension_semantics=("arbitrary", "arbitrary"), vmem_limit_bytes=VMEM_LIMIT),
        name="wkv",
    )(proj, proj, proj, prev0, s0, mu, mul, w0, a0, k_k, k_a, r_k, ln_w, ln_b, wl, bones,
      *[wgt for wgt, _ in casts])
    return outs[0], outs[1], outs[2:]


def _mix_kernel(x_ref, ya_ref, yr_ref, g_ref, wg_ref, wba_ref, wbr_ref, wo_ref, o_ref):
    x = x_ref[...]
    h = _rms(x, g_ref[...]).astype(BF16)
    ya = ya_ref[...].astype(BF16)
    yr = yr_ref[...].astype(BF16)
    acc = x
    for c in range(D_MODEL // MIX_CHUNK):
        ca = slice(c * MIX_CHUNK, (c + 1) * MIX_CHUNK)
        cr = slice(D_MODEL + c * MIX_CHUNK, D_MODEL + (c + 1) * MIX_CHUNK)
        mixed = (jax.nn.sigmoid(_mm(h, wg_ref[:, ca])) * _mm(ya, wba_ref[:, ca])
                 + jax.nn.sigmoid(_mm(h, wg_ref[:, cr])) * _mm(yr, wbr_ref[:, ca]))
        acc = acc + _mm(mixed.astype(BF16), wo_ref[ca, :])
    o_ref[...] = acc


def _resident(shape):
    return pl.BlockSpec(shape, lambda *_: (0,) * len(shape), pipeline_mode=pl.Buffered(1))


def _mix(x, ya, yr, g, wg, wba, wbr, wo, tm):
    m = x.shape[0]
    row = lambda i: (i, 0)
    return pl.pallas_call(
        _mix_kernel,
        out_shape=jax.ShapeDtypeStruct((m, D_MODEL), F32),
        grid=(m // tm,),
        in_specs=[
            pl.BlockSpec((tm, D_MODEL), row),
            pl.BlockSpec((tm, ATTN_WIDTH), row),
            pl.BlockSpec((tm, R_WIDTH), row),
            pl.BlockSpec((1, D_MODEL), lambda i: (0, 0)),
            _resident((D_MODEL, 2 * D_MODEL)),
            _resident((ATTN_WIDTH, D_MODEL)),
            _resident((R_WIDTH, D_MODEL)),
            _resident((D_MODEL, D_MODEL)),
        ],
        out_specs=pl.BlockSpec((tm, D_MODEL), row),
        compiler_params=pltpu.CompilerParams(
            dimension_semantics=("arbitrary",), vmem_limit_bytes=VMEM_LIMIT),
        name="mix",
    )(x, ya, yr, g, wg, wba, wbr, wo)


def _ffn_kernel(x_ref, g_ref, wu_ref, wd_ref, o_ref, h_ref):
    @pl.when(pl.program_id(1) == 0)
    def _():
        x = x_ref[...]
        h_ref[...] = _rms(x, g_ref[...]).astype(BF16)
        o_ref[...] = x

    u = _mm(h_ref[...], wu_ref[...])
    o_ref[...] += _mm(jnp.square(jnp.maximum(u, 0.0)).astype(BF16), wd_ref[...])


def _ffn(x, g, wu, wd, tm, tk):
    m = x.shape[0]
    return pl.pallas_call(
        _ffn_kernel,
        out_shape=jax.ShapeDtypeStruct((m, D_MODEL), F32),
        grid=(m // tm, D_FF // tk),
        in_specs=[
            pl.BlockSpec((tm, D_MODEL), lambda i, k: (i, 0)),
            pl.BlockSpec((1, D_MODEL), lambda i, k: (0, 0)),
            pl.BlockSpec((D_MODEL, tk), lambda i, k: (0, k)),
            pl.BlockSpec((tk, D_MODEL), lambda i, k: (k, 0)),
        ],
        out_specs=pl.BlockSpec((tm, D_MODEL), lambda i, k: (i, 0)),
        scratch_shapes=[pltpu.VMEM((tm, D_MODEL), BF16)],
        compiler_params=pltpu.CompilerParams(
            dimension_semantics=("arbitrary", "arbitrary"), vmem_limit_bytes=VMEM_LIMIT),
        name="ffn",
    )(x, g, wu, wd)


def _ple_kernel(x_ref, pe_ref, wg_ref, wp_ref, g_ref, o_ref):
    x = x_ref[...]
    gate = jax.nn.sigmoid(_mm(x.astype(BF16), wg_ref[...]))
    x = x + gate * _mm(pe_ref[...].astype(BF16), wp_ref[...])
    o_ref[...] = _rms(x, g_ref[...])


def _ple(x, pe, wg, wp, g, tm):
    m = x.shape[0]
    row = lambda i: (i, 0)
    return pl.pallas_call(
        _ple_kernel,
        out_shape=jax.ShapeDtypeStruct((m, D_MODEL), F32),
        grid=(m // tm,),
        in_specs=[
            pl.BlockSpec((tm, D_MODEL), row),
            pl.BlockSpec((tm, PLE_DIM), row),
            _resident((D_MODEL, D_MODEL)),
            _resident((PLE_DIM, D_MODEL)),
            pl.BlockSpec((1, D_MODEL), lambda i: (0, 0)),
        ],
        out_specs=pl.BlockSpec((tm, D_MODEL), row),
        compiler_params=pltpu.CompilerParams(
            dimension_semantics=("arbitrary",), vmem_limit_bytes=VMEM_LIMIT),
        name="ple",
    )(x, pe, wg, wp, g)


def _dense_tail(x, ya, yr, pe, wts, tiles):
    (g_mix, w_gates, wba, wbr, wo, g_ffn, wu, wd, wg, wp, g_fin) = wts
    x = _mix(x, ya, yr, g_mix, w_gates, wba, wbr, wo, tiles["mix_m"])
    x = _ffn(x, g_ffn, wu, wd, tiles["ffn_m"], tiles["ffn_k"])
    return _ple(x, pe, wg, wp, g_fin, tiles["ple_m"])


def _shift_out(proj, batch, seq):
    last = proj.reshape(batch, seq, P_COLS)[:, -1]
    return last[:, P_C:P_C + RWKV_COLS][None]


def kernel(x_prompt, x_sample, cache_k_win, cache_v_win, state_wkv, state_shift, p_prompt, p_sample,
           norm_mix, w_in, attn_sinks, rwkv_mu, rwkv_w0, rwkv_w2, rwkv_a0, rwkv_a2, rwkv_g2,
           rwkv_k_k, rwkv_k_a, rwkv_r_k, rwkv_ln_w, rwkv_ln_b, w_branch_attn, w_branch_rwkv,
           w_out, norm_ffn, w_ff_up, w_ff_down, w_ple_proj, w_ple_gate, norm_final):
    assert w_in.shape[0] == 1, "single-layer step"
    bp, tp = x_prompt.shape[0], x_prompt.shape[1]
    bs, ts = x_sample.shape[0], x_sample.shape[1]
    rw = R_WIDTH

    wl = jnp.zeros((LORA_PAD, 3 * rw), F32)
    wl = wl.at[0:DECAY_LORA, 0:rw].set(rwkv_w2[0])
    wl = wl.at[DECAY_LORA:DECAY_LORA + AAA_LORA, rw:2 * rw].set(rwkv_a2[0])
    wl = wl.at[DECAY_LORA + AAA_LORA:LORA_COLS, 2 * rw:3 * rw].set(rwkv_g2[0])
    wl = wl.astype(BF16)
    seg_id = np.arange(SEG) // R_HEAD
    bones = jnp.asarray(seg_id[:, None] == seg_id[None, :], BF16)
    mu = rwkv_mu[0]
    row = lambda v: v.reshape(1, -1)
    wkv_params = (row(mu[:3 * rw]), row(jnp.pad(mu[3 * rw:], (0, LORA_PAD - LORA_COLS))),
                  row(rwkv_w0[0]), row(rwkv_a0[0]), row(rwkv_k_k[0]), row(rwkv_k_a[0]),
                  row(rwkv_r_k[0]), row(rwkv_ln_w[0]), row(rwkv_ln_b[0]))
    g_mix = row(norm_mix[0])
    sinks = attn_sinks[0]

    ms = bs * ts
    xs = x_sample.reshape(ms, D_MODEL)
    w_in_t = jnp.transpose(w_in[0])
    proj_s, w_in_b = _proj_cast(xs, g_mix, w_in_t, 1024)

    tiles_p = dict(mix_m=256, ffn_m=1024, ffn_k=512, ple_m=512)
    xp = x_prompt.reshape(bp * tp, D_MODEL)
    proj_p = _proj(xp, g_mix, w_in_b, 1024, 1024)
    ya_p = _attn_prompt(proj_p, sinks, bp, tp)
    yr_p, s_p, (wu, wd, wo, wg, wba, wbr, w_gates) = _wkv(
        proj_p, jnp.zeros((bp, 1, SHIFT_COLS), F32), jnp.zeros((bp, R_HEADS, R_HEAD, R_HEAD), F32),
        wkv_params, wl, bones, bp, tp, 64, 4,
        casts=((w_ff_up[0], None), (w_ff_down[0], None), (w_out[0], None), (w_ple_gate[0], None),
               (w_branch_attn[0], None), (w_branch_rwkv[0], None), (w_in_t, (P_GATES, 2 * D_MODEL))))
    dense = (g_mix, w_gates, wba, wbr, wo, row(norm_ffn[0]), wu, wd, wg, w_ple_proj[0].astype(BF16),
             row(norm_final))
    yp = _dense_tail(xp, ya_p, yr_p, p_prompt[0].reshape(bp * tp, PLE_DIM), dense, tiles_p)
    pp3 = proj_p.reshape(bp, tp, P_COLS)[:, -WINDOW:]
    k_p = pp3[:, :, P_K:P_K + KV_WIDTH].reshape(1, bp, WINDOW, N_KV_HEADS, HEAD_DIM)
    v_p = pp3[:, :, P_V:P_V + KV_WIDTH].reshape(1, bp, WINDOW, N_KV_HEADS, HEAD_DIM)

    tiles_s = dict(mix_m=ms, ffn_m=ms, ffn_k=2048, ple_m=ms)
    ck_t = jnp.transpose(cache_k_win[0], (0, 2, 3, 1)).reshape(bs, KV_WIDTH, WINDOW)
    cv_t = jnp.transpose(cache_v_win[0], (0, 2, 3, 1)).reshape(bs, KV_WIDTH, WINDOW)
    ya_s, nk_t, nv_t = _attn_sample(proj_s, sinks, ck_t, cv_t, bs, ts, SAMPLE_GROUP)
    prev_s = jnp.pad(state_shift[0], ((0, 0), (0, LORA_PAD - LORA_COLS))).reshape(bs, 1, SHIFT_COLS)
    yr_s, s_s, _ = _wkv(proj_s, prev_s, state_wkv[0], wkv_params, wl, bones, bs, ts, ts, 1, n_seq=SAMPLE_GROUP)
    ys = _dense_tail(xs, ya_s, yr_s, p_sample[0].reshape(ms, PLE_DIM), dense, tiles_s)

    return (yp.reshape(bp, tp, D_MODEL), ys.reshape(bs, ts, D_MODEL),
            k_p, v_p, s_p[None], _shift_out(proj_p, bp, tp),
            _finish_window(nk_t, proj_s[:, P_K:P_K + KV_WIDTH], bs, ts),
            _finish_window(nv_t, proj_s[:, P_V:P_V + KV_WIDTH], bs, ts),
            s_s[None], _shift_out(proj_s, bs, ts))
```

```python
import functools

import numpy as np
import jax
import jax.numpy as jnp
from jax import lax
from jax.experimental import pallas as pl
from jax.experimental.pallas import tpu as pltpu

F32 = jnp.float32
BF16 = jnp.bfloat16

D_MODEL = 2048
PLE_DIM = 256
HEAD_DIM = 64
N_Q_HEADS = 16
N_KV_HEADS = 4
GQA_GROUP = 4
ATTN_WIDTH = 1024
KV_WIDTH = 256
WINDOW = 128
ALIBI_MAX = 8.0
R_HEAD = 64
R_WIDTH = 1024
R_HEADS = 16
DECAY_LORA = 64
AAA_LORA = 64
GATE_LORA = 160
LORA_COLS = DECAY_LORA + AAA_LORA + GATE_LORA
LORA_PAD = 512
RWKV_COLS = 3 * R_WIDTH + LORA_COLS
D_FF = 4 * D_MODEL
NORM_EPS = 1e-6
GN_EPS = 64e-5

P_Q = 0
P_K = ATTN_WIDTH
P_V = ATTN_WIDTH + KV_WIDTH
P_C = ATTN_WIDTH + 2 * KV_WIDTH
P_L = P_C + 3 * R_WIDTH
P_COLS = P_L + LORA_PAD
P_GATES = P_C + RWKV_COLS
C_BLK = P_C
SHIFT_COLS = 3 * R_WIDTH + LORA_PAD

PAIR = 128
N_PAIRS = R_WIDTH // PAIR
SEG = 256
MIX_CHUNK = 512
SAMPLE_GROUP = 8
PROMPT_BLOCKS = 4

VMEM_LIMIT = 56 * 1024 * 1024


def _mm(a, b):
    return jnp.dot(a, b, preferred_element_type=F32)


def _nt(a, b):
    return lax.dot_general(a, b, (((1,), (1,)), ((), ())), preferred_element_type=F32)


def _tn(a, b):
    return lax.dot_general(a, b, (((0,), (0,)), ((), ())), preferred_element_type=F32)


def _rms(x, g):
    ms = jnp.mean(x * x, axis=-1, keepdims=True)
    return x * lax.rsqrt(ms + NORM_EPS) * g


def _alibi_slope(hq):
    return float(2.0 ** (-ALIBI_MAX * (hq + 1) / N_Q_HEADS))


def _proj_kernel(x_ref, g_ref, w_ref, o_ref, h_ref):
    @pl.when(pl.program_id(1) == 0)
    def _():
        h_ref[...] = _rms(x_ref[...], g_ref[...]).astype(BF16)

    o_ref[...] = _nt(h_ref[...], w_ref[...])


def _proj_cast_kernel(x_ref, g_ref, w_ref, o_ref, wb_ref, h_ref):
    @pl.when(pl.program_id(0) == 0)
    def _():
        h_ref[...] = _rms(x_ref[...], g_ref[...]).astype(BF16)

    wb = w_ref[...].astype(BF16)
    wb_ref[...] = wb
    o_ref[...] = _nt(h_ref[...], wb)


def _proj_cast(x, g, wt_f32, tn):
    m = x.shape[0]
    return pl.pallas_call(
        _proj_cast_kernel,
        out_shape=(jax.ShapeDtypeStruct((m, P_COLS), F32), jax.ShapeDtypeStruct((P_COLS, D_MODEL), BF16)),
        grid=(P_COLS // tn,),
        in_specs=[
            pl.BlockSpec((m, D_MODEL), lambda j: (0, 0)),
            pl.BlockSpec((1, D_MODEL), lambda j: (0, 0)),
            pl.BlockSpec((tn, D_MODEL), lambda j: (j, 0)),
        ],
        out_specs=(pl.BlockSpec((m, tn), lambda j: (0, j)), pl.BlockSpec((tn, D_MODEL), lambda j: (j, 0))),
        scratch_shapes=[pltpu.VMEM((m, D_MODEL), BF16)],
        compiler_params=pltpu.CompilerParams(
            dimension_semantics=("arbitrary",), vmem_limit_bytes=VMEM_LIMIT),
        name="proj_cast",
    )(x, g, wt_f32)


def _proj(x, g, w, tm, tn):
    m = x.shape[0]
    return pl.pallas_call(
        _proj_kernel,
        out_shape=jax.ShapeDtypeStruct((m, P_COLS), F32),
        grid=(m // tm, P_COLS // tn),
        in_specs=[
            pl.BlockSpec((tm, D_MODEL), lambda i, j: (i, 0)),
            pl.BlockSpec((1, D_MODEL), lambda i, j: (0, 0)),
            pl.BlockSpec((tn, D_MODEL), lambda i, j: (j, 0)),
        ],
        out_specs=pl.BlockSpec((tm, tn), lambda i, j: (i, j)),
        scratch_shapes=[pltpu.VMEM((tm, D_MODEL), BF16)],
        compiler_params=pltpu.CompilerParams(
            dimension_semantics=("arbitrary", "arbitrary"), vmem_limit_bytes=VMEM_LIMIT),
        name="proj",
    )(x, g, w)


def _attend_heads(scores, apply_values, sinks):
    heads = range(len(scores))
    m = []
    for h in heads:
        mh = sinks[h]
        for s in scores[h]:
            mh = jnp.maximum(mh, jnp.max(s, axis=-1, keepdims=True))
        m.append(mh)
    ps = [[jnp.exp(s - m[h]) for s in scores[h]] for h in heads]
    den = []
    for h in heads:
        dh = jnp.exp(sinks[h] - m[h])
        for p in ps[h]:
            dh = dh + jnp.sum(p, axis=-1, keepdims=True)
        den.append(dh)
    outs = []
    for h in heads:
        o = None
        for p, pv in zip(ps[h], apply_values[h]):
            t = pv(p.astype(BF16))
            o = t if o is None else o + t
        outs.append(o * (1.0 / den[h]))
    return outs


def _head_slices():
    q_sl = [slice(hq * HEAD_DIM, (hq + 1) * HEAD_DIM) for hq in range(N_Q_HEADS)]
    kv_sl = [slice((hq // GQA_GROUP) * HEAD_DIM, (hq // GQA_GROUP + 1) * HEAD_DIM) for hq in range(N_Q_HEADS)]
    return q_sl, kv_sl


def _alibi_bias(dist, valid):
    slopes = np.array([_alibi_slope(hq) for hq in range(N_Q_HEADS)], np.float32)
    return np.where(valid[None], -slopes[:, None, None] * dist[None].astype(np.float32), -np.inf).astype(np.float32)


def _attn_prompt_kernel(sink_ref, bias_ref, q_ref, kp_ref, kc_ref, vp_ref, vc_ref, o_ref):
    w = WINDOW
    n_blk = q_ref.shape[0] // w
    kj = lax.broadcasted_iota(jnp.int32, (w, 2 * w), 1)
    no_prev = (kj < w) & (pl.program_id(1) == 0)
    q = q_ref[...] * (HEAD_DIM ** -0.5)
    kc, vc = kc_ref[...].astype(BF16), vc_ref[...].astype(BF16)
    ks = jnp.concatenate([kp_ref[...].astype(BF16), kc], axis=0)
    vs = jnp.concatenate([vp_ref[...].astype(BF16), vc], axis=0)
    hqs = range(N_Q_HEADS)
    q_sl, kv_sl = _head_slices()
    scores, values = [], []
    for s in range(n_blk):
        k2, v2 = ks[s * w:(s + 2) * w], vs[s * w:(s + 2) * w]
        qb = q[s * w:(s + 1) * w]
        for hq in hqs:
            sc = _nt(qb[:, q_sl[hq]].astype(BF16), k2[:, kv_sl[hq]]) + bias_ref[hq]
            scores.append([jnp.where(no_prev, -jnp.inf, sc) if s == 0 else sc])
            values.append([functools.partial(_mm, b=v2[:, kv_sl[hq]])])
    outs = _attend_heads(scores, values, [sink_ref[hq] for _ in range(n_blk) for hq in hqs])
    for s in range(n_blk):
        for hq in hqs:
            o_ref[s * w:(s + 1) * w, q_sl[hq]] = outs[s * N_Q_HEADS + hq]


def _attn_prompt(proj, sinks, batch, seq):
    rows = PROMPT_BLOCKS * WINDOW
    ns = seq // rows
    kcol, vcol = P_K // KV_WIDTH, P_V // KV_WIDTH

    def cur(c):
        return lambda b, i: (b * ns + i, c)

    def prev(c):
        return lambda b, i: (b * ns * PROMPT_BLOCKS + jnp.maximum(i * PROMPT_BLOCKS - 1, 0), c)

    ti = np.arange(WINDOW)[:, None]
    kj = np.arange(2 * WINDOW)[None, :]
    dist = ti - kj + WINDOW
    bias = _alibi_bias(dist, (dist >= 0) & (dist <= WINDOW))
    return pl.pallas_call(
        _attn_prompt_kernel,
        out_shape=jax.ShapeDtypeStruct((batch * seq, ATTN_WIDTH), F32),
        grid=(batch, ns),
        in_specs=[
            pl.BlockSpec(memory_space=pltpu.SMEM),
            _resident((N_Q_HEADS, WINDOW, 2 * WINDOW)),
            pl.BlockSpec((rows, ATTN_WIDTH), cur(P_Q // ATTN_WIDTH)),
            pl.BlockSpec((WINDOW, KV_WIDTH), prev(kcol)),
            pl.BlockSpec((rows, KV_WIDTH), cur(kcol)),
            pl.BlockSpec((WINDOW, KV_WIDTH), prev(vcol)),
            pl.BlockSpec((rows, KV_WIDTH), cur(vcol)),
        ],
        out_specs=pl.BlockSpec((rows, ATTN_WIDTH), lambda b, i: (b * ns + i, 0)),
        compiler_params=pltpu.CompilerParams(dimension_semantics=("arbitrary", "arbitrary")),
        name="attn_prompt",
    )(sinks, jnp.asarray(bias), proj, proj, proj, proj, proj)


def _attn_sample_kernel(sink_ref, q_ref, kn_ref, vn_ref, ckt_ref, cvt_ref, o_ref, nkt_ref, nvt_ref, *, seq):
    t, w = seq, WINDOW
    n_seq = q_ref.shape[0] // t
    ti = lax.broadcasted_iota(jnp.int32, (t, w), 0)
    cj = lax.broadcasted_iota(jnp.int32, (t, w), 1)
    dist_c = (ti - cj + w).astype(F32)
    valid_c = cj >= ti
    ti2 = lax.broadcasted_iota(jnp.int32, (t, t), 0)
    tj2 = lax.broadcasted_iota(jnp.int32, (t, t), 1)
    dist_n = (ti2 - tj2).astype(F32)
    valid_n = tj2 <= ti2
    hqs = range(N_Q_HEADS)
    q_sl, kv_sl = _head_slices()
    bias_c = [jnp.where(valid_c, -_alibi_slope(hq) * dist_c, -jnp.inf) for hq in hqs]
    bias_n = [jnp.where(valid_n, -_alibi_slope(hq) * dist_n, -jnp.inf) for hq in hqs]
    q_all = q_ref[...] * (HEAD_DIM ** -0.5)
    is_new = lax.broadcasted_iota(jnp.int32, (1, w), 1) >= w - t
    put = (cj == ti + (w - t)).astype(BF16)

    def place_new(x):
        hi = x.astype(BF16)
        rem = x - hi.astype(F32)
        mid = rem.astype(BF16)
        return _tn(hi, put) + _tn(mid, put) + _tn((rem - mid.astype(F32)).astype(BF16), put)

    scores, values = [], []
    for s in range(n_seq):
        rs = slice(s * t, (s + 1) * t)
        ckt, cvt = ckt_ref[s], cvt_ref[s]
        nkt_ref[s] = jnp.where(is_new, place_new(kn_ref[rs, :]), pltpu.roll(ckt, w - t, axis=1))
        nvt_ref[s] = jnp.where(is_new, place_new(vn_ref[rs, :]), pltpu.roll(cvt, w - t, axis=1))
        cktb, cvtb = ckt.astype(BF16), cvt.astype(BF16)
        knb, vnb = kn_ref[rs, :].astype(BF16), vn_ref[rs, :].astype(BF16)
        q = q_all[rs]
        qs = [q[:, q_sl[hq]].astype(BF16) for hq in hqs]
        scores += [[_mm(qs[hq], cktb[kv_sl[hq], :]) + bias_c[hq], _nt(qs[hq], knb[:, kv_sl[hq]]) + bias_n[hq]]
                   for hq in hqs]
        values += [[functools.partial(_nt, b=cvtb[kv_sl[hq], :]), functools.partial(_mm, b=vnb[:, kv_sl[hq]])]
                   for hq in hqs]
    outs = _attend_heads(scores, values, [sink_ref[hq] for _ in range(n_seq) for hq in hqs])
    for s in range(n_seq):
        for hq in hqs:
            o_ref[s * t:(s + 1) * t, q_sl[hq]] = outs[s * N_Q_HEADS + hq]


def _attn_sample(proj, sinks, cache_kt, cache_vt, batch, seq, n_seq):
    kcol, vcol = P_K // KV_WIDTH, P_V // KV_WIDTH
    rows = n_seq * seq
    win = jax.ShapeDtypeStruct((batch, KV_WIDTH, WINDOW), F32)
    return pl.pallas_call(
        functools.partial(_attn_sample_kernel, seq=seq),
        out_shape=(jax.ShapeDtypeStruct((batch * seq, ATTN_WIDTH), F32), win, win),
        grid=(batch // n_seq,),
        in_specs=[
            pl.BlockSpec(memory_space=pltpu.SMEM),
            pl.BlockSpec((rows, ATTN_WIDTH), lambda b: (b, P_Q // ATTN_WIDTH)),
            pl.BlockSpec((rows, KV_WIDTH), lambda b: (b, kcol)),
            pl.BlockSpec((rows, KV_WIDTH), lambda b: (b, vcol)),
            pl.BlockSpec((n_seq, KV_WIDTH, WINDOW), lambda b: (b, 0, 0)),
            pl.BlockSpec((n_seq, KV_WIDTH, WINDOW), lambda b: (b, 0, 0)),
        ],
        out_specs=(
            pl.BlockSpec((rows, ATTN_WIDTH), lambda b: (b, 0)),
            pl.BlockSpec((n_seq, KV_WIDTH, WINDOW), lambda b: (b, 0, 0)),
            pl.BlockSpec((n_seq, KV_WIDTH, WINDOW), lambda b: (b, 0, 0)),
        ),
        compiler_params=pltpu.CompilerParams(dimension_semantics=("arbitrary",)),
        name="attn_sample",
    )(sinks, proj, proj, proj, cache_kt, cache_vt)


def _seg_sum(x, bones):
    rows = x.shape[0]
    hi = x.astype(BF16).astype(F32)
    lo = x - hi
    groups = [slice(j * SEG, (j + 1) * SEG) for j in range(R_WIDTH // SEG)]
    lhs = jnp.concatenate([t[:, sl] for sl in groups for t in (hi, lo)], axis=0).astype(BF16)
    out = _mm(lhs, bones)
    return jnp.concatenate(
        [out[2 * j * rows:(2 * j + 1) * rows] + out[(2 * j + 1) * rows:(2 * j + 2) * rows]
         for j in range(len(groups))], axis=1)


def _wkv_kernel(pa_ref, pb_ref, pl_ref, prev_ref, s0_ref,
                mu_ref, mul_ref, w0_ref, a0_ref, kk_ref, ka_ref, rk_ref, lnw_ref, lnb_ref,
                wl_ref, bones_ref, *rest, chunk, n_sub, n_seq, n_steps, cast_transposed):
    n_cast = len(cast_transposed)
    cast_src, (y_ref, so_ref) = rest[:n_cast], rest[n_cast:n_cast + 2]
    cast_dst, (s_ref, carry_ref) = rest[n_cast + 2:2 * n_cast + 2], rest[2 * n_cast + 2:]
    for src, dst, transposed in zip(cast_src, cast_dst, cast_transposed):
        dst[...] = (src[...].T if transposed else src[...]).astype(BF16)

    step = pl.program_id(1)
    cs = chunk
    seq_rows = n_sub * cs
    rows = n_seq * seq_rows
    gc = 2 * cs
    hd = R_HEAD
    w = R_WIDTH
    seqs = range(n_seq)

    @pl.when(step == 0)
    def _init():
        s_ref[...] = jnp.zeros(s_ref.shape, F32)
        for q in seqs:
            carry_ref[q:q + 1, :] = prev_ref[q]
            for p in range(N_PAIRS):
                s_ref[q * N_PAIRS + p, 0:hd, 0:hd] = s0_ref[q, 2 * p]
                s_ref[q * N_PAIRS + p, hd:2 * hd, hd:2 * hd] = s0_ref[q, 2 * p + 1]

    row = lax.broadcasted_iota(jnp.int32, (rows, 1), 0)

    def token_shift(x, lo, hi, mu):
        shifted = pltpu.roll(x, 1, axis=0)
        for q in seqs:
            shifted = jnp.where(row == q * seq_rows, carry_ref[q:q + 1, lo:hi], shifted)
        return x + (shifted - x) * mu

    cols = jnp.concatenate([pa_ref[...], pb_ref[...]], axis=1)
    l_raw = pl_ref[...]
    xx = token_shift(cols, 0, 3 * w, mu_ref[...])
    xr, xk, xv = xx[:, 0:w], xx[:, w:2 * w], xx[:, 2 * w:3 * w]
    xl = token_shift(l_raw, 3 * w, SHIFT_COLS, mul_ref[...])
    for q in seqs:
        last = (q + 1) * seq_rows
        carry_ref[q:q + 1, 0:3 * w] = cols[last - 1:last, :]
        carry_ref[q:q + 1, 3 * w:] = l_raw[last - 1:last, :]

    lane_l = lax.broadcasted_iota(jnp.int32, (1, LORA_PAD), 1)
    act = jnp.where(lane_l < DECAY_LORA, jnp.tanh(xl),
                    jnp.where(lane_l < DECAY_LORA + AAA_LORA, xl,
                              jnp.where(lane_l < LORA_COLS, jax.nn.sigmoid(xl), 0.0)))
    up = _mm(act.astype(BF16), wl_ref[...])
    z = -(w0_ref[...] + up[:, 0:w])
    softplus = jnp.maximum(z, 0.0) + jnp.log1p(jnp.exp(-jnp.abs(z)))
    lwd = -jnp.exp(-softplus - 0.5)
    a = jax.nn.sigmoid(a0_ref[...] + up[:, w:2 * w])
    gate = up[:, 2 * w:3 * w]

    bones = bones_ref[...]
    kkn = xk * kk_ref[...]
    kk = kkn * lax.rsqrt(jnp.maximum(_seg_sum(kkn * kkn, bones), 1e-24))
    k2 = xk * (1.0 + (a - 1.0) * ka_ref[...])
    b = kk * a

    tri = (lax.broadcasted_iota(jnp.int32, (cs, cs), 0)
           >= lax.broadcasted_iota(jnp.int32, (cs, cs), 1)).astype(BF16)
    ri = lax.broadcasted_iota(jnp.int32, (gc, gc), 0)
    ci = lax.broadcasted_iota(jnp.int32, (gc, gc), 1)
    same_head = (ri >= cs) == (ci >= cs)
    strict = same_head & (ci < ri)
    incl = same_head & (ci <= ri)
    eye = (ri == ci).astype(F32)
    head0 = lax.broadcasted_iota(jnp.int32, (1, PAIR), 1) < hd

    def stack(x):
        return jnp.concatenate([jnp.where(head0, x, 0.0), jnp.where(head0, 0.0, x)], axis=0).astype(BF16)

    pairs = range(N_PAIRS)
    lanes = [slice(p * PAIR, (p + 1) * PAIR) for p in pairs]
    squarings = cs.bit_length() - 2

    chunks = []
    for sub in range(n_seq * n_sub):
        rs = slice(sub * cs, (sub + 1) * cs)
        lw_c = lwd[rs]
        lw_hi = lw_c.astype(BF16)
        lw_r = lw_c - lw_hi.astype(F32)
        lw_mid = lw_r.astype(BF16)
        cum = _mm(tri, lw_hi) + _mm(tri, lw_mid) + _mm(tri, (lw_r - lw_mid.astype(F32)).astype(BF16))
        cum_last = cum[cs - 1:cs, :]
        e_inv = jnp.exp(-cum)
        e_last = jnp.exp(cum_last - cum)
        kq = kk[rs] * jnp.exp(cum - lw_c)
        rq = xr[rs] * jnp.exp(cum)
        kd = k2[rs] * e_inv
        bd = b[rs] * e_inv
        kdp = k2[rs] * e_last
        bdp = b[rs] * e_last
        xv_c = xv[rs]
        kq_s = [stack(kq[:, sl]) for sl in lanes]
        bd_s = [stack(bd[:, sl]) for sl in lanes]
        kd_s = [stack(kd[:, sl]) for sl in lanes]
        rq_s = [stack(rq[:, sl]) for sl in lanes]
        probes = [jnp.concatenate([kq_s[p], rq_s[p]], axis=0) for p in pairs]
        if gc % PAIR == 0:
            a_all = [_nt(probes[p], jnp.concatenate([bd_s[p], kd_s[p]], axis=0)) for p in pairs]
            a_kb = [jnp.where(strict, t[0:gc, 0:gc], 0.0) for t in a_all]
            a_kk = [jnp.where(strict, t[0:gc, gc:2 * gc], 0.0).astype(BF16) for t in a_all]
            a_rb = [jnp.where(incl, t[gc:2 * gc, 0:gc], 0.0).astype(BF16) for t in a_all]
            a_rk = [jnp.where(incl, t[gc:2 * gc, gc:2 * gc], 0.0).astype(BF16) for t in a_all]
        else:
            a_kb = [jnp.where(strict, _nt(kq_s[p], bd_s[p]), 0.0) for p in pairs]
            a_kk = [jnp.where(strict, _nt(kq_s[p], kd_s[p]), 0.0).astype(BF16) for p in pairs]
            a_rb = [jnp.where(incl, _nt(rq_s[p], bd_s[p]), 0.0).astype(BF16) for p in pairs]
            a_rk = [jnp.where(incl, _nt(rq_s[p], kd_s[p]), 0.0).astype(BF16) for p in pairs]
        inv = [eye - t for t in a_kb]
        if squarings >= 1:
            apow_b = [t.astype(BF16) for t in a_kb]
            apow_b = [_mm(t, t).astype(BF16) for t in apow_b]
            for _ in range(squarings - 1):
                both = [_mm(jnp.concatenate([apow_b[p], inv[p].astype(BF16)], axis=0), apow_b[p]) for p in pairs]
                apow_b = [t[0:gc].astype(BF16) for t in both]
                inv = [inv[p] + both[p][gc:2 * gc] for p in pairs]
            inv = [inv[p] + _mm(inv[p].astype(BF16), apow_b[p]) for p in pairs]
        chunks.append(dict(
            probes=probes, a_kk=a_kk, a_rk=a_rk, a_rb=a_rb, inv=[t.astype(BF16) for t in inv],
            v_s=[stack(xv_c[:, sl]) for sl in lanes],
            upd=[jnp.concatenate([stack(kdp[:, sl]), stack(bdp[:, sl])], axis=0) for sl in lanes],
            p_last=jnp.exp(cum_last)))

    sp = [(q, p) for q in seqs for p in pairs]
    state = {(q, p): s_ref[q * N_PAIRS + p] for q, p in sp}
    y_rows = [None] * (n_seq * n_sub)
    for level in range(n_sub):
        ch = {q: chunks[q * n_sub + level] for q in seqs}
        s_b = {k: state[k].astype(BF16) for k in sp}
        state_t = {(q, p): _nt(ch[q]["probes"][p], s_b[q, p]) for q, p in sp}
        rhs = {(q, p): state_t[q, p][0:gc] + _mm(ch[q]["a_kk"][p], ch[q]["v_s"][p]) for q, p in sp}
        u_b = {(q, p): _mm(ch[q]["inv"][p], rhs[q, p].astype(BF16)).astype(BF16) for q, p in sp}
        if gc % PAIR == 0:
            y_s = {(q, p): state_t[q, p][gc:2 * gc]
                   + _mm(jnp.concatenate([ch[q]["a_rk"][p], -ch[q]["a_rb"][p]], axis=1),
                         jnp.concatenate([ch[q]["v_s"][p], u_b[q, p]], axis=0)) for q, p in sp}
        else:
            y_s = {(q, p): state_t[q, p][gc:2 * gc] + _mm(ch[q]["a_rk"][p], ch[q]["v_s"][p])
                   - _mm(ch[q]["a_rb"][p], u_b[q, p]) for q, p in sp}
        state = {(q, p): state[q, p] * ch[q]["p_last"][:, lanes[p]]
                 + _tn(jnp.concatenate([ch[q]["v_s"][p], -u_b[q, p]], axis=0), ch[q]["upd"][p]) for q, p in sp}
        for q in seqs:
            y_rows[q * n_sub + level] = jnp.concatenate(
                [y_s[q, p][0:cs] + y_s[q, p][cs:gc] for p in pairs], axis=1)
    for q, p in sp:
        s_ref[q * N_PAIRS + p] = state[q, p]

    y = jnp.concatenate(y_rows, axis=0) if len(y_rows) > 1 else y_rows[0]
    mean = _seg_sum(y, bones) * (1.0 / hd)
    d = y - mean
    var = _seg_sum(d * d, bones) * (1.0 / hd)
    yn = d * lax.rsqrt(var + GN_EPS) * lnw_ref[...] + lnb_ref[...]
    bonus = _seg_sum(xr * k2 * rk_ref[...], bones) * xv
    y_ref[...] = (yn + bonus) * gate

    @pl.when(step == n_steps - 1)
    def _fin():
        for q, p in sp:
            so_ref[q, 2 * p] = s_ref[q * N_PAIRS + p, 0:hd, 0:hd]
            so_ref[q, 2 * p + 1] = s_ref[q * N_PAIRS + p, hd:2 * hd, hd:2 * hd]


def _wkv(proj, prev0, s0, params, wl, bones, batch, seq, chunk, n_sub, n_seq=1, casts=()):
    ns = seq // (chunk * n_sub)
    assert n_seq == 1 or ns == 1, "several sequences per step only when a step covers them whole"
    rows = n_seq * chunk * n_sub
    nb = batch // n_seq
    rw = R_WIDTH

    def col(cb):
        return lambda b, c: (b * ns + c, cb)

    const2 = lambda b, c: (0, 0)
    vec = lambda n: pl.BlockSpec((1, n), const2)
    cast_in, cast_out, cast_shapes = [], [], []
    for wgt, span in casts:
        start, count = (0, wgt.shape[0]) if span is None else span
        blk, width = count // (nb * ns), wgt.shape[1]
        assert blk * nb * ns == count, (wgt.shape, span)
        if span is None:
            assert blk % 16 == 0, blk
            cast_in.append(pl.BlockSpec((blk, width), col(0)))
            cast_out.append(pl.BlockSpec((blk, width), col(0)))
            cast_shapes.append(jax.ShapeDtypeStruct((count, width), BF16))
        else:
            assert blk % 128 == 0 and start % 8 == 0, span
            cast_in.append(pl.BlockSpec(
                (pl.Element(blk), pl.Element(width)),
                lambda b, c, start=start, blk=blk: (pl.multiple_of(start + blk * (b * ns + c), 8), 0)))
            cast_out.append(pl.BlockSpec((width, blk), lambda b, c: (0, b * ns + c)))
            cast_shapes.append(jax.ShapeDtypeStruct((width, count), BF16))
    mu, mul, w0, a0, k_k, k_a, r_k, ln_w, ln_b = params
    outs = pl.pallas_call(
        functools.partial(_wkv_kernel, chunk=chunk, n_sub=n_sub, n_seq=n_seq, n_steps=ns,
                          cast_transposed=tuple(span is not None for _, span in casts)),
        out_shape=(jax.ShapeDtypeStruct((batch * seq, rw), F32),
                   jax.ShapeDtypeStruct((batch, R_HEADS, R_HEAD, R_HEAD), F32), *cast_shapes),
        grid=(nb, ns),
        in_specs=[
            pl.BlockSpec((rows, C_BLK), col(P_C // C_BLK)),
            pl.BlockSpec((rows, C_BLK), col(P_C // C_BLK + 1)),
            pl.BlockSpec((rows, LORA_PAD), col(P_L // LORA_PAD)),
            pl.BlockSpec((n_seq, 1, SHIFT_COLS), lambda b, c: (b, 0, 0)),
            pl.BlockSpec((n_seq, R_HEADS, R_HEAD, R_HEAD), lambda b, c: (b, 0, 0, 0)),
            vec(3 * rw), vec(LORA_PAD), vec(rw), vec(rw), vec(rw), vec(rw), vec(rw), vec(rw), vec(rw),
            pl.BlockSpec((LORA_PAD, 3 * rw), const2),
            pl.BlockSpec((SEG, SEG), const2),
            *cast_in,
        ],
        out_specs=(
            pl.BlockSpec((rows, rw), lambda b, c: (b * ns + c, 0)),
            pl.BlockSpec((n_seq, R_HEADS, R_HEAD, R_HEAD), lambda b, c: (b, 0, 0, 0)),
            *cast_out,
        ),
        scratch_shapes=[pltpu.VMEM((n_seq * N_PAIRS, PAIR, PAIR), F32), pltpu.VMEM((n_seq, SHIFT_COLS), F32)],
        compiler_params=pltpu.CompilerParams(
            dimension_semantics=("arbitrary", "arbitrary"), vmem_limit_bytes=VMEM_LIMIT),
        name="wkv",
    )(proj, proj, proj, prev0, s0, mu, mul, w0, a0, k_k, k_a, r_k, ln_w, ln_b, wl, bones,
      *[wgt for wgt, _ in casts])
    return outs[0], outs[1], outs[2:]


def _mix_kernel(x_ref, ya_ref, yr_ref, g_ref, wg_ref, wba_ref, wbr_ref, wo_ref, o_ref):
    x = x_ref[...]
    h = _rms(x, g_ref[...]).astype(BF16)
    ya = ya_ref[...].astype(BF16)
    yr = yr_ref[...].astype(BF16)
    acc = x
    for c in range(D_MODEL // MIX_CHUNK):
        ca = slice(c * MIX_CHUNK, (c + 1) * MIX_CHUNK)
        cr = slice(D_MODEL + c * MIX_CHUNK, D_MODEL + (c + 1) * MIX_CHUNK)
        mixed = (jax.nn.sigmoid(_mm(h, wg_ref[:, ca])) * _mm(ya, wba_ref[:, ca])
                 + jax.nn.sigmoid(_mm(h, wg_ref[:, cr])) * _mm(yr, wbr_ref[:, ca]))
        acc = acc + _mm(mixed.astype(BF16), wo_ref[ca, :])
    o_ref[...] = acc


def _resident(shape):
    return pl.BlockSpec(shape, lambda *_: (0,) * len(shape), pipeline_mode=pl.Buffered(1))


def _mix(x, ya, yr, g, wg, wba, wbr, wo, tm):
    m = x.shape[0]
    row = lambda i: (i, 0)
    return pl.pallas_call(
        _mix_kernel,
        out_shape=jax.ShapeDtypeStruct((m, D_MODEL), F32),
        grid=(m // tm,),
        in_specs=[
            pl.BlockSpec((tm, D_MODEL), row),
            pl.BlockSpec((tm, ATTN_WIDTH), row),
            pl.BlockSpec((tm, R_WIDTH), row),
            pl.BlockSpec((1, D_MODEL), lambda i: (0, 0)),
            _resident((D_MODEL, 2 * D_MODEL)),
            _resident((ATTN_WIDTH, D_MODEL)),
            _resident((R_WIDTH, D_MODEL)),
            _resident((D_MODEL, D_MODEL)),
        ],
        out_specs=pl.BlockSpec((tm, D_MODEL), row),
        compiler_params=pltpu.CompilerParams(
            dimension_semantics=("arbitrary",), vmem_limit_bytes=VMEM_LIMIT),
        name="mix",
    )(x, ya, yr, g, wg, wba, wbr, wo)


def _ffn_kernel(x_ref, g_ref, wu_ref, wd_ref, o_ref, h_ref):
    @pl.when(pl.program_id(1) == 0)
    def _():
        x = x_ref[...]
        h_ref[...] = _rms(x, g_ref[...]).astype(BF16)
        o_ref[...] = x

    u = _mm(h_ref[...], wu_ref[...])
    o_ref[...] += _mm(jnp.square(jnp.maximum(u, 0.0)).astype(BF16), wd_ref[...])


def _ffn(x, g, wu, wd, tm, tk):
    m = x.shape[0]
    return pl.pallas_call(
        _ffn_kernel,
        out_shape=jax.ShapeDtypeStruct((m, D_MODEL), F32),
        grid=(m // tm, D_FF // tk),
        in_specs=[
            pl.BlockSpec((tm, D_MODEL), lambda i, k: (i, 0)),
            pl.BlockSpec((1, D_MODEL), lambda i, k: (0, 0)),
            pl.BlockSpec((D_MODEL, tk), lambda i, k: (0, k)),
            pl.BlockSpec((tk, D_MODEL), lambda i, k: (k, 0)),
        ],
        out_specs=pl.BlockSpec((tm, D_MODEL), lambda i, k: (i, 0)),
        scratch_shapes=[pltpu.VMEM((tm, D_MODEL), BF16)],
        compiler_params=pltpu.CompilerParams(
            dimension_semantics=("arbitrary", "arbitrary"), vmem_limit_bytes=VMEM_LIMIT),
        name="ffn",
    )(x, g, wu, wd)


def _ple_kernel(x_ref, pe_ref, wg_ref, wp_ref, g_ref, o_ref):
    x = x_ref[...]
    gate = jax.nn.sigmoid(_mm(x.astype(BF16), wg_ref[...]))
    x = x + gate * _mm(pe_ref[...].astype(BF16), wp_ref[...])
    o_ref[...] = _rms(x, g_ref[...])


def _ple(x, pe, wg, wp, g, tm):
    m = x.shape[0]
    row = lambda i: (i, 0)
    return pl.pallas_call(
        _ple_kernel,
        out_shape=jax.ShapeDtypeStruct((m, D_MODEL), F32),
        grid=(m // tm,),
        in_specs=[
            pl.BlockSpec((tm, D_MODEL), row),
            pl.BlockSpec((tm, PLE_DIM), row),
            _resident((D_MODEL, D_MODEL)),
            _resident((PLE_DIM, D_MODEL)),
            pl.BlockSpec((1, D_MODEL), lambda i: (0, 0)),
        ],
        out_specs=pl.BlockSpec((tm, D_MODEL), row),
        compiler_params=pltpu.CompilerParams(
            dimension_semantics=("arbitrary",), vmem_limit_bytes=VMEM_LIMIT),
        name="ple",
    )(x, pe, wg, wp, g)


def _dense_tail(x, ya, yr, pe, wts, tiles):
    (g_mix, w_gates, wba, wbr, wo, g_ffn, wu, wd, wg, wp, g_fin) = wts
    x = _mix(x, ya, yr, g_mix, w_gates, wba, wbr, wo, tiles["mix_m"])
    x = _ffn(x, g_ffn, wu, wd, tiles["ffn_m"], tiles["ffn_k"])
    return _ple(x, pe, wg, wp, g_fin, tiles["ple_m"])


def _shift_out(proj, batch, seq):
    last = proj.reshape(batch, seq, P_COLS)[:, -1]
    return last[:, P_C:P_C + RWKV_COLS][None]


def kernel(x_prompt, x_sample, cache_k_win, cache_v_win, state_wkv, state_shift, p_prompt, p_sample,
           norm_mix, w_in, attn_sinks, rwkv_mu, rwkv_w0, rwkv_w2, rwkv_a0, rwkv_a2, rwkv_g2,
           rwkv_k_k, rwkv_k_a, rwkv_r_k, rwkv_ln_w, rwkv_ln_b, w_branch_attn, w_branch_rwkv,
           w_out, norm_ffn, w_ff_up, w_ff_down, w_ple_proj, w_ple_gate, norm_final):
    assert w_in.shape[0] == 1, "single-layer step"
    bp, tp = x_prompt.shape[0], x_prompt.shape[1]
    bs, ts = x_sample.shape[0], x_sample.shape[1]
    rw = R_WIDTH

    wl = jnp.zeros((LORA_PAD, 3 * rw), F32)
    wl = wl.at[0:DECAY_LORA, 0:rw].set(rwkv_w2[0])
    wl = wl.at[DECAY_LORA:DECAY_LORA + AAA_LORA, rw:2 * rw].set(rwkv_a2[0])
    wl = wl.at[DECAY_LORA + AAA_LORA:LORA_COLS, 2 * rw:3 * rw].set(rwkv_g2[0])
    wl = wl.astype(BF16)
    seg_id = np.arange(SEG) // R_HEAD
    bones = jnp.asarray(seg_id[:, None] == seg_id[None, :], BF16)
    mu = rwkv_mu[0]
    row = lambda v: v.reshape(1, -1)
    wkv_params = (row(mu[:3 * rw]), row(jnp.pad(mu[3 * rw:], (0, LORA_PAD - LORA_COLS))),
                  row(rwkv_w0[0]), row(rwkv_a0[0]), row(rwkv_k_k[0]), row(rwkv_k_a[0]),
                  row(rwkv_r_k[0]), row(rwkv_ln_w[0]), row(rwkv_ln_b[0]))
    g_mix = row(norm_mix[0])
    sinks = attn_sinks[0]

    ms = bs * ts
    xs = x_sample.reshape(ms, D_MODEL)
    w_in_t = jnp.transpose(w_in[0])
    proj_s, w_in_b = _proj_cast(xs, g_mix, w_in_t, 1024)

    tiles_p = dict(mix_m=256, ffn_m=1024, ffn_k=512, ple_m=512)
    xp = x_prompt.reshape(bp * tp, D_MODEL)
    proj_p = _proj(xp, g_mix, w_in_b, 1024, 1024)
    ya_p = _attn_prompt(proj_p, sinks, bp, tp)
    yr_p, s_p, (wu, wd, wo, wg, wba, wbr, w_gates) = _wkv(
        proj_p, jnp.zeros((bp, 1, SHIFT_COLS), F32), jnp.zeros((bp, R_HEADS, R_HEAD, R_HEAD), F32),
        wkv_params, wl, bones, bp, tp, 64, 4,
        casts=((w_ff_up[0], None), (w_ff_down[0], None), (w_out[0], None), (w_ple_gate[0], None),
               (w_branch_attn[0], None), (w_branch_rwkv[0], None), (w_in_t, (P_GATES, 2 * D_MODEL))))
    dense = (g_mix, w_gates, wba, wbr, wo, row(norm_ffn[0]), wu, wd, wg, w_ple_proj[0].astype(BF16),
             row(norm_final))
    yp = _dense_tail(xp, ya_p, yr_p, p_prompt[0].reshape(bp * tp, PLE_DIM), dense, tiles_p)
    pp3 = proj_p.reshape(bp, tp, P_COLS)[:, -WINDOW:]
    k_p = pp3[:, :, P_K:P_K + KV_WIDTH].reshape(1, bp, WINDOW, N_KV_HEADS, HEAD_DIM)
    v_p = pp3[:, :, P_V:P_V + KV_WIDTH].reshape(1, bp, WINDOW, N_KV_HEADS, HEAD_DIM)

    tiles_s = dict(mix_m=ms, ffn_m=ms, ffn_k=2048, ple_m=ms)
    ck_t = jnp.transpose(cache_k_win[0], (0, 2, 3, 1)).reshape(bs, KV_WIDTH, WINDOW)
    cv_t = jnp.transpose(cache_v_win[0], (0, 2, 3, 1)).reshape(bs, KV_WIDTH, WINDOW)
    ya_s, nk_t, nv_t = _attn_sample(proj_s, sinks, ck_t, cv_t, bs, ts, SAMPLE_GROUP)
    prev_s = jnp.pad(state_shift[0], ((0, 0), (0, LORA_PAD - LORA_COLS))).reshape(bs, 1, SHIFT_COLS)
    yr_s, s_s, _ = _wkv(proj_s, prev_s, state_wkv[0], wkv_params, wl, bones, bs, ts, ts, 1, n_seq=SAMPLE_GROUP)
    ys = _dense_tail(xs, ya_s, yr_s, p_sample[0].reshape(ms, PLE_DIM), dense, tiles_s)

    return (yp.reshape(bp, tp, D_MODEL), ys.reshape(bs, ts, D_MODEL),
            k_p, v_p, s_p[None], _shift_out(proj_p, bp, tp),
            jnp.transpose(nk_t.reshape(bs, N_KV_HEADS, HEAD_DIM, WINDOW), (0, 3, 1, 2))[None],
            jnp.transpose(nv_t.reshape(bs, N_KV_HEADS, HEAD_DIM, WINDOW), (0, 3, 1, 2))[None],
            s_s[None], _shift_out(proj_s, bs, ts))
```

```python
import functools

import numpy as np
import jax
import jax.numpy as jnp
from jax import lax
from jax.experimental import pallas as pl
from jax.experimental.pallas import tpu as pltpu

F32 = jnp.float32
BF16 = jnp.bfloat16

D_MODEL = 2048
PLE_DIM = 256
HEAD_DIM = 64
N_Q_HEADS = 16
N_KV_HEADS = 4
GQA_GROUP = 4
ATTN_WIDTH = 1024
KV_WIDTH = 256
WINDOW = 128
ALIBI_MAX = 8.0
R_HEAD = 64
R_WIDTH = 1024
R_HEADS = 16
DECAY_LORA = 64
AAA_LORA = 64
GATE_LORA = 160
LORA_COLS = DECAY_LORA + AAA_LORA + GATE_LORA
LORA_PAD = 512
RWKV_COLS = 3 * R_WIDTH + LORA_COLS
D_FF = 4 * D_MODEL
NORM_EPS = 1e-6
GN_EPS = 64e-5

P_Q = 0
P_K = ATTN_WIDTH
P_V = ATTN_WIDTH + KV_WIDTH
P_C = ATTN_WIDTH + 2 * KV_WIDTH
P_L = P_C + 3 * R_WIDTH
P_COLS = P_L + LORA_PAD
P_GATES = P_C + RWKV_COLS
C_BLK = P_C
SHIFT_COLS = 3 * R_WIDTH + LORA_PAD

PAIR = 128
N_PAIRS = R_WIDTH // PAIR
SEG = 256
MIX_CHUNK = 512
SAMPLE_GROUP = 8
PROMPT_BLOCKS = 4

V7X_VMEM_BYTES = 64 * 1024 * 1024
VMEM_LIMIT = 56 * 1024 * 1024


def _mm(a, b):
    return jnp.dot(a, b, preferred_element_type=F32)


def _nt(a, b):
    return lax.dot_general(a, b, (((1,), (1,)), ((), ())), preferred_element_type=F32)


def _tn(a, b):
    return lax.dot_general(a, b, (((0,), (0,)), ((), ())), preferred_element_type=F32)


def _rms(x, g):
    ms = jnp.mean(x * x, axis=-1, keepdims=True)
    return x * lax.rsqrt(ms + NORM_EPS) * g


def _alibi_slope(hq):
    return float(2.0 ** (-ALIBI_MAX * (hq + 1) / N_Q_HEADS))


def _proj_kernel(x_ref, g_ref, w_ref, o_ref, h_ref):
    @pl.when(pl.program_id(1) == 0)
    def _():
        h_ref[...] = _rms(x_ref[...], g_ref[...]).astype(BF16)

    o_ref[...] = _nt(h_ref[...], w_ref[...])


def _proj_cast_kernel(x_ref, g_ref, w_ref, o_ref, wb_ref, h_ref):
    @pl.when(pl.program_id(0) == 0)
    def _():
        h_ref[...] = _rms(x_ref[...], g_ref[...]).astype(BF16)

    wb = w_ref[...].astype(BF16)
    wb_ref[...] = wb
    o_ref[...] = _nt(h_ref[...], wb)


def _proj_cast(x, g, wt_f32, tn):
    m = x.shape[0]
    return pl.pallas_call(
        _proj_cast_kernel,
        out_shape=(jax.ShapeDtypeStruct((m, P_COLS), F32), jax.ShapeDtypeStruct((P_COLS, D_MODEL), BF16)),
        grid=(P_COLS // tn,),
        in_specs=[
            pl.BlockSpec((m, D_MODEL), lambda j: (0, 0)),
            pl.BlockSpec((1, D_MODEL), lambda j: (0, 0)),
            pl.BlockSpec((tn, D_MODEL), lambda j: (j, 0)),
        ],
        out_specs=(pl.BlockSpec((m, tn), lambda j: (0, j)), pl.BlockSpec((tn, D_MODEL), lambda j: (j, 0))),
        scratch_shapes=[pltpu.VMEM((m, D_MODEL), BF16)],
        compiler_params=pltpu.CompilerParams(
            dimension_semantics=("arbitrary",), vmem_limit_bytes=VMEM_LIMIT),
        name="proj_cast",
    )(x, g, wt_f32)


def _proj(x, g, w, tm, tn):
    m = x.shape[0]
    return pl.pallas_call(
        _proj_kernel,
        out_shape=jax.ShapeDtypeStruct((m, P_COLS), F32),
        grid=(m // tm, P_COLS // tn),
        in_specs=[
            pl.BlockSpec((tm, D_MODEL), lambda i, j: (i, 0)),
            pl.BlockSpec((1, D_MODEL), lambda i, j: (0, 0)),
            pl.BlockSpec((tn, D_MODEL), lambda i, j: (j, 0)),
        ],
        out_specs=pl.BlockSpec((tm, tn), lambda i, j: (i, j)),
        scratch_shapes=[pltpu.VMEM((tm, D_MODEL), BF16)],
        compiler_params=pltpu.CompilerParams(
            dimension_semantics=("arbitrary", "arbitrary"), vmem_limit_bytes=VMEM_LIMIT),
        name="proj",
    )(x, g, w)


def _attend_heads(scores, apply_values, sinks):
    heads = range(len(scores))
    m = []
    for h in heads:
        mh = sinks[h]
        for s in scores[h]:
            mh = jnp.maximum(mh, jnp.max(s, axis=-1, keepdims=True))
        m.append(mh)
    ps = [[jnp.exp(s - m[h]) for s in scores[h]] for h in heads]
    den = []
    for h in heads:
        dh = jnp.exp(sinks[h] - m[h])
        for p in ps[h]:
            dh = dh + jnp.sum(p, axis=-1, keepdims=True)
        den.append(dh)
    outs = []
    for h in heads:
        o = None
        for p, pv in zip(ps[h], apply_values[h]):
            t = pv(p.astype(BF16))
            o = t if o is None else o + t
        outs.append(o * (1.0 / den[h]))
    return outs


def _head_slices():
    q_sl = [slice(hq * HEAD_DIM, (hq + 1) * HEAD_DIM) for hq in range(N_Q_HEADS)]
    kv_sl = [slice((hq // GQA_GROUP) * HEAD_DIM, (hq // GQA_GROUP + 1) * HEAD_DIM) for hq in range(N_Q_HEADS)]
    return q_sl, kv_sl


def _alibi_bias(dist, valid):
    slopes = np.array([_alibi_slope(hq) for hq in range(N_Q_HEADS)], np.float32)
    return np.where(valid[None], -slopes[:, None, None] * dist[None].astype(np.float32), -np.inf).astype(np.float32)


def _attn_prompt_kernel(sink_ref, bias_ref, q_ref, kp_ref, kc_ref, vp_ref, vc_ref, o_ref):
    w = WINDOW
    n_blk = q_ref.shape[0] // w
    kj = lax.broadcasted_iota(jnp.int32, (w, 2 * w), 1)
    no_prev = (kj < w) & (pl.program_id(1) == 0)
    q = q_ref[...] * (HEAD_DIM ** -0.5)
    kc, vc = kc_ref[...].astype(BF16), vc_ref[...].astype(BF16)
    ks = jnp.concatenate([kp_ref[...].astype(BF16), kc], axis=0)
    vs = jnp.concatenate([vp_ref[...].astype(BF16), vc], axis=0)
    hqs = range(N_Q_HEADS)
    q_sl, kv_sl = _head_slices()
    scores, values = [], []
    for s in range(n_blk):
        k2, v2 = ks[s * w:(s + 2) * w], vs[s * w:(s + 2) * w]
        qb = q[s * w:(s + 1) * w]
        for hq in hqs:
            sc = _nt(qb[:, q_sl[hq]].astype(BF16), k2[:, kv_sl[hq]]) + bias_ref[hq]
            scores.append([jnp.where(no_prev, -jnp.inf, sc) if s == 0 else sc])
            values.append([functools.partial(_mm, b=v2[:, kv_sl[hq]])])
    outs = _attend_heads(scores, values, [sink_ref[hq] for _ in range(n_blk) for hq in hqs])
    for s in range(n_blk):
        for hq in hqs:
            o_ref[s * w:(s + 1) * w, q_sl[hq]] = outs[s * N_Q_HEADS + hq]


def _attn_prompt(proj, sinks, batch, seq):
    rows = PROMPT_BLOCKS * WINDOW
    ns = seq // rows
    kcol, vcol = P_K // KV_WIDTH, P_V // KV_WIDTH

    def cur(c):
        return lambda b, i: (b * ns + i, c)

    def prev(c):
        return lambda b, i: (b * ns * PROMPT_BLOCKS + jnp.maximum(i * PROMPT_BLOCKS - 1, 0), c)

    ti = np.arange(WINDOW)[:, None]
    kj = np.arange(2 * WINDOW)[None, :]
    dist = ti - kj + WINDOW
    bias = _alibi_bias(dist, (dist >= 0) & (dist <= WINDOW))
    return pl.pallas_call(
        _attn_prompt_kernel,
        out_shape=jax.ShapeDtypeStruct((batch * seq, ATTN_WIDTH), F32),
        grid=(batch, ns),
        in_specs=[
            pl.BlockSpec(memory_space=pltpu.SMEM),
            _resident((N_Q_HEADS, WINDOW, 2 * WINDOW)),
            pl.BlockSpec((rows, ATTN_WIDTH), cur(P_Q // ATTN_WIDTH)),
            pl.BlockSpec((WINDOW, KV_WIDTH), prev(kcol)),
            pl.BlockSpec((rows, KV_WIDTH), cur(kcol)),
            pl.BlockSpec((WINDOW, KV_WIDTH), prev(vcol)),
            pl.BlockSpec((rows, KV_WIDTH), cur(vcol)),
        ],
        out_specs=pl.BlockSpec((rows, ATTN_WIDTH), lambda b, i: (b * ns + i, 0)),
        compiler_params=pltpu.CompilerParams(dimension_semantics=("arbitrary", "arbitrary")),
        name="attn_prompt",
    )(sinks, jnp.asarray(bias), proj, proj, proj, proj, proj)


def _attn_sample_kernel(sink_ref, q_ref, kn_ref, vn_ref, ckt_ref, cvt_ref, o_ref, nkt_ref, nvt_ref, *, seq):
    t, w = seq, WINDOW
    n_seq = q_ref.shape[0] // t
    ti = lax.broadcasted_iota(jnp.int32, (t, w), 0)
    cj = lax.broadcasted_iota(jnp.int32, (t, w), 1)
    dist_c = (ti - cj + w).astype(F32)
    valid_c = cj >= ti
    ti2 = lax.broadcasted_iota(jnp.int32, (t, t), 0)
    tj2 = lax.broadcasted_iota(jnp.int32, (t, t), 1)
    dist_n = (ti2 - tj2).astype(F32)
    valid_n = tj2 <= ti2
    hqs = range(N_Q_HEADS)
    q_sl, kv_sl = _head_slices()
    bias_c = [jnp.where(valid_c, -_alibi_slope(hq) * dist_c, -jnp.inf) for hq in hqs]
    bias_n = [jnp.where(valid_n, -_alibi_slope(hq) * dist_n, -jnp.inf) for hq in hqs]
    q_all = q_ref[...] * (HEAD_DIM ** -0.5)
    is_new = lax.broadcasted_iota(jnp.int32, (1, w), 1) >= w - t
    put = (cj == ti + (w - t)).astype(BF16)

    def place_new(x):
        hi = x.astype(BF16)
        rem = x - hi.astype(F32)
        mid = rem.astype(BF16)
        return _tn(hi, put) + _tn(mid, put) + _tn((rem - mid.astype(F32)).astype(BF16), put)

    scores, values = [], []
    for s in range(n_seq):
        rs = slice(s * t, (s + 1) * t)
        ckt, cvt = ckt_ref[s], cvt_ref[s]
        nkt_ref[s] = jnp.where(is_new, place_new(kn_ref[rs, :]), pltpu.roll(ckt, w - t, axis=1))
        nvt_ref[s] = jnp.where(is_new, place_new(vn_ref[rs, :]), pltpu.roll(cvt, w - t, axis=1))
        cktb, cvtb = ckt.astype(BF16), cvt.astype(BF16)
        knb, vnb = kn_ref[rs, :].astype(BF16), vn_ref[rs, :].astype(BF16)
        q = q_all[rs]
        qs = [q[:, q_sl[hq]].astype(BF16) for hq in hqs]
        scores += [[_mm(qs[hq], cktb[kv_sl[hq], :]) + bias_c[hq], _nt(qs[hq], knb[:, kv_sl[hq]]) + bias_n[hq]]
                   for hq in hqs]
        values += [[functools.partial(_nt, b=cvtb[kv_sl[hq], :]), functools.partial(_mm, b=vnb[:, kv_sl[hq]])]
                   for hq in hqs]
    outs = _attend_heads(scores, values, [sink_ref[hq] for _ in range(n_seq) for hq in hqs])
    for s in range(n_seq):
        for hq in hqs:
            o_ref[s * t:(s + 1) * t, q_sl[hq]] = outs[s * N_Q_HEADS + hq]


def _attn_sample(proj, sinks, cache_kt, cache_vt, batch, seq, n_seq):
    kcol, vcol = P_K // KV_WIDTH, P_V // KV_WIDTH
    rows = n_seq * seq
    win = jax.ShapeDtypeStruct((batch, KV_WIDTH, WINDOW), F32)
    return pl.pallas_call(
        functools.partial(_attn_sample_kernel, seq=seq),
        out_shape=(jax.ShapeDtypeStruct((batch * seq, ATTN_WIDTH), F32), win, win),
        grid=(batch // n_seq,),
        in_specs=[
            pl.BlockSpec(memory_space=pltpu.SMEM),
            pl.BlockSpec((rows, ATTN_WIDTH), lambda b: (b, P_Q // ATTN_WIDTH)),
            pl.BlockSpec((rows, KV_WIDTH), lambda b: (b, kcol)),
            pl.BlockSpec((rows, KV_WIDTH), lambda b: (b, vcol)),
            pl.BlockSpec((n_seq, KV_WIDTH, WINDOW), lambda b: (b, 0, 0)),
            pl.BlockSpec((n_seq, KV_WIDTH, WINDOW), lambda b: (b, 0, 0)),
        ],
        out_specs=(
            pl.BlockSpec((rows, ATTN_WIDTH), lambda b: (b, 0)),
            pl.BlockSpec((n_seq, KV_WIDTH, WINDOW), lambda b: (b, 0, 0)),
            pl.BlockSpec((n_seq, KV_WIDTH, WINDOW), lambda b: (b, 0, 0)),
        ),
        compiler_params=pltpu.CompilerParams(dimension_semantics=("arbitrary",)),
        name="attn_sample",
    )(sinks, proj, proj, proj, cache_kt, cache_vt)


def _seg_sum(x, bones):
    rows = x.shape[0]
    hi = x.astype(BF16).astype(F32)
    lo = x - hi
    groups = [slice(j * SEG, (j + 1) * SEG) for j in range(R_WIDTH // SEG)]
    lhs = jnp.concatenate([t[:, sl] for sl in groups for t in (hi, lo)], axis=0).astype(BF16)
    out = _mm(lhs, bones)
    return jnp.concatenate(
        [out[2 * j * rows:(2 * j + 1) * rows] + out[(2 * j + 1) * rows:(2 * j + 2) * rows]
         for j in range(len(groups))], axis=1)


def _wkv_kernel(pa_ref, pb_ref, pl_ref, prev_ref, s0_ref,
                mu_ref, mul_ref, w0_ref, a0_ref, kk_ref, ka_ref, rk_ref, lnw_ref, lnb_ref,
                wl_ref, bones_ref, *rest, chunk, n_sub, n_seq, n_steps, cast_transposed):
    n_cast = len(cast_transposed)
    cast_src, (y_ref, so_ref) = rest[:n_cast], rest[n_cast:n_cast + 2]
    cast_dst, (s_ref, carry_ref) = rest[n_cast + 2:2 * n_cast + 2], rest[2 * n_cast + 2:]
    for src, dst, transposed in zip(cast_src, cast_dst, cast_transposed):
        dst[...] = (src[...].T if transposed else src[...]).astype(BF16)

    step = pl.program_id(1)
    cs = chunk
    seq_rows = n_sub * cs
    rows = n_seq * seq_rows
    gc = 2 * cs
    hd = R_HEAD
    w = R_WIDTH
    seqs = range(n_seq)

    @pl.when(step == 0)
    def _init():
        s_ref[...] = jnp.zeros(s_ref.shape, F32)
        for q in seqs:
            carry_ref[q:q + 1, :] = prev_ref[q]
            for p in range(N_PAIRS):
                s_ref[q * N_PAIRS + p, 0:hd, 0:hd] = s0_ref[q, 2 * p]
                s_ref[q * N_PAIRS + p, hd:2 * hd, hd:2 * hd] = s0_ref[q, 2 * p + 1]

    row = lax.broadcasted_iota(jnp.int32, (rows, 1), 0)

    def token_shift(x, lo, hi, mu):
        shifted = pltpu.roll(x, 1, axis=0)
        for q in seqs:
            shifted = jnp.where(row == q * seq_rows, carry_ref[q:q + 1, lo:hi], shifted)
        return x + (shifted - x) * mu

    cols = jnp.concatenate([pa_ref[...], pb_ref[...]], axis=1)
    l_raw = pl_ref[...]
    xx = token_shift(cols, 0, 3 * w, mu_ref[...])
    xr, xk, xv = xx[:, 0:w], xx[:, w:2 * w], xx[:, 2 * w:3 * w]
    xl = token_shift(l_raw, 3 * w, SHIFT_COLS, mul_ref[...])
    for q in seqs:
        last = (q + 1) * seq_rows
        carry_ref[q:q + 1, 0:3 * w] = cols[last - 1:last, :]
        carry_ref[q:q + 1, 3 * w:] = l_raw[last - 1:last, :]

    lane_l = lax.broadcasted_iota(jnp.int32, (1, LORA_PAD), 1)
    act = jnp.where(lane_l < DECAY_LORA, jnp.tanh(xl),
                    jnp.where(lane_l < DECAY_LORA + AAA_LORA, xl,
                              jnp.where(lane_l < LORA_COLS, jax.nn.sigmoid(xl), 0.0)))
    up = _mm(act.astype(BF16), wl_ref[...])
    z = -(w0_ref[...] + up[:, 0:w])
    softplus = jnp.maximum(z, 0.0) + jnp.log1p(jnp.exp(-jnp.abs(z)))
    lwd = -jnp.exp(-softplus - 0.5)
    a = jax.nn.sigmoid(a0_ref[...] + up[:, w:2 * w])
    gate = up[:, 2 * w:3 * w]

    bones = bones_ref[...]
    kkn = xk * kk_ref[...]
    kk = kkn * lax.rsqrt(jnp.maximum(_seg_sum(kkn * kkn, bones), 1e-24))
    k2 = xk * (1.0 + (a - 1.0) * ka_ref[...])
    b = kk * a

    tri = (lax.broadcasted_iota(jnp.int32, (cs, cs), 0)
           >= lax.broadcasted_iota(jnp.int32, (cs, cs), 1)).astype(BF16)
    ri = lax.broadcasted_iota(jnp.int32, (gc, gc), 0)
    ci = lax.broadcasted_iota(jnp.int32, (gc, gc), 1)
    same_head = (ri >= cs) == (ci >= cs)
    strict = same_head & (ci < ri)
    incl = same_head & (ci <= ri)
    eye = (ri == ci).astype(F32)
    head0 = lax.broadcasted_iota(jnp.int32, (1, PAIR), 1) < hd

    def stack(x):
        return jnp.concatenate([jnp.where(head0, x, 0.0), jnp.where(head0, 0.0, x)], axis=0).astype(BF16)

    pairs = range(N_PAIRS)
    lanes = [slice(p * PAIR, (p + 1) * PAIR) for p in pairs]
    squarings = cs.bit_length() - 2

    chunks = []
    for sub in range(n_seq * n_sub):
        rs = slice(sub * cs, (sub + 1) * cs)
        lw_c = lwd[rs]
        lw_hi = lw_c.astype(BF16)
        lw_r = lw_c - lw_hi.astype(F32)
        lw_mid = lw_r.astype(BF16)
        cum = _mm(tri, lw_hi) + _mm(tri, lw_mid) + _mm(tri, (lw_r - lw_mid.astype(F32)).astype(BF16))
        cum_last = cum[cs - 1:cs, :]
        e_inv = jnp.exp(-cum)
        e_last = jnp.exp(cum_last - cum)
        kq = kk[rs] * jnp.exp(cum - lw_c)
        rq = xr[rs] * jnp.exp(cum)
        kd = k2[rs] * e_inv
        bd = b[rs] * e_inv
        kdp = k2[rs] * e_last
        bdp = b[rs] * e_last
        xv_c = xv[rs]
        kq_s = [stack(kq[:, sl]) for sl in lanes]
        bd_s = [stack(bd[:, sl]) for sl in lanes]
        kd_s = [stack(kd[:, sl]) for sl in lanes]
        rq_s = [stack(rq[:, sl]) for sl in lanes]
        probes = [jnp.concatenate([kq_s[p], rq_s[p]], axis=0) for p in pairs]
        if gc % PAIR == 0:
            a_all = [_nt(probes[p], jnp.concatenate([bd_s[p], kd_s[p]], axis=0)) for p in pairs]
            a_kb = [jnp.where(strict, t[0:gc, 0:gc], 0.0) for t in a_all]
            a_kk = [jnp.where(strict, t[0:gc, gc:2 * gc], 0.0).astype(BF16) for t in a_all]
            a_rb = [jnp.where(incl, t[gc:2 * gc, 0:gc], 0.0).astype(BF16) for t in a_all]
            a_rk = [jnp.where(incl, t[gc:2 * gc, gc:2 * gc], 0.0).astype(BF16) for t in a_all]
        else:
            a_kb = [jnp.where(strict, _nt(kq_s[p], bd_s[p]), 0.0) for p in pairs]
            a_kk = [jnp.where(strict, _nt(kq_s[p], kd_s[p]), 0.0).astype(BF16) for p in pairs]
            a_rb = [jnp.where(incl, _nt(rq_s[p], bd_s[p]), 0.0).astype(BF16) for p in pairs]
            a_rk = [jnp.where(incl, _nt(rq_s[p], kd_s[p]), 0.0).astype(BF16) for p in pairs]
        inv = [eye - t for t in a_kb]
        if squarings >= 1:
            apow_b = [t.astype(BF16) for t in a_kb]
            apow_b = [_mm(t, t).astype(BF16) for t in apow_b]
            for _ in range(squarings - 1):
                both = [_mm(jnp.concatenate([apow_b[p], inv[p].astype(BF16)], axis=0), apow_b[p]) for p in pairs]
                apow_b = [t[0:gc].astype(BF16) for t in both]
                inv = [inv[p] + both[p][gc:2 * gc] for p in pairs]
            inv = [inv[p] + _mm(inv[p].astype(BF16), apow_b[p]) for p in pairs]
        chunks.append(dict(
            probes=probes, a_kk=a_kk, a_rk=a_rk, a_rb=a_rb, inv=[t.astype(BF16) for t in inv],
            v_s=[stack(xv_c[:, sl]) for sl in lanes],
            upd=[jnp.concatenate([stack(kdp[:, sl]), stack(bdp[:, sl])], axis=0) for sl in lanes],
            p_last=jnp.exp(cum_last)))

    sp = [(q, p) for q in seqs for p in pairs]
    state = {(q, p): s_ref[q * N_PAIRS + p] for q, p in sp}
    y_rows = [None] * (n_seq * n_sub)
    for level in range(n_sub):
        ch = {q: chunks[q * n_sub + level] for q in seqs}
        s_b = {k: state[k].astype(BF16) for k in sp}
        state_t = {(q, p): _nt(ch[q]["probes"][p], s_b[q, p]) for q, p in sp}
        rhs = {(q, p): state_t[q, p][0:gc] + _mm(ch[q]["a_kk"][p], ch[q]["v_s"][p]) for q, p in sp}
        u_b = {(q, p): _mm(ch[q]["inv"][p], rhs[q, p].astype(BF16)).astype(BF16) for q, p in sp}
        if gc % PAIR == 0:
            y_s = {(q, p): state_t[q, p][gc:2 * gc]
                   + _mm(jnp.concatenate([ch[q]["a_rk"][p], -ch[q]["a_rb"][p]], axis=1),
                         jnp.concatenate([ch[q]["v_s"][p], u_b[q, p]], axis=0)) for q, p in sp}
        else:
            y_s = {(q, p): state_t[q, p][gc:2 * gc] + _mm(ch[q]["a_rk"][p], ch[q]["v_s"][p])
                   - _mm(ch[q]["a_rb"][p], u_b[q, p]) for q, p in sp}
        state = {(q, p): state[q, p] * ch[q]["p_last"][:, lanes[p]]
                 + _tn(jnp.concatenate([ch[q]["v_s"][p], -u_b[q, p]], axis=0), ch[q]["upd"][p]) for q, p in sp}
        for q in seqs:
            y_rows[q * n_sub + level] = jnp.concatenate(
                [y_s[q, p][0:cs] + y_s[q, p][cs:gc] for p in pairs], axis=1)
    for q, p in sp:
        s_ref[q * N_PAIRS + p] = state[q, p]

    y = jnp.concatenate(y_rows, axis=0) if len(y_rows) > 1 else y_rows[0]
    mean = _seg_sum(y, bones) * (1.0 / hd)
    d = y - mean
    var = _seg_sum(d * d, bones) * (1.0 / hd)
    yn = d * lax.rsqrt(var + GN_EPS) * lnw_ref[...] + lnb_ref[...]
    bonus = _seg_sum(xr * k2 * rk_ref[...], bones) * xv
    y_ref[...] = (yn + bonus) * gate

    @pl.when(step == n_steps - 1)
    def _fin():
        for q, p in sp:
            so_ref[q, 2 * p] = s_ref[q * N_PAIRS + p, 0:hd, 0:hd]
            so_ref[q, 2 * p + 1] = s_ref[q * N_PAIRS + p, hd:2 * hd, hd:2 * hd]


def _wkv(proj, prev0, s0, params, wl, bones, batch, seq, chunk, n_sub, n_seq=1, casts=()):
    ns = seq // (chunk * n_sub)
    assert n_seq == 1 or ns == 1, "several sequences per step only when a step covers them whole"
    rows = n_seq * chunk * n_sub
    nb = batch // n_seq
    rw = R_WIDTH

    def col(cb):
        return lambda b, c: (b * ns + c, cb)

    const2 = lambda b, c: (0, 0)
    vec = lambda n: pl.BlockSpec((1, n), const2)
    cast_in, cast_out, cast_shapes = [], [], []
    for wgt, span in casts:
        start, count = (0, wgt.shape[0]) if span is None else span
        blk, width = count // (nb * ns), wgt.shape[1]
        assert blk * nb * ns == count, (wgt.shape, span)
        if span is None:
            assert blk % 16 == 0, blk
            cast_in.append(pl.BlockSpec((blk, width), col(0)))
            cast_out.append(pl.BlockSpec((blk, width), col(0)))
            cast_shapes.append(jax.ShapeDtypeStruct((count, width), BF16))
        else:
            assert blk % 128 == 0 and start % 8 == 0, span
            cast_in.append(pl.BlockSpec(
                (pl.Element(blk), pl.Element(width)),
                lambda b, c, start=start, blk=blk: (pl.multiple_of(start + blk * (b * ns + c), 8), 0)))
            cast_out.append(pl.BlockSpec((width, blk), lambda b, c: (0, b * ns + c)))
            cast_shapes.append(jax.ShapeDtypeStruct((width, count), BF16))
    mu, mul, w0, a0, k_k, k_a, r_k, ln_w, ln_b = params
    outs = pl.pallas_call(
        functools.partial(_wkv_kernel, chunk=chunk, n_sub=n_sub, n_seq=n_seq, n_steps=ns,
                          cast_transposed=tuple(span is not None for _, span in casts)),
        out_shape=(jax.ShapeDtypeStruct((batch * seq, rw), F32),
                   jax.ShapeDtypeStruct((batch, R_HEADS, R_HEAD, R_HEAD), F32), *cast_shapes),
        grid=(nb, ns),
        in_specs=[
            pl.BlockSpec((rows, C_BLK), col(P_C // C_BLK)),
            pl.BlockSpec((rows, C_BLK), col(P_C // C_BLK + 1)),
            pl.BlockSpec((rows, LORA_PAD), col(P_L // LORA_PAD)),
            pl.BlockSpec((n_seq, 1, SHIFT_COLS), lambda b, c: (b, 0, 0)),
            pl.BlockSpec((n_seq, R_HEADS, R_HEAD, R_HEAD), lambda b, c: (b, 0, 0, 0)),
            vec(3 * rw), vec(LORA_PAD), vec(rw), vec(rw), vec(rw), vec(rw), vec(rw), vec(rw), vec(rw),
            pl.BlockSpec((LORA_PAD, 3 * rw), const2),
            pl.BlockSpec((SEG, SEG), const2),
            *cast_in,
        ],
        out_specs=(
            pl.BlockSpec((rows, rw), lambda b, c: (b * ns + c, 0)),
            pl.BlockSpec((n_seq, R_HEADS, R_HEAD, R_HEAD), lambda b, c: (b, 0, 0, 0)),
            *cast_out,
        ),
        scratch_shapes=[pltpu.VMEM((n_seq * N_PAIRS, PAIR, PAIR), F32), pltpu.VMEM((n_seq, SHIFT_COLS), F32)],
        compiler_params=pltpu.CompilerParams(
            dimension_semantics=("arbitrary", "arbitrary"), vmem_limit_bytes=VMEM_LIMIT),
        name="wkv",
    )(proj, proj, proj, prev0, s0, mu, mul, w0, a0, k_k, k_a, r_k, ln_w, ln_b, wl, bones,
      *[wgt for wgt, _ in casts])
    return outs[0], outs[1], outs[2:]


def _mix_kernel(x_ref, ya_ref, yr_ref, g_ref, wg_ref, wba_ref, wbr_ref, wo_ref, o_ref):
    x = x_ref[...]
    h = _rms(x, g_ref[...]).astype(BF16)
    ya = ya_ref[...].astype(BF16)
    yr = yr_ref[...].astype(BF16)
    acc = x
    for c in range(D_MODEL // MIX_CHUNK):
        ca = slice(c * MIX_CHUNK, (c + 1) * MIX_CHUNK)
        cr = slice(D_MODEL + c * MIX_CHUNK, D_MODEL + (c + 1) * MIX_CHUNK)
        mixed = (jax.nn.sigmoid(_mm(h, wg_ref[:, ca])) * _mm(ya, wba_ref[:, ca])
                 + jax.nn.sigmoid(_mm(h, wg_ref[:, cr])) * _mm(yr, wbr_ref[:, ca]))
        acc = acc + _mm(mixed.astype(BF16), wo_ref[ca, :])
    o_ref[...] = acc


def _resident(shape):
    return pl.BlockSpec(shape, lambda *_: (0,) * len(shape), pipeline_mode=pl.Buffered(1))


def _mix(x, ya, yr, g, wg, wba, wbr, wo, tm):
    m = x.shape[0]
    row = lambda i: (i, 0)
    return pl.pallas_call(
        _mix_kernel,
        out_shape=jax.ShapeDtypeStruct((m, D_MODEL), F32),
        grid=(m // tm,),
        in_specs=[
            pl.BlockSpec((tm, D_MODEL), row),
            pl.BlockSpec((tm, ATTN_WIDTH), row),
            pl.BlockSpec((tm, R_WIDTH), row),
            pl.BlockSpec((1, D_MODEL), lambda i: (0, 0)),
            _resident((D_MODEL, 2 * D_MODEL)),
            _resident((ATTN_WIDTH, D_MODEL)),
            _resident((R_WIDTH, D_MODEL)),
            _resident((D_MODEL, D_MODEL)),
        ],
        out_specs=pl.BlockSpec((tm, D_MODEL), row),
        compiler_params=pltpu.CompilerParams(
            dimension_semantics=("arbitrary",), vmem_limit_bytes=VMEM_LIMIT),
        name="mix",
    )(x, ya, yr, g, wg, wba, wbr, wo)


def _ffn_kernel(x_ref, g_ref, wu_ref, wd_ref, o_ref, h_ref):
    @pl.when(pl.program_id(1) == 0)
    def _():
        x = x_ref[...]
        h_ref[...] = _rms(x, g_ref[...]).astype(BF16)
        o_ref[...] = x

    u = _mm(h_ref[...], wu_ref[...])
    o_ref[...] += _mm(jnp.square(jnp.maximum(u, 0.0)).astype(BF16), wd_ref[...])


def _ffn_vmem_bytes(tm, tk):
    return 2 * 2 * tm * D_MODEL * 4 + 2 * 2 * D_MODEL * tk * 2 + tm * D_MODEL * 2 + tm * tk * (4 + 2)


def _ffn(x, g, wu, wd, tm, tk):
    m = x.shape[0]
    vmem = _ffn_vmem_bytes(tm, tk)
    assert vmem <= V7X_VMEM_BYTES, (tm, tk)
    return pl.pallas_call(
        _ffn_kernel,
        out_shape=jax.ShapeDtypeStruct((m, D_MODEL), F32),
        grid=(m // tm, D_FF // tk),
        in_specs=[
            pl.BlockSpec((tm, D_MODEL), lambda i, k: (i, 0)),
            pl.BlockSpec((1, D_MODEL), lambda i, k: (0, 0)),
            pl.BlockSpec((D_MODEL, tk), lambda i, k: (0, k)),
            pl.BlockSpec((tk, D_MODEL), lambda i, k: (k, 0)),
        ],
        out_specs=pl.BlockSpec((tm, D_MODEL), lambda i, k: (i, 0)),
        scratch_shapes=[pltpu.VMEM((tm, D_MODEL), BF16)],
        compiler_params=pltpu.CompilerParams(
            dimension_semantics=("arbitrary", "arbitrary"), vmem_limit_bytes=vmem),
        name="ffn",
    )(x, g, wu, wd)


def _ple_kernel(x_ref, pe_ref, wg_ref, wp_ref, g_ref, o_ref):
    x = x_ref[...]
    gate = jax.nn.sigmoid(_mm(x.astype(BF16), wg_ref[...]))
    x = x + gate * _mm(pe_ref[...].astype(BF16), wp_ref[...])
    o_ref[...] = _rms(x, g_ref[...])


def _ple(x, pe, wg, wp, g, tm):
    m = x.shape[0]
    row = lambda i: (i, 0)
    return pl.pallas_call(
        _ple_kernel,
        out_shape=jax.ShapeDtypeStruct((m, D_MODEL), F32),
        grid=(m // tm,),
        in_specs=[
            pl.BlockSpec((tm, D_MODEL), row),
            pl.BlockSpec((tm, PLE_DIM), row),
            _resident((D_MODEL, D_MODEL)),
            _resident((PLE_DIM, D_MODEL)),
            pl.BlockSpec((1, D_MODEL), lambda i: (0, 0)),
        ],
        out_specs=pl.BlockSpec((tm, D_MODEL), row),
        compiler_params=pltpu.CompilerParams(
            dimension_semantics=("arbitrary",), vmem_limit_bytes=VMEM_LIMIT),
        name="ple",
    )(x, pe, wg, wp, g)


def _dense_tail(x, ya, yr, pe, wts, tiles):
    (g_mix, w_gates, wba, wbr, wo, g_ffn, wu, wd, wg, wp, g_fin) = wts
    x = _mix(x, ya, yr, g_mix, w_gates, wba, wbr, wo, tiles["mix_m"])
    x = _ffn(x, g_ffn, wu, wd, tiles["ffn_m"], tiles["ffn_k"])
    return _ple(x, pe, wg, wp, g_fin, tiles["ple_m"])


def _shift_out(proj, batch, seq):
    last = proj.reshape(batch, seq, P_COLS)[:, -1]
    return last[:, P_C:P_C + RWKV_COLS][None]


def kernel(x_prompt, x_sample, cache_k_win, cache_v_win, state_wkv, state_shift, p_prompt, p_sample,
           norm_mix, w_in, attn_sinks, rwkv_mu, rwkv_w0, rwkv_w2, rwkv_a0, rwkv_a2, rwkv_g2,
           rwkv_k_k, rwkv_k_a, rwkv_r_k, rwkv_ln_w, rwkv_ln_b, w_branch_attn, w_branch_rwkv,
           w_out, norm_ffn, w_ff_up, w_ff_down, w_ple_proj, w_ple_gate, norm_final):
    assert w_in.shape[0] == 1, "single-layer step"
    bp, tp = x_prompt.shape[0], x_prompt.shape[1]
    bs, ts = x_sample.shape[0], x_sample.shape[1]
    rw = R_WIDTH

    wl = jnp.zeros((LORA_PAD, 3 * rw), F32)
    wl = wl.at[0:DECAY_LORA, 0:rw].set(rwkv_w2[0])
    wl = wl.at[DECAY_LORA:DECAY_LORA + AAA_LORA, rw:2 * rw].set(rwkv_a2[0])
    wl = wl.at[DECAY_LORA + AAA_LORA:LORA_COLS, 2 * rw:3 * rw].set(rwkv_g2[0])
    wl = wl.astype(BF16)
    seg_id = np.arange(SEG) // R_HEAD
    bones = jnp.asarray(seg_id[:, None] == seg_id[None, :], BF16)
    mu = rwkv_mu[0]
    row = lambda v: v.reshape(1, -1)
    wkv_params = (row(mu[:3 * rw]), row(jnp.pad(mu[3 * rw:], (0, LORA_PAD - LORA_COLS))),
                  row(rwkv_w0[0]), row(rwkv_a0[0]), row(rwkv_k_k[0]), row(rwkv_k_a[0]),
                  row(rwkv_r_k[0]), row(rwkv_ln_w[0]), row(rwkv_ln_b[0]))
    g_mix = row(norm_mix[0])
    sinks = attn_sinks[0]

    ms = bs * ts
    xs = x_sample.reshape(ms, D_MODEL)
    w_in_t = jnp.transpose(w_in[0])
    proj_s, w_in_b = _proj_cast(xs, g_mix, w_in_t, 1024)

    tiles_p = dict(mix_m=256, ffn_m=1024, ffn_k=1024, ple_m=512)
    xp = x_prompt.reshape(bp * tp, D_MODEL)
    proj_p = _proj(xp, g_mix, w_in_b, 1024, 1024)
    ya_p = _attn_prompt(proj_p, sinks, bp, tp)
    yr_p, s_p, (wu, wd, wo, wg, wba, wbr, w_gates) = _wkv(
        proj_p, jnp.zeros((bp, 1, SHIFT_COLS), F32), jnp.zeros((bp, R_HEADS, R_HEAD, R_HEAD), F32),
        wkv_params, wl, bones, bp, tp, 64, 4,
        casts=((w_ff_up[0], None), (w_ff_down[0], None), (w_out[0], None), (w_ple_gate[0], None),
               (w_branch_attn[0], None), (w_branch_rwkv[0], None), (w_in_t, (P_GATES, 2 * D_MODEL))))
    dense = (g_mix, w_gates, wba, wbr, wo, row(norm_ffn[0]), wu, wd, wg, w_ple_proj[0].astype(BF16),
             row(norm_final))
    yp = _dense_tail(xp, ya_p, yr_p, p_prompt[0].reshape(bp * tp, PLE_DIM), dense, tiles_p)
    pp3 = proj_p.reshape(bp, tp, P_COLS)[:, -WINDOW:]
    k_p = pp3[:, :, P_K:P_K + KV_WIDTH].reshape(1, bp, WINDOW, N_KV_HEADS, HEAD_DIM)
    v_p = pp3[:, :, P_V:P_V + KV_WIDTH].reshape(1, bp, WINDOW, N_KV_HEADS, HEAD_DIM)

    tiles_s = dict(mix_m=ms, ffn_m=ms, ffn_k=2048, ple_m=ms)
    ck_t = jnp.transpose(cache_k_win[0], (0, 2, 3, 1)).reshape(bs, KV_WIDTH, WINDOW)
    cv_t = jnp.transpose(cache_v_win[0], (0, 2, 3, 1)).reshape(bs, KV_WIDTH, WINDOW)
    ya_s, nk_t, nv_t = _attn_sample(proj_s, sinks, ck_t, cv_t, bs, ts, SAMPLE_GROUP)
    prev_s = jnp.pad(state_shift[0], ((0, 0), (0, LORA_PAD - LORA_COLS))).reshape(bs, 1, SHIFT_COLS)
    yr_s, s_s, _ = _wkv(proj_s, prev_s, state_wkv[0], wkv_params, wl, bones, bs, ts, ts, 1, n_seq=SAMPLE_GROUP)
    ys = _dense_tail(xs, ya_s, yr_s, p_sample[0].reshape(ms, PLE_DIM), dense, tiles_s)

    return (yp.reshape(bp, tp, D_MODEL), ys.reshape(bs, ts, D_MODEL),
            k_p, v_p, s_p[None], _shift_out(proj_p, bp, tp),
            jnp.transpose(nk_t.reshape(bs, N_KV_HEADS, HEAD_DIM, WINDOW), (0, 3, 1, 2))[None],
            jnp.transpose(nv_t.reshape(bs, N_KV_HEADS, HEAD_DIM, WINDOW), (0, 3, 1, 2))[None],
            s_s[None], _shift_out(proj_s, bs, ts))
```

```python
import functools

import numpy as np
import jax
import jax.numpy as jnp
from jax import lax
from jax.experimental import pallas as pl
from jax.experimental.pallas import tpu as pltpu

F32 = jnp.float32
BF16 = jnp.bfloat16

D_MODEL = 2048
PLE_DIM = 256
HEAD_DIM = 64
N_Q_HEADS = 16
N_KV_HEADS = 4
GQA_GROUP = 4
ATTN_WIDTH = 1024
KV_WIDTH = 256
WINDOW = 128
ALIBI_MAX = 8.0
R_HEAD = 64
R_WIDTH = 1024
R_HEADS = 16
DECAY_LORA = 64
AAA_LORA = 64
GATE_LORA = 160
LORA_COLS = DECAY_LORA + AAA_LORA + GATE_LORA
LORA_PAD = 512
RWKV_COLS = 3 * R_WIDTH + LORA_COLS
D_FF = 4 * D_MODEL
NORM_EPS = 1e-6
GN_EPS = 64e-5

P_Q = 0
P_K = ATTN_WIDTH
P_V = ATTN_WIDTH + KV_WIDTH
P_C = ATTN_WIDTH + 2 * KV_WIDTH
P_L = P_C + 3 * R_WIDTH
P_COLS = P_L + LORA_PAD
P_GATES = P_C + RWKV_COLS
C_BLK = P_C
SHIFT_COLS = 3 * R_WIDTH + LORA_PAD

PAIR = 128
N_PAIRS = R_WIDTH // PAIR
SEG = 256
MIX_CHUNK = 512
SAMPLE_GROUP = 8
PROMPT_BLOCKS = 4

V7X_VMEM_BYTES = 64 * 1024 * 1024
VMEM_LIMIT = 56 * 1024 * 1024


def _mm(a, b):
    return jnp.dot(a, b, preferred_element_type=F32)


def _nt(a, b):
    return lax.dot_general(a, b, (((1,), (1,)), ((), ())), preferred_element_type=F32)


def _tn(a, b):
    return lax.dot_general(a, b, (((0,), (0,)), ((), ())), preferred_element_type=F32)


def _rms(x, g):
    ms = jnp.mean(x * x, axis=-1, keepdims=True)
    return x * lax.rsqrt(ms + NORM_EPS) * g


def _alibi_slope(hq):
    return float(2.0 ** (-ALIBI_MAX * (hq + 1) / N_Q_HEADS))


def _cast_specs(casts, n_steps, step_of):
    in_specs, out_specs, shapes = [], [], []
    for wgt, span in casts:
        start, count = (0, wgt.shape[0]) if span is None else span
        blk, width = count // n_steps, wgt.shape[1]
        assert blk * n_steps == count, (wgt.shape, span, n_steps)
        if span is None:
            assert blk % 16 == 0, blk
            in_specs.append(pl.BlockSpec((blk, width), lambda *g: (step_of(*g), 0)))
            out_specs.append(pl.BlockSpec((blk, width), lambda *g: (step_of(*g), 0)))
            shapes.append(jax.ShapeDtypeStruct((count, width), BF16))
        else:
            assert blk % 128 == 0 and start % 8 == 0, span
            in_specs.append(pl.BlockSpec(
                (pl.Element(blk), pl.Element(width)),
                lambda *g, start=start, blk=blk: (pl.multiple_of(start + blk * step_of(*g), 8), 0)))
            out_specs.append(pl.BlockSpec((width, blk), lambda *g: (0, step_of(*g))))
            shapes.append(jax.ShapeDtypeStruct((width, count), BF16))
    return in_specs, out_specs, shapes, tuple(span is not None for _, span in casts)


def _run_casts(srcs, dsts, transposed):
    for src, dst, tr in zip(srcs, dsts, transposed):
        dst[...] = (src[...].T if tr else src[...]).astype(BF16)


def _proj_kernel(x_ref, g_ref, w_ref, *rest, cast_transposed):
    n_cast = len(cast_transposed)
    cast_src, o_ref, cast_dst, h_ref = rest[:n_cast], rest[n_cast], rest[n_cast + 1:2 * n_cast + 1], rest[-1]
    _run_casts(cast_src, cast_dst, cast_transposed)

    @pl.when(pl.program_id(1) == 0)
    def _():
        h_ref[...] = _rms(x_ref[...], g_ref[...]).astype(BF16)

    o_ref[...] = _nt(h_ref[...], w_ref[...])


def _proj_cast_kernel(x_ref, g_ref, w_ref, o_ref, wb_ref, h_ref):
    @pl.when(pl.program_id(0) == 0)
    def _():
        h_ref[...] = _rms(x_ref[...], g_ref[...]).astype(BF16)

    wb = w_ref[...].astype(BF16)
    wb_ref[...] = wb
    o_ref[...] = _nt(h_ref[...], wb)


def _proj_cast(x, g, wt_f32, tn):
    m = x.shape[0]
    return pl.pallas_call(
        _proj_cast_kernel,
        out_shape=(jax.ShapeDtypeStruct((m, P_COLS), F32), jax.ShapeDtypeStruct((P_COLS, D_MODEL), BF16)),
        grid=(P_COLS // tn,),
        in_specs=[
            pl.BlockSpec((m, D_MODEL), lambda j: (0, 0)),
            pl.BlockSpec((1, D_MODEL), lambda j: (0, 0)),
            pl.BlockSpec((tn, D_MODEL), lambda j: (j, 0)),
        ],
        out_specs=(pl.BlockSpec((m, tn), lambda j: (0, j)), pl.BlockSpec((tn, D_MODEL), lambda j: (j, 0))),
        scratch_shapes=[pltpu.VMEM((m, D_MODEL), BF16)],
        compiler_params=pltpu.CompilerParams(
            dimension_semantics=("arbitrary",), vmem_limit_bytes=VMEM_LIMIT),
        name="proj_cast",
    )(x, g, wt_f32)


def _proj(x, g, w, tm, tn, casts=()):
    m = x.shape[0]
    ni, nj = m // tm, P_COLS // tn
    cast_in, cast_out, cast_shapes, cast_tr = _cast_specs(casts, ni * nj, lambda i, j: i * nj + j)
    outs = pl.pallas_call(
        functools.partial(_proj_kernel, cast_transposed=cast_tr),
        out_shape=(jax.ShapeDtypeStruct((m, P_COLS), F32), *cast_shapes),
        grid=(ni, nj),
        in_specs=[
            pl.BlockSpec((tm, D_MODEL), lambda i, j: (i, 0)),
            pl.BlockSpec((1, D_MODEL), lambda i, j: (0, 0)),
            pl.BlockSpec((tn, D_MODEL), lambda i, j: (j, 0)),
            *cast_in,
        ],
        out_specs=(pl.BlockSpec((tm, tn), lambda i, j: (i, j)), *cast_out),
        scratch_shapes=[pltpu.VMEM((tm, D_MODEL), BF16)],
        compiler_params=pltpu.CompilerParams(
            dimension_semantics=("arbitrary", "arbitrary"), vmem_limit_bytes=VMEM_LIMIT),
        name="proj",
    )(x, g, w, *[wgt for wgt, _ in casts])
    return outs[0], outs[1:]


def _attend_heads(scores, apply_values, sinks):
    heads = range(len(scores))
    m = []
    for h in heads:
        mh = sinks[h]
        for s in scores[h]:
            mh = jnp.maximum(mh, jnp.max(s, axis=-1, keepdims=True))
        m.append(mh)
    ps = [[jnp.exp(s - m[h]) for s in scores[h]] for h in heads]
    den = []
    for h in heads:
        dh = jnp.exp(sinks[h] - m[h])
        for p in ps[h]:
            dh = dh + jnp.sum(p, axis=-1, keepdims=True)
        den.append(dh)
    outs = []
    for h in heads:
        o = None
        for p, pv in zip(ps[h], apply_values[h]):
            t = pv(p.astype(BF16))
            o = t if o is None else o + t
        outs.append(o * (1.0 / den[h]))
    return outs


def _head_slices():
    q_sl = [slice(hq * HEAD_DIM, (hq + 1) * HEAD_DIM) for hq in range(N_Q_HEADS)]
    kv_sl = [slice((hq // GQA_GROUP) * HEAD_DIM, (hq // GQA_GROUP + 1) * HEAD_DIM) for hq in range(N_Q_HEADS)]
    return q_sl, kv_sl


def _alibi_bias(dist, valid):
    slopes = np.array([_alibi_slope(hq) for hq in range(N_Q_HEADS)], np.float32)
    return np.where(valid[None], -slopes[:, None, None] * dist[None].astype(np.float32), -np.inf).astype(np.float32)


def _attn_prompt_kernel(sink_ref, bias_ref, q_ref, kp_ref, kc_ref, vp_ref, vc_ref, o_ref):
    w = WINDOW
    n_blk = q_ref.shape[0] // w
    kj = lax.broadcasted_iota(jnp.int32, (w, 2 * w), 1)
    no_prev = (kj < w) & (pl.program_id(1) == 0)
    q = q_ref[...] * (HEAD_DIM ** -0.5)
    kc, vc = kc_ref[...].astype(BF16), vc_ref[...].astype(BF16)
    ks = jnp.concatenate([kp_ref[...].astype(BF16), kc], axis=0)
    vs = jnp.concatenate([vp_ref[...].astype(BF16), vc], axis=0)
    hqs = range(N_Q_HEADS)
    q_sl, kv_sl = _head_slices()
    scores, values = [], []
    for s in range(n_blk):
        k2, v2 = ks[s * w:(s + 2) * w], vs[s * w:(s + 2) * w]
        qb = q[s * w:(s + 1) * w]
        for hq in hqs:
            sc = _nt(qb[:, q_sl[hq]].astype(BF16), k2[:, kv_sl[hq]]) + bias_ref[hq]
            scores.append([jnp.where(no_prev, -jnp.inf, sc) if s == 0 else sc])
            values.append([functools.partial(_mm, b=v2[:, kv_sl[hq]])])
    outs = _attend_heads(scores, values, [sink_ref[hq] for _ in range(n_blk) for hq in hqs])
    for s in range(n_blk):
        for hq in hqs:
            o_ref[s * w:(s + 1) * w, q_sl[hq]] = outs[s * N_Q_HEADS + hq]


def _attn_prompt(proj, sinks, batch, seq):
    rows = PROMPT_BLOCKS * WINDOW
    ns = seq // rows
    kcol, vcol = P_K // KV_WIDTH, P_V // KV_WIDTH

    def cur(c):
        return lambda b, i: (b * ns + i, c)

    def prev(c):
        return lambda b, i: (b * ns * PROMPT_BLOCKS + jnp.maximum(i * PROMPT_BLOCKS - 1, 0), c)

    ti = np.arange(WINDOW)[:, None]
    kj = np.arange(2 * WINDOW)[None, :]
    dist = ti - kj + WINDOW
    bias = _alibi_bias(dist, (dist >= 0) & (dist <= WINDOW))
    return pl.pallas_call(
        _attn_prompt_kernel,
        out_shape=jax.ShapeDtypeStruct((batch * seq, ATTN_WIDTH), F32),
        grid=(batch, ns),
        in_specs=[
            pl.BlockSpec(memory_space=pltpu.SMEM),
            _resident((N_Q_HEADS, WINDOW, 2 * WINDOW)),
            pl.BlockSpec((rows, ATTN_WIDTH), cur(P_Q // ATTN_WIDTH)),
            pl.BlockSpec((WINDOW, KV_WIDTH), prev(kcol)),
            pl.BlockSpec((rows, KV_WIDTH), cur(kcol)),
            pl.BlockSpec((WINDOW, KV_WIDTH), prev(vcol)),
            pl.BlockSpec((rows, KV_WIDTH), cur(vcol)),
        ],
        out_specs=pl.BlockSpec((rows, ATTN_WIDTH), lambda b, i: (b * ns + i, 0)),
        compiler_params=pltpu.CompilerParams(dimension_semantics=("arbitrary", "arbitrary")),
        name="attn_prompt",
    )(sinks, jnp.asarray(bias), proj, proj, proj, proj, proj)


def _attn_sample_kernel(sink_ref, q_ref, kn_ref, vn_ref, ckt_ref, cvt_ref, o_ref, nkt_ref, nvt_ref, *, seq):
    t, w = seq, WINDOW
    n_seq = q_ref.shape[0] // t
    ti = lax.broadcasted_iota(jnp.int32, (t, w), 0)
    cj = lax.broadcasted_iota(jnp.int32, (t, w), 1)
    dist_c = (ti - cj + w).astype(F32)
    valid_c = cj >= ti
    ti2 = lax.broadcasted_iota(jnp.int32, (t, t), 0)
    tj2 = lax.broadcasted_iota(jnp.int32, (t, t), 1)
    dist_n = (ti2 - tj2).astype(F32)
    valid_n = tj2 <= ti2
    hqs = range(N_Q_HEADS)
    q_sl, kv_sl = _head_slices()
    bias_c = [jnp.where(valid_c, -_alibi_slope(hq) * dist_c, -jnp.inf) for hq in hqs]
    bias_n = [jnp.where(valid_n, -_alibi_slope(hq) * dist_n, -jnp.inf) for hq in hqs]
    q_all = q_ref[...] * (HEAD_DIM ** -0.5)
    is_new = lax.broadcasted_iota(jnp.int32, (1, w), 1) >= w - t
    put = (cj == ti + (w - t)).astype(BF16)

    def place_new(x):
        hi = x.astype(BF16)
        rem = x - hi.astype(F32)
        mid = rem.astype(BF16)
        return _tn(hi, put) + _tn(mid, put) + _tn((rem - mid.astype(F32)).astype(BF16), put)

    scores, values = [], []
    for s in range(n_seq):
        rs = slice(s * t, (s + 1) * t)
        ckt, cvt = ckt_ref[s], cvt_ref[s]
        nkt_ref[s] = jnp.where(is_new, place_new(kn_ref[rs, :]), pltpu.roll(ckt, w - t, axis=1))
        nvt_ref[s] = jnp.where(is_new, place_new(vn_ref[rs, :]), pltpu.roll(cvt, w - t, axis=1))
        cktb, cvtb = ckt.astype(BF16), cvt.astype(BF16)
        knb, vnb = kn_ref[rs, :].astype(BF16), vn_ref[rs, :].astype(BF16)
        q = q_all[rs]
        qs = [q[:, q_sl[hq]].astype(BF16) for hq in hqs]
        scores += [[_mm(qs[hq], cktb[kv_sl[hq], :]) + bias_c[hq], _nt(qs[hq], knb[:, kv_sl[hq]]) + bias_n[hq]]
                   for hq in hqs]
        values += [[functools.partial(_nt, b=cvtb[kv_sl[hq], :]), functools.partial(_mm, b=vnb[:, kv_sl[hq]])]
                   for hq in hqs]
    outs = _attend_heads(scores, values, [sink_ref[hq] for _ in range(n_seq) for hq in hqs])
    for s in range(n_seq):
        for hq in hqs:
            o_ref[s * t:(s + 1) * t, q_sl[hq]] = outs[s * N_Q_HEADS + hq]


def _attn_sample(proj, sinks, cache_kt, cache_vt, batch, seq, n_seq):
    kcol, vcol = P_K // KV_WIDTH, P_V // KV_WIDTH
    rows = n_seq * seq
    win = jax.ShapeDtypeStruct((batch, KV_WIDTH, WINDOW), F32)
    return pl.pallas_call(
        functools.partial(_attn_sample_kernel, seq=seq),
        out_shape=(jax.ShapeDtypeStruct((batch * seq, ATTN_WIDTH), F32), win, win),
        grid=(batch // n_seq,),
        in_specs=[
            pl.BlockSpec(memory_space=pltpu.SMEM),
            pl.BlockSpec((rows, ATTN_WIDTH), lambda b: (b, P_Q // ATTN_WIDTH)),
            pl.BlockSpec((rows, KV_WIDTH), lambda b: (b, kcol)),
            pl.BlockSpec((rows, KV_WIDTH), lambda b: (b, vcol)),
            pl.BlockSpec((n_seq, KV_WIDTH, WINDOW), lambda b: (b, 0, 0)),
            pl.BlockSpec((n_seq, KV_WIDTH, WINDOW), lambda b: (b, 0, 0)),
        ],
        out_specs=(
            pl.BlockSpec((rows, ATTN_WIDTH), lambda b: (b, 0)),
            pl.BlockSpec((n_seq, KV_WIDTH, WINDOW), lambda b: (b, 0, 0)),
            pl.BlockSpec((n_seq, KV_WIDTH, WINDOW), lambda b: (b, 0, 0)),
        ),
        compiler_params=pltpu.CompilerParams(dimension_semantics=("arbitrary",)),
        name="attn_sample",
    )(sinks, proj, proj, proj, cache_kt, cache_vt)


def _seg_sum(x, bones):
    rows = x.shape[0]
    hi = x.astype(BF16).astype(F32)
    lo = x - hi
    groups = [slice(j * SEG, (j + 1) * SEG) for j in range(R_WIDTH // SEG)]
    lhs = jnp.concatenate([t[:, sl] for sl in groups for t in (hi, lo)], axis=0).astype(BF16)
    out = _mm(lhs, bones)
    return jnp.concatenate(
        [out[2 * j * rows:(2 * j + 1) * rows] + out[(2 * j + 1) * rows:(2 * j + 2) * rows]
         for j in range(len(groups))], axis=1)


def _wkv_kernel(pa_ref, pb_ref, pl_ref, prev_ref, s0_ref,
                mu_ref, mul_ref, w0_ref, a0_ref, kk_ref, ka_ref, rk_ref, lnw_ref, lnb_ref,
                wl_ref, bones_ref, *rest, chunk, n_sub, n_seq, n_steps, cast_transposed):
    n_cast = len(cast_transposed)
    cast_src, (y_ref, so_ref) = rest[:n_cast], rest[n_cast:n_cast + 2]
    cast_dst, (s_ref, carry_ref) = rest[n_cast + 2:2 * n_cast + 2], rest[2 * n_cast + 2:]
    _run_casts(cast_src, cast_dst, cast_transposed)

    step = pl.program_id(1)
    cs = chunk
    seq_rows = n_sub * cs
    rows = n_seq * seq_rows
    gc = 2 * cs
    hd = R_HEAD
    w = R_WIDTH
    seqs = range(n_seq)

    @pl.when(step == 0)
    def _init():
        s_ref[...] = jnp.zeros(s_ref.shape, F32)
        for q in seqs:
            carry_ref[q:q + 1, :] = prev_ref[q]
            for p in range(N_PAIRS):
                s_ref[q * N_PAIRS + p, 0:hd, 0:hd] = s0_ref[q, 2 * p]
                s_ref[q * N_PAIRS + p, hd:2 * hd, hd:2 * hd] = s0_ref[q, 2 * p + 1]

    row = lax.broadcasted_iota(jnp.int32, (rows, 1), 0)

    def token_shift(x, lo, hi, mu):
        shifted = pltpu.roll(x, 1, axis=0)
        for q in seqs:
            shifted = jnp.where(row == q * seq_rows, carry_ref[q:q + 1, lo:hi], shifted)
        return x + (shifted - x) * mu

    cols = jnp.concatenate([pa_ref[...], pb_ref[...]], axis=1)
    l_raw = pl_ref[...]
    xx = token_shift(cols, 0, 3 * w, mu_ref[...])
    xr, xk, xv = xx[:, 0:w], xx[:, w:2 * w], xx[:, 2 * w:3 * w]
    xl = token_shift(l_raw, 3 * w, SHIFT_COLS, mul_ref[...])
    for q in seqs:
        last = (q + 1) * seq_rows
        carry_ref[q:q + 1, 0:3 * w] = cols[last - 1:last, :]
        carry_ref[q:q + 1, 3 * w:] = l_raw[last - 1:last, :]

    lane_l = lax.broadcasted_iota(jnp.int32, (1, LORA_PAD), 1)
    act = jnp.where(lane_l < DECAY_LORA, jnp.tanh(xl),
                    jnp.where(lane_l < DECAY_LORA + AAA_LORA, xl,
                              jnp.where(lane_l < LORA_COLS, jax.nn.sigmoid(xl), 0.0)))
    up = _mm(act.astype(BF16), wl_ref[...])
    z = -(w0_ref[...] + up[:, 0:w])
    softplus = jnp.maximum(z, 0.0) + jnp.log1p(jnp.exp(-jnp.abs(z)))
    lwd = -jnp.exp(-softplus - 0.5)
    a = jax.nn.sigmoid(a0_ref[...] + up[:, w:2 * w])
    gate = up[:, 2 * w:3 * w]

    bones = bones_ref[...]
    kkn = xk * kk_ref[...]
    kk = kkn * lax.rsqrt(jnp.maximum(_seg_sum(kkn * kkn, bones), 1e-24))
    k2 = xk * (1.0 + (a - 1.0) * ka_ref[...])
    b = kk * a

    tri = (lax.broadcasted_iota(jnp.int32, (cs, cs), 0)
           >= lax.broadcasted_iota(jnp.int32, (cs, cs), 1)).astype(BF16)
    ri = lax.broadcasted_iota(jnp.int32, (gc, gc), 0)
    ci = lax.broadcasted_iota(jnp.int32, (gc, gc), 1)
    same_head = (ri >= cs) == (ci >= cs)
    strict = same_head & (ci < ri)
    incl = same_head & (ci <= ri)
    eye = (ri == ci).astype(F32)
    head0 = lax.broadcasted_iota(jnp.int32, (1, PAIR), 1) < hd

    def stack(x):
        return jnp.concatenate([jnp.where(head0, x, 0.0), jnp.where(head0, 0.0, x)], axis=0).astype(BF16)

    pairs = range(N_PAIRS)
    lanes = [slice(p * PAIR, (p + 1) * PAIR) for p in pairs]
    squarings = cs.bit_length() - 2

    chunks = []
    for sub in range(n_seq * n_sub):
        rs = slice(sub * cs, (sub + 1) * cs)
        lw_c = lwd[rs]
        lw_hi = lw_c.astype(BF16)
        lw_r = lw_c - lw_hi.astype(F32)
        lw_mid = lw_r.astype(BF16)
        cum = _mm(tri, lw_hi) + _mm(tri, lw_mid) + _mm(tri, (lw_r - lw_mid.astype(F32)).astype(BF16))
        cum_last = cum[cs - 1:cs, :]
        e_inv = jnp.exp(-cum)
        e_last = jnp.exp(cum_last - cum)
        kq = kk[rs] * jnp.exp(cum - lw_c)
        rq = xr[rs] * jnp.exp(cum)
        kd = k2[rs] * e_inv
        bd = b[rs] * e_inv
        kdp = k2[rs] * e_last
        bdp = b[rs] * e_last
        xv_c = xv[rs]
        kq_s = [stack(kq[:, sl]) for sl in lanes]
        bd_s = [stack(bd[:, sl]) for sl in lanes]
        kd_s = [stack(kd[:, sl]) for sl in lanes]
        rq_s = [stack(rq[:, sl]) for sl in lanes]
        probes = [jnp.concatenate([kq_s[p], rq_s[p]], axis=0) for p in pairs]
        if gc % PAIR == 0:
            a_all = [_nt(probes[p], jnp.concatenate([bd_s[p], kd_s[p]], axis=0)) for p in pairs]
            a_kb = [jnp.where(strict, t[0:gc, 0:gc], 0.0) for t in a_all]
            a_kk = [jnp.where(strict, t[0:gc, gc:2 * gc], 0.0).astype(BF16) for t in a_all]
            a_rb = [jnp.where(incl, t[gc:2 * gc, 0:gc], 0.0).astype(BF16) for t in a_all]
            a_rk = [jnp.where(incl, t[gc:2 * gc, gc:2 * gc], 0.0).astype(BF16) for t in a_all]
        else:
            a_kb = [jnp.where(strict, _nt(kq_s[p], bd_s[p]), 0.0) for p in pairs]
            a_kk = [jnp.where(strict, _nt(kq_s[p], kd_s[p]), 0.0).astype(BF16) for p in pairs]
            a_rb = [jnp.where(incl, _nt(rq_s[p], bd_s[p]), 0.0).astype(BF16) for p in pairs]
            a_rk = [jnp.where(incl, _nt(rq_s[p], kd_s[p]), 0.0).astype(BF16) for p in pairs]
        inv = [eye - t for t in a_kb]
        if squarings >= 1:
            apow_b = [t.astype(BF16) for t in a_kb]
            apow_b = [_mm(t, t).astype(BF16) for t in apow_b]
            for _ in range(squarings - 1):
                both = [_mm(jnp.concatenate([apow_b[p], inv[p].astype(BF16)], axis=0), apow_b[p]) for p in pairs]
                apow_b = [t[0:gc].astype(BF16) for t in both]
                inv = [inv[p] + both[p][gc:2 * gc] for p in pairs]
            inv = [inv[p] + _mm(inv[p].astype(BF16), apow_b[p]) for p in pairs]
        chunks.append(dict(
            probes=probes, a_kk=a_kk, a_rk=a_rk, a_rb=a_rb, inv=[t.astype(BF16) for t in inv],
            v_s=[stack(xv_c[:, sl]) for sl in lanes],
            upd=[jnp.concatenate([stack(kdp[:, sl]), stack(bdp[:, sl])], axis=0) for sl in lanes],
            p_last=jnp.exp(cum_last)))

    sp = [(q, p) for q in seqs for p in pairs]
    state = {(q, p): s_ref[q * N_PAIRS + p] for q, p in sp}
    y_rows = [None] * (n_seq * n_sub)
    for level in range(n_sub):
        ch = {q: chunks[q * n_sub + level] for q in seqs}
        s_b = {k: state[k].astype(BF16) for k in sp}
        state_t = {(q, p): _nt(ch[q]["probes"][p], s_b[q, p]) for q, p in sp}
        rhs = {(q, p): state_t[q, p][0:gc] + _mm(ch[q]["a_kk"][p], ch[q]["v_s"][p]) for q, p in sp}
        u_b = {(q, p): _mm(ch[q]["inv"][p], rhs[q, p].astype(BF16)).astype(BF16) for q, p in sp}
        if gc % PAIR == 0:
            y_s = {(q, p): state_t[q, p][gc:2 * gc]
                   + _mm(jnp.concatenate([ch[q]["a_rk"][p], -ch[q]["a_rb"][p]], axis=1),
                         jnp.concatenate([ch[q]["v_s"][p], u_b[q, p]], axis=0)) for q, p in sp}
        else:
            y_s = {(q, p): state_t[q, p][gc:2 * gc] + _mm(ch[q]["a_rk"][p], ch[q]["v_s"][p])
                   - _mm(ch[q]["a_rb"][p], u_b[q, p]) for q, p in sp}
        state = {(q, p): state[q, p] * ch[q]["p_last"][:, lanes[p]]
                 + _tn(jnp.concatenate([ch[q]["v_s"][p], -u_b[q, p]], axis=0), ch[q]["upd"][p]) for q, p in sp}
        for q in seqs:
            y_rows[q * n_sub + level] = jnp.concatenate(
                [y_s[q, p][0:cs] + y_s[q, p][cs:gc] for p in pairs], axis=1)
    for q, p in sp:
        s_ref[q * N_PAIRS + p] = state[q, p]

    y = jnp.concatenate(y_rows, axis=0) if len(y_rows) > 1 else y_rows[0]
    mean = _seg_sum(y, bones) * (1.0 / hd)
    d = y - mean
    var = _seg_sum(d * d, bones) * (1.0 / hd)
    yn = d * lax.rsqrt(var + GN_EPS) * lnw_ref[...] + lnb_ref[...]
    bonus = _seg_sum(xr * k2 * rk_ref[...], bones) * xv
    y_ref[...] = (yn + bonus) * gate

    @pl.when(step == n_steps - 1)
    def _fin():
        for q, p in sp:
            so_ref[q, 2 * p] = s_ref[q * N_PAIRS + p, 0:hd, 0:hd]
            so_ref[q, 2 * p + 1] = s_ref[q * N_PAIRS + p, hd:2 * hd, hd:2 * hd]


def _wkv(proj, prev0, s0, params, wl, bones, batch, seq, chunk, n_sub, n_seq=1, casts=()):
    ns = seq // (chunk * n_sub)
    assert n_seq == 1 or ns == 1, "several sequences per step only when a step covers them whole"
    rows = n_seq * chunk * n_sub
    nb = batch // n_seq
    rw = R_WIDTH

    def col(cb):
        return lambda b, c: (b * ns + c, cb)

    const2 = lambda b, c: (0, 0)
    vec = lambda n: pl.BlockSpec((1, n), const2)
    cast_in, cast_out, cast_shapes, cast_tr = _cast_specs(casts, nb * ns, lambda b, c: b * ns + c)
    mu, mul, w0, a0, k_k, k_a, r_k, ln_w, ln_b = params
    outs = pl.pallas_call(
        functools.partial(_wkv_kernel, chunk=chunk, n_sub=n_sub, n_seq=n_seq, n_steps=ns,
                          cast_transposed=cast_tr),
        out_shape=(jax.ShapeDtypeStruct((batch * seq, rw), F32),
                   jax.ShapeDtypeStruct((batch, R_HEADS, R_HEAD, R_HEAD), F32), *cast_shapes),
        grid=(nb, ns),
        in_specs=[
            pl.BlockSpec((rows, C_BLK), col(P_C // C_BLK)),
            pl.BlockSpec((rows, C_BLK), col(P_C // C_BLK + 1)),
            pl.BlockSpec((rows, LORA_PAD), col(P_L // LORA_PAD)),
            pl.BlockSpec((n_seq, 1, SHIFT_COLS), lambda b, c: (b, 0, 0)),
            pl.BlockSpec((n_seq, R_HEADS, R_HEAD, R_HEAD), lambda b, c: (b, 0, 0, 0)),
            vec(3 * rw), vec(LORA_PAD), vec(rw), vec(rw), vec(rw), vec(rw), vec(rw), vec(rw), vec(rw),
            pl.BlockSpec((LORA_PAD, 3 * rw), const2),
            pl.BlockSpec((SEG, SEG), const2),
            *cast_in,
        ],
        out_specs=(
            pl.BlockSpec((rows, rw), lambda b, c: (b * ns + c, 0)),
            pl.BlockSpec((n_seq, R_HEADS, R_HEAD, R_HEAD), lambda b, c: (b, 0, 0, 0)),
            *cast_out,
        ),
        scratch_shapes=[pltpu.VMEM((n_seq * N_PAIRS, PAIR, PAIR), F32), pltpu.VMEM((n_seq, SHIFT_COLS), F32)],
        compiler_params=pltpu.CompilerParams(
            dimension_semantics=("arbitrary", "arbitrary"), vmem_limit_bytes=VMEM_LIMIT),
        name="wkv",
    )(proj, proj, proj, prev0, s0, mu, mul, w0, a0, k_k, k_a, r_k, ln_w, ln_b, wl, bones,
      *[wgt for wgt, _ in casts])
    return outs[0], outs[1], outs[2:]


def _mix_kernel(x_ref, ya_ref, yr_ref, g_ref, wg_ref, wba_ref, wbr_ref, wo_ref, o_ref):
    x = x_ref[...]
    h = _rms(x, g_ref[...]).astype(BF16)
    ya = ya_ref[...].astype(BF16)
    yr = yr_ref[...].astype(BF16)
    acc = x
    for c in range(D_MODEL // MIX_CHUNK):
        ca = slice(c * MIX_CHUNK, (c + 1) * MIX_CHUNK)
        cr = slice(D_MODEL + c * MIX_CHUNK, D_MODEL + (c + 1) * MIX_CHUNK)
        mixed = (jax.nn.sigmoid(_mm(h, wg_ref[:, ca])) * _mm(ya, wba_ref[:, ca])
                 + jax.nn.sigmoid(_mm(h, wg_ref[:, cr])) * _mm(yr, wbr_ref[:, ca]))
        acc = acc + _mm(mixed.astype(BF16), wo_ref[ca, :])
    o_ref[...] = acc


def _resident(shape):
    return pl.BlockSpec(shape, lambda *_: (0,) * len(shape), pipeline_mode=pl.Buffered(1))


def _mix(x, ya, yr, g, wg, wba, wbr, wo, tm):
    m = x.shape[0]
    row = lambda i: (i, 0)
    return pl.pallas_call(
        _mix_kernel,
        out_shape=jax.ShapeDtypeStruct((m, D_MODEL), F32),
        grid=(m // tm,),
        in_specs=[
            pl.BlockSpec((tm, D_MODEL), row),
            pl.BlockSpec((tm, ATTN_WIDTH), row),
            pl.BlockSpec((tm, R_WIDTH), row),
            pl.BlockSpec((1, D_MODEL), lambda i: (0, 0)),
            _resident((D_MODEL, 2 * D_MODEL)),
            _resident((ATTN_WIDTH, D_MODEL)),
            _resident((R_WIDTH, D_MODEL)),
            _resident((D_MODEL, D_MODEL)),
        ],
        out_specs=pl.BlockSpec((tm, D_MODEL), row),
        compiler_params=pltpu.CompilerParams(
            dimension_semantics=("arbitrary",), vmem_limit_bytes=VMEM_LIMIT),
        name="mix",
    )(x, ya, yr, g, wg, wba, wbr, wo)


def _ffn_kernel(x_ref, g_ref, wu_ref, wd_ref, o_ref, h_ref):
    @pl.when(pl.program_id(1) == 0)
    def _():
        x = x_ref[...]
        h_ref[...] = _rms(x, g_ref[...]).astype(BF16)
        o_ref[...] = x

    u = _mm(h_ref[...], wu_ref[...])
    o_ref[...] += _mm(jnp.square(jnp.maximum(u, 0.0)).astype(BF16), wd_ref[...])


def _ffn_vmem_bytes(tm, tk):
    return 2 * 2 * tm * D_MODEL * 4 + 2 * 2 * D_MODEL * tk * 2 + tm * D_MODEL * 2 + tm * tk * (4 + 2)


def _ffn(x, g, wu, wd, tm, tk):
    m = x.shape[0]
    vmem = max(VMEM_LIMIT, _ffn_vmem_bytes(tm, tk))
    assert vmem <= V7X_VMEM_BYTES, (tm, tk)
    return pl.pallas_call(
        _ffn_kernel,
        out_shape=jax.ShapeDtypeStruct((m, D_MODEL), F32),
        grid=(m // tm, D_FF // tk),
        in_specs=[
            pl.BlockSpec((tm, D_MODEL), lambda i, k: (i, 0)),
            pl.BlockSpec((1, D_MODEL), lambda i, k: (0, 0)),
            pl.BlockSpec((D_MODEL, tk), lambda i, k: (0, k)),
            pl.BlockSpec((tk, D_MODEL), lambda i, k: (k, 0)),
        ],
        out_specs=pl.BlockSpec((tm, D_MODEL), lambda i, k: (i, 0)),
        scratch_shapes=[pltpu.VMEM((tm, D_MODEL), BF16)],
        compiler_params=pltpu.CompilerParams(
            dimension_semantics=("arbitrary", "arbitrary"), vmem_limit_bytes=vmem),
        name="ffn",
    )(x, g, wu, wd)


def _ple_kernel(x_ref, pe_ref, wg_ref, wp_ref, g_ref, o_ref):
    x = x_ref[...]
    gate = jax.nn.sigmoid(_mm(x.astype(BF16), wg_ref[...]))
    x = x + gate * _mm(pe_ref[...].astype(BF16), wp_ref[...])
    o_ref[...] = _rms(x, g_ref[...])


def _ple(x, pe, wg, wp, g, tm):
    m = x.shape[0]
    row = lambda i: (i, 0)
    return pl.pallas_call(
        _ple_kernel,
        out_shape=jax.ShapeDtypeStruct((m, D_MODEL), F32),
        grid=(m // tm,),
        in_specs=[
            pl.BlockSpec((tm, D_MODEL), row),
            pl.BlockSpec((tm, PLE_DIM), row),
            _resident((D_MODEL, D_MODEL)),
            _resident((PLE_DIM, D_MODEL)),
            pl.BlockSpec((1, D_MODEL), lambda i: (0, 0)),
        ],
        out_specs=pl.BlockSpec((tm, D_MODEL), row),
        compiler_params=pltpu.CompilerParams(
            dimension_semantics=("arbitrary",), vmem_limit_bytes=VMEM_LIMIT),
        name="ple",
    )(x, pe, wg, wp, g)


def _dense_tail(x, ya, yr, pe, wts, tiles):
    (g_mix, w_gates, wba, wbr, wo, g_ffn, wu, wd, wg, wp, g_fin) = wts
    x = _mix(x, ya, yr, g_mix, w_gates, wba, wbr, wo, tiles["mix_m"])
    x = _ffn(x, g_ffn, wu, wd, tiles["ffn_m"], tiles["ffn_k"])
    return _ple(x, pe, wg, wp, g_fin, tiles["ple_m"])


def _shift_out(proj, batch, seq):
    last = proj.reshape(batch, seq, P_COLS)[:, -1]
    return last[:, P_C:P_C + RWKV_COLS][None]


def kernel(x_prompt, x_sample, cache_k_win, cache_v_win, state_wkv, state_shift, p_prompt, p_sample,
           norm_mix, w_in, attn_sinks, rwkv_mu, rwkv_w0, rwkv_w2, rwkv_a0, rwkv_a2, rwkv_g2,
           rwkv_k_k, rwkv_k_a, rwkv_r_k, rwkv_ln_w, rwkv_ln_b, w_branch_attn, w_branch_rwkv,
           w_out, norm_ffn, w_ff_up, w_ff_down, w_ple_proj, w_ple_gate, norm_final):
    assert w_in.shape[0] == 1, "single-layer step"
    bp, tp = x_prompt.shape[0], x_prompt.shape[1]
    bs, ts = x_sample.shape[0], x_sample.shape[1]
    rw = R_WIDTH

    wl = jnp.zeros((LORA_PAD, 3 * rw), F32)
    wl = wl.at[0:DECAY_LORA, 0:rw].set(rwkv_w2[0])
    wl = wl.at[DECAY_LORA:DECAY_LORA + AAA_LORA, rw:2 * rw].set(rwkv_a2[0])
    wl = wl.at[DECAY_LORA + AAA_LORA:LORA_COLS, 2 * rw:3 * rw].set(rwkv_g2[0])
    wl = wl.astype(BF16)
    seg_id = np.arange(SEG) // R_HEAD
    bones = jnp.asarray(seg_id[:, None] == seg_id[None, :], BF16)
    mu = rwkv_mu[0]
    row = lambda v: v.reshape(1, -1)
    wkv_params = (row(mu[:3 * rw]), row(jnp.pad(mu[3 * rw:], (0, LORA_PAD - LORA_COLS))),
                  row(rwkv_w0[0]), row(rwkv_a0[0]), row(rwkv_k_k[0]), row(rwkv_k_a[0]),
                  row(rwkv_r_k[0]), row(rwkv_ln_w[0]), row(rwkv_ln_b[0]))
    g_mix = row(norm_mix[0])
    sinks = attn_sinks[0]

    ms = bs * ts
    xs = x_sample.reshape(ms, D_MODEL)
    w_in_t = jnp.transpose(w_in[0])
    proj_s, w_in_b = _proj_cast(xs, g_mix, w_in_t, 1024)

    tiles_p = dict(mix_m=256, ffn_m=1024, ffn_k=1024, ple_m=512)
    xp = x_prompt.reshape(bp * tp, D_MODEL)
    proj_p, (wu, wd) = _proj(xp, g_mix, w_in_b, 1024, 1280, casts=((w_ff_up[0], None), (w_ff_down[0], None)))
    ya_p = _attn_prompt(proj_p, sinks, bp, tp)
    yr_p, s_p, (wo, wg, wba, wbr, w_gates) = _wkv(
        proj_p, jnp.zeros((bp, 1, SHIFT_COLS), F32), jnp.zeros((bp, R_HEADS, R_HEAD, R_HEAD), F32),
        wkv_params, wl, bones, bp, tp, 64, 4,
        casts=((w_out[0], None), (w_ple_gate[0], None),
               (w_branch_attn[0], None), (w_branch_rwkv[0], None), (w_in_t, (P_GATES, 2 * D_MODEL))))
    dense = (g_mix, w_gates, wba, wbr, wo, row(norm_ffn[0]), wu, wd, wg, w_ple_proj[0].astype(BF16),
             row(norm_final))
    yp = _dense_tail(xp, ya_p, yr_p, p_prompt[0].reshape(bp * tp, PLE_DIM), dense, tiles_p)
    pp3 = proj_p.reshape(bp, tp, P_COLS)[:, -WINDOW:]
    k_p = pp3[:, :, P_K:P_K + KV_WIDTH].reshape(1, bp, WINDOW, N_KV_HEADS, HEAD_DIM)
    v_p = pp3[:, :, P_V:P_V + KV_WIDTH].reshape(1, bp, WINDOW, N_KV_HEADS, HEAD_DIM)

    tiles_s = dict(mix_m=ms, ffn_m=ms, ffn_k=2048, ple_m=ms)
    ck_t = jnp.transpose(cache_k_win[0], (0, 2, 3, 1)).reshape(bs, KV_WIDTH, WINDOW)
    cv_t = jnp.transpose(cache_v_win[0], (0, 2, 3, 1)).reshape(bs, KV_WIDTH, WINDOW)
    ya_s, nk_t, nv_t = _attn_sample(proj_s, sinks, ck_t, cv_t, bs, ts, SAMPLE_GROUP)
    prev_s = jnp.pad(state_shift[0], ((0, 0), (0, LORA_PAD - LORA_COLS))).reshape(bs, 1, SHIFT_COLS)
    yr_s, s_s, _ = _wkv(proj_s, prev_s, state_wkv[0], wkv_params, wl, bones, bs, ts, ts, 1, n_seq=SAMPLE_GROUP)
    ys = _dense_tail(xs, ya_s, yr_s, p_sample[0].reshape(ms, PLE_DIM), dense, tiles_s)

    return (yp.reshape(bp, tp, D_MODEL), ys.reshape(bs, ts, D_MODEL),
            k_p, v_p, s_p[None], _shift_out(proj_p, bp, tp),
            jnp.transpose(nk_t.reshape(bs, N_KV_HEADS, HEAD_DIM, WINDOW), (0, 3, 1, 2))[None],
            jnp.transpose(nv_t.reshape(bs, N_KV_HEADS, HEAD_DIM, WINDOW), (0, 3, 1, 2))[None],
            s_s[None], _shift_out(proj_s, bs, ts))
```

```python
import functools

import numpy as np
import jax
import jax.numpy as jnp
from jax import lax
from jax.experimental import pallas as pl
from jax.experimental.pallas import tpu as pltpu

F32 = jnp.float32
BF16 = jnp.bfloat16

D_MODEL = 2048
PLE_DIM = 256
HEAD_DIM = 64
N_Q_HEADS = 16
N_KV_HEADS = 4
GQA_GROUP = 4
ATTN_WIDTH = 1024
KV_WIDTH = 256
WINDOW = 128
ALIBI_MAX = 8.0
R_HEAD = 64
R_WIDTH = 1024
R_HEADS = 16
DECAY_LORA = 64
AAA_LORA = 64
GATE_LORA = 160
LORA_COLS = DECAY_LORA + AAA_LORA + GATE_LORA
LORA_PAD = 512
RWKV_COLS = 3 * R_WIDTH + LORA_COLS
D_FF = 4 * D_MODEL
NORM_EPS = 1e-6
GN_EPS = 64e-5

P_Q = 0
P_K = ATTN_WIDTH
P_V = ATTN_WIDTH + KV_WIDTH
P_C = ATTN_WIDTH + 2 * KV_WIDTH
P_L = P_C + 3 * R_WIDTH
P_COLS = P_L + LORA_PAD
P_GATES = P_C + RWKV_COLS
C_BLK = P_C
SHIFT_COLS = 3 * R_WIDTH + LORA_PAD

PAIR = 128
N_PAIRS = R_WIDTH // PAIR
SEG = 256
MIX_CHUNK = 512
SAMPLE_GROUP = 8
PROMPT_BLOCKS = 4

V7X_VMEM_BYTES = 64 * 1024 * 1024
VMEM_LIMIT = 56 * 1024 * 1024


def _mm(a, b):
    return jnp.dot(a, b, preferred_element_type=F32)


def _nt(a, b):
    return lax.dot_general(a, b, (((1,), (1,)), ((), ())), preferred_element_type=F32)


def _tn(a, b):
    return lax.dot_general(a, b, (((0,), (0,)), ((), ())), preferred_element_type=F32)


def _rms(x, g):
    ms = jnp.mean(x * x, axis=-1, keepdims=True)
    return x * lax.rsqrt(ms + NORM_EPS) * g


def _alibi_slope(hq):
    return float(2.0 ** (-ALIBI_MAX * (hq + 1) / N_Q_HEADS))


def _cast_specs(casts, n_steps, step_of):
    in_specs, out_specs, shapes = [], [], []
    for wgt, span in casts:
        start, count = (0, wgt.shape[0]) if span is None else span
        blk, width = count // n_steps, wgt.shape[1]
        assert blk * n_steps == count, (wgt.shape, span, n_steps)
        if span is None:
            assert blk % 16 == 0, blk
            in_specs.append(pl.BlockSpec((blk, width), lambda *g: (step_of(*g), 0)))
            out_specs.append(pl.BlockSpec((blk, width), lambda *g: (step_of(*g), 0)))
            shapes.append(jax.ShapeDtypeStruct((count, width), BF16))
        else:
            assert blk % 128 == 0 and start % 8 == 0, span
            in_specs.append(pl.BlockSpec(
                (pl.Element(blk), pl.Element(width)),
                lambda *g, start=start, blk=blk: (pl.multiple_of(start + blk * step_of(*g), 8), 0)))
            out_specs.append(pl.BlockSpec((width, blk), lambda *g: (0, step_of(*g))))
            shapes.append(jax.ShapeDtypeStruct((width, count), BF16))
    return in_specs, out_specs, shapes, tuple(span is not None for _, span in casts)


def _run_casts(srcs, dsts, transposed):
    for src, dst, tr in zip(srcs, dsts, transposed):
        dst[...] = (src[...].T if tr else src[...]).astype(BF16)


def _proj_kernel(x_ref, g_ref, w_ref, o_ref, h_ref):
    @pl.when(pl.program_id(1) == 0)
    def _():
        h_ref[...] = _rms(x_ref[...], g_ref[...]).astype(BF16)

    o_ref[...] = _nt(h_ref[...], w_ref[...])


def _proj_cast_kernel(x_ref, g_ref, w_ref, o_ref, wb_ref, h_ref):
    @pl.when(pl.program_id(0) == 0)
    def _():
        h_ref[...] = _rms(x_ref[...], g_ref[...]).astype(BF16)

    wb = w_ref[...].astype(BF16)
    wb_ref[...] = wb
    o_ref[...] = _nt(h_ref[...], wb)


def _proj_cast(x, g, wt_f32, tn):
    m = x.shape[0]
    return pl.pallas_call(
        _proj_cast_kernel,
        out_shape=(jax.ShapeDtypeStruct((m, P_COLS), F32), jax.ShapeDtypeStruct((P_COLS, D_MODEL), BF16)),
        grid=(P_COLS // tn,),
        in_specs=[
            pl.BlockSpec((m, D_MODEL), lambda j: (0, 0)),
            pl.BlockSpec((1, D_MODEL), lambda j: (0, 0)),
            pl.BlockSpec((tn, D_MODEL), lambda j: (j, 0)),
        ],
        out_specs=(pl.BlockSpec((m, tn), lambda j: (0, j)), pl.BlockSpec((tn, D_MODEL), lambda j: (j, 0))),
        scratch_shapes=[pltpu.VMEM((m, D_MODEL), BF16)],
        compiler_params=pltpu.CompilerParams(
            dimension_semantics=("arbitrary",), vmem_limit_bytes=VMEM_LIMIT),
        name="proj_cast",
    )(x, g, wt_f32)


def _proj_vmem_bytes(tm, tn):
    return 2 * tm * D_MODEL * 4 + tm * D_MODEL * 2 + 2 * tn * D_MODEL * 2 + 4 * tm * tn * 4


def _proj(x, g, w, tm, tn):
    m = x.shape[0]
    vmem = max(VMEM_LIMIT, _proj_vmem_bytes(tm, tn))
    assert vmem <= V7X_VMEM_BYTES, (tm, tn)
    return pl.pallas_call(
        _proj_kernel,
        out_shape=jax.ShapeDtypeStruct((m, P_COLS), F32),
        grid=(m // tm, P_COLS // tn),
        in_specs=[
            pl.BlockSpec((tm, D_MODEL), lambda i, j: (i, 0)),
            pl.BlockSpec((1, D_MODEL), lambda i, j: (0, 0)),
            pl.BlockSpec((tn, D_MODEL), lambda i, j: (j, 0)),
        ],
        out_specs=pl.BlockSpec((tm, tn), lambda i, j: (i, j)),
        scratch_shapes=[pltpu.VMEM((tm, D_MODEL), BF16)],
        compiler_params=pltpu.CompilerParams(
            dimension_semantics=("arbitrary", "arbitrary"), vmem_limit_bytes=vmem),
        name="proj",
    )(x, g, w)


def _attend_heads(scores, apply_values, sinks):
    heads = range(len(scores))
    m = []
    for h in heads:
        mh = sinks[h]
        for s in scores[h]:
            mh = jnp.maximum(mh, jnp.max(s, axis=-1, keepdims=True))
        m.append(mh)
    ps = [[jnp.exp(s - m[h]) for s in scores[h]] for h in heads]
    den = []
    for h in heads:
        dh = jnp.exp(sinks[h] - m[h])
        for p in ps[h]:
            dh = dh + jnp.sum(p, axis=-1, keepdims=True)
        den.append(dh)
    outs = []
    for h in heads:
        o = None
        for p, pv in zip(ps[h], apply_values[h]):
            t = pv(p.astype(BF16))
            o = t if o is None else o + t
        outs.append(o * (1.0 / den[h]))
    return outs


def _head_slices():
    q_sl = [slice(hq * HEAD_DIM, (hq + 1) * HEAD_DIM) for hq in range(N_Q_HEADS)]
    kv_sl = [slice((hq // GQA_GROUP) * HEAD_DIM, (hq // GQA_GROUP + 1) * HEAD_DIM) for hq in range(N_Q_HEADS)]
    return q_sl, kv_sl


def _alibi_bias(dist, valid):
    slopes = np.array([_alibi_slope(hq) for hq in range(N_Q_HEADS)], np.float32)
    return np.where(valid[None], -slopes[:, None, None] * dist[None].astype(np.float32), -np.inf).astype(np.float32)


def _attn_prompt_kernel(sink_ref, bias_ref, q_ref, kp_ref, kc_ref, vp_ref, vc_ref, o_ref):
    w = WINDOW
    n_blk = q_ref.shape[0] // w
    kj = lax.broadcasted_iota(jnp.int32, (w, 2 * w), 1)
    no_prev = (kj < w) & (pl.program_id(1) == 0)
    q = q_ref[...] * (HEAD_DIM ** -0.5)
    kc, vc = kc_ref[...].astype(BF16), vc_ref[...].astype(BF16)
    ks = jnp.concatenate([kp_ref[...].astype(BF16), kc], axis=0)
    vs = jnp.concatenate([vp_ref[...].astype(BF16), vc], axis=0)
    hqs = range(N_Q_HEADS)
    q_sl, kv_sl = _head_slices()
    scores, values = [], []
    for s in range(n_blk):
        k2, v2 = ks[s * w:(s + 2) * w], vs[s * w:(s + 2) * w]
        qb = q[s * w:(s + 1) * w]
        for hq in hqs:
            sc = _nt(qb[:, q_sl[hq]].astype(BF16), k2[:, kv_sl[hq]]) + bias_ref[hq]
            scores.append([jnp.where(no_prev, -jnp.inf, sc) if s == 0 else sc])
            values.append([functools.partial(_mm, b=v2[:, kv_sl[hq]])])
    outs = _attend_heads(scores, values, [sink_ref[hq] for _ in range(n_blk) for hq in hqs])
    for s in range(n_blk):
        for hq in hqs:
            o_ref[s * w:(s + 1) * w, q_sl[hq]] = outs[s * N_Q_HEADS + hq]


def _attn_prompt(proj, sinks, batch, seq):
    rows = PROMPT_BLOCKS * WINDOW
    ns = seq // rows
    kcol, vcol = P_K // KV_WIDTH, P_V // KV_WIDTH

    def cur(c):
        return lambda b, i: (b * ns + i, c)

    def prev(c):
        return lambda b, i: (b * ns * PROMPT_BLOCKS + jnp.maximum(i * PROMPT_BLOCKS - 1, 0), c)

    ti = np.arange(WINDOW)[:, None]
    kj = np.arange(2 * WINDOW)[None, :]
    dist = ti - kj + WINDOW
    bias = _alibi_bias(dist, (dist >= 0) & (dist <= WINDOW))
    return pl.pallas_call(
        _attn_prompt_kernel,
        out_shape=jax.ShapeDtypeStruct((batch * seq, ATTN_WIDTH), F32),
        grid=(batch, ns),
        in_specs=[
            pl.BlockSpec(memory_space=pltpu.SMEM),
            _resident((N_Q_HEADS, WINDOW, 2 * WINDOW)),
            pl.BlockSpec((rows, ATTN_WIDTH), cur(P_Q // ATTN_WIDTH)),
            pl.BlockSpec((WINDOW, KV_WIDTH), prev(kcol)),
            pl.BlockSpec((rows, KV_WIDTH), cur(kcol)),
            pl.BlockSpec((WINDOW, KV_WIDTH), prev(vcol)),
            pl.BlockSpec((rows, KV_WIDTH), cur(vcol)),
        ],
        out_specs=pl.BlockSpec((rows, ATTN_WIDTH), lambda b, i: (b * ns + i, 0)),
        compiler_params=pltpu.CompilerParams(dimension_semantics=("arbitrary", "arbitrary")),
        name="attn_prompt",
    )(sinks, jnp.asarray(bias), proj, proj, proj, proj, proj)


def _attn_sample_kernel(sink_ref, q_ref, kn_ref, vn_ref, ckt_ref, cvt_ref, o_ref, nkt_ref, nvt_ref, *, seq):
    t, w = seq, WINDOW
    n_seq = q_ref.shape[0] // t
    ti = lax.broadcasted_iota(jnp.int32, (t, w), 0)
    cj = lax.broadcasted_iota(jnp.int32, (t, w), 1)
    dist_c = (ti - cj + w).astype(F32)
    valid_c = cj >= ti
    ti2 = lax.broadcasted_iota(jnp.int32, (t, t), 0)
    tj2 = lax.broadcasted_iota(jnp.int32, (t, t), 1)
    dist_n = (ti2 - tj2).astype(F32)
    valid_n = tj2 <= ti2
    hqs = range(N_Q_HEADS)
    q_sl, kv_sl = _head_slices()
    bias_c = [jnp.where(valid_c, -_alibi_slope(hq) * dist_c, -jnp.inf) for hq in hqs]
    bias_n = [jnp.where(valid_n, -_alibi_slope(hq) * dist_n, -jnp.inf) for hq in hqs]
    q_all = q_ref[...] * (HEAD_DIM ** -0.5)
    is_new = lax.broadcasted_iota(jnp.int32, (1, w), 1) >= w - t
    put = (cj == ti + (w - t)).astype(BF16)

    def place_new(x):
        hi = x.astype(BF16)
        rem = x - hi.astype(F32)
        mid = rem.astype(BF16)
        return _tn(hi, put) + _tn(mid, put) + _tn((rem - mid.astype(F32)).astype(BF16), put)

    scores, values = [], []
    for s in range(n_seq):
        rs = slice(s * t, (s + 1) * t)
        ckt, cvt = ckt_ref[s], cvt_ref[s]
        nkt_ref[s] = jnp.where(is_new, place_new(kn_ref[rs, :]), pltpu.roll(ckt, w - t, axis=1))
        nvt_ref[s] = jnp.where(is_new, place_new(vn_ref[rs, :]), pltpu.roll(cvt, w - t, axis=1))
        cktb, cvtb = ckt.astype(BF16), cvt.astype(BF16)
        knb, vnb = kn_ref[rs, :].astype(BF16), vn_ref[rs, :].astype(BF16)
        q = q_all[rs]
        qs = [q[:, q_sl[hq]].astype(BF16) for hq in hqs]
        scores += [[_mm(qs[hq], cktb[kv_sl[hq], :]) + bias_c[hq], _nt(qs[hq], knb[:, kv_sl[hq]]) + bias_n[hq]]
                   for hq in hqs]
        values += [[functools.partial(_nt, b=cvtb[kv_sl[hq], :]), functools.partial(_mm, b=vnb[:, kv_sl[hq]])]
                   for hq in hqs]
    outs = _attend_heads(scores, values, [sink_ref[hq] for _ in range(n_seq) for hq in hqs])
    for s in range(n_seq):
        for hq in hqs:
            o_ref[s * t:(s + 1) * t, q_sl[hq]] = outs[s * N_Q_HEADS + hq]


def _attn_sample(proj, sinks, cache_kt, cache_vt, batch, seq, n_seq):
    kcol, vcol = P_K // KV_WIDTH, P_V // KV_WIDTH
    rows = n_seq * seq
    win = jax.ShapeDtypeStruct((batch, KV_WIDTH, WINDOW), F32)
    return pl.pallas_call(
        functools.partial(_attn_sample_kernel, seq=seq),
        out_shape=(jax.ShapeDtypeStruct((batch * seq, ATTN_WIDTH), F32), win, win),
        grid=(batch // n_seq,),
        in_specs=[
            pl.BlockSpec(memory_space=pltpu.SMEM),
            pl.BlockSpec((rows, ATTN_WIDTH), lambda b: (b, P_Q // ATTN_WIDTH)),
            pl.BlockSpec((rows, KV_WIDTH), lambda b: (b, kcol)),
            pl.BlockSpec((rows, KV_WIDTH), lambda b: (b, vcol)),
            pl.BlockSpec((n_seq, KV_WIDTH, WINDOW), lambda b: (b, 0, 0)),
            pl.BlockSpec((n_seq, KV_WIDTH, WINDOW), lambda b: (b, 0, 0)),
        ],
        out_specs=(
            pl.BlockSpec((rows, ATTN_WIDTH), lambda b: (b, 0)),
            pl.BlockSpec((n_seq, KV_WIDTH, WINDOW), lambda b: (b, 0, 0)),
            pl.BlockSpec((n_seq, KV_WIDTH, WINDOW), lambda b: (b, 0, 0)),
        ),
        compiler_params=pltpu.CompilerParams(dimension_semantics=("arbitrary",)),
        name="attn_sample",
    )(sinks, proj, proj, proj, cache_kt, cache_vt)


def _seg_sum(x, bones):
    rows = x.shape[0]
    hi = x.astype(BF16).astype(F32)
    lo = x - hi
    groups = [slice(j * SEG, (j + 1) * SEG) for j in range(R_WIDTH // SEG)]
    lhs = jnp.concatenate([t[:, sl] for sl in groups for t in (hi, lo)], axis=0).astype(BF16)
    out = _mm(lhs, bones)
    return jnp.concatenate(
        [out[2 * j * rows:(2 * j + 1) * rows] + out[(2 * j + 1) * rows:(2 * j + 2) * rows]
         for j in range(len(groups))], axis=1)


def _wkv_kernel(pa_ref, pb_ref, pl_ref, prev_ref, s0_ref,
                mu_ref, mul_ref, w0_ref, a0_ref, kk_ref, ka_ref, rk_ref, lnw_ref, lnb_ref,
                wl_ref, bones_ref, *rest, chunk, n_sub, n_seq, n_steps, cast_transposed):
    n_cast = len(cast_transposed)
    cast_src, (y_ref, so_ref) = rest[:n_cast], rest[n_cast:n_cast + 2]
    cast_dst, (s_ref, carry_ref) = rest[n_cast + 2:2 * n_cast + 2], rest[2 * n_cast + 2:]
    _run_casts(cast_src, cast_dst, cast_transposed)

    step = pl.program_id(1)
    cs = chunk
    seq_rows = n_sub * cs
    rows = n_seq * seq_rows
    gc = 2 * cs
    hd = R_HEAD
    w = R_WIDTH
    seqs = range(n_seq)

    @pl.when(step == 0)
    def _init():
        s_ref[...] = jnp.zeros(s_ref.shape, F32)
        for q in seqs:
            carry_ref[q:q + 1, :] = prev_ref[q]
            for p in range(N_PAIRS):
                s_ref[q * N_PAIRS + p, 0:hd, 0:hd] = s0_ref[q, 2 * p]
                s_ref[q * N_PAIRS + p, hd:2 * hd, hd:2 * hd] = s0_ref[q, 2 * p + 1]

    row = lax.broadcasted_iota(jnp.int32, (rows, 1), 0)

    def token_shift(x, lo, hi, mu):
        shifted = pltpu.roll(x, 1, axis=0)
        for q in seqs:
            shifted = jnp.where(row == q * seq_rows, carry_ref[q:q + 1, lo:hi], shifted)
        return x + (shifted - x) * mu

    cols = jnp.concatenate([pa_ref[...], pb_ref[...]], axis=1)
    l_raw = pl_ref[...]
    xx = token_shift(cols, 0, 3 * w, mu_ref[...])
    xr, xk, xv = xx[:, 0:w], xx[:, w:2 * w], xx[:, 2 * w:3 * w]
    xl = token_shift(l_raw, 3 * w, SHIFT_COLS, mul_ref[...])
    for q in seqs:
        last = (q + 1) * seq_rows
        carry_ref[q:q + 1, 0:3 * w] = cols[last - 1:last, :]
        carry_ref[q:q + 1, 3 * w:] = l_raw[last - 1:last, :]

    lane_l = lax.broadcasted_iota(jnp.int32, (1, LORA_PAD), 1)
    act = jnp.where(lane_l < DECAY_LORA, jnp.tanh(xl),
                    jnp.where(lane_l < DECAY_LORA + AAA_LORA, xl,
                              jnp.where(lane_l < LORA_COLS, jax.nn.sigmoid(xl), 0.0)))
    up = _mm(act.astype(BF16), wl_ref[...])
    z = -(w0_ref[...] + up[:, 0:w])
    softplus = jnp.maximum(z, 0.0) + jnp.log1p(jnp.exp(-jnp.abs(z)))
    lwd = -jnp.exp(-softplus - 0.5)
    a = jax.nn.sigmoid(a0_ref[...] + up[:, w:2 * w])
    gate = up[:, 2 * w:3 * w]

    bones = bones_ref[...]
    kkn = xk * kk_ref[...]
    kk = kkn * lax.rsqrt(jnp.maximum(_seg_sum(kkn * kkn, bones), 1e-24))
    k2 = xk * (1.0 + (a - 1.0) * ka_ref[...])
    b = kk * a

    tri = (lax.broadcasted_iota(jnp.int32, (cs, cs), 0)
           >= lax.broadcasted_iota(jnp.int32, (cs, cs), 1)).astype(BF16)
    ri = lax.broadcasted_iota(jnp.int32, (gc, gc), 0)
    ci = lax.broadcasted_iota(jnp.int32, (gc, gc), 1)
    same_head = (ri >= cs) == (ci >= cs)
    strict = same_head & (ci < ri)
    incl = same_head & (ci <= ri)
    eye = (ri == ci).astype(F32)
    head0 = lax.broadcasted_iota(jnp.int32, (1, PAIR), 1) < hd

    def stack(x):
        return jnp.concatenate([jnp.where(head0, x, 0.0), jnp.where(head0, 0.0, x)], axis=0).astype(BF16)

    pairs = range(N_PAIRS)
    lanes = [slice(p * PAIR, (p + 1) * PAIR) for p in pairs]
    squarings = cs.bit_length() - 2

    chunks = []
    for sub in range(n_seq * n_sub):
        rs = slice(sub * cs, (sub + 1) * cs)
        lw_c = lwd[rs]
        lw_hi = lw_c.astype(BF16)
        lw_r = lw_c - lw_hi.astype(F32)
        lw_mid = lw_r.astype(BF16)
        cum = _mm(tri, lw_hi) + _mm(tri, lw_mid) + _mm(tri, (lw_r - lw_mid.astype(F32)).astype(BF16))
        cum_last = cum[cs - 1:cs, :]
        e_inv = jnp.exp(-cum)
        e_last = jnp.exp(cum_last - cum)
        kq = kk[rs] * jnp.exp(cum - lw_c)
        rq = xr[rs] * jnp.exp(cum)
        kd = k2[rs] * e_inv
        bd = b[rs] * e_inv
        kdp = k2[rs] * e_last
        bdp = b[rs] * e_last
        xv_c = xv[rs]
        kq_s = [stack(kq[:, sl]) for sl in lanes]
        bd_s = [stack(bd[:, sl]) for sl in lanes]
        kd_s = [stack(kd[:, sl]) for sl in lanes]
        rq_s = [stack(rq[:, sl]) for sl in lanes]
        probes = [jnp.concatenate([kq_s[p], rq_s[p]], axis=0) for p in pairs]
        if gc % PAIR == 0:
            a_all = [_nt(probes[p], jnp.concatenate([bd_s[p], kd_s[p]], axis=0)) for p in pairs]
            a_kb = [jnp.where(strict, t[0:gc, 0:gc], 0.0) for t in a_all]
            a_kk = [jnp.where(strict, t[0:gc, gc:2 * gc], 0.0).astype(BF16) for t in a_all]
            a_rb = [jnp.where(incl, t[gc:2 * gc, 0:gc], 0.0).astype(BF16) for t in a_all]
            a_rk = [jnp.where(incl, t[gc:2 * gc, gc:2 * gc], 0.0).astype(BF16) for t in a_all]
        else:
            a_kb = [jnp.where(strict, _nt(kq_s[p], bd_s[p]), 0.0) for p in pairs]
            a_kk = [jnp.where(strict, _nt(kq_s[p], kd_s[p]), 0.0).astype(BF16) for p in pairs]
            a_rb = [jnp.where(incl, _nt(rq_s[p], bd_s[p]), 0.0).astype(BF16) for p in pairs]
            a_rk = [jnp.where(incl, _nt(rq_s[p], kd_s[p]), 0.0).astype(BF16) for p in pairs]
        inv = [eye - t for t in a_kb]
        if squarings >= 1:
            apow_b = [t.astype(BF16) for t in a_kb]
            apow_b = [_mm(t, t).astype(BF16) for t in apow_b]
            for _ in range(squarings - 1):
                both = [_mm(jnp.concatenate([apow_b[p], inv[p].astype(BF16)], axis=0), apow_b[p]) for p in pairs]
                apow_b = [t[0:gc].astype(BF16) for t in both]
                inv = [inv[p] + both[p][gc:2 * gc] for p in pairs]
            inv = [inv[p] + _mm(inv[p].astype(BF16), apow_b[p]) for p in pairs]
        chunks.append(dict(
            probes=probes, a_kk=a_kk, a_rk=a_rk, a_rb=a_rb, inv=[t.astype(BF16) for t in inv],
            v_s=[stack(xv_c[:, sl]) for sl in lanes],
            upd=[jnp.concatenate([stack(kdp[:, sl]), stack(bdp[:, sl])], axis=0) for sl in lanes],
            p_last=jnp.exp(cum_last)))

    sp = [(q, p) for q in seqs for p in pairs]
    state = {(q, p): s_ref[q * N_PAIRS + p] for q, p in sp}
    y_rows = [None] * (n_seq * n_sub)
    for level in range(n_sub):
        ch = {q: chunks[q * n_sub + level] for q in seqs}
        s_b = {k: state[k].astype(BF16) for k in sp}
        state_t = {(q, p): _nt(ch[q]["probes"][p], s_b[q, p]) for q, p in sp}
        rhs = {(q, p): state_t[q, p][0:gc] + _mm(ch[q]["a_kk"][p], ch[q]["v_s"][p]) for q, p in sp}
        u_b = {(q, p): _mm(ch[q]["inv"][p], rhs[q, p].astype(BF16)).astype(BF16) for q, p in sp}
        if gc % PAIR == 0:
            y_s = {(q, p): state_t[q, p][gc:2 * gc]
                   + _mm(jnp.concatenate([ch[q]["a_rk"][p], -ch[q]["a_rb"][p]], axis=1),
                         jnp.concatenate([ch[q]["v_s"][p], u_b[q, p]], axis=0)) for q, p in sp}
        else:
            y_s = {(q, p): state_t[q, p][gc:2 * gc] + _mm(ch[q]["a_rk"][p], ch[q]["v_s"][p])
                   - _mm(ch[q]["a_rb"][p], u_b[q, p]) for q, p in sp}
        state = {(q, p): state[q, p] * ch[q]["p_last"][:, lanes[p]]
                 + _tn(jnp.concatenate([ch[q]["v_s"][p], -u_b[q, p]], axis=0), ch[q]["upd"][p]) for q, p in sp}
        for q in seqs:
            y_rows[q * n_sub + level] = jnp.concatenate(
                [y_s[q, p][0:cs] + y_s[q, p][cs:gc] for p in pairs], axis=1)
    for q, p in sp:
        s_ref[q * N_PAIRS + p] = state[q, p]

    y = jnp.concatenate(y_rows, axis=0) if len(y_rows) > 1 else y_rows[0]
    mean = _seg_sum(y, bones) * (1.0 / hd)
    d = y - mean
    var = _seg_sum(d * d, bones) * (1.0 / hd)
    yn = d * lax.rsqrt(var + GN_EPS) * lnw_ref[...] + lnb_ref[...]
    bonus = _seg_sum(xr * k2 * rk_ref[...], bones) * xv
    y_ref[...] = (yn + bonus) * gate

    @pl.when(step == n_steps - 1)
    def _fin():
        for q, p in sp:
            so_ref[q, 2 * p] = s_ref[q * N_PAIRS + p, 0:hd, 0:hd]
            so_ref[q, 2 * p + 1] = s_ref[q * N_PAIRS + p, hd:2 * hd, hd:2 * hd]


def _wkv(proj, prev0, s0, params, wl, bones, batch, seq, chunk, n_sub, n_seq=1, casts=()):
    ns = seq // (chunk * n_sub)
    assert n_seq == 1 or ns == 1, "several sequences per step only when a step covers them whole"
    rows = n_seq * chunk * n_sub
    nb = batch // n_seq
    rw = R_WIDTH

    def col(cb):
        return lambda b, c: (b * ns + c, cb)

    const2 = lambda b, c: (0, 0)
    vec = lambda n: pl.BlockSpec((1, n), const2)
    cast_in, cast_out, cast_shapes, cast_tr = _cast_specs(casts, nb * ns, lambda b, c: b * ns + c)
    mu, mul, w0, a0, k_k, k_a, r_k, ln_w, ln_b = params
    outs = pl.pallas_call(
        functools.partial(_wkv_kernel, chunk=chunk, n_sub=n_sub, n_seq=n_seq, n_steps=ns,
                          cast_transposed=cast_tr),
        out_shape=(jax.ShapeDtypeStruct((batch * seq, rw), F32),
                   jax.ShapeDtypeStruct((batch, R_HEADS, R_HEAD, R_HEAD), F32), *cast_shapes),
        grid=(nb, ns),
        in_specs=[
            pl.BlockSpec((rows, C_BLK), col(P_C // C_BLK)),
            pl.BlockSpec((rows, C_BLK), col(P_C // C_BLK + 1)),
            pl.BlockSpec((rows, LORA_PAD), col(P_L // LORA_PAD)),
            pl.BlockSpec((n_seq, 1, SHIFT_COLS), lambda b, c: (b, 0, 0)),
            pl.BlockSpec((n_seq, R_HEADS, R_HEAD, R_HEAD), lambda b, c: (b, 0, 0, 0)),
            vec(3 * rw), vec(LORA_PAD), vec(rw), vec(rw), vec(rw), vec(rw), vec(rw), vec(rw), vec(rw),
            pl.BlockSpec((LORA_PAD, 3 * rw), const2),
            pl.BlockSpec((SEG, SEG), const2),
            *cast_in,
        ],
        out_specs=(
            pl.BlockSpec((rows, rw), lambda b, c: (b * ns + c, 0)),
            pl.BlockSpec((n_seq, R_HEADS, R_HEAD, R_HEAD), lambda b, c: (b, 0, 0, 0)),
            *cast_out,
        ),
        scratch_shapes=[pltpu.VMEM((n_seq * N_PAIRS, PAIR, PAIR), F32), pltpu.VMEM((n_seq, SHIFT_COLS), F32)],
        compiler_params=pltpu.CompilerParams(
            dimension_semantics=("arbitrary", "arbitrary"), vmem_limit_bytes=VMEM_LIMIT),
        name="wkv",
    )(proj, proj, proj, prev0, s0, mu, mul, w0, a0, k_k, k_a, r_k, ln_w, ln_b, wl, bones,
      *[wgt for wgt, _ in casts])
    return outs[0], outs[1], outs[2:]


def _mix_kernel(x_ref, ya_ref, yr_ref, g_ref, wg_ref, wba_ref, wbr_ref, wo_ref, o_ref):
    x = x_ref[...]
    h = _rms(x, g_ref[...]).astype(BF16)
    ya = ya_ref[...].astype(BF16)
    yr = yr_ref[...].astype(BF16)
    acc = x
    for c in range(D_MODEL // MIX_CHUNK):
        ca = slice(c * MIX_CHUNK, (c + 1) * MIX_CHUNK)
        cr = slice(D_MODEL + c * MIX_CHUNK, D_MODEL + (c + 1) * MIX_CHUNK)
        mixed = (jax.nn.sigmoid(_mm(h, wg_ref[:, ca])) * _mm(ya, wba_ref[:, ca])
                 + jax.nn.sigmoid(_mm(h, wg_ref[:, cr])) * _mm(yr, wbr_ref[:, ca]))
        acc = acc + _mm(mixed.astype(BF16), wo_ref[ca, :])
    o_ref[...] = acc


def _resident(shape):
    return pl.BlockSpec(shape, lambda *_: (0,) * len(shape), pipeline_mode=pl.Buffered(1))


def _mix(x, ya, yr, g, wg, wba, wbr, wo, tm):
    m = x.shape[0]
    row = lambda i: (i, 0)
    return pl.pallas_call(
        _mix_kernel,
        out_shape=jax.ShapeDtypeStruct((m, D_MODEL), F32),
        grid=(m // tm,),
        in_specs=[
            pl.BlockSpec((tm, D_MODEL), row),
            pl.BlockSpec((tm, ATTN_WIDTH), row),
            pl.BlockSpec((tm, R_WIDTH), row),
            pl.BlockSpec((1, D_MODEL), lambda i: (0, 0)),
            _resident((D_MODEL, 2 * D_MODEL)),
            _resident((ATTN_WIDTH, D_MODEL)),
            _resident((R_WIDTH, D_MODEL)),
            _resident((D_MODEL, D_MODEL)),
        ],
        out_specs=pl.BlockSpec((tm, D_MODEL), row),
        compiler_params=pltpu.CompilerParams(
            dimension_semantics=("arbitrary",), vmem_limit_bytes=VMEM_LIMIT),
        name="mix",
    )(x, ya, yr, g, wg, wba, wbr, wo)


def _ffn_kernel(x_ref, g_ref, wu_ref, wd_ref, o_ref, h_ref):
    @pl.when(pl.program_id(1) == 0)
    def _():
        x = x_ref[...]
        h_ref[...] = _rms(x, g_ref[...]).astype(BF16)
        o_ref[...] = x

    u = _mm(h_ref[...], wu_ref[...])
    o_ref[...] += _mm(jnp.square(jnp.maximum(u, 0.0)).astype(BF16), wd_ref[...])


def _ffn_vmem_bytes(tm, tk):
    return 2 * 2 * tm * D_MODEL * 4 + 2 * 2 * D_MODEL * tk * 2 + tm * D_MODEL * 2 + tm * tk * (4 + 2)


def _ffn(x, g, wu, wd, tm, tk):
    m = x.shape[0]
    vmem = max(VMEM_LIMIT, _ffn_vmem_bytes(tm, tk))
    assert vmem <= V7X_VMEM_BYTES, (tm, tk)
    return pl.pallas_call(
        _ffn_kernel,
        out_shape=jax.ShapeDtypeStruct((m, D_MODEL), F32),
        grid=(m // tm, D_FF // tk),
        in_specs=[
            pl.BlockSpec((tm, D_MODEL), lambda i, k: (i, 0)),
            pl.BlockSpec((1, D_MODEL), lambda i, k: (0, 0)),
            pl.BlockSpec((D_MODEL, tk), lambda i, k: (0, k)),
            pl.BlockSpec((tk, D_MODEL), lambda i, k: (k, 0)),
        ],
        out_specs=pl.BlockSpec((tm, D_MODEL), lambda i, k: (i, 0)),
        scratch_shapes=[pltpu.VMEM((tm, D_MODEL), BF16)],
        compiler_params=pltpu.CompilerParams(
            dimension_semantics=("arbitrary", "arbitrary"), vmem_limit_bytes=vmem),
        name="ffn",
    )(x, g, wu, wd)


def _ple_kernel(x_ref, pe_ref, wg_ref, wp_ref, g_ref, o_ref):
    x = x_ref[...]
    gate = jax.nn.sigmoid(_mm(x.astype(BF16), wg_ref[...]))
    x = x + gate * _mm(pe_ref[...].astype(BF16), wp_ref[...])
    o_ref[...] = _rms(x, g_ref[...])


def _ple(x, pe, wg, wp, g, tm):
    m = x.shape[0]
    row = lambda i: (i, 0)
    return pl.pallas_call(
        _ple_kernel,
        out_shape=jax.ShapeDtypeStruct((m, D_MODEL), F32),
        grid=(m // tm,),
        in_specs=[
            pl.BlockSpec((tm, D_MODEL), row),
            pl.BlockSpec((tm, PLE_DIM), row),
            _resident((D_MODEL, D_MODEL)),
            _resident((PLE_DIM, D_MODEL)),
            pl.BlockSpec((1, D_MODEL), lambda i: (0, 0)),
        ],
        out_specs=pl.BlockSpec((tm, D_MODEL), row),
        compiler_params=pltpu.CompilerParams(
            dimension_semantics=("arbitrary",), vmem_limit_bytes=VMEM_LIMIT),
        name="ple",
    )(x, pe, wg, wp, g)


def _dense_tail(x, ya, yr, pe, wts, tiles):
    (g_mix, w_gates, wba, wbr, wo, g_ffn, wu, wd, wg, wp, g_fin) = wts
    x = _mix(x, ya, yr, g_mix, w_gates, wba, wbr, wo, tiles["mix_m"])
    x = _ffn(x, g_ffn, wu, wd, tiles["ffn_m"], tiles["ffn_k"])
    return _ple(x, pe, wg, wp, g_fin, tiles["ple_m"])


def _shift_out(proj, batch, seq):
    last = proj.reshape(batch, seq, P_COLS)[:, -1]
    return last[:, P_C:P_C + RWKV_COLS][None]


def kernel(x_prompt, x_sample, cache_k_win, cache_v_win, state_wkv, state_shift, p_prompt, p_sample,
           norm_mix, w_in, attn_sinks, rwkv_mu, rwkv_w0, rwkv_w2, rwkv_a0, rwkv_a2, rwkv_g2,
           rwkv_k_k, rwkv_k_a, rwkv_r_k, rwkv_ln_w, rwkv_ln_b, w_branch_attn, w_branch_rwkv,
           w_out, norm_ffn, w_ff_up, w_ff_down, w_ple_proj, w_ple_gate, norm_final):
    assert w_in.shape[0] == 1, "single-layer step"
    bp, tp = x_prompt.shape[0], x_prompt.shape[1]
    bs, ts = x_sample.shape[0], x_sample.shape[1]
    rw = R_WIDTH

    wl = jnp.zeros((LORA_PAD, 3 * rw), F32)
    wl = wl.at[0:DECAY_LORA, 0:rw].set(rwkv_w2[0])
    wl = wl.at[DECAY_LORA:DECAY_LORA + AAA_LORA, rw:2 * rw].set(rwkv_a2[0])
    wl = wl.at[DECAY_LORA + AAA_LORA:LORA_COLS, 2 * rw:3 * rw].set(rwkv_g2[0])
    wl = wl.astype(BF16)
    seg_id = np.arange(SEG) // R_HEAD
    bones = jnp.asarray(seg_id[:, None] == seg_id[None, :], BF16)
    mu = rwkv_mu[0]
    row = lambda v: v.reshape(1, -1)
    wkv_params = (row(mu[:3 * rw]), row(jnp.pad(mu[3 * rw:], (0, LORA_PAD - LORA_COLS))),
                  row(rwkv_w0[0]), row(rwkv_a0[0]), row(rwkv_k_k[0]), row(rwkv_k_a[0]),
                  row(rwkv_r_k[0]), row(rwkv_ln_w[0]), row(rwkv_ln_b[0]))
    g_mix = row(norm_mix[0])
    sinks = attn_sinks[0]

    ms = bs * ts
    xs = x_sample.reshape(ms, D_MODEL)
    w_in_t = jnp.transpose(w_in[0])
    proj_s, w_in_b = _proj_cast(xs, g_mix, w_in_t, 1024)

    tiles_p = dict(mix_m=256, ffn_m=1024, ffn_k=1024, ple_m=512)
    xp = x_prompt.reshape(bp * tp, D_MODEL)
    proj_p = _proj(xp, g_mix, w_in_b, 2048, 512)
    ya_p = _attn_prompt(proj_p, sinks, bp, tp)
    yr_p, s_p, (wu, wd, wo, wg, wba, wbr, w_gates) = _wkv(
        proj_p, jnp.zeros((bp, 1, SHIFT_COLS), F32), jnp.zeros((bp, R_HEADS, R_HEAD, R_HEAD), F32),
        wkv_params, wl, bones, bp, tp, 64, 4,
        casts=((w_ff_up[0], None), (w_ff_down[0], None), (w_out[0], None), (w_ple_gate[0], None),
               (w_branch_attn[0], None), (w_branch_rwkv[0], None), (w_in_t, (P_GATES, 2 * D_MODEL))))
    dense = (g_mix, w_gates, wba, wbr, wo, row(norm_ffn[0]), wu, wd, wg, w_ple_proj[0].astype(BF16),
             row(norm_final))
    yp = _dense_tail(xp, ya_p, yr_p, p_prompt[0].reshape(bp * tp, PLE_DIM), dense, tiles_p)
    pp3 = proj_p.reshape(bp, tp, P_COLS)[:, -WINDOW:]
    k_p = pp3[:, :, P_K:P_K + KV_WIDTH].reshape(1, bp, WINDOW, N_KV_HEADS, HEAD_DIM)
    v_p = pp3[:, :, P_V:P_V + KV_WIDTH].reshape(1, bp, WINDOW, N_KV_HEADS, HEAD_DIM)

    tiles_s = dict(mix_m=ms, ffn_m=ms, ffn_k=2048, ple_m=ms)
    ck_t = jnp.transpose(cache_k_win[0], (0, 2, 3, 1)).reshape(bs, KV_WIDTH, WINDOW)
    cv_t = jnp.transpose(cache_v_win[0], (0, 2, 3, 1)).reshape(bs, KV_WIDTH, WINDOW)
    ya_s, nk_t, nv_t = _attn_sample(proj_s, sinks, ck_t, cv_t, bs, ts, SAMPLE_GROUP)
    prev_s = jnp.pad(state_shift[0], ((0, 0), (0, LORA_PAD - LORA_COLS))).reshape(bs, 1, SHIFT_COLS)
    yr_s, s_s, _ = _wkv(proj_s, prev_s, state_wkv[0], wkv_params, wl, bones, bs, ts, ts, 1, n_seq=SAMPLE_GROUP)
    ys = _dense_tail(xs, ya_s, yr_s, p_sample[0].reshape(ms, PLE_DIM), dense, tiles_s)

    return (yp.reshape(bp, tp, D_MODEL), ys.reshape(bs, ts, D_MODEL),
            k_p, v_p, s_p[None], _shift_out(proj_p, bp, tp),
            jnp.transpose(nk_t.reshape(bs, N_KV_HEADS, HEAD_DIM, WINDOW), (0, 3, 1, 2))[None],
            jnp.transpose(nv_t.reshape(bs, N_KV_HEADS, HEAD_DIM, WINDOW), (0, 3, 1, 2))[None],
            s_s[None], _shift_out(proj_s, bs, ts))
```

```python
import functools

import numpy as np
import jax
import jax.numpy as jnp
from jax import lax
from jax.experimental import pallas as pl
from jax.experimental.pallas import tpu as pltpu

F32 = jnp.float32
BF16 = jnp.bfloat16

D_MODEL = 2048
PLE_DIM = 256
HEAD_DIM = 64
N_Q_HEADS = 16
N_KV_HEADS = 4
GQA_GROUP = 4
ATTN_WIDTH = 1024
KV_WIDTH = 256
WINDOW = 128
ALIBI_MAX = 8.0
R_HEAD = 64
R_WIDTH = 1024
R_HEADS = 16
DECAY_LORA = 64
AAA_LORA = 64
GATE_LORA = 160
LORA_COLS = DECAY_LORA + AAA_LORA + GATE_LORA
LORA_PAD = 512
RWKV_COLS = 3 * R_WIDTH + LORA_COLS
D_FF = 4 * D_MODEL
NORM_EPS = 1e-6
GN_EPS = 64e-5

P_Q = 0
P_K = ATTN_WIDTH
P_V = ATTN_WIDTH + KV_WIDTH
P_C = ATTN_WIDTH + 2 * KV_WIDTH
P_L = P_C + 3 * R_WIDTH
P_COLS = P_L + LORA_PAD
P_GATES = P_C + RWKV_COLS
C_BLK = P_C
SHIFT_COLS = 3 * R_WIDTH + LORA_PAD

PAIR = 128
N_PAIRS = R_WIDTH // PAIR
SEG = 256
MIX_CHUNK = 2048
SAMPLE_GROUP = 8
PROMPT_BLOCKS = 4

V7X_VMEM_BYTES = 64 * 1024 * 1024
VMEM_LIMIT = 56 * 1024 * 1024


def _mm(a, b):
    return jnp.dot(a, b, preferred_element_type=F32)


def _nt(a, b):
    return lax.dot_general(a, b, (((1,), (1,)), ((), ())), preferred_element_type=F32)


def _tn(a, b):
    return lax.dot_general(a, b, (((0,), (0,)), ((), ())), preferred_element_type=F32)


def _rms(x, g):
    ms = jnp.mean(x * x, axis=-1, keepdims=True)
    return x * lax.rsqrt(ms + NORM_EPS) * g


def _alibi_slope(hq):
    return float(2.0 ** (-ALIBI_MAX * (hq + 1) / N_Q_HEADS))


def _cast_specs(casts, n_steps, step_of):
    in_specs, out_specs, shapes = [], [], []
    for wgt, span in casts:
        start, count = (0, wgt.shape[0]) if span is None else span
        blk, width = count // n_steps, wgt.shape[1]
        assert blk * n_steps == count, (wgt.shape, span, n_steps)
        if span is None:
            assert blk % 16 == 0, blk
            in_specs.append(pl.BlockSpec((blk, width), lambda *g: (step_of(*g), 0)))
            out_specs.append(pl.BlockSpec((blk, width), lambda *g: (step_of(*g), 0)))
            shapes.append(jax.ShapeDtypeStruct((count, width), BF16))
        else:
            assert blk % 128 == 0 and start % 8 == 0, span
            in_specs.append(pl.BlockSpec(
                (pl.Element(blk), pl.Element(width)),
                lambda *g, start=start, blk=blk: (pl.multiple_of(start + blk * step_of(*g), 8), 0)))
            out_specs.append(pl.BlockSpec((width, blk), lambda *g: (0, step_of(*g))))
            shapes.append(jax.ShapeDtypeStruct((width, count), BF16))
    return in_specs, out_specs, shapes, tuple(span is not None for _, span in casts)


def _run_casts(srcs, dsts, transposed):
    for src, dst, tr in zip(srcs, dsts, transposed):
        dst[...] = (src[...].T if tr else src[...]).astype(BF16)


def _proj_kernel(x_ref, g_ref, w_ref, o_ref, h_ref):
    @pl.when(pl.program_id(1) == 0)
    def _():
        h_ref[...] = _rms(x_ref[...], g_ref[...]).astype(BF16)

    o_ref[...] = _nt(h_ref[...], w_ref[...])


def _proj_cast_kernel(x_ref, g_ref, w_ref, o_ref, wb_ref, h_ref):
    @pl.when(pl.program_id(0) == 0)
    def _():
        h_ref[...] = _rms(x_ref[...], g_ref[...]).astype(BF16)

    wb = w_ref[...].astype(BF16)
    wb_ref[...] = wb
    o_ref[...] = _nt(h_ref[...], wb)


def _proj_cast(x, g, wt_f32, tn):
    m = x.shape[0]
    return pl.pallas_call(
        _proj_cast_kernel,
        out_shape=(jax.ShapeDtypeStruct((m, P_COLS), F32), jax.ShapeDtypeStruct((P_COLS, D_MODEL), BF16)),
        grid=(P_COLS // tn,),
        in_specs=[
            pl.BlockSpec((m, D_MODEL), lambda j: (0, 0)),
            pl.BlockSpec((1, D_MODEL), lambda j: (0, 0)),
            pl.BlockSpec((tn, D_MODEL), lambda j: (j, 0)),
        ],
        out_specs=(pl.BlockSpec((m, tn), lambda j: (0, j)), pl.BlockSpec((tn, D_MODEL), lambda j: (j, 0))),
        scratch_shapes=[pltpu.VMEM((m, D_MODEL), BF16)],
        compiler_params=pltpu.CompilerParams(
            dimension_semantics=("arbitrary",), vmem_limit_bytes=VMEM_LIMIT),
        name="proj_cast",
    )(x, g, wt_f32)


def _proj(x, g, w, tm, tn):
    m = x.shape[0]
    return pl.pallas_call(
        _proj_kernel,
        out_shape=jax.ShapeDtypeStruct((m, P_COLS), F32),
        grid=(m // tm, P_COLS // tn),
        in_specs=[
            pl.BlockSpec((tm, D_MODEL), lambda i, j: (i, 0)),
            pl.BlockSpec((1, D_MODEL), lambda i, j: (0, 0)),
            pl.BlockSpec((tn, D_MODEL), lambda i, j: (j, 0)),
        ],
        out_specs=pl.BlockSpec((tm, tn), lambda i, j: (i, j)),
        scratch_shapes=[pltpu.VMEM((tm, D_MODEL), BF16)],
        compiler_params=pltpu.CompilerParams(
            dimension_semantics=("arbitrary", "arbitrary"), vmem_limit_bytes=VMEM_LIMIT),
        name="proj",
    )(x, g, w)


def _attend_heads(scores, apply_values, sinks):
    heads = range(len(scores))
    m = []
    for h in heads:
        mh = sinks[h]
        for s in scores[h]:
            mh = jnp.maximum(mh, jnp.max(s, axis=-1, keepdims=True))
        m.append(mh)
    ps = [[jnp.exp(s - m[h]) for s in scores[h]] for h in heads]
    den = []
    for h in heads:
        dh = jnp.exp(sinks[h] - m[h])
        for p in ps[h]:
            dh = dh + jnp.sum(p, axis=-1, keepdims=True)
        den.append(dh)
    outs = []
    for h in heads:
        o = None
        for p, pv in zip(ps[h], apply_values[h]):
            t = pv(p.astype(BF16))
            o = t if o is None else o + t
        outs.append(o * (1.0 / den[h]))
    return outs


def _head_slices():
    q_sl = [slice(hq * HEAD_DIM, (hq + 1) * HEAD_DIM) for hq in range(N_Q_HEADS)]
    kv_sl = [slice((hq // GQA_GROUP) * HEAD_DIM, (hq // GQA_GROUP + 1) * HEAD_DIM) for hq in range(N_Q_HEADS)]
    return q_sl, kv_sl


def _alibi_bias(dist, valid):
    slopes = np.array([_alibi_slope(hq) for hq in range(N_Q_HEADS)], np.float32)
    return np.where(valid[None], -slopes[:, None, None] * dist[None].astype(np.float32), -np.inf).astype(np.float32)


def _attn_prompt_kernel(sink_ref, bias_ref, q_ref, kp_ref, kc_ref, vp_ref, vc_ref, o_ref):
    w = WINDOW
    n_blk = q_ref.shape[0] // w
    kj = lax.broadcasted_iota(jnp.int32, (w, 2 * w), 1)
    no_prev = (kj < w) & (pl.program_id(1) == 0)
    q = q_ref[...] * (HEAD_DIM ** -0.5)
    kc, vc = kc_ref[...].astype(BF16), vc_ref[...].astype(BF16)
    ks = jnp.concatenate([kp_ref[...].astype(BF16), kc], axis=0)
    vs = jnp.concatenate([vp_ref[...].astype(BF16), vc], axis=0)
    hqs = range(N_Q_HEADS)
    q_sl, kv_sl = _head_slices()
    scores, values = [], []
    for s in range(n_blk):
        k2, v2 = ks[s * w:(s + 2) * w], vs[s * w:(s + 2) * w]
        qb = q[s * w:(s + 1) * w]
        for hq in hqs:
            sc = _nt(qb[:, q_sl[hq]].astype(BF16), k2[:, kv_sl[hq]]) + bias_ref[hq]
            scores.append([jnp.where(no_prev, -jnp.inf, sc) if s == 0 else sc])
            values.append([functools.partial(_mm, b=v2[:, kv_sl[hq]])])
    outs = _attend_heads(scores, values, [sink_ref[hq] for _ in range(n_blk) for hq in hqs])
    for s in range(n_blk):
        for hq in hqs:
            o_ref[s * w:(s + 1) * w, q_sl[hq]] = outs[s * N_Q_HEADS + hq]


def _attn_prompt(proj, sinks, batch, seq):
    rows = PROMPT_BLOCKS * WINDOW
    ns = seq // rows
    kcol, vcol = P_K // KV_WIDTH, P_V // KV_WIDTH

    def cur(c):
        return lambda b, i: (b * ns + i, c)

    def prev(c):
        return lambda b, i: (b * ns * PROMPT_BLOCKS + jnp.maximum(i * PROMPT_BLOCKS - 1, 0), c)

    ti = np.arange(WINDOW)[:, None]
    kj = np.arange(2 * WINDOW)[None, :]
    dist = ti - kj + WINDOW
    bias = _alibi_bias(dist, (dist >= 0) & (dist <= WINDOW))
    return pl.pallas_call(
        _attn_prompt_kernel,
        out_shape=jax.ShapeDtypeStruct((batch * seq, ATTN_WIDTH), F32),
        grid=(batch, ns),
        in_specs=[
            pl.BlockSpec(memory_space=pltpu.SMEM),
            _resident((N_Q_HEADS, WINDOW, 2 * WINDOW)),
            pl.BlockSpec((rows, ATTN_WIDTH), cur(P_Q // ATTN_WIDTH)),
            pl.BlockSpec((WINDOW, KV_WIDTH), prev(kcol)),
            pl.BlockSpec((rows, KV_WIDTH), cur(kcol)),
            pl.BlockSpec((WINDOW, KV_WIDTH), prev(vcol)),
            pl.BlockSpec((rows, KV_WIDTH), cur(vcol)),
        ],
        out_specs=pl.BlockSpec((rows, ATTN_WIDTH), lambda b, i: (b * ns + i, 0)),
        compiler_params=pltpu.CompilerParams(dimension_semantics=("arbitrary", "arbitrary")),
        name="attn_prompt",
    )(sinks, jnp.asarray(bias), proj, proj, proj, proj, proj)


def _attn_sample_kernel(sink_ref, q_ref, kn_ref, vn_ref, ckt_ref, cvt_ref, o_ref, nkt_ref, nvt_ref, *, seq):
    t, w = seq, WINDOW
    n_seq = q_ref.shape[0] // t
    ti = lax.broadcasted_iota(jnp.int32, (t, w), 0)
    cj = lax.broadcasted_iota(jnp.int32, (t, w), 1)
    dist_c = (ti - cj + w).astype(F32)
    valid_c = cj >= ti
    ti2 = lax.broadcasted_iota(jnp.int32, (t, t), 0)
    tj2 = lax.broadcasted_iota(jnp.int32, (t, t), 1)
    dist_n = (ti2 - tj2).astype(F32)
    valid_n = tj2 <= ti2
    hqs = range(N_Q_HEADS)
    q_sl, kv_sl = _head_slices()
    bias_c = [jnp.where(valid_c, -_alibi_slope(hq) * dist_c, -jnp.inf) for hq in hqs]
    bias_n = [jnp.where(valid_n, -_alibi_slope(hq) * dist_n, -jnp.inf) for hq in hqs]
    q_all = q_ref[...] * (HEAD_DIM ** -0.5)
    is_new = lax.broadcasted_iota(jnp.int32, (1, w), 1) >= w - t
    put = (cj == ti + (w - t)).astype(BF16)

    def place_new(x):
        hi = x.astype(BF16)
        rem = x - hi.astype(F32)
        mid = rem.astype(BF16)
        return _tn(hi, put) + _tn(mid, put) + _tn((rem - mid.astype(F32)).astype(BF16), put)

    scores, values = [], []
    for s in range(n_seq):
        rs = slice(s * t, (s + 1) * t)
        ckt, cvt = ckt_ref[s], cvt_ref[s]
        nkt_ref[s] = jnp.where(is_new, place_new(kn_ref[rs, :]), pltpu.roll(ckt, w - t, axis=1))
        nvt_ref[s] = jnp.where(is_new, place_new(vn_ref[rs, :]), pltpu.roll(cvt, w - t, axis=1))
        cktb, cvtb = ckt.astype(BF16), cvt.astype(BF16)
        knb, vnb = kn_ref[rs, :].astype(BF16), vn_ref[rs, :].astype(BF16)
        q = q_all[rs]
        qs = [q[:, q_sl[hq]].astype(BF16) for hq in hqs]
        scores += [[_mm(qs[hq], cktb[kv_sl[hq], :]) + bias_c[hq], _nt(qs[hq], knb[:, kv_sl[hq]]) + bias_n[hq]]
                   for hq in hqs]
        values += [[functools.partial(_nt, b=cvtb[kv_sl[hq], :]), functools.partial(_mm, b=vnb[:, kv_sl[hq]])]
                   for hq in hqs]
    outs = _attend_heads(scores, values, [sink_ref[hq] for _ in range(n_seq) for hq in hqs])
    for s in range(n_seq):
        for hq in hqs:
            o_ref[s * t:(s + 1) * t, q_sl[hq]] = outs[s * N_Q_HEADS + hq]


def _attn_sample(proj, sinks, cache_kt, cache_vt, batch, seq, n_seq):
    kcol, vcol = P_K // KV_WIDTH, P_V // KV_WIDTH
    rows = n_seq * seq
    win = jax.ShapeDtypeStruct((batch, KV_WIDTH, WINDOW), F32)
    return pl.pallas_call(
        functools.partial(_attn_sample_kernel, seq=seq),
        out_shape=(jax.ShapeDtypeStruct((batch * seq, ATTN_WIDTH), F32), win, win),
        grid=(batch // n_seq,),
        in_specs=[
            pl.BlockSpec(memory_space=pltpu.SMEM),
            pl.BlockSpec((rows, ATTN_WIDTH), lambda b: (b, P_Q // ATTN_WIDTH)),
            pl.BlockSpec((rows, KV_WIDTH), lambda b: (b, kcol)),
            pl.BlockSpec((rows, KV_WIDTH), lambda b: (b, vcol)),
            pl.BlockSpec((n_seq, KV_WIDTH, WINDOW), lambda b: (b, 0, 0)),
            pl.BlockSpec((n_seq, KV_WIDTH, WINDOW), lambda b: (b, 0, 0)),
        ],
        out_specs=(
            pl.BlockSpec((rows, ATTN_WIDTH), lambda b: (b, 0)),
            pl.BlockSpec((n_seq, KV_WIDTH, WINDOW), lambda b: (b, 0, 0)),
            pl.BlockSpec((n_seq, KV_WIDTH, WINDOW), lambda b: (b, 0, 0)),
        ),
        compiler_params=pltpu.CompilerParams(dimension_semantics=("arbitrary",)),
        name="attn_sample",
    )(sinks, proj, proj, proj, cache_kt, cache_vt)


def _seg_sum(x, bones):
    rows = x.shape[0]
    hi = x.astype(BF16).astype(F32)
    lo = x - hi
    groups = [slice(j * SEG, (j + 1) * SEG) for j in range(R_WIDTH // SEG)]
    lhs = jnp.concatenate([t[:, sl] for sl in groups for t in (hi, lo)], axis=0).astype(BF16)
    out = _mm(lhs, bones)
    return jnp.concatenate(
        [out[2 * j * rows:(2 * j + 1) * rows] + out[(2 * j + 1) * rows:(2 * j + 2) * rows]
         for j in range(len(groups))], axis=1)


def _wkv_kernel(pa_ref, pb_ref, pl_ref, prev_ref, s0_ref,
                mu_ref, mul_ref, w0_ref, a0_ref, kk_ref, ka_ref, rk_ref, lnw_ref, lnb_ref,
                wl_ref, bones_ref, *rest, chunk, n_sub, n_seq, n_steps, cast_transposed):
    n_cast = len(cast_transposed)
    cast_src, (y_ref, so_ref) = rest[:n_cast], rest[n_cast:n_cast + 2]
    cast_dst, (s_ref, carry_ref) = rest[n_cast + 2:2 * n_cast + 2], rest[2 * n_cast + 2:]
    _run_casts(cast_src, cast_dst, cast_transposed)

    step = pl.program_id(1)
    cs = chunk
    seq_rows = n_sub * cs
    rows = n_seq * seq_rows
    gc = 2 * cs
    hd = R_HEAD
    w = R_WIDTH
    seqs = range(n_seq)

    @pl.when(step == 0)
    def _init():
        s_ref[...] = jnp.zeros(s_ref.shape, F32)
        for q in seqs:
            carry_ref[q:q + 1, :] = prev_ref[q]
            for p in range(N_PAIRS):
                s_ref[q * N_PAIRS + p, 0:hd, 0:hd] = s0_ref[q, 2 * p]
                s_ref[q * N_PAIRS + p, hd:2 * hd, hd:2 * hd] = s0_ref[q, 2 * p + 1]

    row = lax.broadcasted_iota(jnp.int32, (rows, 1), 0)

    def token_shift(x, lo, hi, mu):
        shifted = pltpu.roll(x, 1, axis=0)
        for q in seqs:
            shifted = jnp.where(row == q * seq_rows, carry_ref[q:q + 1, lo:hi], shifted)
        return x + (shifted - x) * mu

    cols = jnp.concatenate([pa_ref[...], pb_ref[...]], axis=1)
    l_raw = pl_ref[...]
    xx = token_shift(cols, 0, 3 * w, mu_ref[...])
    xr, xk, xv = xx[:, 0:w], xx[:, w:2 * w], xx[:, 2 * w:3 * w]
    xl = token_shift(l_raw, 3 * w, SHIFT_COLS, mul_ref[...])
    for q in seqs:
        last = (q + 1) * seq_rows
        carry_ref[q:q + 1, 0:3 * w] = cols[last - 1:last, :]
        carry_ref[q:q + 1, 3 * w:] = l_raw[last - 1:last, :]

    lane_l = lax.broadcasted_iota(jnp.int32, (1, LORA_PAD), 1)
    act = jnp.where(lane_l < DECAY_LORA, jnp.tanh(xl),
                    jnp.where(lane_l < DECAY_LORA + AAA_LORA, xl,
                              jnp.where(lane_l < LORA_COLS, jax.nn.sigmoid(xl), 0.0)))
    up = _mm(act.astype(BF16), wl_ref[...])
    z = -(w0_ref[...] + up[:, 0:w])
    softplus = jnp.maximum(z, 0.0) + jnp.log1p(jnp.exp(-jnp.abs(z)))
    lwd = -jnp.exp(-softplus - 0.5)
    a = jax.nn.sigmoid(a0_ref[...] + up[:, w:2 * w])
    gate = up[:, 2 * w:3 * w]

    bones = bones_ref[...]
    kkn = xk * kk_ref[...]
    kk = kkn * lax.rsqrt(jnp.maximum(_seg_sum(kkn * kkn, bones), 1e-24))
    k2 = xk * (1.0 + (a - 1.0) * ka_ref[...])
    b = kk * a

    tri = (lax.broadcasted_iota(jnp.int32, (cs, cs), 0)
           >= lax.broadcasted_iota(jnp.int32, (cs, cs), 1)).astype(BF16)
    ri = lax.broadcasted_iota(jnp.int32, (gc, gc), 0)
    ci = lax.broadcasted_iota(jnp.int32, (gc, gc), 1)
    same_head = (ri >= cs) == (ci >= cs)
    strict = same_head & (ci < ri)
    incl = same_head & (ci <= ri)
    eye = (ri == ci).astype(F32)
    head0 = lax.broadcasted_iota(jnp.int32, (1, PAIR), 1) < hd

    def stack(x):
        return jnp.concatenate([jnp.where(head0, x, 0.0), jnp.where(head0, 0.0, x)], axis=0).astype(BF16)

    pairs = range(N_PAIRS)
    lanes = [slice(p * PAIR, (p + 1) * PAIR) for p in pairs]
    squarings = cs.bit_length() - 2

    chunks = []
    for sub in range(n_seq * n_sub):
        rs = slice(sub * cs, (sub + 1) * cs)
        lw_c = lwd[rs]
        lw_hi = lw_c.astype(BF16)
        lw_r = lw_c - lw_hi.astype(F32)
        lw_mid = lw_r.astype(BF16)
        cum = _mm(tri, lw_hi) + _mm(tri, lw_mid) + _mm(tri, (lw_r - lw_mid.astype(F32)).astype(BF16))
        cum_last = cum[cs - 1:cs, :]
        e_inv = jnp.exp(-cum)
        e_last = jnp.exp(cum_last - cum)
        kq = kk[rs] * jnp.exp(cum - lw_c)
        rq = xr[rs] * jnp.exp(cum)
        kd = k2[rs] * e_inv
        bd = b[rs] * e_inv
        kdp = k2[rs] * e_last
        bdp = b[rs] * e_last
        xv_c = xv[rs]
        kq_s = [stack(kq[:, sl]) for sl in lanes]
        bd_s = [stack(bd[:, sl]) for sl in lanes]
        kd_s = [stack(kd[:, sl]) for sl in lanes]
        rq_s = [stack(rq[:, sl]) for sl in lanes]
        probes = [jnp.concatenate([kq_s[p], rq_s[p]], axis=0) for p in pairs]
        if gc % PAIR == 0:
            a_all = [_nt(probes[p], jnp.concatenate([bd_s[p], kd_s[p]], axis=0)) for p in pairs]
            a_kb = [jnp.where(strict, t[0:gc, 0:gc], 0.0) for t in a_all]
            a_kk = [jnp.where(strict, t[0:gc, gc:2 * gc], 0.0).astype(BF16) for t in a_all]
            a_rb = [jnp.where(incl, t[gc:2 * gc, 0:gc], 0.0).astype(BF16) for t in a_all]
            a_rk = [jnp.where(incl, t[gc:2 * gc, gc:2 * gc], 0.0).astype(BF16) for t in a_all]
        else:
            a_kb = [jnp.where(strict, _nt(kq_s[p], bd_s[p]), 0.0) for p in pairs]
            a_kk = [jnp.where(strict, _nt(kq_s[p], kd_s[p]), 0.0).astype(BF16) for p in pairs]
            a_rb = [jnp.where(incl, _nt(rq_s[p], bd_s[p]), 0.0).astype(BF16) for p in pairs]
            a_rk = [jnp.where(incl, _nt(rq_s[p], kd_s[p]), 0.0).astype(BF16) for p in pairs]
        inv = [eye - t for t in a_kb]
        if squarings >= 1:
            apow_b = [t.astype(BF16) for t in a_kb]
            apow_b = [_mm(t, t).astype(BF16) for t in apow_b]
            for _ in range(squarings - 1):
                both = [_mm(jnp.concatenate([apow_b[p], inv[p].astype(BF16)], axis=0), apow_b[p]) for p in pairs]
                apow_b = [t[0:gc].astype(BF16) for t in both]
                inv = [inv[p] + both[p][gc:2 * gc] for p in pairs]
            inv = [inv[p] + _mm(inv[p].astype(BF16), apow_b[p]) for p in pairs]
        chunks.append(dict(
            probes=probes, a_kk=a_kk, a_rk=a_rk, a_rb=a_rb, inv=[t.astype(BF16) for t in inv],
            v_s=[stack(xv_c[:, sl]) for sl in lanes],
            upd=[jnp.concatenate([stack(kdp[:, sl]), stack(bdp[:, sl])], axis=0) for sl in lanes],
            p_last=jnp.exp(cum_last)))

    sp = [(q, p) for q in seqs for p in pairs]
    state = {(q, p): s_ref[q * N_PAIRS + p] for q, p in sp}
    y_rows = [None] * (n_seq * n_sub)
    for level in range(n_sub):
        ch = {q: chunks[q * n_sub + level] for q in seqs}
        s_b = {k: state[k].astype(BF16) for k in sp}
        state_t = {(q, p): _nt(ch[q]["probes"][p], s_b[q, p]) for q, p in sp}
        rhs = {(q, p): state_t[q, p][0:gc] + _mm(ch[q]["a_kk"][p], ch[q]["v_s"][p]) for q, p in sp}
        u_b = {(q, p): _mm(ch[q]["inv"][p], rhs[q, p].astype(BF16)).astype(BF16) for q, p in sp}
        if gc % PAIR == 0:
            y_s = {(q, p): state_t[q, p][gc:2 * gc]
                   + _mm(jnp.concatenate([ch[q]["a_rk"][p], -ch[q]["a_rb"][p]], axis=1),
                         jnp.concatenate([ch[q]["v_s"][p], u_b[q, p]], axis=0)) for q, p in sp}
        else:
            y_s = {(q, p): state_t[q, p][gc:2 * gc] + _mm(ch[q]["a_rk"][p], ch[q]["v_s"][p])
                   - _mm(ch[q]["a_rb"][p], u_b[q, p]) for q, p in sp}
        state = {(q, p): state[q, p] * ch[q]["p_last"][:, lanes[p]]
                 + _tn(jnp.concatenate([ch[q]["v_s"][p], -u_b[q, p]], axis=0), ch[q]["upd"][p]) for q, p in sp}
        for q in seqs:
            y_rows[q * n_sub + level] = jnp.concatenate(
                [y_s[q, p][0:cs] + y_s[q, p][cs:gc] for p in pairs], axis=1)
    for q, p in sp:
        s_ref[q * N_PAIRS + p] = state[q, p]

    y = jnp.concatenate(y_rows, axis=0) if len(y_rows) > 1 else y_rows[0]
    mean = _seg_sum(y, bones) * (1.0 / hd)
    d = y - mean
    var = _seg_sum(d * d, bones) * (1.0 / hd)
    yn = d * lax.rsqrt(var + GN_EPS) * lnw_ref[...] + lnb_ref[...]
    bonus = _seg_sum(xr * k2 * rk_ref[...], bones) * xv
    y_ref[...] = (yn + bonus) * gate

    @pl.when(step == n_steps - 1)
    def _fin():
        for q, p in sp:
            so_ref[q, 2 * p] = s_ref[q * N_PAIRS + p, 0:hd, 0:hd]
            so_ref[q, 2 * p + 1] = s_ref[q * N_PAIRS + p, hd:2 * hd, hd:2 * hd]


def _wkv(proj, prev0, s0, params, wl, bones, batch, seq, chunk, n_sub, n_seq=1, casts=()):
    ns = seq // (chunk * n_sub)
    assert n_seq == 1 or ns == 1, "several sequences per step only when a step covers them whole"
    rows = n_seq * chunk * n_sub
    nb = batch // n_seq
    rw = R_WIDTH

    def col(cb):
        return lambda b, c: (b * ns + c, cb)

    const2 = lambda b, c: (0, 0)
    vec = lambda n: pl.BlockSpec((1, n), const2)
    cast_in, cast_out, cast_shapes, cast_tr = _cast_specs(casts, nb * ns, lambda b, c: b * ns + c)
    mu, mul, w0, a0, k_k, k_a, r_k, ln_w, ln_b = params
    outs = pl.pallas_call(
        functools.partial(_wkv_kernel, chunk=chunk, n_sub=n_sub, n_seq=n_seq, n_steps=ns,
                          cast_transposed=cast_tr),
        out_shape=(jax.ShapeDtypeStruct((batch * seq, rw), F32),
                   jax.ShapeDtypeStruct((batch, R_HEADS, R_HEAD, R_HEAD), F32), *cast_shapes),
        grid=(nb, ns),
        in_specs=[
            pl.BlockSpec((rows, C_BLK), col(P_C // C_BLK)),
            pl.BlockSpec((rows, C_BLK), col(P_C // C_BLK + 1)),
            pl.BlockSpec((rows, LORA_PAD), col(P_L // LORA_PAD)),
            pl.BlockSpec((n_seq, 1, SHIFT_COLS), lambda b, c: (b, 0, 0)),
            pl.BlockSpec((n_seq, R_HEADS, R_HEAD, R_HEAD), lambda b, c: (b, 0, 0, 0)),
            vec(3 * rw), vec(LORA_PAD), vec(rw), vec(rw), vec(rw), vec(rw), vec(rw), vec(rw), vec(rw),
            pl.BlockSpec((LORA_PAD, 3 * rw), const2),
            pl.BlockSpec((SEG, SEG), const2),
            *cast_in,
        ],
        out_specs=(
            pl.BlockSpec((rows, rw), lambda b, c: (b * ns + c, 0)),
            pl.BlockSpec((n_seq, R_HEADS, R_HEAD, R_HEAD), lambda b, c: (b, 0, 0, 0)),
            *cast_out,
        ),
        scratch_shapes=[pltpu.VMEM((n_seq * N_PAIRS, PAIR, PAIR), F32), pltpu.VMEM((n_seq, SHIFT_COLS), F32)],
        compiler_params=pltpu.CompilerParams(
            dimension_semantics=("arbitrary", "arbitrary"), vmem_limit_bytes=VMEM_LIMIT),
        name="wkv",
    )(proj, proj, proj, prev0, s0, mu, mul, w0, a0, k_k, k_a, r_k, ln_w, ln_b, wl, bones,
      *[wgt for wgt, _ in casts])
    return outs[0], outs[1], outs[2:]


def _mix_kernel(x_ref, ya_ref, yr_ref, g_ref, wg_ref, wba_ref, wbr_ref, wo_ref, o_ref):
    x = x_ref[...]
    h = _rms(x, g_ref[...]).astype(BF16)
    ya = ya_ref[...].astype(BF16)
    yr = yr_ref[...].astype(BF16)
    acc = x
    for c in range(D_MODEL // MIX_CHUNK):
        ca = slice(c * MIX_CHUNK, (c + 1) * MIX_CHUNK)
        cr = slice(D_MODEL + c * MIX_CHUNK, D_MODEL + (c + 1) * MIX_CHUNK)
        mixed = (jax.nn.sigmoid(_mm(h, wg_ref[:, ca])) * _mm(ya, wba_ref[:, ca])
                 + jax.nn.sigmoid(_mm(h, wg_ref[:, cr])) * _mm(yr, wbr_ref[:, ca]))
        acc = acc + _mm(mixed.astype(BF16), wo_ref[ca, :])
    o_ref[...] = acc


def _resident(shape):
    return pl.BlockSpec(shape, lambda *_: (0,) * len(shape), pipeline_mode=pl.Buffered(1))


def _mix(x, ya, yr, g, wg, wba, wbr, wo, tm):
    m = x.shape[0]
    row = lambda i: (i, 0)
    return pl.pallas_call(
        _mix_kernel,
        out_shape=jax.ShapeDtypeStruct((m, D_MODEL), F32),
        grid=(m // tm,),
        in_specs=[
            pl.BlockSpec((tm, D_MODEL), row),
            pl.BlockSpec((tm, ATTN_WIDTH), row),
            pl.BlockSpec((tm, R_WIDTH), row),
            pl.BlockSpec((1, D_MODEL), lambda i: (0, 0)),
            _resident((D_MODEL, 2 * D_MODEL)),
            _resident((ATTN_WIDTH, D_MODEL)),
            _resident((R_WIDTH, D_MODEL)),
            _resident((D_MODEL, D_MODEL)),
        ],
        out_specs=pl.BlockSpec((tm, D_MODEL), row),
        compiler_params=pltpu.CompilerParams(
            dimension_semantics=("arbitrary",), vmem_limit_bytes=VMEM_LIMIT),
        name="mix",
    )(x, ya, yr, g, wg, wba, wbr, wo)


def _ffn_kernel(x_ref, g_ref, wu_ref, wd_ref, o_ref, h_ref):
    @pl.when(pl.program_id(1) == 0)
    def _():
        x = x_ref[...]
        h_ref[...] = _rms(x, g_ref[...]).astype(BF16)
        o_ref[...] = x

    u = _mm(h_ref[...], wu_ref[...])
    o_ref[...] += _mm(jnp.square(jnp.maximum(u, 0.0)).astype(BF16), wd_ref[...])


def _ffn_vmem_bytes(tm, tk):
    return 2 * 2 * tm * D_MODEL * 4 + 2 * 2 * D_MODEL * tk * 2 + tm * D_MODEL * 2 + tm * tk * (4 + 2)


def _ffn(x, g, wu, wd, tm, tk):
    m = x.shape[0]
    vmem = max(VMEM_LIMIT, _ffn_vmem_bytes(tm, tk))
    assert vmem <= V7X_VMEM_BYTES, (tm, tk)
    return pl.pallas_call(
        _ffn_kernel,
        out_shape=jax.ShapeDtypeStruct((m, D_MODEL), F32),
        grid=(m // tm, D_FF // tk),
        in_specs=[
            pl.BlockSpec((tm, D_MODEL), lambda i, k: (i, 0)),
            pl.BlockSpec((1, D_MODEL), lambda i, k: (0, 0)),
            pl.BlockSpec((D_MODEL, tk), lambda i, k: (0, k)),
            pl.BlockSpec((tk, D_MODEL), lambda i, k: (k, 0)),
        ],
        out_specs=pl.BlockSpec((tm, D_MODEL), lambda i, k: (i, 0)),
        scratch_shapes=[pltpu.VMEM((tm, D_MODEL), BF16)],
        compiler_params=pltpu.CompilerParams(
            dimension_semantics=("arbitrary", "arbitrary"), vmem_limit_bytes=vmem),
        name="ffn",
    )(x, g, wu, wd)


def _ple_kernel(x_ref, pe_ref, wg_ref, wp_ref, g_ref, o_ref):
    x = x_ref[...]
    gate = jax.nn.sigmoid(_mm(x.astype(BF16), wg_ref[...]))
    x = x + gate * _mm(pe_ref[...].astype(BF16), wp_ref[...])
    o_ref[...] = _rms(x, g_ref[...])


def _ple(x, pe, wg, wp, g, tm):
    m = x.shape[0]
    row = lambda i: (i, 0)
    return pl.pallas_call(
        _ple_kernel,
        out_shape=jax.ShapeDtypeStruct((m, D_MODEL), F32),
        grid=(m // tm,),
        in_specs=[
            pl.BlockSpec((tm, D_MODEL), row),
            pl.BlockSpec((tm, PLE_DIM), row),
            _resident((D_MODEL, D_MODEL)),
            _resident((PLE_DIM, D_MODEL)),
            pl.BlockSpec((1, D_MODEL), lambda i: (0, 0)),
        ],
        out_specs=pl.BlockSpec((tm, D_MODEL), row),
        compiler_params=pltpu.CompilerParams(
            dimension_semantics=("arbitrary",), vmem_limit_bytes=VMEM_LIMIT),
        name="ple",
    )(x, pe, wg, wp, g)


def _dense_tail(x, ya, yr, pe, wts, tiles):
    (g_mix, w_gates, wba, wbr, wo, g_ffn, wu, wd, wg, wp, g_fin) = wts
    x = _mix(x, ya, yr, g_mix, w_gates, wba, wbr, wo, tiles["mix_m"])
    x = _ffn(x, g_ffn, wu, wd, tiles["ffn_m"], tiles["ffn_k"])
    return _ple(x, pe, wg, wp, g_fin, tiles["ple_m"])


def _shift_out(proj, batch, seq):
    last = proj.reshape(batch, seq, P_COLS)[:, -1]
    return last[:, P_C:P_C + RWKV_COLS][None]


def kernel(x_prompt, x_sample, cache_k_win, cache_v_win, state_wkv, state_shift, p_prompt, p_sample,
           norm_mix, w_in, attn_sinks, rwkv_mu, rwkv_w0, rwkv_w2, rwkv_a0, rwkv_a2, rwkv_g2,
           rwkv_k_k, rwkv_k_a, rwkv_r_k, rwkv_ln_w, rwkv_ln_b, w_branch_attn, w_branch_rwkv,
           w_out, norm_ffn, w_ff_up, w_ff_down, w_ple_proj, w_ple_gate, norm_final):
    assert w_in.shape[0] == 1, "single-layer step"
    bp, tp = x_prompt.shape[0], x_prompt.shape[1]
    bs, ts = x_sample.shape[0], x_sample.shape[1]
    rw = R_WIDTH

    wl = jnp.zeros((LORA_PAD, 3 * rw), F32)
    wl = wl.at[0:DECAY_LORA, 0:rw].set(rwkv_w2[0])
    wl = wl.at[DECAY_LORA:DECAY_LORA + AAA_LORA, rw:2 * rw].set(rwkv_a2[0])
    wl = wl.at[DECAY_LORA + AAA_LORA:LORA_COLS, 2 * rw:3 * rw].set(rwkv_g2[0])
    wl = wl.astype(BF16)
    seg_id = np.arange(SEG) // R_HEAD
    bones = jnp.asarray(seg_id[:, None] == seg_id[None, :], BF16)
    mu = rwkv_mu[0]
    row = lambda v: v.reshape(1, -1)
    wkv_params = (row(mu[:3 * rw]), row(jnp.pad(mu[3 * rw:], (0, LORA_PAD - LORA_COLS))),
                  row(rwkv_w0[0]), row(rwkv_a0[0]), row(rwkv_k_k[0]), row(rwkv_k_a[0]),
                  row(rwkv_r_k[0]), row(rwkv_ln_w[0]), row(rwkv_ln_b[0]))
    g_mix = row(norm_mix[0])
    sinks = attn_sinks[0]

    ms = bs * ts
    xs = x_sample.reshape(ms, D_MODEL)
    w_in_t = jnp.transpose(w_in[0])
    proj_s, w_in_b = _proj_cast(xs, g_mix, w_in_t, 1024)

    tiles_p = dict(mix_m=256, ffn_m=1024, ffn_k=1024, ple_m=512)
    xp = x_prompt.reshape(bp * tp, D_MODEL)
    proj_p = _proj(xp, g_mix, w_in_b, 1024, 1280)
    ya_p = _attn_prompt(proj_p, sinks, bp, tp)
    yr_p, s_p, (wu, wd, wo, wg, wba, wbr, w_gates) = _wkv(
        proj_p, jnp.zeros((bp, 1, SHIFT_COLS), F32), jnp.zeros((bp, R_HEADS, R_HEAD, R_HEAD), F32),
        wkv_params, wl, bones, bp, tp, 64, 4,
        casts=((w_ff_up[0], None), (w_ff_down[0], None), (w_out[0], None), (w_ple_gate[0], None),
               (w_branch_attn[0], None), (w_branch_rwkv[0], None), (w_in_t, (P_GATES, 2 * D_MODEL))))
    dense = (g_mix, w_gates, wba, wbr, wo, row(norm_ffn[0]), wu, wd, wg, w_ple_proj[0].astype(BF16),
             row(norm_final))
    yp = _dense_tail(xp, ya_p, yr_p, p_prompt[0].reshape(bp * tp, PLE_DIM), dense, tiles_p)
    pp3 = proj_p.reshape(bp, tp, P_COLS)[:, -WINDOW:]
    k_p = pp3[:, :, P_K:P_K + KV_WIDTH].reshape(1, bp, WINDOW, N_KV_HEADS, HEAD_DIM)
    v_p = pp3[:, :, P_V:P_V + KV_WIDTH].reshape(1, bp, WINDOW, N_KV_HEADS, HEAD_DIM)

    tiles_s = dict(mix_m=ms, ffn_m=ms, ffn_k=2048, ple_m=ms)
    ck_t = jnp.transpose(cache_k_win[0], (0, 2, 3, 1)).reshape(bs, KV_WIDTH, WINDOW)
    cv_t = jnp.transpose(cache_v_win[0], (0, 2, 3, 1)).reshape(bs, KV_WIDTH, WINDOW)
    ya_s, nk_t, nv_t = _attn_sample(proj_s, sinks, ck_t, cv_t, bs, ts, SAMPLE_GROUP)
    prev_s = jnp.pad(state_shift[0], ((0, 0), (0, LORA_PAD - LORA_COLS))).reshape(bs, 1, SHIFT_COLS)
    yr_s, s_s, _ = _wkv(proj_s, prev_s, state_wkv[0], wkv_params, wl, bones, bs, ts, ts, 1, n_seq=SAMPLE_GROUP)
    ys = _dense_tail(xs, ya_s, yr_s, p_sample[0].reshape(ms, PLE_DIM), dense, tiles_s)

    return (yp.reshape(bp, tp, D_MODEL), ys.reshape(bs, ts, D_MODEL),
            k_p, v_p, s_p[None], _shift_out(proj_p, bp, tp),
            jnp.transpose(nk_t.reshape(bs, N_KV_HEADS, HEAD_DIM, WINDOW), (0, 3, 1, 2))[None],
            jnp.transpose(nv_t.reshape(bs, N_KV_HEADS, HEAD_DIM, WINDOW), (0, 3, 1, 2))[None],
            s_s[None], _shift_out(proj_s, bs, ts))
```

```python
import functools

import numpy as np
import jax
import jax.numpy as jnp
from jax import lax
from jax.experimental import pallas as pl
from jax.experimental.pallas import tpu as pltpu

F32 = jnp.float32
BF16 = jnp.bfloat16

D_MODEL = 2048
PLE_DIM = 256
HEAD_DIM = 64
N_Q_HEADS = 16
N_KV_HEADS = 4
GQA_GROUP = 4
ATTN_WIDTH = 1024
KV_WIDTH = 256
WINDOW = 128
ALIBI_MAX = 8.0
R_HEAD = 64
R_WIDTH = 1024
R_HEADS = 16
DECAY_LORA = 64
AAA_LORA = 64
GATE_LORA = 160
LORA_COLS = DECAY_LORA + AAA_LORA + GATE_LORA
LORA_PAD = 512
RWKV_COLS = 3 * R_WIDTH + LORA_COLS
D_FF = 4 * D_MODEL
NORM_EPS = 1e-6
GN_EPS = 64e-5

P_Q = 0
P_K = ATTN_WIDTH
P_V = ATTN_WIDTH + KV_WIDTH
P_C = ATTN_WIDTH + 2 * KV_WIDTH
P_L = P_C + 3 * R_WIDTH
P_COLS = P_L + LORA_PAD
P_GATES = P_C + RWKV_COLS
C_BLK = P_C
SHIFT_COLS = 3 * R_WIDTH + LORA_PAD

PAIR = 128
N_PAIRS = R_WIDTH // PAIR
SEG = 256
MIX_CHUNK = 1024
SAMPLE_GROUP = 8
PROMPT_BLOCKS = 4

V7X_VMEM_BYTES = 64 * 1024 * 1024
VMEM_LIMIT = 56 * 1024 * 1024


def _mm(a, b):
    return jnp.dot(a, b, preferred_element_type=F32)


def _nt(a, b):
    return lax.dot_general(a, b, (((1,), (1,)), ((), ())), preferred_element_type=F32)


def _tn(a, b):
    return lax.dot_general(a, b, (((0,), (0,)), ((), ())), preferred_element_type=F32)


def _rms(x, g):
    ms = jnp.mean(x * x, axis=-1, keepdims=True)
    return x * lax.rsqrt(ms + NORM_EPS) * g


def _alibi_slope(hq):
    return float(2.0 ** (-ALIBI_MAX * (hq + 1) / N_Q_HEADS))


def _cast_specs(casts, n_steps, step_of):
    in_specs, out_specs, shapes = [], [], []
    for wgt, span in casts:
        start, count = (0, wgt.shape[0]) if span is None else span
        blk, width = count // n_steps, wgt.shape[1]
        assert blk * n_steps == count, (wgt.shape, span, n_steps)
        if span is None:
            assert blk % 16 == 0, blk
            in_specs.append(pl.BlockSpec((blk, width), lambda *g: (step_of(*g), 0)))
            out_specs.append(pl.BlockSpec((blk, width), lambda *g: (step_of(*g), 0)))
            shapes.append(jax.ShapeDtypeStruct((count, width), BF16))
        else:
            assert blk % 128 == 0 and start % 8 == 0, span
            in_specs.append(pl.BlockSpec(
                (pl.Element(blk), pl.Element(width)),
                lambda *g, start=start, blk=blk: (pl.multiple_of(start + blk * step_of(*g), 8), 0)))
            out_specs.append(pl.BlockSpec((width, blk), lambda *g: (0, step_of(*g))))
            shapes.append(jax.ShapeDtypeStruct((width, count), BF16))
    return in_specs, out_specs, shapes, tuple(span is not None for _, span in casts)


def _run_casts(srcs, dsts, transposed):
    for src, dst, tr in zip(srcs, dsts, transposed):
        dst[...] = (src[...].T if tr else src[...]).astype(BF16)


def _proj_kernel(x_ref, g_ref, w_ref, o_ref, h_ref):
    @pl.when(pl.program_id(1) == 0)
    def _():
        h_ref[...] = _rms(x_ref[...], g_ref[...]).astype(BF16)

    o_ref[...] = _nt(h_ref[...], w_ref[...])


def _proj_cast_kernel(x_ref, g_ref, w_ref, o_ref, wb_ref, h_ref):
    @pl.when(pl.program_id(0) == 0)
    def _():
        h_ref[...] = _rms(x_ref[...], g_ref[...]).astype(BF16)

    wb = w_ref[...].astype(BF16)
    wb_ref[...] = wb
    o_ref[...] = _nt(h_ref[...], wb)


def _proj_cast(x, g, wt_f32, tn):
    m = x.shape[0]
    return pl.pallas_call(
        _proj_cast_kernel,
        out_shape=(jax.ShapeDtypeStruct((m, P_COLS), F32), jax.ShapeDtypeStruct((P_COLS, D_MODEL), BF16)),
        grid=(P_COLS // tn,),
        in_specs=[
            pl.BlockSpec((m, D_MODEL), lambda j: (0, 0)),
            pl.BlockSpec((1, D_MODEL), lambda j: (0, 0)),
            pl.BlockSpec((tn, D_MODEL), lambda j: (j, 0)),
        ],
        out_specs=(pl.BlockSpec((m, tn), lambda j: (0, j)), pl.BlockSpec((tn, D_MODEL), lambda j: (j, 0))),
        scratch_shapes=[pltpu.VMEM((m, D_MODEL), BF16)],
        compiler_params=pltpu.CompilerParams(
            dimension_semantics=("arbitrary",), vmem_limit_bytes=VMEM_LIMIT),
        name="proj_cast",
    )(x, g, wt_f32)


def _proj(x, g, w, tm, tn):
    m = x.shape[0]
    return pl.pallas_call(
        _proj_kernel,
        out_shape=jax.ShapeDtypeStruct((m, P_COLS), F32),
        grid=(m // tm, P_COLS // tn),
        in_specs=[
            pl.BlockSpec((tm, D_MODEL), lambda i, j: (i, 0)),
            pl.BlockSpec((1, D_MODEL), lambda i, j: (0, 0)),
            pl.BlockSpec((tn, D_MODEL), lambda i, j: (j, 0)),
        ],
        out_specs=pl.BlockSpec((tm, tn), lambda i, j: (i, j)),
        scratch_shapes=[pltpu.VMEM((tm, D_MODEL), BF16)],
        compiler_params=pltpu.CompilerParams(
            dimension_semantics=("arbitrary", "arbitrary"), vmem_limit_bytes=VMEM_LIMIT),
        name="proj",
    )(x, g, w)


def _attend_heads(scores, apply_values, sinks):
    heads = range(len(scores))
    m = []
    for h in heads:
        mh = sinks[h]
        for s in scores[h]:
            mh = jnp.maximum(mh, jnp.max(s, axis=-1, keepdims=True))
        m.append(mh)
    ps = [[jnp.exp(s - m[h]) for s in scores[h]] for h in heads]
    den = []
    for h in heads:
        dh = jnp.exp(sinks[h] - m[h])
        for p in ps[h]:
            dh = dh + jnp.sum(p, axis=-1, keepdims=True)
        den.append(dh)
    outs = []
    for h in heads:
        o = None
        for p, pv in zip(ps[h], apply_values[h]):
            t = pv(p.astype(BF16))
            o = t if o is None else o + t
        outs.append(o * (1.0 / den[h]))
    return outs


def _head_slices():
    q_sl = [slice(hq * HEAD_DIM, (hq + 1) * HEAD_DIM) for hq in range(N_Q_HEADS)]
    kv_sl = [slice((hq // GQA_GROUP) * HEAD_DIM, (hq // GQA_GROUP + 1) * HEAD_DIM) for hq in range(N_Q_HEADS)]
    return q_sl, kv_sl


def _alibi_bias(dist, valid):
    slopes = np.array([_alibi_slope(hq) for hq in range(N_Q_HEADS)], np.float32)
    return np.where(valid[None], -slopes[:, None, None] * dist[None].astype(np.float32), -np.inf).astype(np.float32)


def _attn_prompt_kernel(sink_ref, bias_ref, q_ref, kp_ref, kc_ref, vp_ref, vc_ref, o_ref):
    w, grp = WINDOW, GQA_GROUP
    n_blk = q_ref.shape[0] // w
    kj = lax.broadcasted_iota(jnp.int32, (grp * w, 2 * w), 1)
    no_prev = (kj < w) & (pl.program_id(1) == 0)
    member = lax.broadcasted_iota(jnp.int32, (grp * w, 1), 0) // w
    q = q_ref[...] * (HEAD_DIM ** -0.5)
    kc, vc = kc_ref[...].astype(BF16), vc_ref[...].astype(BF16)
    ks = jnp.concatenate([kp_ref[...].astype(BF16), kc], axis=0)
    vs = jnp.concatenate([vp_ref[...].astype(BF16), vc], axis=0)
    q_sl, kv_sl = _head_slices()
    sinks = []
    for h in range(N_KV_HEADS):
        col = jnp.full((grp * w, 1), sink_ref[h * grp], F32)
        for g in range(1, grp):
            col = jnp.where(member == g, sink_ref[h * grp + g], col)
        sinks.append(col)
    scores, values = [], []
    for s in range(n_blk):
        k2, v2 = ks[s * w:(s + 2) * w], vs[s * w:(s + 2) * w]
        qb = q[s * w:(s + 1) * w]
        for h in range(N_KV_HEADS):
            q4 = jnp.concatenate([qb[:, q_sl[h * grp + g]] for g in range(grp)], axis=0).astype(BF16)
            sc = _nt(q4, k2[:, kv_sl[h * grp]]) + bias_ref[h]
            scores.append([jnp.where(no_prev, -jnp.inf, sc) if s == 0 else sc])
            values.append([functools.partial(_mm, b=v2[:, kv_sl[h * grp]])])
    outs = _attend_heads(scores, values, sinks * n_blk)
    for s in range(n_blk):
        for h in range(N_KV_HEADS):
            o4 = outs[s * N_KV_HEADS + h]
            for g in range(grp):
                o_ref[s * w:(s + 1) * w, q_sl[h * grp + g]] = o4[g * w:(g + 1) * w]


def _attn_prompt(proj, sinks, batch, seq):
    rows = PROMPT_BLOCKS * WINDOW
    ns = seq // rows
    kcol, vcol = P_K // KV_WIDTH, P_V // KV_WIDTH

    def cur(c):
        return lambda b, i: (b * ns + i, c)

    def prev(c):
        return lambda b, i: (b * ns * PROMPT_BLOCKS + jnp.maximum(i * PROMPT_BLOCKS - 1, 0), c)

    ti = np.arange(WINDOW)[:, None]
    kj = np.arange(2 * WINDOW)[None, :]
    dist = ti - kj + WINDOW
    bias = _alibi_bias(dist, (dist >= 0) & (dist <= WINDOW))
    bias = bias.reshape(N_KV_HEADS, GQA_GROUP * WINDOW, 2 * WINDOW)
    return pl.pallas_call(
        _attn_prompt_kernel,
        out_shape=jax.ShapeDtypeStruct((batch * seq, ATTN_WIDTH), F32),
        grid=(batch, ns),
        in_specs=[
            pl.BlockSpec(memory_space=pltpu.SMEM),
            _resident((N_KV_HEADS, GQA_GROUP * WINDOW, 2 * WINDOW)),
            pl.BlockSpec((rows, ATTN_WIDTH), cur(P_Q // ATTN_WIDTH)),
            pl.BlockSpec((WINDOW, KV_WIDTH), prev(kcol)),
            pl.BlockSpec((rows, KV_WIDTH), cur(kcol)),
            pl.BlockSpec((WINDOW, KV_WIDTH), prev(vcol)),
            pl.BlockSpec((rows, KV_WIDTH), cur(vcol)),
        ],
        out_specs=pl.BlockSpec((rows, ATTN_WIDTH), lambda b, i: (b * ns + i, 0)),
        compiler_params=pltpu.CompilerParams(dimension_semantics=("arbitrary", "arbitrary")),
        name="attn_prompt",
    )(sinks, jnp.asarray(bias), proj, proj, proj, proj, proj)


def _attn_sample_kernel(sink_ref, q_ref, kn_ref, vn_ref, ckt_ref, cvt_ref, o_ref, nkt_ref, nvt_ref, *, seq):
    t, w = seq, WINDOW
    n_seq = q_ref.shape[0] // t
    ti = lax.broadcasted_iota(jnp.int32, (t, w), 0)
    cj = lax.broadcasted_iota(jnp.int32, (t, w), 1)
    dist_c = (ti - cj + w).astype(F32)
    valid_c = cj >= ti
    ti2 = lax.broadcasted_iota(jnp.int32, (t, t), 0)
    tj2 = lax.broadcasted_iota(jnp.int32, (t, t), 1)
    dist_n = (ti2 - tj2).astype(F32)
    valid_n = tj2 <= ti2
    hqs = range(N_Q_HEADS)
    q_sl, kv_sl = _head_slices()
    bias_c = [jnp.where(valid_c, -_alibi_slope(hq) * dist_c, -jnp.inf) for hq in hqs]
    bias_n = [jnp.where(valid_n, -_alibi_slope(hq) * dist_n, -jnp.inf) for hq in hqs]
    q_all = q_ref[...] * (HEAD_DIM ** -0.5)
    is_new = lax.broadcasted_iota(jnp.int32, (1, w), 1) >= w - t
    put = (cj == ti + (w - t)).astype(BF16)

    def place_new(x):
        hi = x.astype(BF16)
        rem = x - hi.astype(F32)
        mid = rem.astype(BF16)
        return _tn(hi, put) + _tn(mid, put) + _tn((rem - mid.astype(F32)).astype(BF16), put)

    scores, values = [], []
    for s in range(n_seq):
        rs = slice(s * t, (s + 1) * t)
        ckt, cvt = ckt_ref[s], cvt_ref[s]
        nkt_ref[s] = jnp.where(is_new, place_new(kn_ref[rs, :]), pltpu.roll(ckt, w - t, axis=1))
        nvt_ref[s] = jnp.where(is_new, place_new(vn_ref[rs, :]), pltpu.roll(cvt, w - t, axis=1))
        cktb, cvtb = ckt.astype(BF16), cvt.astype(BF16)
        knb, vnb = kn_ref[rs, :].astype(BF16), vn_ref[rs, :].astype(BF16)
        q = q_all[rs]
        qs = [q[:, q_sl[hq]].astype(BF16) for hq in hqs]
        scores += [[_mm(qs[hq], cktb[kv_sl[hq], :]) + bias_c[hq], _nt(qs[hq], knb[:, kv_sl[hq]]) + bias_n[hq]]
                   for hq in hqs]
        values += [[functools.partial(_nt, b=cvtb[kv_sl[hq], :]), functools.partial(_mm, b=vnb[:, kv_sl[hq]])]
                   for hq in hqs]
    outs = _attend_heads(scores, values, [sink_ref[hq] for _ in range(n_seq) for hq in hqs])
    for s in range(n_seq):
        for hq in hqs:
            o_ref[s * t:(s + 1) * t, q_sl[hq]] = outs[s * N_Q_HEADS + hq]


def _attn_sample(proj, sinks, cache_kt, cache_vt, batch, seq, n_seq):
    kcol, vcol = P_K // KV_WIDTH, P_V // KV_WIDTH
    rows = n_seq * seq
    win = jax.ShapeDtypeStruct((batch, KV_WIDTH, WINDOW), F32)
    return pl.pallas_call(
        functools.partial(_attn_sample_kernel, seq=seq),
        out_shape=(jax.ShapeDtypeStruct((batch * seq, ATTN_WIDTH), F32), win, win),
        grid=(batch // n_seq,),
        in_specs=[
            pl.BlockSpec(memory_space=pltpu.SMEM),
            pl.BlockSpec((rows, ATTN_WIDTH), lambda b: (b, P_Q // ATTN_WIDTH)),
            pl.BlockSpec((rows, KV_WIDTH), lambda b: (b, kcol)),
            pl.BlockSpec((rows, KV_WIDTH), lambda b: (b, vcol)),
            pl.BlockSpec((n_seq, KV_WIDTH, WINDOW), lambda b: (b, 0, 0)),
            pl.BlockSpec((n_seq, KV_WIDTH, WINDOW), lambda b: (b, 0, 0)),
        ],
        out_specs=(
            pl.BlockSpec((rows, ATTN_WIDTH), lambda b: (b, 0)),
            pl.BlockSpec((n_seq, KV_WIDTH, WINDOW), lambda b: (b, 0, 0)),
            pl.BlockSpec((n_seq, KV_WIDTH, WINDOW), lambda b: (b, 0, 0)),
        ),
        compiler_params=pltpu.CompilerParams(dimension_semantics=("arbitrary",)),
        name="attn_sample",
    )(sinks, proj, proj, proj, cache_kt, cache_vt)


def _seg_sum(x, bones):
    rows = x.shape[0]
    hi = x.astype(BF16).astype(F32)
    lo = x - hi
    groups = [slice(j * SEG, (j + 1) * SEG) for j in range(R_WIDTH // SEG)]
    lhs = jnp.concatenate([t[:, sl] for sl in groups for t in (hi, lo)], axis=0).astype(BF16)
    out = _mm(lhs, bones)
    return jnp.concatenate(
        [out[2 * j * rows:(2 * j + 1) * rows] + out[(2 * j + 1) * rows:(2 * j + 2) * rows]
         for j in range(len(groups))], axis=1)


def _wkv_kernel(pa_ref, pb_ref, pl_ref, prev_ref, s0_ref,
                mu_ref, mul_ref, w0_ref, a0_ref, kk_ref, ka_ref, rk_ref, lnw_ref, lnb_ref,
                wl_ref, bones_ref, *rest, chunk, n_sub, n_seq, n_steps, cast_transposed):
    n_cast = len(cast_transposed)
    cast_src, (y_ref, so_ref) = rest[:n_cast], rest[n_cast:n_cast + 2]
    cast_dst, (s_ref, carry_ref) = rest[n_cast + 2:2 * n_cast + 2], rest[2 * n_cast + 2:]
    _run_casts(cast_src, cast_dst, cast_transposed)

    step = pl.program_id(1)
    cs = chunk
    seq_rows = n_sub * cs
    rows = n_seq * seq_rows
    gc = 2 * cs
    hd = R_HEAD
    w = R_WIDTH
    seqs = range(n_seq)

    @pl.when(step == 0)
    def _init():
        s_ref[...] = jnp.zeros(s_ref.shape, F32)
        for q in seqs:
            carry_ref[q:q + 1, :] = prev_ref[q]
            for p in range(N_PAIRS):
                s_ref[q * N_PAIRS + p, 0:hd, 0:hd] = s0_ref[q, 2 * p]
                s_ref[q * N_PAIRS + p, hd:2 * hd, hd:2 * hd] = s0_ref[q, 2 * p + 1]

    row = lax.broadcasted_iota(jnp.int32, (rows, 1), 0)

    def token_shift(x, lo, hi, mu):
        shifted = pltpu.roll(x, 1, axis=0)
        for q in seqs:
            shifted = jnp.where(row == q * seq_rows, carry_ref[q:q + 1, lo:hi], shifted)
        return x + (shifted - x) * mu

    cols = jnp.concatenate([pa_ref[...], pb_ref[...]], axis=1)
    l_raw = pl_ref[...]
    xx = token_shift(cols, 0, 3 * w, mu_ref[...])
    xr, xk, xv = xx[:, 0:w], xx[:, w:2 * w], xx[:, 2 * w:3 * w]
    xl = token_shift(l_raw, 3 * w, SHIFT_COLS, mul_ref[...])
    for q in seqs:
        last = (q + 1) * seq_rows
        carry_ref[q:q + 1, 0:3 * w] = cols[last - 1:last, :]
        carry_ref[q:q + 1, 3 * w:] = l_raw[last - 1:last, :]

    lane_l = lax.broadcasted_iota(jnp.int32, (1, LORA_PAD), 1)
    act = jnp.where(lane_l < DECAY_LORA, jnp.tanh(xl),
                    jnp.where(lane_l < DECAY_LORA + AAA_LORA, xl,
                              jnp.where(lane_l < LORA_COLS, jax.nn.sigmoid(xl), 0.0)))
    up = _mm(act.astype(BF16), wl_ref[...])
    z = -(w0_ref[...] + up[:, 0:w])
    softplus = jnp.maximum(z, 0.0) + jnp.log1p(jnp.exp(-jnp.abs(z)))
    lwd = -jnp.exp(-softplus - 0.5)
    a = jax.nn.sigmoid(a0_ref[...] + up[:, w:2 * w])
    gate = up[:, 2 * w:3 * w]

    bones = bones_ref[...]
    kkn = xk * kk_ref[...]
    kk = kkn * lax.rsqrt(jnp.maximum(_seg_sum(kkn * kkn, bones), 1e-24))
    k2 = xk * (1.0 + (a - 1.0) * ka_ref[...])
    b = kk * a

    tri = (lax.broadcasted_iota(jnp.int32, (cs, cs), 0)
           >= lax.broadcasted_iota(jnp.int32, (cs, cs), 1)).astype(BF16)
    ri = lax.broadcasted_iota(jnp.int32, (gc, gc), 0)
    ci = lax.broadcasted_iota(jnp.int32, (gc, gc), 1)
    same_head = (ri >= cs) == (ci >= cs)
    strict = same_head & (ci < ri)
    incl = same_head & (ci <= ri)
    eye = (ri == ci).astype(F32)
    head0 = lax.broadcasted_iota(jnp.int32, (1, PAIR), 1) < hd

    def stack(x):
        return jnp.concatenate([jnp.where(head0, x, 0.0), jnp.where(head0, 0.0, x)], axis=0).astype(BF16)

    pairs = range(N_PAIRS)
    lanes = [slice(p * PAIR, (p + 1) * PAIR) for p in pairs]
    squarings = cs.bit_length() - 2

    chunks = []
    for sub in range(n_seq * n_sub):
        rs = slice(sub * cs, (sub + 1) * cs)
        lw_c = lwd[rs]
        lw_hi = lw_c.astype(BF16)
        lw_r = lw_c - lw_hi.astype(F32)
        lw_mid = lw_r.astype(BF16)
        cum = _mm(tri, lw_hi) + _mm(tri, lw_mid) + _mm(tri, (lw_r - lw_mid.astype(F32)).astype(BF16))
        cum_last = cum[cs - 1:cs, :]
        e_inv = jnp.exp(-cum)
        e_last = jnp.exp(cum_last - cum)
        kq = kk[rs] * jnp.exp(cum - lw_c)
        rq = xr[rs] * jnp.exp(cum)
        kd = k2[rs] * e_inv
        bd = b[rs] * e_inv
        kdp = k2[rs] * e_last
        bdp = b[rs] * e_last
        xv_c = xv[rs]
        kq_s = [stack(kq[:, sl]) for sl in lanes]
        bd_s = [stack(bd[:, sl]) for sl in lanes]
        kd_s = [stack(kd[:, sl]) for sl in lanes]
        rq_s = [stack(rq[:, sl]) for sl in lanes]
        probes = [jnp.concatenate([kq_s[p], rq_s[p]], axis=0) for p in pairs]
        if gc % PAIR == 0:
            a_all = [_nt(probes[p], jnp.concatenate([bd_s[p], kd_s[p]], axis=0)) for p in pairs]
            a_kb = [jnp.where(strict, t[0:gc, 0:gc], 0.0) for t in a_all]
            a_kk = [jnp.where(strict, t[0:gc, gc:2 * gc], 0.0).astype(BF16) for t in a_all]
            a_rb = [jnp.where(incl, t[gc:2 * gc, 0:gc], 0.0).astype(BF16) for t in a_all]
            a_rk = [jnp.where(incl, t[gc:2 * gc, gc:2 * gc], 0.0).astype(BF16) for t in a_all]
        else:
            a_kb = [jnp.where(strict, _nt(kq_s[p], bd_s[p]), 0.0) for p in pairs]
            a_kk = [jnp.where(strict, _nt(kq_s[p], kd_s[p]), 0.0).astype(BF16) for p in pairs]
            a_rb = [jnp.where(incl, _nt(rq_s[p], bd_s[p]), 0.0).astype(BF16) for p in pairs]
            a_rk = [jnp.where(incl, _nt(rq_s[p], kd_s[p]), 0.0).astype(BF16) for p in pairs]
        inv = [eye - t for t in a_kb]
        if squarings >= 1:
            apow_b = [t.astype(BF16) for t in a_kb]
            apow_b = [_mm(t, t).astype(BF16) for t in apow_b]
            for _ in range(squarings - 1):
                both = [_mm(jnp.concatenate([apow_b[p], inv[p].astype(BF16)], axis=0), apow_b[p]) for p in pairs]
                apow_b = [t[0:gc].astype(BF16) for t in both]
                inv = [inv[p] + both[p][gc:2 * gc] for p in pairs]
            inv = [inv[p] + _mm(inv[p].astype(BF16), apow_b[p]) for p in pairs]
        chunks.append(dict(
            probes=probes, a_kk=a_kk, a_rk=a_rk, a_rb=a_rb, inv=[t.astype(BF16) for t in inv],
            v_s=[stack(xv_c[:, sl]) for sl in lanes],
            upd=[jnp.concatenate([stack(kdp[:, sl]), stack(bdp[:, sl])], axis=0) for sl in lanes],
            p_last=jnp.exp(cum_last)))

    sp = [(q, p) for q in seqs for p in pairs]
    state = {(q, p): s_ref[q * N_PAIRS + p] for q, p in sp}
    y_rows = [None] * (n_seq * n_sub)
    for level in range(n_sub):
        ch = {q: chunks[q * n_sub + level] for q in seqs}
        s_b = {k: state[k].astype(BF16) for k in sp}
        state_t = {(q, p): _nt(ch[q]["probes"][p], s_b[q, p]) for q, p in sp}
        rhs = {(q, p): state_t[q, p][0:gc] + _mm(ch[q]["a_kk"][p], ch[q]["v_s"][p]) for q, p in sp}
        u_b = {(q, p): _mm(ch[q]["inv"][p], rhs[q, p].astype(BF16)).astype(BF16) for q, p in sp}
        if gc % PAIR == 0:
            y_s = {(q, p): state_t[q, p][gc:2 * gc]
                   + _mm(jnp.concatenate([ch[q]["a_rk"][p], -ch[q]["a_rb"][p]], axis=1),
                         jnp.concatenate([ch[q]["v_s"][p], u_b[q, p]], axis=0)) for q, p in sp}
        else:
            y_s = {(q, p): state_t[q, p][gc:2 * gc] + _mm(ch[q]["a_rk"][p], ch[q]["v_s"][p])
                   - _mm(ch[q]["a_rb"][p], u_b[q, p]) for q, p in sp}
        state = {(q, p): state[q, p] * ch[q]["p_last"][:, lanes[p]]
                 + _tn(jnp.concatenate([ch[q]["v_s"][p], -u_b[q, p]], axis=0), ch[q]["upd"][p]) for q, p in sp}
        for q in seqs:
            y_rows[q * n_sub + level] = jnp.concatenate(
                [y_s[q, p][0:cs] + y_s[q, p][cs:gc] for p in pairs], axis=1)
    for q, p in sp:
        s_ref[q * N_PAIRS + p] = state[q, p]

    y = jnp.concatenate(y_rows, axis=0) if len(y_rows) > 1 else y_rows[0]
    mean = _seg_sum(y, bones) * (1.0 / hd)
    d = y - mean
    var = _seg_sum(d * d, bones) * (1.0 / hd)
    yn = d * lax.rsqrt(var + GN_EPS) * lnw_ref[...] + lnb_ref[...]
    bonus = _seg_sum(xr * k2 * rk_ref[...], bones) * xv
    y_ref[...] = (yn + bonus) * gate

    @pl.when(step == n_steps - 1)
    def _fin():
        for q, p in sp:
            so_ref[q, 2 * p] = s_ref[q * N_PAIRS + p, 0:hd, 0:hd]
            so_ref[q, 2 * p + 1] = s_ref[q * N_PAIRS + p, hd:2 * hd, hd:2 * hd]


def _wkv(proj, prev0, s0, params, wl, bones, batch, seq, chunk, n_sub, n_seq=1, casts=()):
    ns = seq // (chunk * n_sub)
    assert n_seq == 1 or ns == 1, "several sequences per step only when a step covers them whole"
    rows = n_seq * chunk * n_sub
    nb = batch // n_seq
    rw = R_WIDTH

    def col(cb):
        return lambda b, c: (b * ns + c, cb)

    const2 = lambda b, c: (0, 0)
    vec = lambda n: pl.BlockSpec((1, n), const2)
    cast_in, cast_out, cast_shapes, cast_tr = _cast_specs(casts, nb * ns, lambda b, c: b * ns + c)
    mu, mul, w0, a0, k_k, k_a, r_k, ln_w, ln_b = params
    outs = pl.pallas_call(
        functools.partial(_wkv_kernel, chunk=chunk, n_sub=n_sub, n_seq=n_seq, n_steps=ns,
                          cast_transposed=cast_tr),
        out_shape=(jax.ShapeDtypeStruct((batch * seq, rw), F32),
                   jax.ShapeDtypeStruct((batch, R_HEADS, R_HEAD, R_HEAD), F32), *cast_shapes),
        grid=(nb, ns),
        in_specs=[
            pl.BlockSpec((rows, C_BLK), col(P_C // C_BLK)),
            pl.BlockSpec((rows, C_BLK), col(P_C // C_BLK + 1)),
            pl.BlockSpec((rows, LORA_PAD), col(P_L // LORA_PAD)),
            pl.BlockSpec((n_seq, 1, SHIFT_COLS), lambda b, c: (b, 0, 0)),
            pl.BlockSpec((n_seq, R_HEADS, R_HEAD, R_HEAD), lambda b, c: (b, 0, 0, 0)),
            vec(3 * rw), vec(LORA_PAD), vec(rw), vec(rw), vec(rw), vec(rw), vec(rw), vec(rw), vec(rw),
            pl.BlockSpec((LORA_PAD, 3 * rw), const2),
            pl.BlockSpec((SEG, SEG), const2),
            *cast_in,
        ],
        out_specs=(
            pl.BlockSpec((rows, rw), lambda b, c: (b * ns + c, 0)),
            pl.BlockSpec((n_seq, R_HEADS, R_HEAD, R_HEAD), lambda b, c: (b, 0, 0, 0)),
            *cast_out,
        ),
        scratch_shapes=[pltpu.VMEM((n_seq * N_PAIRS, PAIR, PAIR), F32), pltpu.VMEM((n_seq, SHIFT_COLS), F32)],
        compiler_params=pltpu.CompilerParams(
            dimension_semantics=("arbitrary", "arbitrary"), vmem_limit_bytes=VMEM_LIMIT),
        name="wkv",
    )(proj, proj, proj, prev0, s0, mu, mul, w0, a0, k_k, k_a, r_k, ln_w, ln_b, wl, bones,
      *[wgt for wgt, _ in casts])
    return outs[0], outs[1], outs[2:]


def _mix_kernel(x_ref, ya_ref, yr_ref, g_ref, wg_ref, wba_ref, wbr_ref, wo_ref, o_ref):
    x = x_ref[...]
    h = _rms(x, g_ref[...]).astype(BF16)
    ya = ya_ref[...].astype(BF16)
    yr = yr_ref[...].astype(BF16)
    acc = x
    for c in range(D_MODEL // MIX_CHUNK):
        ca = slice(c * MIX_CHUNK, (c + 1) * MIX_CHUNK)
        cr = slice(D_MODEL + c * MIX_CHUNK, D_MODEL + (c + 1) * MIX_CHUNK)
        mixed = (jax.nn.sigmoid(_mm(h, wg_ref[:, ca])) * _mm(ya, wba_ref[:, ca])
                 + jax.nn.sigmoid(_mm(h, wg_ref[:, cr])) * _mm(yr, wbr_ref[:, ca]))
        acc = acc + _mm(mixed.astype(BF16), wo_ref[ca, :])
    o_ref[...] = acc


def _resident(shape):
    return pl.BlockSpec(shape, lambda *_: (0,) * len(shape), pipeline_mode=pl.Buffered(1))


def _mix(x, ya, yr, g, wg, wba, wbr, wo, tm):
    m = x.shape[0]
    row = lambda i: (i, 0)
    return pl.pallas_call(
        _mix_kernel,
        out_shape=jax.ShapeDtypeStruct((m, D_MODEL), F32),
        grid=(m // tm,),
        in_specs=[
            pl.BlockSpec((tm, D_MODEL), row),
            pl.BlockSpec((tm, ATTN_WIDTH), row),
            pl.BlockSpec((tm, R_WIDTH), row),
            pl.BlockSpec((1, D_MODEL), lambda i: (0, 0)),
            _resident((D_MODEL, 2 * D_MODEL)),
            _resident((ATTN_WIDTH, D_MODEL)),
            _resident((R_WIDTH, D_MODEL)),
            _resident((D_MODEL, D_MODEL)),
        ],
        out_specs=pl.BlockSpec((tm, D_MODEL), row),
        compiler_params=pltpu.CompilerParams(
            dimension_semantics=("arbitrary",), vmem_limit_bytes=VMEM_LIMIT),
        name="mix",
    )(x, ya, yr, g, wg, wba, wbr, wo)


def _ffn_kernel(x_ref, g_ref, wu_ref, wd_ref, o_ref, h_ref):
    @pl.when(pl.program_id(1) == 0)
    def _():
        x = x_ref[...]
        h_ref[...] = _rms(x, g_ref[...]).astype(BF16)
        o_ref[...] = x

    u = _mm(h_ref[...], wu_ref[...])
    o_ref[...] += _mm(jnp.square(jnp.maximum(u, 0.0)).astype(BF16), wd_ref[...])


def _ffn_vmem_bytes(tm, tk):
    return 2 * 2 * tm * D_MODEL * 4 + 2 * 2 * D_MODEL * tk * 2 + tm * D_MODEL * 2 + tm * tk * (4 + 2)


def _ffn(x, g, wu, wd, tm, tk):
    m = x.shape[0]
    vmem = max(VMEM_LIMIT, _ffn_vmem_bytes(tm, tk))
    assert vmem <= V7X_VMEM_BYTES, (tm, tk)
    return pl.pallas_call(
        _ffn_kernel,
        out_shape=jax.ShapeDtypeStruct((m, D_MODEL), F32),
        grid=(m // tm, D_FF // tk),
        in_specs=[
            pl.BlockSpec((tm, D_MODEL), lambda i, k: (i, 0)),
            pl.BlockSpec((1, D_MODEL), lambda i, k: (0, 0)),
            pl.BlockSpec((D_MODEL, tk), lambda i, k: (0, k)),
            pl.BlockSpec((tk, D_MODEL), lambda i, k: (k, 0)),
        ],
        out_specs=pl.BlockSpec((tm, D_MODEL), lambda i, k: (i, 0)),
        scratch_shapes=[pltpu.VMEM((tm, D_MODEL), BF16)],
        compiler_params=pltpu.CompilerParams(
            dimension_semantics=("arbitrary", "arbitrary"), vmem_limit_bytes=vmem),
        name="ffn",
    )(x, g, wu, wd)


def _ple_kernel(x_ref, pe_ref, wg_ref, wp_ref, g_ref, o_ref):
    x = x_ref[...]
    gate = jax.nn.sigmoid(_mm(x.astype(BF16), wg_ref[...]))
    x = x + gate * _mm(pe_ref[...].astype(BF16), wp_ref[...])
    o_ref[...] = _rms(x, g_ref[...])


def _ple(x, pe, wg, wp, g, tm):
    m = x.shape[0]
    row = lambda i: (i, 0)
    return pl.pallas_call(
        _ple_kernel,
        out_shape=jax.ShapeDtypeStruct((m, D_MODEL), F32),
        grid=(m // tm,),
        in_specs=[
            pl.BlockSpec((tm, D_MODEL), row),
            pl.BlockSpec((tm, PLE_DIM), row),
            _resident((D_MODEL, D_MODEL)),
            _resident((PLE_DIM, D_MODEL)),
            pl.BlockSpec((1, D_MODEL), lambda i: (0, 0)),
        ],
        out_specs=pl.BlockSpec((tm, D_MODEL), row),
        compiler_params=pltpu.CompilerParams(
            dimension_semantics=("arbitrary",), vmem_limit_bytes=VMEM_LIMIT),
        name="ple",
    )(x, pe, wg, wp, g)


def _dense_tail(x, ya, yr, pe, wts, tiles):
    (g_mix, w_gates, wba, wbr, wo, g_ffn, wu, wd, wg, wp, g_fin) = wts
    x = _mix(x, ya, yr, g_mix, w_gates, wba, wbr, wo, tiles["mix_m"])
    x = _ffn(x, g_ffn, wu, wd, tiles["ffn_m"], tiles["ffn_k"])
    return _ple(x, pe, wg, wp, g_fin, tiles["ple_m"])


def _shift_out(proj, batch, seq):
    last = proj.reshape(batch, seq, P_COLS)[:, -1]
    return last[:, P_C:P_C + RWKV_COLS][None]


def kernel(x_prompt, x_sample, cache_k_win, cache_v_win, state_wkv, state_shift, p_prompt, p_sample,
           norm_mix, w_in, attn_sinks, rwkv_mu, rwkv_w0, rwkv_w2, rwkv_a0, rwkv_a2, rwkv_g2,
           rwkv_k_k, rwkv_k_a, rwkv_r_k, rwkv_ln_w, rwkv_ln_b, w_branch_attn, w_branch_rwkv,
           w_out, norm_ffn, w_ff_up, w_ff_down, w_ple_proj, w_ple_gate, norm_final):
    assert w_in.shape[0] == 1, "single-layer step"
    bp, tp = x_prompt.shape[0], x_prompt.shape[1]
    bs, ts = x_sample.shape[0], x_sample.shape[1]
    rw = R_WIDTH

    wl = jnp.zeros((LORA_PAD, 3 * rw), F32)
    wl = wl.at[0:DECAY_LORA, 0:rw].set(rwkv_w2[0])
    wl = wl.at[DECAY_LORA:DECAY_LORA + AAA_LORA, rw:2 * rw].set(rwkv_a2[0])
    wl = wl.at[DECAY_LORA + AAA_LORA:LORA_COLS, 2 * rw:3 * rw].set(rwkv_g2[0])
    wl = wl.astype(BF16)
    seg_id = np.arange(SEG) // R_HEAD
    bones = jnp.asarray(seg_id[:, None] == seg_id[None, :], BF16)
    mu = rwkv_mu[0]
    row = lambda v: v.reshape(1, -1)
    wkv_params = (row(mu[:3 * rw]), row(jnp.pad(mu[3 * rw:], (0, LORA_PAD - LORA_COLS))),
                  row(rwkv_w0[0]), row(rwkv_a0[0]), row(rwkv_k_k[0]), row(rwkv_k_a[0]),
                  row(rwkv_r_k[0]), row(rwkv_ln_w[0]), row(rwkv_ln_b[0]))
    g_mix = row(norm_mix[0])
    sinks = attn_sinks[0]

    ms = bs * ts
    xs = x_sample.reshape(ms, D_MODEL)
    w_in_t = jnp.transpose(w_in[0])
    proj_s, w_in_b = _proj_cast(xs, g_mix, w_in_t, 1024)

    tiles_p = dict(mix_m=256, ffn_m=1024, ffn_k=1024, ple_m=512)
    xp = x_prompt.reshape(bp * tp, D_MODEL)
    proj_p = _proj(xp, g_mix, w_in_b, 1024, 1280)
    ya_p = _attn_prompt(proj_p, sinks, bp, tp)
    yr_p, s_p, (wu, wd, wo, wg, wba, wbr, w_gates) = _wkv(
        proj_p, jnp.zeros((bp, 1, SHIFT_COLS), F32), jnp.zeros((bp, R_HEADS, R_HEAD, R_HEAD), F32),
        wkv_params, wl, bones, bp, tp, 64, 4,
        casts=((w_ff_up[0], None), (w_ff_down[0], None), (w_out[0], None), (w_ple_gate[0], None),
               (w_branch_attn[0], None), (w_branch_rwkv[0], None), (w_in_t, (P_GATES, 2 * D_MODEL))))
    dense = (g_mix, w_gates, wba, wbr, wo, row(norm_ffn[0]), wu, wd, wg, w_ple_proj[0].astype(BF16),
             row(norm_final))
    yp = _dense_tail(xp, ya_p, yr_p, p_prompt[0].reshape(bp * tp, PLE_DIM), dense, tiles_p)
    pp3 = proj_p.reshape(bp, tp, P_COLS)[:, -WINDOW:]
    k_p = pp3[:, :, P_K:P_K + KV_WIDTH].reshape(1, bp, WINDOW, N_KV_HEADS, HEAD_DIM)
    v_p = pp3[:, :, P_V:P_V + KV_WIDTH].reshape(1, bp, WINDOW, N_KV_HEADS, HEAD_DIM)

    tiles_s = dict(mix_m=ms, ffn_m=ms, ffn_k=2048, ple_m=ms)
    ck_t = jnp.transpose(cache_k_win[0], (0, 2, 3, 1)).reshape(bs, KV_WIDTH, WINDOW)
    cv_t = jnp.transpose(cache_v_win[0], (0, 2, 3, 1)).reshape(bs, KV_WIDTH, WINDOW)
    ya_s, nk_t, nv_t = _attn_sample(proj_s, sinks, ck_t, cv_t, bs, ts, SAMPLE_GROUP)
    prev_s = jnp.pad(state_shift[0], ((0, 0), (0, LORA_PAD - LORA_COLS))).reshape(bs, 1, SHIFT_COLS)
    yr_s, s_s, _ = _wkv(proj_s, prev_s, state_wkv[0], wkv_params, wl, bones, bs, ts, ts, 1, n_seq=SAMPLE_GROUP)
    ys = _dense_tail(xs, ya_s, yr_s, p_sample[0].reshape(ms, PLE_DIM), dense, tiles_s)

    return (yp.reshape(bp, tp, D_MODEL), ys.reshape(bs, ts, D_MODEL),
            k_p, v_p, s_p[None], _shift_out(proj_p, bp, tp),
            jnp.transpose(nk_t.reshape(bs, N_KV_HEADS, HEAD_DIM, WINDOW), (0, 3, 1, 2))[None],
            jnp.transpose(nv_t.reshape(bs, N_KV_HEADS, HEAD_DIM, WINDOW), (0, 3, 1, 2))[None],
            s_s[None], _shift_out(proj_s, bs, ts))
```

```python
import functools

import numpy as np
import jax
import jax.numpy as jnp
from jax import lax
from jax.experimental import pallas as pl
from jax.experimental.pallas import tpu as pltpu

F32 = jnp.float32
BF16 = jnp.bfloat16

D_MODEL = 2048
PLE_DIM = 256
HEAD_DIM = 64
N_Q_HEADS = 16
N_KV_HEADS = 4
GQA_GROUP = 4
ATTN_WIDTH = 1024
KV_WIDTH = 256
WINDOW = 128
ALIBI_MAX = 8.0
R_HEAD = 64
R_WIDTH = 1024
R_HEADS = 16
DECAY_LORA = 64
AAA_LORA = 64
GATE_LORA = 160
LORA_COLS = DECAY_LORA + AAA_LORA + GATE_LORA
LORA_PAD = 512
RWKV_COLS = 3 * R_WIDTH + LORA_COLS
D_FF = 4 * D_MODEL
NORM_EPS = 1e-6
GN_EPS = 64e-5

P_Q = 0
P_K = ATTN_WIDTH
P_V = ATTN_WIDTH + KV_WIDTH
P_C = ATTN_WIDTH + 2 * KV_WIDTH
P_L = P_C + 3 * R_WIDTH
P_COLS = P_L + LORA_PAD
P_GATES = P_C + RWKV_COLS
C_BLK = P_C
SHIFT_COLS = 3 * R_WIDTH + LORA_PAD

PAIR = 128
N_PAIRS = R_WIDTH // PAIR
SEG = 256
MIX_CHUNK = 1024
SAMPLE_GROUP = 8
PROMPT_BLOCKS = 4

V7X_VMEM_BYTES = 64 * 1024 * 1024
VMEM_LIMIT = 56 * 1024 * 1024


def _mm(a, b):
    return jnp.dot(a, b, preferred_element_type=F32)


def _nt(a, b):
    return lax.dot_general(a, b, (((1,), (1,)), ((), ())), preferred_element_type=F32)


def _tn(a, b):
    return lax.dot_general(a, b, (((0,), (0,)), ((), ())), preferred_element_type=F32)


def _rms(x, g):
    ms = jnp.mean(x * x, axis=-1, keepdims=True)
    return x * lax.rsqrt(ms + NORM_EPS) * g


def _alibi_slope(hq):
    return float(2.0 ** (-ALIBI_MAX * (hq + 1) / N_Q_HEADS))


def _cast_specs(casts, n_steps, step_of):
    in_specs, out_specs, shapes = [], [], []
    for wgt, span in casts:
        start, count = (0, wgt.shape[0]) if span is None else span
        blk, width = count // n_steps, wgt.shape[1]
        assert blk * n_steps == count, (wgt.shape, span, n_steps)
        if span is None:
            assert blk % 16 == 0, blk
            in_specs.append(pl.BlockSpec((blk, width), lambda *g: (step_of(*g), 0)))
            out_specs.append(pl.BlockSpec((blk, width), lambda *g: (step_of(*g), 0)))
            shapes.append(jax.ShapeDtypeStruct((count, width), BF16))
        else:
            assert blk % 128 == 0 and start % 8 == 0, span
            in_specs.append(pl.BlockSpec(
                (pl.Element(blk), pl.Element(width)),
                lambda *g, start=start, blk=blk: (pl.multiple_of(start + blk * step_of(*g), 8), 0)))
            out_specs.append(pl.BlockSpec((width, blk), lambda *g: (0, step_of(*g))))
            shapes.append(jax.ShapeDtypeStruct((width, count), BF16))
    return in_specs, out_specs, shapes, tuple(span is not None for _, span in casts)


def _run_casts(srcs, dsts, transposed):
    for src, dst, tr in zip(srcs, dsts, transposed):
        dst[...] = (src[...].T if tr else src[...]).astype(BF16)


def _proj_kernel(x_ref, g_ref, w_ref, o_ref, h_ref):
    @pl.when(pl.program_id(1) == 0)
    def _():
        h_ref[...] = _rms(x_ref[...], g_ref[...]).astype(BF16)

    o_ref[...] = _nt(h_ref[...], w_ref[...])


def _proj_cast_kernel(x_ref, g_ref, w_ref, o_ref, wb_ref, h_ref):
    @pl.when(pl.program_id(0) == 0)
    def _():
        h_ref[...] = _rms(x_ref[...], g_ref[...]).astype(BF16)

    wb = w_ref[...].astype(BF16)
    wb_ref[...] = wb
    o_ref[...] = _nt(h_ref[...], wb)


def _proj_cast(x, g, wt_f32, tn):
    m = x.shape[0]
    return pl.pallas_call(
        _proj_cast_kernel,
        out_shape=(jax.ShapeDtypeStruct((m, P_COLS), F32), jax.ShapeDtypeStruct((P_COLS, D_MODEL), BF16)),
        grid=(P_COLS // tn,),
        in_specs=[
            pl.BlockSpec((m, D_MODEL), lambda j: (0, 0)),
            pl.BlockSpec((1, D_MODEL), lambda j: (0, 0)),
            pl.BlockSpec((tn, D_MODEL), lambda j: (j, 0)),
        ],
        out_specs=(pl.BlockSpec((m, tn), lambda j: (0, j)), pl.BlockSpec((tn, D_MODEL), lambda j: (j, 0))),
        scratch_shapes=[pltpu.VMEM((m, D_MODEL), BF16)],
        compiler_params=pltpu.CompilerParams(
            dimension_semantics=("arbitrary",), vmem_limit_bytes=VMEM_LIMIT),
        name="proj_cast",
    )(x, g, wt_f32)


def _proj(x, g, w, tm, tn):
    m = x.shape[0]
    return pl.pallas_call(
        _proj_kernel,
        out_shape=jax.ShapeDtypeStruct((m, P_COLS), F32),
        grid=(m // tm, P_COLS // tn),
        in_specs=[
            pl.BlockSpec((tm, D_MODEL), lambda i, j: (i, 0)),
            pl.BlockSpec((1, D_MODEL), lambda i, j: (0, 0)),
            pl.BlockSpec((tn, D_MODEL), lambda i, j: (j, 0)),
        ],
        out_specs=pl.BlockSpec((tm, tn), lambda i, j: (i, j)),
        scratch_shapes=[pltpu.VMEM((tm, D_MODEL), BF16)],
        compiler_params=pltpu.CompilerParams(
            dimension_semantics=("arbitrary", "arbitrary"), vmem_limit_bytes=VMEM_LIMIT),
        name="proj",
    )(x, g, w)


def _attend_heads(scores, apply_values, sinks):
    heads = range(len(scores))
    m = []
    for h in heads:
        mh = sinks[h]
        for s in scores[h]:
            mh = jnp.maximum(mh, jnp.max(s, axis=-1, keepdims=True))
        m.append(mh)
    ps = [[jnp.exp(s - m[h]) for s in scores[h]] for h in heads]
    den = []
    for h in heads:
        dh = jnp.exp(sinks[h] - m[h])
        for p in ps[h]:
            dh = dh + jnp.sum(p, axis=-1, keepdims=True)
        den.append(dh)
    outs = []
    for h in heads:
        o = None
        for p, pv in zip(ps[h], apply_values[h]):
            t = pv(p.astype(BF16))
            o = t if o is None else o + t
        outs.append(o * (1.0 / den[h]))
    return outs


def _head_slices():
    q_sl = [slice(hq * HEAD_DIM, (hq + 1) * HEAD_DIM) for hq in range(N_Q_HEADS)]
    kv_sl = [slice((hq // GQA_GROUP) * HEAD_DIM, (hq // GQA_GROUP + 1) * HEAD_DIM) for hq in range(N_Q_HEADS)]
    return q_sl, kv_sl


def _alibi_bias(dist, valid):
    slopes = np.array([_alibi_slope(hq) for hq in range(N_Q_HEADS)], np.float32)
    return np.where(valid[None], -slopes[:, None, None] * dist[None].astype(np.float32), -np.inf).astype(np.float32)


def _attn_prompt_kernel(sink_ref, bias_ref, q_ref, kp_ref, kc_ref, vp_ref, vc_ref, o_ref):
    w, grp = WINDOW, GQA_GROUP
    n_blk = q_ref.shape[0] // w
    kj = lax.broadcasted_iota(jnp.int32, (grp * w, 2 * w), 1)
    no_prev = (kj < w) & (pl.program_id(1) == 0)
    member = lax.broadcasted_iota(jnp.int32, (grp * w, 1), 0) // w
    q = q_ref[...] * (HEAD_DIM ** -0.5)
    kc, vc = kc_ref[...].astype(BF16), vc_ref[...].astype(BF16)
    ks = jnp.concatenate([kp_ref[...].astype(BF16), kc], axis=0)
    vs = jnp.concatenate([vp_ref[...].astype(BF16), vc], axis=0)
    q_sl, kv_sl = _head_slices()
    sinks = []
    for h in range(N_KV_HEADS):
        col = jnp.full((grp * w, 1), sink_ref[h * grp], F32)
        for g in range(1, grp):
            col = jnp.where(member == g, sink_ref[h * grp + g], col)
        sinks.append(col)
    scores, values = [], []
    for s in range(n_blk):
        k2, v2 = ks[s * w:(s + 2) * w], vs[s * w:(s + 2) * w]
        qb = q[s * w:(s + 1) * w]
        for h in range(N_KV_HEADS):
            q4 = jnp.concatenate([qb[:, q_sl[h * grp + g]] for g in range(grp)], axis=0).astype(BF16)
            sc = _nt(q4, k2[:, kv_sl[h * grp]]) + bias_ref[h]
            scores.append([jnp.where(no_prev, -jnp.inf, sc) if s == 0 else sc])
            values.append([functools.partial(_mm, b=v2[:, kv_sl[h * grp]])])
    outs = _attend_heads(scores, values, sinks * n_blk)
    for s in range(n_blk):
        for h in range(N_KV_HEADS):
            o4 = outs[s * N_KV_HEADS + h]
            for g in range(grp):
                o_ref[s * w:(s + 1) * w, q_sl[h * grp + g]] = o4[g * w:(g + 1) * w]


def _attn_prompt(proj, sinks, batch, seq):
    rows = PROMPT_BLOCKS * WINDOW
    ns = seq // rows
    kcol, vcol = P_K // KV_WIDTH, P_V // KV_WIDTH

    def cur(c):
        return lambda b, i: (b * ns + i, c)

    def prev(c):
        return lambda b, i: (b * ns * PROMPT_BLOCKS + jnp.maximum(i * PROMPT_BLOCKS - 1, 0), c)

    ti = np.arange(WINDOW)[:, None]
    kj = np.arange(2 * WINDOW)[None, :]
    dist = ti - kj + WINDOW
    bias = _alibi_bias(dist, (dist >= 0) & (dist <= WINDOW))
    bias = bias.reshape(N_KV_HEADS, GQA_GROUP * WINDOW, 2 * WINDOW)
    return pl.pallas_call(
        _attn_prompt_kernel,
        out_shape=jax.ShapeDtypeStruct((batch * seq, ATTN_WIDTH), F32),
        grid=(batch, ns),
        in_specs=[
            pl.BlockSpec(memory_space=pltpu.SMEM),
            _resident((N_KV_HEADS, GQA_GROUP * WINDOW, 2 * WINDOW)),
            pl.BlockSpec((rows, ATTN_WIDTH), cur(P_Q // ATTN_WIDTH)),
            pl.BlockSpec((WINDOW, KV_WIDTH), prev(kcol)),
            pl.BlockSpec((rows, KV_WIDTH), cur(kcol)),
            pl.BlockSpec((WINDOW, KV_WIDTH), prev(vcol)),
            pl.BlockSpec((rows, KV_WIDTH), cur(vcol)),
        ],
        out_specs=pl.BlockSpec((rows, ATTN_WIDTH), lambda b, i: (b * ns + i, 0)),
        compiler_params=pltpu.CompilerParams(dimension_semantics=("arbitrary", "arbitrary")),
        name="attn_prompt",
    )(sinks, jnp.asarray(bias), proj, proj, proj, proj, proj)


def _attn_sample_kernel(sink_ref, q_ref, kn_ref, vn_ref, ckt_ref, cvt_ref, o_ref, nkt_ref, nvt_ref, *, seq):
    t, w = seq, WINDOW
    n_seq = q_ref.shape[0] // t
    ti = lax.broadcasted_iota(jnp.int32, (t, w), 0)
    cj = lax.broadcasted_iota(jnp.int32, (t, w), 1)
    dist_c = (ti - cj + w).astype(F32)
    valid_c = cj >= ti
    ti2 = lax.broadcasted_iota(jnp.int32, (t, t), 0)
    tj2 = lax.broadcasted_iota(jnp.int32, (t, t), 1)
    dist_n = (ti2 - tj2).astype(F32)
    valid_n = tj2 <= ti2
    grp, kvs = GQA_GROUP, range(N_KV_HEADS)
    q_sl, kv_sl = _head_slices()
    member = lax.broadcasted_iota(jnp.int32, (grp * t, 1), 0) // t

    def per_member(fn):
        return [jnp.concatenate([fn(h * grp + g) for g in range(grp)], axis=0) for h in kvs]

    bias_c = per_member(lambda hq: jnp.where(valid_c, -_alibi_slope(hq) * dist_c, -jnp.inf))
    bias_n = per_member(lambda hq: jnp.where(valid_n, -_alibi_slope(hq) * dist_n, -jnp.inf))
    sinks = []
    for h in kvs:
        col = jnp.full((grp * t, 1), sink_ref[h * grp], F32)
        for g in range(1, grp):
            col = jnp.where(member == g, sink_ref[h * grp + g], col)
        sinks.append(col)
    q_all = q_ref[...] * (HEAD_DIM ** -0.5)
    is_new = lax.broadcasted_iota(jnp.int32, (1, w), 1) >= w - t
    put = (cj == ti + (w - t)).astype(BF16)

    def place_new(x):
        hi = x.astype(BF16)
        rem = x - hi.astype(F32)
        mid = rem.astype(BF16)
        return _tn(hi, put) + _tn(mid, put) + _tn((rem - mid.astype(F32)).astype(BF16), put)

    scores, values = [], []
    for s in range(n_seq):
        rs = slice(s * t, (s + 1) * t)
        ckt, cvt = ckt_ref[s], cvt_ref[s]
        nkt_ref[s] = jnp.where(is_new, place_new(kn_ref[rs, :]), pltpu.roll(ckt, w - t, axis=1))
        nvt_ref[s] = jnp.where(is_new, place_new(vn_ref[rs, :]), pltpu.roll(cvt, w - t, axis=1))
        cktb, cvtb = ckt.astype(BF16), cvt.astype(BF16)
        knb, vnb = kn_ref[rs, :].astype(BF16), vn_ref[rs, :].astype(BF16)
        q = q_all[rs]
        qs = per_member(lambda hq: q[:, q_sl[hq]])
        for h in kvs:
            kv = kv_sl[h * grp]
            q4 = qs[h].astype(BF16)
            scores.append([_mm(q4, cktb[kv, :]) + bias_c[h], _nt(q4, knb[:, kv]) + bias_n[h]])
            values.append([functools.partial(_nt, b=cvtb[kv, :]), functools.partial(_mm, b=vnb[:, kv])])
    outs = _attend_heads(scores, values, sinks * n_seq)
    for s in range(n_seq):
        for h in kvs:
            o4 = outs[s * N_KV_HEADS + h]
            for g in range(grp):
                o_ref[s * t:(s + 1) * t, q_sl[h * grp + g]] = o4[g * t:(g + 1) * t]


def _attn_sample(proj, sinks, cache_kt, cache_vt, batch, seq, n_seq):
    kcol, vcol = P_K // KV_WIDTH, P_V // KV_WIDTH
    rows = n_seq * seq
    win = jax.ShapeDtypeStruct((batch, KV_WIDTH, WINDOW), F32)
    return pl.pallas_call(
        functools.partial(_attn_sample_kernel, seq=seq),
        out_shape=(jax.ShapeDtypeStruct((batch * seq, ATTN_WIDTH), F32), win, win),
        grid=(batch // n_seq,),
        in_specs=[
            pl.BlockSpec(memory_space=pltpu.SMEM),
            pl.BlockSpec((rows, ATTN_WIDTH), lambda b: (b, P_Q // ATTN_WIDTH)),
            pl.BlockSpec((rows, KV_WIDTH), lambda b: (b, kcol)),
            pl.BlockSpec((rows, KV_WIDTH), lambda b: (b, vcol)),
            pl.BlockSpec((n_seq, KV_WIDTH, WINDOW), lambda b: (b, 0, 0)),
            pl.BlockSpec((n_seq, KV_WIDTH, WINDOW), lambda b: (b, 0, 0)),
        ],
        out_specs=(
            pl.BlockSpec((rows, ATTN_WIDTH), lambda b: (b, 0)),
            pl.BlockSpec((n_seq, KV_WIDTH, WINDOW), lambda b: (b, 0, 0)),
            pl.BlockSpec((n_seq, KV_WIDTH, WINDOW), lambda b: (b, 0, 0)),
        ),
        compiler_params=pltpu.CompilerParams(dimension_semantics=("arbitrary",)),
        name="attn_sample",
    )(sinks, proj, proj, proj, cache_kt, cache_vt)


def _seg_sum(x, bones):
    rows = x.shape[0]
    hi = x.astype(BF16).astype(F32)
    lo = x - hi
    groups = [slice(j * SEG, (j + 1) * SEG) for j in range(R_WIDTH // SEG)]
    lhs = jnp.concatenate([t[:, sl] for sl in groups for t in (hi, lo)], axis=0).astype(BF16)
    out = _mm(lhs, bones)
    return jnp.concatenate(
        [out[2 * j * rows:(2 * j + 1) * rows] + out[(2 * j + 1) * rows:(2 * j + 2) * rows]
         for j in range(len(groups))], axis=1)


def _wkv_kernel(pa_ref, pb_ref, pl_ref, prev_ref, s0_ref,
                mu_ref, mul_ref, w0_ref, a0_ref, kk_ref, ka_ref, rk_ref, lnw_ref, lnb_ref,
                wl_ref, bones_ref, *rest, chunk, n_sub, n_seq, n_steps, cast_transposed):
    n_cast = len(cast_transposed)
    cast_src, (y_ref, so_ref) = rest[:n_cast], rest[n_cast:n_cast + 2]
    cast_dst, (s_ref, carry_ref) = rest[n_cast + 2:2 * n_cast + 2], rest[2 * n_cast + 2:]
    _run_casts(cast_src, cast_dst, cast_transposed)

    step = pl.program_id(1)
    cs = chunk
    seq_rows = n_sub * cs
    rows = n_seq * seq_rows
    gc = 2 * cs
    hd = R_HEAD
    w = R_WIDTH
    seqs = range(n_seq)

    @pl.when(step == 0)
    def _init():
        s_ref[...] = jnp.zeros(s_ref.shape, F32)
        for q in seqs:
            carry_ref[q:q + 1, :] = prev_ref[q]
            for p in range(N_PAIRS):
                s_ref[q * N_PAIRS + p, 0:hd, 0:hd] = s0_ref[q, 2 * p]
                s_ref[q * N_PAIRS + p, hd:2 * hd, hd:2 * hd] = s0_ref[q, 2 * p + 1]

    row = lax.broadcasted_iota(jnp.int32, (rows, 1), 0)

    def token_shift(x, lo, hi, mu):
        shifted = pltpu.roll(x, 1, axis=0)
        for q in seqs:
            shifted = jnp.where(row == q * seq_rows, carry_ref[q:q + 1, lo:hi], shifted)
        return x + (shifted - x) * mu

    cols = jnp.concatenate([pa_ref[...], pb_ref[...]], axis=1)
    l_raw = pl_ref[...]
    xx = token_shift(cols, 0, 3 * w, mu_ref[...])
    xr, xk, xv = xx[:, 0:w], xx[:, w:2 * w], xx[:, 2 * w:3 * w]
    xl = token_shift(l_raw, 3 * w, SHIFT_COLS, mul_ref[...])
    for q in seqs:
        last = (q + 1) * seq_rows
        carry_ref[q:q + 1, 0:3 * w] = cols[last - 1:last, :]
        carry_ref[q:q + 1, 3 * w:] = l_raw[last - 1:last, :]

    lane_l = lax.broadcasted_iota(jnp.int32, (1, LORA_PAD), 1)
    act = jnp.where(lane_l < DECAY_LORA, jnp.tanh(xl),
                    jnp.where(lane_l < DECAY_LORA + AAA_LORA, xl,
                              jnp.where(lane_l < LORA_COLS, jax.nn.sigmoid(xl), 0.0)))
    up = _mm(act.astype(BF16), wl_ref[...])
    z = -(w0_ref[...] + up[:, 0:w])
    softplus = jnp.maximum(z, 0.0) + jnp.log1p(jnp.exp(-jnp.abs(z)))
    lwd = -jnp.exp(-softplus - 0.5)
    a = jax.nn.sigmoid(a0_ref[...] + up[:, w:2 * w])
    gate = up[:, 2 * w:3 * w]

    bones = bones_ref[...]
    kkn = xk * kk_ref[...]
    kk = kkn * lax.rsqrt(jnp.maximum(_seg_sum(kkn * kkn, bones), 1e-24))
    k2 = xk * (1.0 + (a - 1.0) * ka_ref[...])
    b = kk * a

    tri = (lax.broadcasted_iota(jnp.int32, (cs, cs), 0)
           >= lax.broadcasted_iota(jnp.int32, (cs, cs), 1)).astype(BF16)
    ri = lax.broadcasted_iota(jnp.int32, (gc, gc), 0)
    ci = lax.broadcasted_iota(jnp.int32, (gc, gc), 1)
    same_head = (ri >= cs) == (ci >= cs)
    strict = same_head & (ci < ri)
    incl = same_head & (ci <= ri)
    eye = (ri == ci).astype(F32)
    head0 = lax.broadcasted_iota(jnp.int32, (1, PAIR), 1) < hd

    def stack(x):
        return jnp.concatenate([jnp.where(head0, x, 0.0), jnp.where(head0, 0.0, x)], axis=0).astype(BF16)

    pairs = range(N_PAIRS)
    lanes = [slice(p * PAIR, (p + 1) * PAIR) for p in pairs]
    squarings = cs.bit_length() - 2

    chunks = []
    for sub in range(n_seq * n_sub):
        rs = slice(sub * cs, (sub + 1) * cs)
        lw_c = lwd[rs]
        lw_hi = lw_c.astype(BF16)
        lw_r = lw_c - lw_hi.astype(F32)
        lw_mid = lw_r.astype(BF16)
        cum = _mm(tri, lw_hi) + _mm(tri, lw_mid) + _mm(tri, (lw_r - lw_mid.astype(F32)).astype(BF16))
        cum_last = cum[cs - 1:cs, :]
        e_inv = jnp.exp(-cum)
        e_last = jnp.exp(cum_last - cum)
        kq = kk[rs] * jnp.exp(cum - lw_c)
        rq = xr[rs] * jnp.exp(cum)
        kd = k2[rs] * e_inv
        bd = b[rs] * e_inv
        kdp = k2[rs] * e_last
        bdp = b[rs] * e_last
        xv_c = xv[rs]
        kq_s = [stack(kq[:, sl]) for sl in lanes]
        bd_s = [stack(bd[:, sl]) for sl in lanes]
        kd_s = [stack(kd[:, sl]) for sl in lanes]
        rq_s = [stack(rq[:, sl]) for sl in lanes]
        probes = [jnp.concatenate([kq_s[p], rq_s[p]], axis=0) for p in pairs]
        if gc % PAIR == 0:
            a_all = [_nt(probes[p], jnp.concatenate([bd_s[p], kd_s[p]], axis=0)) for p in pairs]
            a_kb = [jnp.where(strict, t[0:gc, 0:gc], 0.0) for t in a_all]
            a_kk = [jnp.where(strict, t[0:gc, gc:2 * gc], 0.0).astype(BF16) for t in a_all]
            a_rb = [jnp.where(incl, t[gc:2 * gc, 0:gc], 0.0).astype(BF16) for t in a_all]
            a_rk = [jnp.where(incl, t[gc:2 * gc, gc:2 * gc], 0.0).astype(BF16) for t in a_all]
        else:
            a_kb = [jnp.where(strict, _nt(kq_s[p], bd_s[p]), 0.0) for p in pairs]
            a_kk = [jnp.where(strict, _nt(kq_s[p], kd_s[p]), 0.0).astype(BF16) for p in pairs]
            a_rb = [jnp.where(incl, _nt(rq_s[p], bd_s[p]), 0.0).astype(BF16) for p in pairs]
            a_rk = [jnp.where(incl, _nt(rq_s[p], kd_s[p]), 0.0).astype(BF16) for p in pairs]
        inv = [eye - t for t in a_kb]
        if squarings >= 1:
            apow_b = [t.astype(BF16) for t in a_kb]
            apow_b = [_mm(t, t).astype(BF16) for t in apow_b]
            for _ in range(squarings - 1):
                both = [_mm(jnp.concatenate([apow_b[p], inv[p].astype(BF16)], axis=0), apow_b[p]) for p in pairs]
                apow_b = [t[0:gc].astype(BF16) for t in both]
                inv = [inv[p] + both[p][gc:2 * gc] for p in pairs]
            inv = [inv[p] + _mm(inv[p].astype(BF16), apow_b[p]) for p in pairs]
        chunks.append(dict(
            probes=probes, a_kk=a_kk, a_rk=a_rk, a_rb=a_rb, inv=[t.astype(BF16) for t in inv],
            v_s=[stack(xv_c[:, sl]) for sl in lanes],
            upd=[jnp.concatenate([stack(kdp[:, sl]), stack(bdp[:, sl])], axis=0) for sl in lanes],
            p_last=jnp.exp(cum_last)))

    sp = [(q, p) for q in seqs for p in pairs]
    state = {(q, p): s_ref[q * N_PAIRS + p] for q, p in sp}
    y_rows = [None] * (n_seq * n_sub)
    for level in range(n_sub):
        ch = {q: chunks[q * n_sub + level] for q in seqs}
        s_b = {k: state[k].astype(BF16) for k in sp}
        state_t = {(q, p): _nt(ch[q]["probes"][p], s_b[q, p]) for q, p in sp}
        rhs = {(q, p): state_t[q, p][0:gc] + _mm(ch[q]["a_kk"][p], ch[q]["v_s"][p]) for q, p in sp}
        u_b = {(q, p): _mm(ch[q]["inv"][p], rhs[q, p].astype(BF16)).astype(BF16) for q, p in sp}
        if gc % PAIR == 0:
            y_s = {(q, p): state_t[q, p][gc:2 * gc]
                   + _mm(jnp.concatenate([ch[q]["a_rk"][p], -ch[q]["a_rb"][p]], axis=1),
                         jnp.concatenate([ch[q]["v_s"][p], u_b[q, p]], axis=0)) for q, p in sp}
        else:
            y_s = {(q, p): state_t[q, p][gc:2 * gc] + _mm(ch[q]["a_rk"][p], ch[q]["v_s"][p])
                   - _mm(ch[q]["a_rb"][p], u_b[q, p]) for q, p in sp}
        state = {(q, p): state[q, p] * ch[q]["p_last"][:, lanes[p]]
                 + _tn(jnp.concatenate([ch[q]["v_s"][p], -u_b[q, p]], axis=0), ch[q]["upd"][p]) for q, p in sp}
        for q in seqs:
            y_rows[q * n_sub + level] = jnp.concatenate(
                [y_s[q, p][0:cs] + y_s[q, p][cs:gc] for p in pairs], axis=1)
    for q, p in sp:
        s_ref[q * N_PAIRS + p] = state[q, p]

    y = jnp.concatenate(y_rows, axis=0) if len(y_rows) > 1 else y_rows[0]
    mean = _seg_sum(y, bones) * (1.0 / hd)
    d = y - mean
    var = _seg_sum(d * d, bones) * (1.0 / hd)
    yn = d * lax.rsqrt(var + GN_EPS) * lnw_ref[...] + lnb_ref[...]
    bonus = _seg_sum(xr * k2 * rk_ref[...], bones) * xv
    y_ref[...] = (yn + bonus) * gate

    @pl.when(step == n_steps - 1)
    def _fin():
        for q, p in sp:
            so_ref[q, 2 * p] = s_ref[q * N_PAIRS + p, 0:hd, 0:hd]
            so_ref[q, 2 * p + 1] = s_ref[q * N_PAIRS + p, hd:2 * hd, hd:2 * hd]


def _wkv(proj, prev0, s0, params, wl, bones, batch, seq, chunk, n_sub, n_seq=1, casts=()):
    ns = seq // (chunk * n_sub)
    assert n_seq == 1 or ns == 1, "several sequences per step only when a step covers them whole"
    rows = n_seq * chunk * n_sub
    nb = batch // n_seq
    rw = R_WIDTH

    def col(cb):
        return lambda b, c: (b * ns + c, cb)

    const2 = lambda b, c: (0, 0)
    vec = lambda n: pl.BlockSpec((1, n), const2)
    cast_in, cast_out, cast_shapes, cast_tr = _cast_specs(casts, nb * ns, lambda b, c: b * ns + c)
    mu, mul, w0, a0, k_k, k_a, r_k, ln_w, ln_b = params
    outs = pl.pallas_call(
        functools.partial(_wkv_kernel, chunk=chunk, n_sub=n_sub, n_seq=n_seq, n_steps=ns,
                          cast_transposed=cast_tr),
        out_shape=(jax.ShapeDtypeStruct((batch * seq, rw), F32),
                   jax.ShapeDtypeStruct((batch, R_HEADS, R_HEAD, R_HEAD), F32), *cast_shapes),
        grid=(nb, ns),
        in_specs=[
            pl.BlockSpec((rows, C_BLK), col(P_C // C_BLK)),
            pl.BlockSpec((rows, C_BLK), col(P_C // C_BLK + 1)),
            pl.BlockSpec((rows, LORA_PAD), col(P_L // LORA_PAD)),
            pl.BlockSpec((n_seq, 1, SHIFT_COLS), lambda b, c: (b, 0, 0)),
            pl.BlockSpec((n_seq, R_HEADS, R_HEAD, R_HEAD), lambda b, c: (b, 0, 0, 0)),
            vec(3 * rw), vec(LORA_PAD), vec(rw), vec(rw), vec(rw), vec(rw), vec(rw), vec(rw), vec(rw),
            pl.BlockSpec((LORA_PAD, 3 * rw), const2),
            pl.BlockSpec((SEG, SEG), const2),
            *cast_in,
        ],
        out_specs=(
            pl.BlockSpec((rows, rw), lambda b, c: (b * ns + c, 0)),
            pl.BlockSpec((n_seq, R_HEADS, R_HEAD, R_HEAD), lambda b, c: (b, 0, 0, 0)),
            *cast_out,
        ),
        scratch_shapes=[pltpu.VMEM((n_seq * N_PAIRS, PAIR, PAIR), F32), pltpu.VMEM((n_seq, SHIFT_COLS), F32)],
        compiler_params=pltpu.CompilerParams(
            dimension_semantics=("arbitrary", "arbitrary"), vmem_limit_bytes=VMEM_LIMIT),
        name="wkv",
    )(proj, proj, proj, prev0, s0, mu, mul, w0, a0, k_k, k_a, r_k, ln_w, ln_b, wl, bones,
      *[wgt for wgt, _ in casts])
    return outs[0], outs[1], outs[2:]


def _mix_kernel(x_ref, ya_ref, yr_ref, g_ref, wg_ref, wba_ref, wbr_ref, wo_ref, o_ref):
    x = x_ref[...]
    h = _rms(x, g_ref[...]).astype(BF16)
    ya = ya_ref[...].astype(BF16)
    yr = yr_ref[...].astype(BF16)
    acc = x
    for c in range(D_MODEL // MIX_CHUNK):
        ca = slice(c * MIX_CHUNK, (c + 1) * MIX_CHUNK)
        cr = slice(D_MODEL + c * MIX_CHUNK, D_MODEL + (c + 1) * MIX_CHUNK)
        mixed = (jax.nn.sigmoid(_mm(h, wg_ref[:, ca])) * _mm(ya, wba_ref[:, ca])
                 + jax.nn.sigmoid(_mm(h, wg_ref[:, cr])) * _mm(yr, wbr_ref[:, ca]))
        acc = acc + _mm(mixed.astype(BF16), wo_ref[ca, :])
    o_ref[...] = acc


def _resident(shape):
    return pl.BlockSpec(shape, lambda *_: (0,) * len(shape), pipeline_mode=pl.Buffered(1))


def _mix(x, ya, yr, g, wg, wba, wbr, wo, tm):
    m = x.shape[0]
    row = lambda i: (i, 0)
    return pl.pallas_call(
        _mix_kernel,
        out_shape=jax.ShapeDtypeStruct((m, D_MODEL), F32),
        grid=(m // tm,),
        in_specs=[
            pl.BlockSpec((tm, D_MODEL), row),
            pl.BlockSpec((tm, ATTN_WIDTH), row),
            pl.BlockSpec((tm, R_WIDTH), row),
            pl.BlockSpec((1, D_MODEL), lambda i: (0, 0)),
            _resident((D_MODEL, 2 * D_MODEL)),
            _resident((ATTN_WIDTH, D_MODEL)),
            _resident((R_WIDTH, D_MODEL)),
            _resident((D_MODEL, D_MODEL)),
        ],
        out_specs=pl.BlockSpec((tm, D_MODEL), row),
        compiler_params=pltpu.CompilerParams(
            dimension_semantics=("arbitrary",), vmem_limit_bytes=VMEM_LIMIT),
        name="mix",
    )(x, ya, yr, g, wg, wba, wbr, wo)


def _ffn_kernel(x_ref, g_ref, wu_ref, wd_ref, o_ref, h_ref):
    @pl.when(pl.program_id(1) == 0)
    def _():
        x = x_ref[...]
        h_ref[...] = _rms(x, g_ref[...]).astype(BF16)
        o_ref[...] = x

    u = _mm(h_ref[...], wu_ref[...])
    o_ref[...] += _mm(jnp.square(jnp.maximum(u, 0.0)).astype(BF16), wd_ref[...])


def _ffn_vmem_bytes(tm, tk):
    return 2 * 2 * tm * D_MODEL * 4 + 2 * 2 * D_MODEL * tk * 2 + tm * D_MODEL * 2 + tm * tk * (4 + 2)


def _ffn(x, g, wu, wd, tm, tk):
    m = x.shape[0]
    vmem = max(VMEM_LIMIT, _ffn_vmem_bytes(tm, tk))
    assert vmem <= V7X_VMEM_BYTES, (tm, tk)
    return pl.pallas_call(
        _ffn_kernel,
        out_shape=jax.ShapeDtypeStruct((m, D_MODEL), F32),
        grid=(m // tm, D_FF // tk),
        in_specs=[
            pl.BlockSpec((tm, D_MODEL), lambda i, k: (i, 0)),
            pl.BlockSpec((1, D_MODEL), lambda i, k: (0, 0)),
            pl.BlockSpec((D_MODEL, tk), lambda i, k: (0, k)),
            pl.BlockSpec((tk, D_MODEL), lambda i, k: (k, 0)),
        ],
        out_specs=pl.BlockSpec((tm, D_MODEL), lambda i, k: (i, 0)),
        scratch_shapes=[pltpu.VMEM((tm, D_MODEL), BF16)],
        compiler_params=pltpu.CompilerParams(
            dimension_semantics=("arbitrary", "arbitrary"), vmem_limit_bytes=vmem),
        name="ffn",
    )(x, g, wu, wd)


def _ple_kernel(x_ref, pe_ref, wg_ref, wp_ref, g_ref, o_ref):
    x = x_ref[...]
    gate = jax.nn.sigmoid(_mm(x.astype(BF16), wg_ref[...]))
    x = x + gate * _mm(pe_ref[...].astype(BF16), wp_ref[...])
    o_ref[...] = _rms(x, g_ref[...])


def _ple(x, pe, wg, wp, g, tm):
    m = x.shape[0]
    row = lambda i: (i, 0)
    return pl.pallas_call(
        _ple_kernel,
        out_shape=jax.ShapeDtypeStruct((m, D_MODEL), F32),
        grid=(m // tm,),
        in_specs=[
            pl.BlockSpec((tm, D_MODEL), row),
            pl.BlockSpec((tm, PLE_DIM), row),
            _resident((D_MODEL, D_MODEL)),
            _resident((PLE_DIM, D_MODEL)),
            pl.BlockSpec((1, D_MODEL), lambda i: (0, 0)),
        ],
        out_specs=pl.BlockSpec((tm, D_MODEL), row),
        compiler_params=pltpu.CompilerParams(
            dimension_semantics=("arbitrary",), vmem_limit_bytes=VMEM_LIMIT),
        name="ple",
    )(x, pe, wg, wp, g)


def _dense_tail(x, ya, yr, pe, wts, tiles):
    (g_mix, w_gates, wba, wbr, wo, g_ffn, wu, wd, wg, wp, g_fin) = wts
    x = _mix(x, ya, yr, g_mix, w_gates, wba, wbr, wo, tiles["mix_m"])
    x = _ffn(x, g_ffn, wu, wd, tiles["ffn_m"], tiles["ffn_k"])
    return _ple(x, pe, wg, wp, g_fin, tiles["ple_m"])


def _shift_out(proj, batch, seq):
    last = proj.reshape(batch, seq, P_COLS)[:, -1]
    return last[:, P_C:P_C + RWKV_COLS][None]


def kernel(x_prompt, x_sample, cache_k_win, cache_v_win, state_wkv, state_shift, p_prompt, p_sample,
           norm_mix, w_in, attn_sinks, rwkv_mu, rwkv_w0, rwkv_w2, rwkv_a0, rwkv_a2, rwkv_g2,
           rwkv_k_k, rwkv_k_a, rwkv_r_k, rwkv_ln_w, rwkv_ln_b, w_branch_attn, w_branch_rwkv,
           w_out, norm_ffn, w_ff_up, w_ff_down, w_ple_proj, w_ple_gate, norm_final):
    assert w_in.shape[0] == 1, "single-layer step"
    bp, tp = x_prompt.shape[0], x_prompt.shape[1]
    bs, ts = x_sample.shape[0], x_sample.shape[1]
    rw = R_WIDTH

    wl = jnp.zeros((LORA_PAD, 3 * rw), F32)
    wl = wl.at[0:DECAY_LORA, 0:rw].set(rwkv_w2[0])
    wl = wl.at[DECAY_LORA:DECAY_LORA + AAA_LORA, rw:2 * rw].set(rwkv_a2[0])
    wl = wl.at[DECAY_LORA + AAA_LORA:LORA_COLS, 2 * rw:3 * rw].set(rwkv_g2[0])
    wl = wl.astype(BF16)
    seg_id = np.arange(SEG) // R_HEAD
    bones = jnp.asarray(seg_id[:, None] == seg_id[None, :], BF16)
    mu = rwkv_mu[0]
    row = lambda v: v.reshape(1, -1)
    wkv_params = (row(mu[:3 * rw]), row(jnp.pad(mu[3 * rw:], (0, LORA_PAD - LORA_COLS))),
                  row(rwkv_w0[0]), row(rwkv_a0[0]), row(rwkv_k_k[0]), row(rwkv_k_a[0]),
                  row(rwkv_r_k[0]), row(rwkv_ln_w[0]), row(rwkv_ln_b[0]))
    g_mix = row(norm_mix[0])
    sinks = attn_sinks[0]

    ms = bs * ts
    xs = x_sample.reshape(ms, D_MODEL)
    w_in_t = jnp.transpose(w_in[0])
    proj_s, w_in_b = _proj_cast(xs, g_mix, w_in_t, 1024)

    tiles_p = dict(mix_m=256, ffn_m=1024, ffn_k=1024, ple_m=512)
    xp = x_prompt.reshape(bp * tp, D_MODEL)
    proj_p = _proj(xp, g_mix, w_in_b, 1024, 1280)
    ya_p = _attn_prompt(proj_p, sinks, bp, tp)
    yr_p, s_p, (wu, wd, wo, wg, wba, wbr, w_gates) = _wkv(
        proj_p, jnp.zeros((bp, 1, SHIFT_COLS), F32), jnp.zeros((bp, R_HEADS, R_HEAD, R_HEAD), F32),
        wkv_params, wl, bones, bp, tp, 64, 4,
        casts=((w_ff_up[0], None), (w_ff_down[0], None), (w_out[0], None), (w_ple_gate[0], None),
               (w_branch_attn[0], None), (w_branch_rwkv[0], None), (w_in_t, (P_GATES, 2 * D_MODEL))))
    dense = (g_mix, w_gates, wba, wbr, wo, row(norm_ffn[0]), wu, wd, wg, w_ple_proj[0].astype(BF16),
             row(norm_final))
    yp = _dense_tail(xp, ya_p, yr_p, p_prompt[0].reshape(bp * tp, PLE_DIM), dense, tiles_p)
    pp3 = proj_p.reshape(bp, tp, P_COLS)[:, -WINDOW:]
    k_p = pp3[:, :, P_K:P_K + KV_WIDTH].reshape(1, bp, WINDOW, N_KV_HEADS, HEAD_DIM)
    v_p = pp3[:, :, P_V:P_V + KV_WIDTH].reshape(1, bp, WINDOW, N_KV_HEADS, HEAD_DIM)

    tiles_s = dict(mix_m=ms, ffn_m=ms, ffn_k=2048, ple_m=ms)
    ck_t = jnp.transpose(cache_k_win[0], (0, 2, 3, 1)).reshape(bs, KV_WIDTH, WINDOW)
    cv_t = jnp.transpose(cache_v_win[0], (0, 2, 3, 1)).reshape(bs, KV_WIDTH, WINDOW)
    ya_s, nk_t, nv_t = _attn_sample(proj_s, sinks, ck_t, cv_t, bs, ts, SAMPLE_GROUP)
    prev_s = jnp.pad(state_shift[0], ((0, 0), (0, LORA_PAD - LORA_COLS))).reshape(bs, 1, SHIFT_COLS)
    yr_s, s_s, _ = _wkv(proj_s, prev_s, state_wkv[0], wkv_params, wl, bones, bs, ts, ts, 1, n_seq=SAMPLE_GROUP)
    ys = _dense_tail(xs, ya_s, yr_s, p_sample[0].reshape(ms, PLE_DIM), dense, tiles_s)

    return (yp.reshape(bp, tp, D_MODEL), ys.reshape(bs, ts, D_MODEL),
            k_p, v_p, s_p[None], _shift_out(proj_p, bp, tp),
            jnp.transpose(nk_t.reshape(bs, N_KV_HEADS, HEAD_DIM, WINDOW), (0, 3, 1, 2))[None],
            jnp.transpose(nv_t.reshape(bs, N_KV_HEADS, HEAD_DIM, WINDOW), (0, 3, 1, 2))[None],
            s_s[None], _shift_out(proj_s, bs, ts))
```

```python
import functools

import numpy as np
import jax
import jax.numpy as jnp
from jax import lax
from jax.experimental import pallas as pl
from jax.experimental.pallas import tpu as pltpu

F32 = jnp.float32
BF16 = jnp.bfloat16

D_MODEL = 2048
PLE_DIM = 256
HEAD_DIM = 64
N_Q_HEADS = 16
N_KV_HEADS = 4
GQA_GROUP = 4
ATTN_WIDTH = 1024
KV_WIDTH = 256
WINDOW = 128
ALIBI_MAX = 8.0
R_HEAD = 64
R_WIDTH = 1024
R_HEADS = 16
DECAY_LORA = 64
AAA_LORA = 64
GATE_LORA = 160
LORA_COLS = DECAY_LORA + AAA_LORA + GATE_LORA
LORA_PAD = 512
RWKV_COLS = 3 * R_WIDTH + LORA_COLS
D_FF = 4 * D_MODEL
NORM_EPS = 1e-6
GN_EPS = 64e-5

P_Q = 0
P_K = ATTN_WIDTH
P_V = ATTN_WIDTH + KV_WIDTH
P_C = ATTN_WIDTH + 2 * KV_WIDTH
P_L = P_C + 3 * R_WIDTH
P_COLS = P_L + LORA_PAD
P_GATES = P_C + RWKV_COLS
C_BLK = P_C
SHIFT_COLS = 3 * R_WIDTH + LORA_PAD

PAIR = 128
N_PAIRS = R_WIDTH // PAIR
SEG = 256
MIX_CHUNK = 1024
SAMPLE_GROUP = 8
PROMPT_BLOCKS = 8

V7X_VMEM_BYTES = 64 * 1024 * 1024
VMEM_LIMIT = 56 * 1024 * 1024


def _mm(a, b):
    return jnp.dot(a, b, preferred_element_type=F32)


def _nt(a, b):
    return lax.dot_general(a, b, (((1,), (1,)), ((), ())), preferred_element_type=F32)


def _tn(a, b):
    return lax.dot_general(a, b, (((0,), (0,)), ((), ())), preferred_element_type=F32)


def _rms(x, g):
    ms = jnp.mean(x * x, axis=-1, keepdims=True)
    return x * lax.rsqrt(ms + NORM_EPS) * g


def _alibi_slope(hq):
    return float(2.0 ** (-ALIBI_MAX * (hq + 1) / N_Q_HEADS))


def _cast_specs(casts, n_steps, step_of):
    in_specs, out_specs, shapes = [], [], []
    for wgt, span in casts:
        start, count = (0, wgt.shape[0]) if span is None else span
        blk, width = count // n_steps, wgt.shape[1]
        assert blk * n_steps == count, (wgt.shape, span, n_steps)
        if span is None:
            assert blk % 16 == 0, blk
            in_specs.append(pl.BlockSpec((blk, width), lambda *g: (step_of(*g), 0)))
            out_specs.append(pl.BlockSpec((blk, width), lambda *g: (step_of(*g), 0)))
            shapes.append(jax.ShapeDtypeStruct((count, width), BF16))
        else:
            assert blk % 128 == 0 and start % 8 == 0, span
            in_specs.append(pl.BlockSpec(
                (pl.Element(blk), pl.Element(width)),
                lambda *g, start=start, blk=blk: (pl.multiple_of(start + blk * step_of(*g), 8), 0)))
            out_specs.append(pl.BlockSpec((width, blk), lambda *g: (0, step_of(*g))))
            shapes.append(jax.ShapeDtypeStruct((width, count), BF16))
    return in_specs, out_specs, shapes, tuple(span is not None for _, span in casts)


def _run_casts(srcs, dsts, transposed):
    for src, dst, tr in zip(srcs, dsts, transposed):
        dst[...] = (src[...].T if tr else src[...]).astype(BF16)


def _proj_kernel(x_ref, g_ref, w_ref, o_ref, h_ref):
    @pl.when(pl.program_id(1) == 0)
    def _():
        h_ref[...] = _rms(x_ref[...], g_ref[...]).astype(BF16)

    o_ref[...] = _nt(h_ref[...], w_ref[...])


def _proj_cast_kernel(x_ref, g_ref, w_ref, o_ref, wb_ref, h_ref):
    @pl.when(pl.program_id(0) == 0)
    def _():
        h_ref[...] = _rms(x_ref[...], g_ref[...]).astype(BF16)

    wb = w_ref[...].astype(BF16)
    wb_ref[...] = wb
    o_ref[...] = _nt(h_ref[...], wb)


def _proj_cast(x, g, wt_f32, tn):
    m = x.shape[0]
    return pl.pallas_call(
        _proj_cast_kernel,
        out_shape=(jax.ShapeDtypeStruct((m, P_COLS), F32), jax.ShapeDtypeStruct((P_COLS, D_MODEL), BF16)),
        grid=(P_COLS // tn,),
        in_specs=[
            pl.BlockSpec((m, D_MODEL), lambda j: (0, 0)),
            pl.BlockSpec((1, D_MODEL), lambda j: (0, 0)),
            pl.BlockSpec((tn, D_MODEL), lambda j: (j, 0)),
        ],
        out_specs=(pl.BlockSpec((m, tn), lambda j: (0, j)), pl.BlockSpec((tn, D_MODEL), lambda j: (j, 0))),
        scratch_shapes=[pltpu.VMEM((m, D_MODEL), BF16)],
        compiler_params=pltpu.CompilerParams(
            dimension_semantics=("arbitrary",), vmem_limit_bytes=VMEM_LIMIT),
        name="proj_cast",
    )(x, g, wt_f32)


def _proj(x, g, w, tm, tn):
    m = x.shape[0]
    return pl.pallas_call(
        _proj_kernel,
        out_shape=jax.ShapeDtypeStruct((m, P_COLS), F32),
        grid=(m // tm, P_COLS // tn),
        in_specs=[
            pl.BlockSpec((tm, D_MODEL), lambda i, j: (i, 0)),
            pl.BlockSpec((1, D_MODEL), lambda i, j: (0, 0)),
            pl.BlockSpec((tn, D_MODEL), lambda i, j: (j, 0)),
        ],
        out_specs=pl.BlockSpec((tm, tn), lambda i, j: (i, j)),
        scratch_shapes=[pltpu.VMEM((tm, D_MODEL), BF16)],
        compiler_params=pltpu.CompilerParams(
            dimension_semantics=("arbitrary", "arbitrary"), vmem_limit_bytes=VMEM_LIMIT),
        name="proj",
    )(x, g, w)


def _attend_heads(scores, apply_values, sinks):
    heads = range(len(scores))
    m = []
    for h in heads:
        mh = sinks[h]
        for s in scores[h]:
            mh = jnp.maximum(mh, jnp.max(s, axis=-1, keepdims=True))
        m.append(mh)
    ps = [[jnp.exp(s - m[h]) for s in scores[h]] for h in heads]
    den = []
    for h in heads:
        dh = jnp.exp(sinks[h] - m[h])
        for p in ps[h]:
            dh = dh + jnp.sum(p, axis=-1, keepdims=True)
        den.append(dh)
    outs = []
    for h in heads:
        o = None
        for p, pv in zip(ps[h], apply_values[h]):
            t = pv(p.astype(BF16))
            o = t if o is None else o + t
        outs.append(o * (1.0 / den[h]))
    return outs


def _head_slices():
    q_sl = [slice(hq * HEAD_DIM, (hq + 1) * HEAD_DIM) for hq in range(N_Q_HEADS)]
    kv_sl = [slice((hq // GQA_GROUP) * HEAD_DIM, (hq // GQA_GROUP + 1) * HEAD_DIM) for hq in range(N_Q_HEADS)]
    return q_sl, kv_sl


def _alibi_bias(dist, valid):
    slopes = np.array([_alibi_slope(hq) for hq in range(N_Q_HEADS)], np.float32)
    return np.where(valid[None], -slopes[:, None, None] * dist[None].astype(np.float32), -np.inf).astype(np.float32)


def _attn_prompt_kernel(sink_ref, bias_ref, q_ref, kp_ref, kc_ref, vp_ref, vc_ref, o_ref):
    w, grp = WINDOW, GQA_GROUP
    n_blk = q_ref.shape[0] // w
    kj = lax.broadcasted_iota(jnp.int32, (grp * w, 2 * w), 1)
    no_prev = (kj < w) & (pl.program_id(1) == 0)
    member = lax.broadcasted_iota(jnp.int32, (grp * w, 1), 0) // w
    q = q_ref[...] * (HEAD_DIM ** -0.5)
    kc, vc = kc_ref[...].astype(BF16), vc_ref[...].astype(BF16)
    ks = jnp.concatenate([kp_ref[...].astype(BF16), kc], axis=0)
    vs = jnp.concatenate([vp_ref[...].astype(BF16), vc], axis=0)
    q_sl, kv_sl = _head_slices()
    sinks = []
    for h in range(N_KV_HEADS):
        col = jnp.full((grp * w, 1), sink_ref[h * grp], F32)
        for g in range(1, grp):
            col = jnp.where(member == g, sink_ref[h * grp + g], col)
        sinks.append(col)
    scores, values = [], []
    for s in range(n_blk):
        k2, v2 = ks[s * w:(s + 2) * w], vs[s * w:(s + 2) * w]
        qb = q[s * w:(s + 1) * w]
        for h in range(N_KV_HEADS):
            q4 = jnp.concatenate([qb[:, q_sl[h * grp + g]] for g in range(grp)], axis=0).astype(BF16)
            sc = _nt(q4, k2[:, kv_sl[h * grp]]) + bias_ref[h]
            scores.append([jnp.where(no_prev, -jnp.inf, sc) if s == 0 else sc])
            values.append([functools.partial(_mm, b=v2[:, kv_sl[h * grp]])])
    outs = _attend_heads(scores, values, sinks * n_blk)
    for s in range(n_blk):
        for h in range(N_KV_HEADS):
            o4 = outs[s * N_KV_HEADS + h]
            for g in range(grp):
                o_ref[s * w:(s + 1) * w, q_sl[h * grp + g]] = o4[g * w:(g + 1) * w]


def _attn_prompt(proj, sinks, batch, seq):
    rows = PROMPT_BLOCKS * WINDOW
    ns = seq // rows
    kcol, vcol = P_K // KV_WIDTH, P_V // KV_WIDTH

    def cur(c):
        return lambda b, i: (b * ns + i, c)

    def prev(c):
        return lambda b, i: (b * ns * PROMPT_BLOCKS + jnp.maximum(i * PROMPT_BLOCKS - 1, 0), c)

    ti = np.arange(WINDOW)[:, None]
    kj = np.arange(2 * WINDOW)[None, :]
    dist = ti - kj + WINDOW
    bias = _alibi_bias(dist, (dist >= 0) & (dist <= WINDOW))
    bias = bias.reshape(N_KV_HEADS, GQA_GROUP * WINDOW, 2 * WINDOW)
    return pl.pallas_call(
        _attn_prompt_kernel,
        out_shape=jax.ShapeDtypeStruct((batch * seq, ATTN_WIDTH), F32),
        grid=(batch, ns),
        in_specs=[
            pl.BlockSpec(memory_space=pltpu.SMEM),
            _resident((N_KV_HEADS, GQA_GROUP * WINDOW, 2 * WINDOW)),
            pl.BlockSpec((rows, ATTN_WIDTH), cur(P_Q // ATTN_WIDTH)),
            pl.BlockSpec((WINDOW, KV_WIDTH), prev(kcol)),
            pl.BlockSpec((rows, KV_WIDTH), cur(kcol)),
            pl.BlockSpec((WINDOW, KV_WIDTH), prev(vcol)),
            pl.BlockSpec((rows, KV_WIDTH), cur(vcol)),
        ],
        out_specs=pl.BlockSpec((rows, ATTN_WIDTH), lambda b, i: (b * ns + i, 0)),
        compiler_params=pltpu.CompilerParams(dimension_semantics=("arbitrary", "arbitrary")),
        name="attn_prompt",
    )(sinks, jnp.asarray(bias), proj, proj, proj, proj, proj)


def _attn_sample_kernel(sink_ref, q_ref, kn_ref, vn_ref, ckt_ref, cvt_ref, o_ref, nkt_ref, nvt_ref, *, seq):
    t, w = seq, WINDOW
    n_seq = q_ref.shape[0] // t
    ti = lax.broadcasted_iota(jnp.int32, (t, w), 0)
    cj = lax.broadcasted_iota(jnp.int32, (t, w), 1)
    dist_c = (ti - cj + w).astype(F32)
    valid_c = cj >= ti
    ti2 = lax.broadcasted_iota(jnp.int32, (t, t), 0)
    tj2 = lax.broadcasted_iota(jnp.int32, (t, t), 1)
    dist_n = (ti2 - tj2).astype(F32)
    valid_n = tj2 <= ti2
    grp, kvs = GQA_GROUP, range(N_KV_HEADS)
    q_sl, kv_sl = _head_slices()
    member = lax.broadcasted_iota(jnp.int32, (grp * t, 1), 0) // t

    def per_member(fn):
        return [jnp.concatenate([fn(h * grp + g) for g in range(grp)], axis=0) for h in kvs]

    bias_c = per_member(lambda hq: jnp.where(valid_c, -_alibi_slope(hq) * dist_c, -jnp.inf))
    bias_n = per_member(lambda hq: jnp.where(valid_n, -_alibi_slope(hq) * dist_n, -jnp.inf))
    sinks = []
    for h in kvs:
        col = jnp.full((grp * t, 1), sink_ref[h * grp], F32)
        for g in range(1, grp):
            col = jnp.where(member == g, sink_ref[h * grp + g], col)
        sinks.append(col)
    q_all = q_ref[...] * (HEAD_DIM ** -0.5)
    is_new = lax.broadcasted_iota(jnp.int32, (1, w), 1) >= w - t
    put = (cj == ti + (w - t)).astype(BF16)

    def place_new(x):
        hi = x.astype(BF16)
        rem = x - hi.astype(F32)
        mid = rem.astype(BF16)
        return _tn(hi, put) + _tn(mid, put) + _tn((rem - mid.astype(F32)).astype(BF16), put)

    scores, values = [], []
    for s in range(n_seq):
        rs = slice(s * t, (s + 1) * t)
        ckt, cvt = ckt_ref[s], cvt_ref[s]
        nkt_ref[s] = jnp.where(is_new, place_new(kn_ref[rs, :]), pltpu.roll(ckt, w - t, axis=1))
        nvt_ref[s] = jnp.where(is_new, place_new(vn_ref[rs, :]), pltpu.roll(cvt, w - t, axis=1))
        cktb, cvtb = ckt.astype(BF16), cvt.astype(BF16)
        knb, vnb = kn_ref[rs, :].astype(BF16), vn_ref[rs, :].astype(BF16)
        q = q_all[rs]
        qs = per_member(lambda hq: q[:, q_sl[hq]])
        for h in kvs:
            kv = kv_sl[h * grp]
            q4 = qs[h].astype(BF16)
            scores.append([_mm(q4, cktb[kv, :]) + bias_c[h], _nt(q4, knb[:, kv]) + bias_n[h]])
            values.append([functools.partial(_nt, b=cvtb[kv, :]), functools.partial(_mm, b=vnb[:, kv])])
    outs = _attend_heads(scores, values, sinks * n_seq)
    for s in range(n_seq):
        for h in kvs:
            o4 = outs[s * N_KV_HEADS + h]
            for g in range(grp):
                o_ref[s * t:(s + 1) * t, q_sl[h * grp + g]] = o4[g * t:(g + 1) * t]


def _attn_sample(proj, sinks, cache_kt, cache_vt, batch, seq, n_seq):
    kcol, vcol = P_K // KV_WIDTH, P_V // KV_WIDTH
    rows = n_seq * seq
    win = jax.ShapeDtypeStruct((batch, KV_WIDTH, WINDOW), F32)
    return pl.pallas_call(
        functools.partial(_attn_sample_kernel, seq=seq),
        out_shape=(jax.ShapeDtypeStruct((batch * seq, ATTN_WIDTH), F32), win, win),
        grid=(batch // n_seq,),
        in_specs=[
            pl.BlockSpec(memory_space=pltpu.SMEM),
            pl.BlockSpec((rows, ATTN_WIDTH), lambda b: (b, P_Q // ATTN_WIDTH)),
            pl.BlockSpec((rows, KV_WIDTH), lambda b: (b, kcol)),
            pl.BlockSpec((rows, KV_WIDTH), lambda b: (b, vcol)),
            pl.BlockSpec((n_seq, KV_WIDTH, WINDOW), lambda b: (b, 0, 0)),
            pl.BlockSpec((n_seq, KV_WIDTH, WINDOW), lambda b: (b, 0, 0)),
        ],
        out_specs=(
            pl.BlockSpec((rows, ATTN_WIDTH), lambda b: (b, 0)),
            pl.BlockSpec((n_seq, KV_WIDTH, WINDOW), lambda b: (b, 0, 0)),
            pl.BlockSpec((n_seq, KV_WIDTH, WINDOW), lambda b: (b, 0, 0)),
        ),
        compiler_params=pltpu.CompilerParams(dimension_semantics=("arbitrary",)),
        name="attn_sample",
    )(sinks, proj, proj, proj, cache_kt, cache_vt)


def _seg_sum(x, bones):
    rows = x.shape[0]
    hi = x.astype(BF16).astype(F32)
    lo = x - hi
    groups = [slice(j * SEG, (j + 1) * SEG) for j in range(R_WIDTH // SEG)]
    lhs = jnp.concatenate([t[:, sl] for sl in groups for t in (hi, lo)], axis=0).astype(BF16)
    out = _mm(lhs, bones)
    return jnp.concatenate(
        [out[2 * j * rows:(2 * j + 1) * rows] + out[(2 * j + 1) * rows:(2 * j + 2) * rows]
         for j in range(len(groups))], axis=1)


def _wkv_kernel(pa_ref, pb_ref, pl_ref, prev_ref, s0_ref,
                mu_ref, mul_ref, w0_ref, a0_ref, kk_ref, ka_ref, rk_ref, lnw_ref, lnb_ref,
                wl_ref, bones_ref, *rest, chunk, n_sub, n_seq, n_steps, cast_transposed):
    n_cast = len(cast_transposed)
    cast_src, (y_ref, so_ref) = rest[:n_cast], rest[n_cast:n_cast + 2]
    cast_dst, (s_ref, carry_ref) = rest[n_cast + 2:2 * n_cast + 2], rest[2 * n_cast + 2:]
    _run_casts(cast_src, cast_dst, cast_transposed)

    step = pl.program_id(1)
    cs = chunk
    seq_rows = n_sub * cs
    rows = n_seq * seq_rows
    gc = 2 * cs
    hd = R_HEAD
    w = R_WIDTH
    seqs = range(n_seq)

    @pl.when(step == 0)
    def _init():
        s_ref[...] = jnp.zeros(s_ref.shape, F32)
        for q in seqs:
            carry_ref[q:q + 1, :] = prev_ref[q]
            for p in range(N_PAIRS):
                s_ref[q * N_PAIRS + p, 0:hd, 0:hd] = s0_ref[q, 2 * p]
                s_ref[q * N_PAIRS + p, hd:2 * hd, hd:2 * hd] = s0_ref[q, 2 * p + 1]

    row = lax.broadcasted_iota(jnp.int32, (rows, 1), 0)

    def token_shift(x, lo, hi, mu):
        shifted = pltpu.roll(x, 1, axis=0)
        for q in seqs:
            shifted = jnp.where(row == q * seq_rows, carry_ref[q:q + 1, lo:hi], shifted)
        return x + (shifted - x) * mu

    cols = jnp.concatenate([pa_ref[...], pb_ref[...]], axis=1)
    l_raw = pl_ref[...]
    xx = token_shift(cols, 0, 3 * w, mu_ref[...])
    xr, xk, xv = xx[:, 0:w], xx[:, w:2 * w], xx[:, 2 * w:3 * w]
    xl = token_shift(l_raw, 3 * w, SHIFT_COLS, mul_ref[...])
    for q in seqs:
        last = (q + 1) * seq_rows
        carry_ref[q:q + 1, 0:3 * w] = cols[last - 1:last, :]
        carry_ref[q:q + 1, 3 * w:] = l_raw[last - 1:last, :]

    lane_l = lax.broadcasted_iota(jnp.int32, (1, LORA_PAD), 1)
    act = jnp.where(lane_l < DECAY_LORA, jnp.tanh(xl),
                    jnp.where(lane_l < DECAY_LORA + AAA_LORA, xl,
                              jnp.where(lane_l < LORA_COLS, jax.nn.sigmoid(xl), 0.0)))
    up = _mm(act.astype(BF16), wl_ref[...])
    z = -(w0_ref[...] + up[:, 0:w])
    softplus = jnp.maximum(z, 0.0) + jnp.log1p(jnp.exp(-jnp.abs(z)))
    lwd = -jnp.exp(-softplus - 0.5)
    a = jax.nn.sigmoid(a0_ref[...] + up[:, w:2 * w])
    gate = up[:, 2 * w:3 * w]

    bones = bones_ref[...]
    kkn = xk * kk_ref[...]
    kk = kkn * lax.rsqrt(jnp.maximum(_seg_sum(kkn * kkn, bones), 1e-24))
    k2 = xk * (1.0 + (a - 1.0) * ka_ref[...])
    b = kk * a

    tri = (lax.broadcasted_iota(jnp.int32, (cs, cs), 0)
           >= lax.broadcasted_iota(jnp.int32, (cs, cs), 1)).astype(BF16)
    ri = lax.broadcasted_iota(jnp.int32, (gc, gc), 0)
    ci = lax.broadcasted_iota(jnp.int32, (gc, gc), 1)
    same_head = (ri >= cs) == (ci >= cs)
    strict = same_head & (ci < ri)
    incl = same_head & (ci <= ri)
    eye = (ri == ci).astype(F32)
    head0 = lax.broadcasted_iota(jnp.int32, (1, PAIR), 1) < hd

    def stack(x):
        return jnp.concatenate([jnp.where(head0, x, 0.0), jnp.where(head0, 0.0, x)], axis=0).astype(BF16)

    pairs = range(N_PAIRS)
    lanes = [slice(p * PAIR, (p + 1) * PAIR) for p in pairs]
    squarings = cs.bit_length() - 2

    chunks = []
    for sub in range(n_seq * n_sub):
        rs = slice(sub * cs, (sub + 1) * cs)
        lw_c = lwd[rs]
        lw_hi = lw_c.astype(BF16)
        lw_r = lw_c - lw_hi.astype(F32)
        lw_mid = lw_r.astype(BF16)
        cum = _mm(tri, lw_hi) + _mm(tri, lw_mid) + _mm(tri, (lw_r - lw_mid.astype(F32)).astype(BF16))
        cum_last = cum[cs - 1:cs, :]
        e_inv = jnp.exp(-cum)
        e_last = jnp.exp(cum_last - cum)
        kq = kk[rs] * jnp.exp(cum - lw_c)
        rq = xr[rs] * jnp.exp(cum)
        kd = k2[rs] * e_inv
        bd = b[rs] * e_inv
        kdp = k2[rs] * e_last
        bdp = b[rs] * e_last
        xv_c = xv[rs]
        kq_s = [stack(kq[:, sl]) for sl in lanes]
        bd_s = [stack(bd[:, sl]) for sl in lanes]
        kd_s = [stack(kd[:, sl]) for sl in lanes]
        rq_s = [stack(rq[:, sl]) for sl in lanes]
        probes = [jnp.concatenate([kq_s[p], rq_s[p]], axis=0) for p in pairs]
        if gc % PAIR == 0:
            a_all = [_nt(probes[p], jnp.concatenate([bd_s[p], kd_s[p]], axis=0)) for p in pairs]
            a_kb = [jnp.where(strict, t[0:gc, 0:gc], 0.0) for t in a_all]
            a_kk = [jnp.where(strict, t[0:gc, gc:2 * gc], 0.0).astype(BF16) for t in a_all]
            a_rb = [jnp.where(incl, t[gc:2 * gc, 0:gc], 0.0).astype(BF16) for t in a_all]
            a_rk = [jnp.where(incl, t[gc:2 * gc, gc:2 * gc], 0.0).astype(BF16) for t in a_all]
        else:
            a_kb = [jnp.where(strict, _nt(kq_s[p], bd_s[p]), 0.0) for p in pairs]
            a_kk = [jnp.where(strict, _nt(kq_s[p], kd_s[p]), 0.0).astype(BF16) for p in pairs]
            a_rb = [jnp.where(incl, _nt(rq_s[p], bd_s[p]), 0.0).astype(BF16) for p in pairs]
            a_rk = [jnp.where(incl, _nt(rq_s[p], kd_s[p]), 0.0).astype(BF16) for p in pairs]
        inv = [eye - t for t in a_kb]
        if squarings >= 1:
            apow_b = [t.astype(BF16) for t in a_kb]
            apow_b = [_mm(t, t).astype(BF16) for t in apow_b]
            for _ in range(squarings - 1):
                both = [_mm(jnp.concatenate([apow_b[p], inv[p].astype(BF16)], axis=0), apow_b[p]) for p in pairs]
                apow_b = [t[0:gc].astype(BF16) for t in both]
                inv = [inv[p] + both[p][gc:2 * gc] for p in pairs]
            inv = [inv[p] + _mm(inv[p].astype(BF16), apow_b[p]) for p in pairs]
        chunks.append(dict(
            probes=probes, a_kk=a_kk, a_rk=a_rk, a_rb=a_rb, inv=[t.astype(BF16) for t in inv],
            v_s=[stack(xv_c[:, sl]) for sl in lanes],
            upd=[jnp.concatenate([stack(kdp[:, sl]), stack(bdp[:, sl])], axis=0) for sl in lanes],
            p_last=jnp.exp(cum_last)))

    sp = [(q, p) for q in seqs for p in pairs]
    state = {(q, p): s_ref[q * N_PAIRS + p] for q, p in sp}
    y_rows = [None] * (n_seq * n_sub)
    for level in range(n_sub):
        ch = {q: chunks[q * n_sub + level] for q in seqs}
        s_b = {k: state[k].astype(BF16) for k in sp}
        state_t = {(q, p): _nt(ch[q]["probes"][p], s_b[q, p]) for q, p in sp}
        rhs = {(q, p): state_t[q, p][0:gc] + _mm(ch[q]["a_kk"][p], ch[q]["v_s"][p]) for q, p in sp}
        u_b = {(q, p): _mm(ch[q]["inv"][p], rhs[q, p].astype(BF16)).astype(BF16) for q, p in sp}
        if gc % PAIR == 0:
            y_s = {(q, p): state_t[q, p][gc:2 * gc]
                   + _mm(jnp.concatenate([ch[q]["a_rk"][p], -ch[q]["a_rb"][p]], axis=1),
                         jnp.concatenate([ch[q]["v_s"][p], u_b[q, p]], axis=0)) for q, p in sp}
        else:
            y_s = {(q, p): state_t[q, p][gc:2 * gc] + _mm(ch[q]["a_rk"][p], ch[q]["v_s"][p])
                   - _mm(ch[q]["a_rb"][p], u_b[q, p]) for q, p in sp}
        state = {(q, p): state[q, p] * ch[q]["p_last"][:, lanes[p]]
                 + _tn(jnp.concatenate([ch[q]["v_s"][p], -u_b[q, p]], axis=0), ch[q]["upd"][p]) for q, p in sp}
        for q in seqs:
            y_rows[q * n_sub + level] = jnp.concatenate(
                [y_s[q, p][0:cs] + y_s[q, p][cs:gc] for p in pairs], axis=1)
    for q, p in sp:
        s_ref[q * N_PAIRS + p] = state[q, p]

    y = jnp.concatenate(y_rows, axis=0) if len(y_rows) > 1 else y_rows[0]
    mean = _seg_sum(y, bones) * (1.0 / hd)
    d = y - mean
    var = _seg_sum(d * d, bones) * (1.0 / hd)
    yn = d * lax.rsqrt(var + GN_EPS) * lnw_ref[...] + lnb_ref[...]
    bonus = _seg_sum(xr * k2 * rk_ref[...], bones) * xv
    y_ref[...] = (yn + bonus) * gate

    @pl.when(step == n_steps - 1)
    def _fin():
        for q, p in sp:
            so_ref[q, 2 * p] = s_ref[q * N_PAIRS + p, 0:hd, 0:hd]
            so_ref[q, 2 * p + 1] = s_ref[q * N_PAIRS + p, hd:2 * hd, hd:2 * hd]


def _wkv(proj, prev0, s0, params, wl, bones, batch, seq, chunk, n_sub, n_seq=1, casts=()):
    ns = seq // (chunk * n_sub)
    assert n_seq == 1 or ns == 1, "several sequences per step only when a step covers them whole"
    rows = n_seq * chunk * n_sub
    nb = batch // n_seq
    rw = R_WIDTH

    def col(cb):
        return lambda b, c: (b * ns + c, cb)

    const2 = lambda b, c: (0, 0)
    vec = lambda n: pl.BlockSpec((1, n), const2)
    cast_in, cast_out, cast_shapes, cast_tr = _cast_specs(casts, nb * ns, lambda b, c: b * ns + c)
    mu, mul, w0, a0, k_k, k_a, r_k, ln_w, ln_b = params
    outs = pl.pallas_call(
        functools.partial(_wkv_kernel, chunk=chunk, n_sub=n_sub, n_seq=n_seq, n_steps=ns,
                          cast_transposed=cast_tr),
        out_shape=(jax.ShapeDtypeStruct((batch * seq, rw), F32),
                   jax.ShapeDtypeStruct((batch, R_HEADS, R_HEAD, R_HEAD), F32), *cast_shapes),
        grid=(nb, ns),
        in_specs=[
            pl.BlockSpec((rows, C_BLK), col(P_C // C_BLK)),
            pl.BlockSpec((rows, C_BLK), col(P_C // C_BLK + 1)),
            pl.BlockSpec((rows, LORA_PAD), col(P_L // LORA_PAD)),
            pl.BlockSpec((n_seq, 1, SHIFT_COLS), lambda b, c: (b, 0, 0)),
            pl.BlockSpec((n_seq, R_HEADS, R_HEAD, R_HEAD), lambda b, c: (b, 0, 0, 0)),
            vec(3 * rw), vec(LORA_PAD), vec(rw), vec(rw), vec(rw), vec(rw), vec(rw), vec(rw), vec(rw),
            pl.BlockSpec((LORA_PAD, 3 * rw), const2),
            pl.BlockSpec((SEG, SEG), const2),
            *cast_in,
        ],
        out_specs=(
            pl.BlockSpec((rows, rw), lambda b, c: (b * ns + c, 0)),
            pl.BlockSpec((n_seq, R_HEADS, R_HEAD, R_HEAD), lambda b, c: (b, 0, 0, 0)),
            *cast_out,
        ),
        scratch_shapes=[pltpu.VMEM((n_seq * N_PAIRS, PAIR, PAIR), F32), pltpu.VMEM((n_seq, SHIFT_COLS), F32)],
        compiler_params=pltpu.CompilerParams(
            dimension_semantics=("arbitrary", "arbitrary"), vmem_limit_bytes=VMEM_LIMIT),
        name="wkv",
    )(proj, proj, proj, prev0, s0, mu, mul, w0, a0, k_k, k_a, r_k, ln_w, ln_b, wl, bones,
      *[wgt for wgt, _ in casts])
    return outs[0], outs[1], outs[2:]


def _mix_kernel(x_ref, ya_ref, yr_ref, g_ref, wg_ref, wba_ref, wbr_ref, wo_ref, o_ref):
    x = x_ref[...]
    h = _rms(x, g_ref[...]).astype(BF16)
    ya = ya_ref[...].astype(BF16)
    yr = yr_ref[...].astype(BF16)
    acc = x
    for c in range(D_MODEL // MIX_CHUNK):
        ca = slice(c * MIX_CHUNK, (c + 1) * MIX_CHUNK)
        cr = slice(D_MODEL + c * MIX_CHUNK, D_MODEL + (c + 1) * MIX_CHUNK)
        mixed = (jax.nn.sigmoid(_mm(h, wg_ref[:, ca])) * _mm(ya, wba_ref[:, ca])
                 + jax.nn.sigmoid(_mm(h, wg_ref[:, cr])) * _mm(yr, wbr_ref[:, ca]))
        acc = acc + _mm(mixed.astype(BF16), wo_ref[ca, :])
    o_ref[...] = acc


def _resident(shape):
    return pl.BlockSpec(shape, lambda *_: (0,) * len(shape), pipeline_mode=pl.Buffered(1))


def _mix(x, ya, yr, g, wg, wba, wbr, wo, tm):
    m = x.shape[0]
    row = lambda i: (i, 0)
    return pl.pallas_call(
        _mix_kernel,
        out_shape=jax.ShapeDtypeStruct((m, D_MODEL), F32),
        grid=(m // tm,),
        in_specs=[
            pl.BlockSpec((tm, D_MODEL), row),
            pl.BlockSpec((tm, ATTN_WIDTH), row),
            pl.BlockSpec((tm, R_WIDTH), row),
            pl.BlockSpec((1, D_MODEL), lambda i: (0, 0)),
            _resident((D_MODEL, 2 * D_MODEL)),
            _resident((ATTN_WIDTH, D_MODEL)),
            _resident((R_WIDTH, D_MODEL)),
            _resident((D_MODEL, D_MODEL)),
        ],
        out_specs=pl.BlockSpec((tm, D_MODEL), row),
        compiler_params=pltpu.CompilerParams(
            dimension_semantics=("arbitrary",), vmem_limit_bytes=VMEM_LIMIT),
        name="mix",
    )(x, ya, yr, g, wg, wba, wbr, wo)


def _ffn_kernel(x_ref, g_ref, wu_ref, wd_ref, o_ref, h_ref):
    @pl.when(pl.program_id(1) == 0)
    def _():
        x = x_ref[...]
        h_ref[...] = _rms(x, g_ref[...]).astype(BF16)
        o_ref[...] = x

    u = _mm(h_ref[...], wu_ref[...])
    o_ref[...] += _mm(jnp.square(jnp.maximum(u, 0.0)).astype(BF16), wd_ref[...])


def _ffn_vmem_bytes(tm, tk):
    return 2 * 2 * tm * D_MODEL * 4 + 2 * 2 * D_MODEL * tk * 2 + tm * D_MODEL * 2 + tm * tk * (4 + 2)


def _ffn(x, g, wu, wd, tm, tk):
    m = x.shape[0]
    vmem = max(VMEM_LIMIT, _ffn_vmem_bytes(tm, tk))
    assert vmem <= V7X_VMEM_BYTES, (tm, tk)
    return pl.pallas_call(
        _ffn_kernel,
        out_shape=jax.ShapeDtypeStruct((m, D_MODEL), F32),
        grid=(m // tm, D_FF // tk),
        in_specs=[
            pl.BlockSpec((tm, D_MODEL), lambda i, k: (i, 0)),
            pl.BlockSpec((1, D_MODEL), lambda i, k: (0, 0)),
            pl.BlockSpec((D_MODEL, tk), lambda i, k: (0, k)),
            pl.BlockSpec((tk, D_MODEL), lambda i, k: (k, 0)),
        ],
        out_specs=pl.BlockSpec((tm, D_MODEL), lambda i, k: (i, 0)),
        scratch_shapes=[pltpu.VMEM((tm, D_MODEL), BF16)],
        compiler_params=pltpu.CompilerParams(
            dimension_semantics=("arbitrary", "arbitrary"), vmem_limit_bytes=vmem),
        name="ffn",
    )(x, g, wu, wd)


def _ple_kernel(x_ref, pe_ref, wg_ref, wp_ref, g_ref, o_ref):
    x = x_ref[...]
    gate = jax.nn.sigmoid(_mm(x.astype(BF16), wg_ref[...]))
    x = x + gate * _mm(pe_ref[...].astype(BF16), wp_ref[...])
    o_ref[...] = _rms(x, g_ref[...])


def _ple(x, pe, wg, wp, g, tm):
    m = x.shape[0]
    row = lambda i: (i, 0)
    return pl.pallas_call(
        _ple_kernel,
        out_shape=jax.ShapeDtypeStruct((m, D_MODEL), F32),
        grid=(m // tm,),
        in_specs=[
            pl.BlockSpec((tm, D_MODEL), row),
            pl.BlockSpec((tm, PLE_DIM), row),
            _resident((D_MODEL, D_MODEL)),
            _resident((PLE_DIM, D_MODEL)),
            pl.BlockSpec((1, D_MODEL), lambda i: (0, 0)),
        ],
        out_specs=pl.BlockSpec((tm, D_MODEL), row),
        compiler_params=pltpu.CompilerParams(
            dimension_semantics=("arbitrary",), vmem_limit_bytes=VMEM_LIMIT),
        name="ple",
    )(x, pe, wg, wp, g)


def _dense_tail(x, ya, yr, pe, wts, tiles):
    (g_mix, w_gates, wba, wbr, wo, g_ffn, wu, wd, wg, wp, g_fin) = wts
    x = _mix(x, ya, yr, g_mix, w_gates, wba, wbr, wo, tiles["mix_m"])
    x = _ffn(x, g_ffn, wu, wd, tiles["ffn_m"], tiles["ffn_k"])
    return _ple(x, pe, wg, wp, g_fin, tiles["ple_m"])


def _shift_out(proj, batch, seq):
    last = proj.reshape(batch, seq, P_COLS)[:, -1]
    return last[:, P_C:P_C + RWKV_COLS][None]


def kernel(x_prompt, x_sample, cache_k_win, cache_v_win, state_wkv, state_shift, p_prompt, p_sample,
           norm_mix, w_in, attn_sinks, rwkv_mu, rwkv_w0, rwkv_w2, rwkv_a0, rwkv_a2, rwkv_g2,
           rwkv_k_k, rwkv_k_a, rwkv_r_k, rwkv_ln_w, rwkv_ln_b, w_branch_attn, w_branch_rwkv,
           w_out, norm_ffn, w_ff_up, w_ff_down, w_ple_proj, w_ple_gate, norm_final):
    assert w_in.shape[0] == 1, "single-layer step"
    bp, tp = x_prompt.shape[0], x_prompt.shape[1]
    bs, ts = x_sample.shape[0], x_sample.shape[1]
    rw = R_WIDTH

    wl = jnp.zeros((LORA_PAD, 3 * rw), F32)
    wl = wl.at[0:DECAY_LORA, 0:rw].set(rwkv_w2[0])
    wl = wl.at[DECAY_LORA:DECAY_LORA + AAA_LORA, rw:2 * rw].set(rwkv_a2[0])
    wl = wl.at[DECAY_LORA + AAA_LORA:LORA_COLS, 2 * rw:3 * rw].set(rwkv_g2[0])
    wl = wl.astype(BF16)
    seg_id = np.arange(SEG) // R_HEAD
    bones = jnp.asarray(seg_id[:, None] == seg_id[None, :], BF16)
    mu = rwkv_mu[0]
    row = lambda v: v.reshape(1, -1)
    wkv_params = (row(mu[:3 * rw]), row(jnp.pad(mu[3 * rw:], (0, LORA_PAD - LORA_COLS))),
                  row(rwkv_w0[0]), row(rwkv_a0[0]), row(rwkv_k_k[0]), row(rwkv_k_a[0]),
                  row(rwkv_r_k[0]), row(rwkv_ln_w[0]), row(rwkv_ln_b[0]))
    g_mix = row(norm_mix[0])
    sinks = attn_sinks[0]

    ms = bs * ts
    xs = x_sample.reshape(ms, D_MODEL)
    w_in_t = jnp.transpose(w_in[0])
    proj_s, w_in_b = _proj_cast(xs, g_mix, w_in_t, 1024)

    tiles_p = dict(mix_m=256, ffn_m=1024, ffn_k=1024, ple_m=512)
    xp = x_prompt.reshape(bp * tp, D_MODEL)
    proj_p = _proj(xp, g_mix, w_in_b, 1024, 1280)
    ya_p = _attn_prompt(proj_p, sinks, bp, tp)
    yr_p, s_p, (wu, wd, wo, wg, wba, wbr, w_gates) = _wkv(
        proj_p, jnp.zeros((bp, 1, SHIFT_COLS), F32), jnp.zeros((bp, R_HEADS, R_HEAD, R_HEAD), F32),
        wkv_params, wl, bones, bp, tp, 64, 4,
        casts=((w_ff_up[0], None), (w_ff_down[0], None), (w_out[0], None), (w_ple_gate[0], None),
               (w_branch_attn[0], None), (w_branch_rwkv[0], None), (w_in_t, (P_GATES, 2 * D_MODEL))))
    dense = (g_mix, w_gates, wba, wbr, wo, row(norm_ffn[0]), wu, wd, wg, w_ple_proj[0].astype(BF16),
             row(norm_final))
    yp = _dense_tail(xp, ya_p, yr_p, p_prompt[0].reshape(bp * tp, PLE_DIM), dense, tiles_p)
    pp3 = proj_p.reshape(bp, tp, P_COLS)[:, -WINDOW:]
    k_p = pp3[:, :, P_K:P_K + KV_WIDTH].reshape(1, bp, WINDOW, N_KV_HEADS, HEAD_DIM)
    v_p = pp3[:, :, P_V:P_V + KV_WIDTH].reshape(1, bp, WINDOW, N_KV_HEADS, HEAD_DIM)

    tiles_s = dict(mix_m=ms, ffn_m=ms, ffn_k=2048, ple_m=ms)
    ck_t = jnp.transpose(cache_k_win[0], (0, 2, 3, 1)).reshape(bs, KV_WIDTH, WINDOW)
    cv_t = jnp.transpose(cache_v_win[0], (0, 2, 3, 1)).reshape(bs, KV_WIDTH, WINDOW)
    ya_s, nk_t, nv_t = _attn_sample(proj_s, sinks, ck_t, cv_t, bs, ts, SAMPLE_GROUP)
    prev_s = jnp.pad(state_shift[0], ((0, 0), (0, LORA_PAD - LORA_COLS))).reshape(bs, 1, SHIFT_COLS)
    yr_s, s_s, _ = _wkv(proj_s, prev_s, state_wkv[0], wkv_params, wl, bones, bs, ts, ts, 1, n_seq=SAMPLE_GROUP)
    ys = _dense_tail(xs, ya_s, yr_s, p_sample[0].reshape(ms, PLE_DIM), dense, tiles_s)

    return (yp.reshape(bp, tp, D_MODEL), ys.reshape(bs, ts, D_MODEL),
            k_p, v_p, s_p[None], _shift_out(proj_p, bp, tp),
            jnp.transpose(nk_t.reshape(bs, N_KV_HEADS, HEAD_DIM, WINDOW), (0, 3, 1, 2))[None],
            jnp.transpose(nv_t.reshape(bs, N_KV_HEADS, HEAD_DIM, WINDOW), (0, 3, 1, 2))[None],
            s_s[None], _shift_out(proj_s, bs, ts))
```
